```python
import math
import jax, jax.numpy as jnp
from jax import lax
import numpy as np

D_MODEL = 2048
BATCH = 2
SEQ = 4096
DEPTH = 2

D_MIX = D_MODEL
MLA_HEADS = 6
QK_NOPE = 128
QK_ROPE = 64
QK_HEAD = QK_NOPE + QK_ROPE
V_HEAD = 128
Q_LORA = 512
KV_LORA = 256
MLA_WIDTH = MLA_HEADS * V_HEAD
ROPE_THETA = 10000.0
Q_BLOCK = 128
CONV_WIDTH = 512
CONV_K = 31
SSD_HEADS = 12
SSD_HEAD_DIM = 64
SSD_WIDTH = SSD_HEADS * SSD_HEAD_DIM
SSD_GROUPS = 2
SSD_STATE = 128
SSD_CONV_K = 4
SSD_CHUNK = 128
SSD_XBC = SSD_WIDTH + 2 * SSD_GROUPS * SSD_STATE

NORM_EPS = 1e-6
LN_EPS = 1e-5

IN_SPLITS = (Q_LORA, KV_LORA, QK_ROPE, MLA_WIDTH,
             2 * CONV_WIDTH, CONV_WIDTH,
             SSD_WIDTH, SSD_XBC, SSD_HEADS)
D_IN_PROJ = sum(IN_SPLITS)

kernel_name = 'hybrid_mla_conformer_ssd_parallel'


def rms_norm(x, g, eps=NORM_EPS):
    xf = x.astype(jnp.float32)
    y = xf * lax.rsqrt(jnp.mean(xf * xf, axis=-1, keepdims=True) + eps)
    return (y * g.astype(jnp.float32)).astype(x.dtype)


def layer_norm(x, g, b, eps=LN_EPS):
    xf = x.astype(jnp.float32)
    mu = jnp.mean(xf, axis=-1, keepdims=True)
    var = jnp.mean(jnp.square(xf - mu), axis=-1, keepdims=True)
    y = (xf - mu) * lax.rsqrt(var + eps)
    return (y * g.astype(jnp.float32) + b.astype(jnp.float32)).astype(x.dtype)


def causal_depthwise_conv(x, w, b):
    k = w.shape[0]
    y = lax.conv_general_dilated(
        x, w[:, None, :].astype(x.dtype), window_strides=(1,), padding=[(k - 1, 0)],
        dimension_numbers=('NWC', 'WIO', 'NWC'), feature_group_count=x.shape[-1])
    return y + b.astype(x.dtype)


def apply_rope(x, pos):
    half = x.shape[-1] // 2
    inv_freq = ROPE_THETA ** (-jnp.arange(half, dtype=jnp.float32) / half)
    ang = pos.astype(jnp.float32)[:, None] * inv_freq[None, :]
    cos = jnp.cos(ang)[None, :, None, :]
    sin = jnp.sin(ang)[None, :, None, :]
    xf = x.astype(jnp.float32)
    x1, x2 = xf[..., :half], xf[..., half:]
    return jnp.concatenate([x1 * cos - x2 * sin, x2 * cos + x1 * sin], axis=-1).astype(x.dtype)


def blocked_causal_attention(q, k, v):
    bsz, s, h, dqk = q.shape
    dv = v.shape[-1]
    nb = s // Q_BLOCK
    scale = 1.0 / math.sqrt(dqk)
    qb = q.reshape(bsz, nb, Q_BLOCK, h, dqk).transpose(1, 0, 2, 3, 4)
    kpos = jnp.arange(s)

    def one_block(args):
        qi, i = args
        sc = jnp.einsum('bqhd,bkhd->bhqk', qi, k, preferred_element_type=jnp.float32) * scale
        qpos = i * Q_BLOCK + jnp.arange(Q_BLOCK)
        mask = kpos[None, :] <= qpos[:, None]
        sc = jnp.where(mask[None, None], sc, -jnp.inf)
        p = jax.nn.softmax(sc, axis=-1).astype(v.dtype)
        return jnp.einsum('bhqk,bkhd->bqhd', p, v)

    o = lax.map(one_block, (qb, jnp.arange(nb)))
    return o.transpose(1, 0, 2, 3, 4).reshape(bsz, s, h * dv)


def mla_mixer(c_q, c_kv, k_pe, q_a_norm, w_q_b, kv_a_norm, w_kv_b, q_norm, k_norm):
    bsz, s, _ = c_q.shape
    q = (rms_norm(c_q, q_a_norm) @ w_q_b).reshape(bsz, s, MLA_HEADS, QK_HEAD)
    kv = (rms_norm(c_kv, kv_a_norm) @ w_kv_b).reshape(bsz, s, MLA_HEADS, QK_NOPE + V_HEAD)
    k_nope, v = kv[..., :QK_NOPE], kv[..., QK_NOPE:]
    k_pe_h = jnp.broadcast_to(k_pe[:, :, None, :], (bsz, s, MLA_HEADS, QK_ROPE))
    k = jnp.concatenate([k_nope, k_pe_h], axis=-1)
    q = rms_norm(q, q_norm)
    k = rms_norm(k, k_norm)
    pos = jnp.arange(s)
    q = jnp.concatenate([q[..., :QK_NOPE], apply_rope(q[..., QK_NOPE:], pos)], axis=-1)
    k = jnp.concatenate([k[..., :QK_NOPE], apply_rope(k[..., QK_NOPE:], pos)], axis=-1)
    return blocked_causal_attention(q, k, v)


def conformer_conv_mixer(u, w_dw, b_dw, ln_g, ln_b, w_pw):
    a, g = u[..., :CONV_WIDTH], u[..., CONV_WIDTH:]
    h = a * jax.nn.sigmoid(g)
    h = causal_depthwise_conv(h, w_dw, b_dw)
    h = jax.nn.silu(layer_norm(h, ln_g, ln_b))
    return h @ w_pw


def ssd_chunked(x, dt, a_neg, bm, cm):
    bsz, s, h, p = x.shape
    g, n = bm.shape[2], bm.shape[3]
    r = h // g
    l = SSD_CHUNK
    c = s // l
    xf = x.astype(jnp.float32).reshape(bsz, c, l, g, r, p)
    dtc = dt.reshape(bsz, c, l, g, r)
    bc = bm.astype(jnp.float32).reshape(bsz, c, l, g, n)
    cc = cm.astype(jnp.float32).reshape(bsz, c, l, g, n)
    xdt = xf * dtc[..., None]
    a_cs = jnp.cumsum(dtc * a_neg.reshape(g, r), axis=2)
    seg = a_cs[:, :, :, None] - a_cs[:, :, None, :]
    causal = jnp.tril(jnp.ones((l, l), dtype=bool))
    decay = jnp.exp(jnp.where(causal[None, None, :, :, None, None], seg, -jnp.inf))
    cb = jnp.einsum('bclgn,bcsgn->bclsg', cc, bc)
    y_diag = jnp.einsum('bclsg,bclsgr,bcsgrp->bclgrp', cb, decay, xdt)
    decay_to_end = jnp.exp(a_cs[:, :, -1:] - a_cs)
    states = jnp.einsum('bclgn,bclgr,bclgrp->bcgrpn', bc, decay_to_end, xdt)
    chunk_decay = jnp.exp(a_cs[:, :, -1])

    def step(hstate, inp):
        st, dec = inp
        return hstate * dec[..., None, None] + st, hstate

    h0 = jnp.zeros((bsz, g, r, p, n), jnp.float32)
    _, prev = lax.scan(step, h0, (states.transpose(1, 0, 2, 3, 4, 5), chunk_decay.transpose(1, 0, 2, 3)))
    prev = prev.transpose(1, 0, 2, 3, 4, 5)
    y_off = jnp.einsum('bclgn,bcgrpn,bclgr->bclgrp', cc, prev, jnp.exp(a_cs))
    return (y_diag + y_off).reshape(bsz, s, h, p)


def ssd_mixer(z, xbc, dt_raw, conv_w, conv_b, dt_bias, a_log, d_skip, norm_g):
    bsz, s, _ = z.shape
    xbc = jax.nn.silu(causal_depthwise_conv(xbc, conv_w, conv_b))
    xs = xbc[..., :SSD_WIDTH].reshape(bsz, s, SSD_HEADS, SSD_HEAD_DIM)
    bm = xbc[..., SSD_WIDTH:SSD_WIDTH + SSD_GROUPS * SSD_STATE].reshape(bsz, s, SSD_GROUPS, SSD_STATE)
    cm = xbc[..., SSD_WIDTH + SSD_GROUPS * SSD_STATE:].reshape(bsz, s, SSD_GROUPS, SSD_STATE)
    dt = jax.nn.softplus(dt_raw.astype(jnp.float32) + dt_bias.astype(jnp.float32))
    a_neg = -jnp.exp(a_log.astype(jnp.float32))
    y = ssd_chunked(xs, dt, a_neg, bm, cm)
    y = y + d_skip.astype(jnp.float32)[:, None] * xs.astype(jnp.float32)
    y = y.reshape(bsz, s, SSD_WIDTH).astype(z.dtype)
    return rms_norm(y * jax.nn.silu(z), norm_g)


def hybrid_layer(x, norm_g, w_in, q_a_norm, w_q_b, kv_a_norm, w_kv_b, q_norm, k_norm,
                 conv_dw_w, conv_dw_b, conv_ln_g, conv_ln_b, conv_pw_w,
                 ssd_conv_w, ssd_conv_b, ssd_dt_bias, ssd_A_log, ssd_D, ssd_norm_g, w_out):
    h = rms_norm(x, norm_g)
    u = h @ w_in
    offs = np.cumsum(IN_SPLITS)[:-1].tolist()
    c_q, c_kv, k_pe, g_mla, u_conv, g_conv, z, xbc, dt_raw = jnp.split(u, offs, axis=-1)
    o_mla = mla_mixer(c_q, c_kv, k_pe, q_a_norm, w_q_b, kv_a_norm, w_kv_b, q_norm, k_norm) * jax.nn.silu(g_mla)
    o_conv = conformer_conv_mixer(u_conv, conv_dw_w, conv_dw_b, conv_ln_g, conv_ln_b, conv_pw_w) * jax.nn.silu(g_conv)
    o_ssd = ssd_mixer(z, xbc, dt_raw, ssd_conv_w, ssd_conv_b, ssd_dt_bias, ssd_A_log, ssd_D, ssd_norm_g)
    y = jnp.concatenate([o_mla, o_conv, o_ssd], axis=-1) @ w_out
    return x + y


def setup_inputs(seed: int = 0) -> dict:
    key = jax.random.key(seed)
    ks = jax.random.split(key, 24)
    f32 = jnp.float32

    def nrm(k, shape, fan_in):
        return jax.random.normal(k, shape, f32) * (fan_in ** -0.5)

    def gain(k, shape):
        return 1.0 + 0.02 * jax.random.normal(k, shape, f32)

    def small(k, shape):
        return 0.02 * jax.random.normal(k, shape, f32)

    dt0 = jnp.exp(jax.random.uniform(ks[17], (DEPTH, SSD_HEADS), f32, math.log(1e-3), math.log(1e-1)))
    return {
        'x': jax.random.normal(ks[0], (BATCH, SEQ, D_MODEL), f32),
        'norm_g': gain(ks[1], (DEPTH, D_MODEL)),
        'w_in': nrm(ks[2], (DEPTH, D_MODEL, D_IN_PROJ), D_MODEL),
        'q_a_norm': gain(ks[3], (DEPTH, Q_LORA)),
        'w_q_b': nrm(ks[4], (DEPTH, Q_LORA, MLA_HEADS * QK_HEAD), Q_LORA),
        'kv_a_norm': gain(ks[5], (DEPTH, KV_LORA)),
        'w_kv_b': nrm(ks[6], (DEPTH, KV_LORA, MLA_HEADS * (QK_NOPE + V_HEAD)), KV_LORA),
        'q_norm': gain(ks[7], (DEPTH, QK_HEAD)),
        'k_norm': gain(ks[8], (DEPTH, QK_HEAD)),
        'conv_dw_w': nrm(ks[9], (DEPTH, CONV_K, CONV_WIDTH), CONV_K),
        'conv_dw_b': small(ks[10], (DEPTH, CONV_WIDTH)),
        'conv_ln_g': gain(ks[11], (DEPTH, CONV_WIDTH)),
        'conv_ln_b': small(ks[12], (DEPTH, CONV_WIDTH)),
        'conv_pw_w': nrm(ks[13], (DEPTH, CONV_WIDTH, CONV_WIDTH), CONV_WIDTH),
        'ssd_conv_w': nrm(ks[14], (DEPTH, SSD_CONV_K, SSD_XBC), SSD_CONV_K),
        'ssd_conv_b': small(ks[15], (DEPTH, SSD_XBC)),
        'ssd_dt_bias': dt0 + jnp.log(-jnp.expm1(-dt0)),
        'ssd_A_log': jnp.log(jax.random.uniform(ks[18], (DEPTH, SSD_HEADS), f32, 1.0, 16.0)),
        'ssd_D': gain(ks[19], (DEPTH, SSD_HEADS)),
        'ssd_norm_g': gain(ks[20], (DEPTH, SSD_WIDTH)),
        'w_out': nrm(ks[21], (DEPTH, D_MIX, D_MODEL), D_MIX),
    }


def reference(x, norm_g, w_in, q_a_norm, w_q_b, kv_a_norm, w_kv_b, q_norm, k_norm,
              conv_dw_w, conv_dw_b, conv_ln_g, conv_ln_b, conv_pw_w,
              ssd_conv_w, ssd_conv_b, ssd_dt_bias, ssd_A_log, ssd_D, ssd_norm_g, w_out):
    for i in range(DEPTH):
        x = hybrid_layer(x, norm_g[i], w_in[i], q_a_norm[i], w_q_b[i], kv_a_norm[i], w_kv_b[i],
                         q_norm[i], k_norm[i], conv_dw_w[i], conv_dw_b[i], conv_ln_g[i],
                         conv_ln_b[i], conv_pw_w[i], ssd_conv_w[i], ssd_conv_b[i],
                         ssd_dt_bias[i], ssd_A_log[i], ssd_D[i], ssd_norm_g[i], w_out[i])
    return x
```

```python
import functools
import math

import jax
import jax.numpy as jnp
from jax import lax
from jax.experimental import pallas as pl
from jax.experimental.pallas import tpu as pltpu

F32 = jnp.float32
BF16 = jnp.bfloat16

MLA_HEADS = 6
QK_NOPE = 128
QK_ROPE = 64
QK_HEAD = QK_NOPE + QK_ROPE
V_HEAD = 128
Q_LORA = 512
KV_LORA = 256
MLA_WIDTH = MLA_HEADS * V_HEAD
ROPE_THETA = 10000.0
CONV_WIDTH = 512
CONV_K = 31
SSD_HEADS = 12
SSD_HEAD_DIM = 64
SSD_WIDTH = SSD_HEADS * SSD_HEAD_DIM
SSD_GROUPS = 2
SSD_STATE = 128
SSD_CONV_K = 4
SSD_CHUNK = 128
SSD_XBC = SSD_WIDTH + 2 * SSD_GROUPS * SSD_STATE
NORM_EPS = 1e-6
LN_EPS = 1e-5

LANES = 128
SUBLANES = 8
QK_PAD = 2 * LANES
KD_WIDTH = LANES
DT_LANE0 = QK_ROPE
CONV_HALO = 32
SSD_HALO = SUBLANES
VMEM_LIMIT = 56 * 1024 * 1024

SEG_WIDTHS = (Q_LORA, KV_LORA, KD_WIDTH, MLA_WIDTH, 2 * CONV_WIDTH, CONV_WIDTH, SSD_WIDTH, SSD_XBC)


def _sigmoid(x):
    return 1.0 / (1.0 + jnp.exp(-x))


def _silu(x):
    return x * _sigmoid(x)


def _softplus(x):
    return jnp.maximum(x, 0.0) + jnp.log1p(jnp.exp(-jnp.abs(x)))


def _rms(x, g, eps=NORM_EPS):
    ms = jnp.mean(x * x, axis=-1, keepdims=True)
    return x * lax.rsqrt(ms + eps) * g


def _params(*sem):
    return pltpu.CompilerParams(dimension_semantics=sem, vmem_limit_bytes=VMEM_LIMIT)


def _resident(shape):
    nd = len(shape)
    return pl.BlockSpec(shape, lambda *_: (0,) * nd, pipeline_mode=pl.Buffered(1))


def _in_proj_kernel(x_ref, g_ref, w_ref, *out_refs):
    h = _rms(x_ref[...], g_ref[...]).astype(BF16)
    off = 0
    for o_ref in out_refs:
        n = o_ref.shape[-1]
        o_ref[...] = jnp.dot(h, w_ref[:, off:off + n], preferred_element_type=F32).astype(o_ref.dtype)
        off += n


def _in_proj(x2, g, w, tm):
    m, d = x2.shape
    n_total = w.shape[1]
    assert n_total == sum(SEG_WIDTHS)
    return pl.pallas_call(
        _in_proj_kernel,
        grid=(m // tm,),
        in_specs=[pl.BlockSpec((tm, d), lambda i: (i, 0)),
                  _resident((1, d)),
                  _resident((d, n_total))],
        out_specs=[pl.BlockSpec((tm, n), lambda i: (i, 0)) for n in SEG_WIDTHS],
        out_shape=[jax.ShapeDtypeStruct((m, n), F32) for n in SEG_WIDTHS],
        compiler_params=_params("parallel"),
        name="in_proj",
    )(x2, g, w)


def _mla_prep_kernel(cq_ref, ckv_ref, kd_ref, qan_ref, wq_ref, kvan_ref, wkv_ref, qn_ref, kn_ref,
                     cos_ref, sa_ref, sb_ref, q_ref, k_ref, v_ref):
    scale = 1.0 / math.sqrt(QK_HEAD)
    hq = _rms(cq_ref[...], qan_ref[...]).astype(BF16)
    qf = jnp.dot(hq, wq_ref[...], preferred_element_type=F32)
    hkv = _rms(ckv_ref[...], kvan_ref[...]).astype(BF16)
    kvf = jnp.dot(hkv, wkv_ref[...], preferred_element_type=F32)

    cos = cos_ref[...]
    sa = sa_ref[...]
    sb = sb_ref[...]

    def rope(r):
        return r * cos + pltpu.roll(r, LANES - QK_ROPE // 2, 1) * sa + pltpu.roll(r, QK_ROPE // 2, 1) * sb

    kd = kd_ref[...]
    lane = lax.broadcasted_iota(jnp.int32, kd.shape, 1)
    kpe = jnp.where(lane < QK_ROPE, kd, 0.0)
    kpe_ss = jnp.sum(kpe * kpe, axis=-1, keepdims=True)

    qn_w = qn_ref[...]
    kn_w = kn_ref[...]
    for h in range(MLA_HEADS):
        qa = qf[:, h * QK_PAD:h * QK_PAD + LANES]
        qb = qf[:, h * QK_PAD + LANES:(h + 1) * QK_PAD]
        ss = jnp.sum(qa * qa, axis=-1, keepdims=True) + jnp.sum(qb * qb, axis=-1, keepdims=True)
        inv = lax.rsqrt(ss * (1.0 / QK_HEAD) + NORM_EPS)
        q_ref[0, h, :, 0:LANES] = (qa * inv * qn_w[:, 0:LANES] * scale).astype(q_ref.dtype)
        q_ref[0, h, :, LANES:QK_PAD] = (rope(qb * inv * qn_w[:, LANES:QK_PAD]) * scale).astype(q_ref.dtype)

        ka = kvf[:, h * LANES:(h + 1) * LANES]
        ss = jnp.sum(ka * ka, axis=-1, keepdims=True) + kpe_ss
        inv = lax.rsqrt(ss * (1.0 / QK_HEAD) + NORM_EPS)
        k_ref[0, h, :, 0:LANES] = (ka * inv * kn_w[:, 0:LANES]).astype(k_ref.dtype)
        k_ref[0, h, :, LANES:QK_PAD] = rope(kpe * inv * kn_w[:, LANES:QK_PAD]).astype(k_ref.dtype)

        v_ref[0, h, :, :] = kvf[:, MLA_WIDTH + h * V_HEAD:MLA_WIDTH + (h + 1) * V_HEAD].astype(v_ref.dtype)


def _mla_prep(cq, ckv, kd, qan, wq, kvan, wkv, qn, kn, cos_t, sa_t, sb_t, bsz, seq, tm):
    nt = seq // tm
    row = lambda b, i: (b * nt + i, 0)
    pos = lambda b, i: (i, 0)
    hd = MLA_HEADS
    return pl.pallas_call(
        _mla_prep_kernel,
        grid=(bsz, nt),
        in_specs=[pl.BlockSpec((tm, Q_LORA), row),
                  pl.BlockSpec((tm, KV_LORA), row),
                  pl.BlockSpec((tm, KD_WIDTH), row),
                  _resident(qan.shape), _resident(wq.shape), _resident(kvan.shape), _resident(wkv.shape),
                  _resident(qn.shape), _resident(kn.shape),
                  pl.BlockSpec((tm, LANES), pos), pl.BlockSpec((tm, LANES), pos), pl.BlockSpec((tm, LANES), pos)],
        out_specs=[pl.BlockSpec((1, hd, tm, QK_PAD), lambda b, i: (b, 0, i, 0)),
                   pl.BlockSpec((1, hd, tm, QK_PAD), lambda b, i: (b, 0, i, 0)),
                   pl.BlockSpec((1, hd, tm, V_HEAD), lambda b, i: (b, 0, i, 0))],
        out_shape=[jax.ShapeDtypeStruct((bsz, hd, seq, QK_PAD), BF16),
                   jax.ShapeDtypeStruct((bsz, hd, seq, QK_PAD), BF16),
                   jax.ShapeDtypeStruct((bsz, hd, seq, V_HEAD), BF16)],
        compiler_params=_params("parallel", "parallel"),
        name="mla_prep",
    )(cq, ckv, kd, qan, wq, kvan, wkv, qn, kn, cos_t, sa_t, sb_t)


def _attn_kernel(q_ref, k_ref, v_ref, g_ref, o_ref, m_sc, l_sc, acc_sc):
    qi = pl.program_id(1)
    ki = pl.program_id(2)
    tq = q_ref.shape[2]
    tk = k_ref.shape[2]

    @pl.when(ki == 0)
    def _():
        m_sc[...] = jnp.full(m_sc.shape, -jnp.inf, F32)
        l_sc[...] = jnp.zeros(l_sc.shape, F32)
        acc_sc[...] = jnp.zeros(acc_sc.shape, F32)

    def step(masked):
        for h in range(MLA_HEADS):
            s = lax.dot_general(q_ref[0, h], k_ref[0, h], (((1,), (1,)), ((), ())),
                                preferred_element_type=F32)
            if masked:
                r = lax.broadcasted_iota(jnp.int32, (tq, tk), 0)
                c = lax.broadcasted_iota(jnp.int32, (tq, tk), 1)
                s = jnp.where(r >= c, s, -jnp.inf)
            m_prev = m_sc[h]
            m_new = jnp.maximum(m_prev, jnp.max(s, axis=-1, keepdims=True))
            alpha = jnp.exp(m_prev - m_new)
            p = jnp.exp(s - m_new)
            l_sc[h] = alpha * l_sc[h] + jnp.sum(p, axis=-1, keepdims=True)
            acc_sc[h] = alpha * acc_sc[h] + jnp.dot(p.astype(BF16), v_ref[0, h], preferred_element_type=F32)
            m_sc[h] = m_new

    @pl.when(ki < qi)
    def _():
        step(False)

    @pl.when(ki == qi)
    def _():
        step(True)
        for h in range(MLA_HEADS):
            o = acc_sc[h] / l_sc[h]
            g = g_ref[:, h * V_HEAD:(h + 1) * V_HEAD]
            o_ref[:, h * V_HEAD:(h + 1) * V_HEAD] = (o * _silu(g)).astype(o_ref.dtype)


def _attention(q, k, v, gmla, bsz, seq, tq):
    hd = MLA_HEADS
    nq = seq // tq
    kv_idx = lambda b, i, j: (b, 0, jnp.minimum(i, j), 0)
    row = lambda b, i, j: (b * nq + i, 0)
    return pl.pallas_call(
        _attn_kernel,
        grid=(bsz, nq, nq),
        in_specs=[pl.BlockSpec((1, hd, tq, QK_PAD), lambda b, i, j: (b, 0, i, 0)),
                  pl.BlockSpec((1, hd, tq, QK_PAD), kv_idx),
                  pl.BlockSpec((1, hd, tq, V_HEAD), kv_idx),
                  pl.BlockSpec((tq, MLA_WIDTH), row)],
        out_specs=pl.BlockSpec((tq, MLA_WIDTH), row),
        out_shape=jax.ShapeDtypeStruct((bsz * seq, MLA_WIDTH), BF16),
        scratch_shapes=[pltpu.VMEM((hd, tq, 1), F32), pltpu.VMEM((hd, tq, 1), F32),
                        pltpu.VMEM((hd, tq, V_HEAD), F32)],
        compiler_params=_params("parallel", "parallel", "arbitrary"),
        name="attention",
    )(q, k, v, gmla)


def _conv_kernel(u_ref, gc_ref, wdw_ref, bdw_ref, lng_ref, lnb_ref, wpw_ref, o_ref, hbuf, ybuf, *, strip):
    i = pl.program_id(1)
    tt = o_ref.shape[0]
    cw = CONV_WIDTH

    @pl.when(i == 0)
    def _():
        hbuf[0:CONV_HALO, :] = jnp.zeros((CONV_HALO, cw), F32)

    @pl.when(i > 0)
    def _():
        hbuf[0:CONV_HALO, :] = hbuf[tt:tt + CONV_HALO, :]

    hbuf[CONV_HALO:CONV_HALO + tt, :] = u_ref[:, 0:cw] * _sigmoid(u_ref[:, cw:2 * cw])

    base = CONV_HALO - (CONV_K - 1)

    def do_strip(s, carry):
        r0 = pl.multiple_of(s * strip, strip)
        win = hbuf[pl.ds(r0, strip + CONV_HALO), :]
        acc = jnp.broadcast_to(bdw_ref[...], (strip, cw))
        for res in range(SUBLANES):
            offs = [o for o in range(base, base + CONV_K) if o % SUBLANES == res]
            shifted = win[res:res + (offs[-1] - res) + strip, :]
            for o in offs:
                kk = o - base
                acc = acc + wdw_ref[kk:kk + 1, :] * shifted[o - res:o - res + strip, :]
        ybuf[pl.ds(r0, strip), :] = acc
        return carry

    lax.fori_loop(0, tt // strip, do_strip, 0)

    y = ybuf[...]
    mu = jnp.mean(y, axis=-1, keepdims=True)
    yc = y - mu
    var = jnp.mean(yc * yc, axis=-1, keepdims=True)
    hn = _silu(yc * lax.rsqrt(var + LN_EPS) * lng_ref[...] + lnb_ref[...])
    out = jnp.dot(hn.astype(BF16), wpw_ref[...], preferred_element_type=F32)
    o_ref[...] = (out * _silu(gc_ref[...])).astype(o_ref.dtype)


def _conv(uconv, gconv, wdw, bdw, lng, lnb, wpw, bsz, seq, tt):
    nt = seq // tt
    row = lambda b, i: (b * nt + i, 0)
    strip = min(32, tt)
    return pl.pallas_call(
        functools.partial(_conv_kernel, strip=strip),
        grid=(bsz, nt),
        in_specs=[pl.BlockSpec((tt, 2 * CONV_WIDTH), row),
                  pl.BlockSpec((tt, CONV_WIDTH), row),
                  _resident(wdw.shape), _resident(bdw.shape), _resident(lng.shape), _resident(lnb.shape),
                  _resident(wpw.shape)],
        out_specs=pl.BlockSpec((tt, CONV_WIDTH), row),
        out_shape=jax.ShapeDtypeStruct((bsz * seq, CONV_WIDTH), BF16),
        scratch_shapes=[pltpu.VMEM((tt + CONV_HALO, CONV_WIDTH), F32), pltpu.VMEM((tt, CONV_WIDTH), F32)],
        compiler_params=_params("parallel", "arbitrary"),
        name="conformer_conv",
    )(uconv, gconv, wdw, bdw, lng, lnb, wpw)


def _split3(x):
    hi = x.astype(BF16)
    r1 = x - hi.astype(F32)
    mid = r1.astype(BF16)
    lo = (r1 - mid.astype(F32)).astype(BF16)
    return hi, mid, lo


def _ssd_kernel(z_ref, xbc_ref, kd_ref, cw_ref, cb_ref, dtb_ref, alog_ref, dsk_ref, ng_ref, o_ref,
                xbuf, xact, state):
    i = pl.program_id(1)
    ts = o_ref.shape[0]
    L = SSD_CHUNK
    hp = SSD_HEAD_DIM
    gw = SSD_WIDTH // SSD_GROUPS
    heads_per_group = SSD_HEADS // SSD_GROUPS

    @pl.when(i == 0)
    def _():
        xbuf[0:SSD_HALO, :] = jnp.zeros((SSD_HALO, SSD_XBC), F32)
        state[...] = jnp.zeros(state.shape, F32)

    @pl.when(i > 0)
    def _():
        xbuf[0:SSD_HALO, :] = xbuf[ts:ts + SSD_HALO, :]

    xbuf[SSD_HALO:SSD_HALO + ts, :] = xbc_ref[...]
    base = SSD_HALO - (SSD_CONV_K - 1)
    acc = jnp.broadcast_to(cb_ref[...], (ts, SSD_XBC))
    for kk in range(SSD_CONV_K):
        acc = acc + cw_ref[kk:kk + 1, :] * xbuf[base + kk:base + kk + ts, :]
    xact[...] = _silu(acc)

    lane = lax.broadcasted_iota(jnp.int32, (1, LANES), 1)
    is_dt = (lane >= DT_LANE0) & (lane < DT_LANE0 + SSD_HEADS)
    a_neg = jnp.where(is_dt, -jnp.exp(alog_ref[...]), 0.0)
    rr = lax.broadcasted_iota(jnp.int32, (L, L), 0)
    cc = lax.broadcasted_iota(jnp.int32, (L, L), 1)
    causal = rr >= cc
    tri = causal.astype(BF16)
    left = lax.broadcasted_iota(jnp.int32, (L, LANES), 1) < hp

    def chunk(c, carry):
        r0 = pl.multiple_of(c * L, L)
        xs = xact[pl.ds(r0, L), 0:SSD_WIDTH]
        dt = _softplus(kd_ref[pl.ds(r0, L), :] + dtb_ref[...])
        dta = dt * a_neg
        hi, mid, lo = _split3(dta)
        acs = (jnp.dot(tri, hi, preferred_element_type=F32)
               + jnp.dot(tri, mid, preferred_element_type=F32)
               + jnp.dot(tri, lo, preferred_element_type=F32))
        acs_t = acs.T

        def col(a, h):
            return jnp.broadcast_to(a[:, DT_LANE0 + h:DT_LANE0 + h + 1], (L, LANES))

        dt_cols = [col(dt, h) for h in range(SSD_HEADS)]
        acs_cols = [col(acs, h) for h in range(SSD_HEADS)]
        npair = SSD_HEADS // 2
        dt_exp = jnp.concatenate([jnp.where(left, dt_cols[2 * j], dt_cols[2 * j + 1]) for j in range(npair)], axis=1)
        acs_exp = jnp.concatenate([jnp.where(left, acs_cols[2 * j], acs_cols[2 * j + 1]) for j in range(npair)], axis=1)
        last = acs_exp[L - 1:L, :]
        xdt = xs * dt_exp
        xdt_b = xdt.astype(BF16)
        xdt_end = (xdt * jnp.exp(last - acs_exp)).astype(BF16)
        grow = jnp.exp(acs_exp)
        st_prev = state[...]
        st_prev_b = st_prev.astype(BF16)

        y_parts = []
        st_parts = []
        for g in range(SSD_GROUPS):
            bg = xact[pl.ds(r0, L), SSD_WIDTH + g * SSD_STATE:SSD_WIDTH + (g + 1) * SSD_STATE]
            cg = xact[pl.ds(r0, L), SSD_WIDTH + (SSD_GROUPS + g) * SSD_STATE:SSD_WIDTH + (SSD_GROUPS + g + 1) * SSD_STATE]
            bg_b = bg.astype(BF16)
            cg_b = cg.astype(BF16)
            cb = lax.dot_general(cg_b, bg_b, (((1,), (1,)), ((), ())), preferred_element_type=F32)
            st_parts.append(jnp.dot(bg.T.astype(BF16), xdt_end[:, g * gw:(g + 1) * gw],
                                    preferred_element_type=F32))
            y_off = jnp.dot(cg_b, st_prev_b[:, g * gw:(g + 1) * gw], preferred_element_type=F32)
            y_off = y_off * grow[:, g * gw:(g + 1) * gw]
            for jj in range(heads_per_group // 2):
                j = g * (heads_per_group // 2) + jj
                pair = xdt_b[:, j * LANES:(j + 1) * LANES]
                yd = []
                for h in (2 * j, 2 * j + 1):
                    seg = acs_cols[h] - acs_t[DT_LANE0 + h:DT_LANE0 + h + 1, :]
                    decay = jnp.exp(jnp.where(causal, seg, -jnp.inf))
                    yd.append(jnp.dot((cb * decay).astype(BF16), pair, preferred_element_type=F32))
                y_parts.append(jnp.where(left, yd[0], yd[1]) + y_off[:, jj * LANES:(jj + 1) * LANES])

        y = jnp.concatenate(y_parts, axis=1) + dsk_ref[...] * xs
        state[...] = st_prev * jnp.exp(last) + jnp.concatenate(st_parts, axis=1)
        yz = y * _silu(z_ref[pl.ds(r0, L), :])
        o_ref[pl.ds(r0, L), :] = _rms(yz, ng_ref[...]).astype(o_ref.dtype)
        return carry

    lax.fori_loop(0, ts // L, chunk, 0)


def _ssd(z, xbc, kd, cw, cb, dtb, alog, dsk, ng, bsz, seq, ts):
    nt = seq // ts
    row = lambda b, i: (b * nt + i, 0)
    return pl.pallas_call(
        _ssd_kernel,
        grid=(bsz, nt),
        in_specs=[pl.BlockSpec((ts, SSD_WIDTH), row),
                  pl.BlockSpec((ts, SSD_XBC), row),
                  pl.BlockSpec((ts, KD_WIDTH), row),
                  _resident(cw.shape), _resident(cb.shape), _resident(dtb.shape), _resident(alog.shape),
                  _resident(dsk.shape), _resident(ng.shape)],
        out_specs=pl.BlockSpec((ts, SSD_WIDTH), row),
        out_shape=jax.ShapeDtypeStruct((bsz * seq, SSD_WIDTH), BF16),
        scratch_shapes=[pltpu.VMEM((ts + SSD_HALO, SSD_XBC), F32), pltpu.VMEM((ts, SSD_XBC), F32),
                        pltpu.VMEM((SSD_STATE, SSD_WIDTH), F32)],
        compiler_params=_params("parallel", "arbitrary"),
        name="ssd",
    )(z, xbc, kd, cw, cb, dtb, alog, dsk, ng)


def _out_proj_kernel(x_ref, a_ref, c_ref, s_ref, w_ref, o_ref):
    n0 = a_ref.shape[1]
    n1 = n0 + c_ref.shape[1]
    n2 = n1 + s_ref.shape[1]
    y = jnp.dot(a_ref[...], w_ref[0:n0, :], preferred_element_type=F32)
    y = y + jnp.dot(c_ref[...], w_ref[n0:n1, :], preferred_element_type=F32)
    y = y + jnp.dot(s_ref[...], w_ref[n1:n2, :], preferred_element_type=F32)
    o_ref[...] = x_ref[...] + y


def _out_proj(x2, o_mla, o_conv, o_ssd, w, tm):
    m, d = x2.shape
    row = lambda i: (i, 0)
    return pl.pallas_call(
        _out_proj_kernel,
        grid=(m // tm,),
        in_specs=[pl.BlockSpec((tm, d), row),
                  pl.BlockSpec((tm, o_mla.shape[1]), row),
                  pl.BlockSpec((tm, o_conv.shape[1]), row),
                  pl.BlockSpec((tm, o_ssd.shape[1]), row),
                  _resident(w.shape)],
        out_specs=pl.BlockSpec((tm, d), row),
        out_shape=jax.ShapeDtypeStruct((m, d), F32),
        compiler_params=_params("parallel"),
        name="out_proj",
    )(x2, o_mla, o_conv, o_ssd, w)


def _pad_lanes(v, lane0, width=LANES):
    return jnp.zeros((1, width), F32).at[0, lane0:lane0 + v.shape[0]].set(v.astype(F32))


def _rope_tables(seq):
    half = QK_ROPE // 2
    inv_freq = ROPE_THETA ** (-jnp.arange(half, dtype=F32) / half)
    ang = jnp.arange(seq).astype(F32)[:, None] * inv_freq[None, :]
    cos, sin = jnp.cos(ang), jnp.sin(ang)
    z = lambda n: jnp.zeros((seq, n), F32)
    cos_t = jnp.concatenate([cos, cos, z(LANES - QK_ROPE)], axis=1)
    sa_t = jnp.concatenate([-sin, z(LANES - half)], axis=1)
    sb_t = jnp.concatenate([z(half), sin, z(LANES - QK_ROPE)], axis=1)
    return cos_t, sa_t, sb_t


def _layer(x2, bsz, seq, tabs, norm_g, w_in, q_a_norm, w_q_b, kv_a_norm, w_kv_b, q_norm, k_norm,
           conv_dw_w, conv_dw_b, conv_ln_g, conv_ln_b, conv_pw_w,
           ssd_conv_w, ssd_conv_b, ssd_dt_bias, ssd_A_log, ssd_D, ssd_norm_g, w_out):
    d = x2.shape[1]
    o_kpe = Q_LORA + KV_LORA
    o_gmla = o_kpe + QK_ROPE
    o_dt = w_in.shape[1] - SSD_HEADS
    w_kd = jnp.concatenate([w_in[:, o_kpe:o_gmla], w_in[:, o_dt:],
                            jnp.zeros((d, KD_WIDTH - QK_ROPE - SSD_HEADS), w_in.dtype)], axis=1)
    w_in_k = jnp.concatenate([w_in[:, :o_kpe], w_kd, w_in[:, o_gmla:o_dt]], axis=1).astype(BF16)

    row = lambda v: v.reshape(1, -1).astype(F32)
    wq = jnp.pad(w_q_b.reshape(Q_LORA, MLA_HEADS, QK_HEAD), ((0, 0), (0, 0), (0, QK_PAD - QK_HEAD)))
    wq = wq.reshape(Q_LORA, MLA_HEADS * QK_PAD).astype(BF16)
    wkv3 = w_kv_b.reshape(KV_LORA, MLA_HEADS, QK_NOPE + V_HEAD)
    wkv = jnp.concatenate([wkv3[:, :, :QK_NOPE].reshape(KV_LORA, -1), wkv3[:, :, QK_NOPE:].reshape(KV_LORA, -1)],
                          axis=1).astype(BF16)
    qn = _pad_lanes(q_norm, 0, QK_PAD)
    kn = _pad_lanes(k_norm, 0, QK_PAD)

    tm_in = min(256, bsz * seq)
    cq, ckv, kd, gmla, uconv, gconv, z, xbc = _in_proj(x2, row(norm_g), w_in_k, tm_in)

    t_seq = min(512, seq)
    q, k, v = _mla_prep(cq, ckv, kd, row(q_a_norm), wq, row(kv_a_norm), wkv, qn, kn, *tabs, bsz, seq, t_seq)
    o_mla = _attention(q, k, v, gmla, bsz, seq, t_seq)

    wdw = jnp.pad(conv_dw_w.astype(F32), ((0, CONV_HALO - CONV_K), (0, 0)))
    o_conv = _conv(uconv, gconv, wdw, row(conv_dw_b), row(conv_ln_g), row(conv_ln_b),
                   conv_pw_w.astype(BF16), bsz, seq, t_seq)

    cw = jnp.pad(ssd_conv_w.astype(F32), ((0, SUBLANES - SSD_CONV_K), (0, 0)))
    o_ssd = _ssd(z, xbc, kd, cw, row(ssd_conv_b), _pad_lanes(ssd_dt_bias, DT_LANE0), _pad_lanes(ssd_A_log, DT_LANE0),
                 row(jnp.repeat(ssd_D, SSD_HEAD_DIM)), row(ssd_norm_g), bsz, seq, t_seq)

    return _out_proj(x2, o_mla, o_conv, o_ssd, w_out.astype(BF16), min(512, bsz * seq))


def kernel(x, norm_g, w_in, q_a_norm, w_q_b, kv_a_norm, w_kv_b, q_norm, k_norm, conv_dw_w, conv_dw_b, conv_ln_g,
           conv_ln_b, conv_pw_w, ssd_conv_w, ssd_conv_b, ssd_dt_bias, ssd_A_log, ssd_D, ssd_norm_g, w_out):
    bsz, seq, d = x.shape
    x2 = x.reshape(bsz * seq, d)
    tabs = _rope_tables(seq)
    layer_params = (norm_g, w_in, q_a_norm, w_q_b, kv_a_norm, w_kv_b, q_norm, k_norm, conv_dw_w, conv_dw_b,
                    conv_ln_g, conv_ln_b, conv_pw_w, ssd_conv_w, ssd_conv_b, ssd_dt_bias, ssd_A_log, ssd_D,
                    ssd_norm_g, w_out)
    for i in range(norm_g.shape[0]):
        x2 = _layer(x2, bsz, seq, tabs, *(p[i] for p in layer_params))
    return x2.reshape(bsz, seq, d)
```

```python
import functools
import math

import jax
import jax.numpy as jnp
from jax import lax
from jax.experimental import pallas as pl
from jax.experimental.pallas import tpu as pltpu

F32 = jnp.float32
BF16 = jnp.bfloat16

MLA_HEADS = 6
QK_NOPE = 128
QK_ROPE = 64
QK_HEAD = QK_NOPE + QK_ROPE
V_HEAD = 128
Q_LORA = 512
KV_LORA = 256
MLA_WIDTH = MLA_HEADS * V_HEAD
ROPE_THETA = 10000.0
CONV_WIDTH = 512
CONV_K = 31
SSD_HEADS = 12
SSD_HEAD_DIM = 64
SSD_WIDTH = SSD_HEADS * SSD_HEAD_DIM
SSD_GROUPS = 2
SSD_STATE = 128
SSD_CONV_K = 4
SSD_CHUNK = 128
SSD_XBC = SSD_WIDTH + 2 * SSD_GROUPS * SSD_STATE
NORM_EPS = 1e-6
LN_EPS = 1e-5

LANES = 128
SUBLANES = 8
QK_PAD = 2 * LANES
KD_WIDTH = LANES
DT_LANE0 = QK_ROPE
CONV_HALO = 32
SSD_HALO = SUBLANES
VMEM_LIMIT = 56 * 1024 * 1024

SEG_WIDTHS = (Q_LORA, KV_LORA, KD_WIDTH, MLA_WIDTH, 2 * CONV_WIDTH, CONV_WIDTH, SSD_WIDTH, SSD_XBC)


def _sigmoid(x):
    return 1.0 / (1.0 + jnp.exp(-x))


def _silu(x):
    return x * _sigmoid(x)


def _softplus(x):
    return jnp.maximum(x, 0.0) + jnp.log1p(jnp.exp(-jnp.abs(x)))


def _rms(x, g, eps=NORM_EPS):
    ms = jnp.mean(x * x, axis=-1, keepdims=True)
    return x * lax.rsqrt(ms + eps) * g


def _params(*sem):
    return pltpu.CompilerParams(dimension_semantics=sem, vmem_limit_bytes=VMEM_LIMIT)


def _resident(stacked, layer):
    _, a, b = stacked.shape
    return pl.BlockSpec((None, a, b), lambda *_: (layer, 0, 0), pipeline_mode=pl.Buffered(1))


def _prep_w_in_kernel(w_ref, o_ref):
    w = w_ref[...]
    o_gmla = Q_LORA + KV_LORA + QK_ROPE
    o_dt = w.shape[1] - SSD_HEADS
    pad = jnp.zeros((w.shape[0], KD_WIDTH - QK_ROPE - SSD_HEADS), F32)
    o_ref[...] = jnp.concatenate([w[:, :o_gmla], w[:, o_dt:], pad, w[:, o_gmla:o_dt]], axis=1).astype(o_ref.dtype)


def _prep_w_in(w_in, tk):
    depth, d, n = w_in.shape
    n_out = sum(SEG_WIDTHS)
    return pl.pallas_call(
        _prep_w_in_kernel,
        grid=(depth, d // tk),
        in_specs=[pl.BlockSpec((None, tk, n), lambda l, i: (l, i, 0))],
        out_specs=pl.BlockSpec((None, tk, n_out), lambda l, i: (l, i, 0)),
        out_shape=jax.ShapeDtypeStruct((depth, d, n_out), BF16),
        compiler_params=_params("parallel", "parallel"),
        name="prep_w_in",
    )(w_in)


def _prep_small_kernel(wq_ref, wkv_ref, wpw_ref, oq_ref, okv_ref, opw_ref):
    wq = wq_ref[...]
    zq = jnp.zeros((wq.shape[0], QK_PAD - QK_HEAD), F32)
    parts = []
    for h in range(MLA_HEADS):
        parts += [wq[:, h * QK_HEAD:(h + 1) * QK_HEAD], zq]
    oq_ref[...] = jnp.concatenate(parts, axis=1).astype(oq_ref.dtype)
    wkv = wkv_ref[...]
    per = QK_NOPE + V_HEAD
    k_cols = [wkv[:, h * per:h * per + QK_NOPE] for h in range(MLA_HEADS)]
    v_cols = [wkv[:, h * per + QK_NOPE:(h + 1) * per] for h in range(MLA_HEADS)]
    okv_ref[...] = jnp.concatenate(k_cols + v_cols, axis=1).astype(okv_ref.dtype)
    opw_ref[...] = wpw_ref[...].astype(opw_ref.dtype)


def _prep_small(w_q_b, w_kv_b, conv_pw_w):
    depth = w_q_b.shape[0]
    whole = lambda a: pl.BlockSpec((None,) + a.shape[1:], lambda l: (l, 0, 0))
    shapes = [(depth, Q_LORA, MLA_HEADS * QK_PAD), w_kv_b.shape, conv_pw_w.shape]
    return pl.pallas_call(
        _prep_small_kernel,
        grid=(depth,),
        in_specs=[whole(w_q_b), whole(w_kv_b), whole(conv_pw_w)],
        out_specs=[pl.BlockSpec((None,) + s[1:], lambda l: (l, 0, 0)) for s in shapes],
        out_shape=[jax.ShapeDtypeStruct(s, BF16) for s in shapes],
        compiler_params=_params("parallel"),
        name="prep_small",
    )(w_q_b, w_kv_b, conv_pw_w)


def _cast_kernel(w_ref, o_ref):
    o_ref[...] = w_ref[...].astype(o_ref.dtype)


def _prep_cast(w, tk):
    depth, k, n = w.shape
    return pl.pallas_call(
        _cast_kernel,
        grid=(depth, k // tk),
        in_specs=[pl.BlockSpec((None, tk, n), lambda l, i: (l, i, 0))],
        out_specs=pl.BlockSpec((None, tk, n), lambda l, i: (l, i, 0)),
        out_shape=jax.ShapeDtypeStruct(w.shape, BF16),
        compiler_params=_params("parallel", "parallel"),
        name="prep_cast",
    )(w)


def _in_proj_kernel(x_ref, g_ref, w_ref, *out_refs):
    h = _rms(x_ref[...], g_ref[...]).astype(BF16)
    off = 0
    for o_ref in out_refs:
        n = o_ref.shape[-1]
        o_ref[...] = jnp.dot(h, w_ref[:, off:off + n], preferred_element_type=F32).astype(o_ref.dtype)
        off += n


def _in_proj(layer, x2, g, w, tm):
    m, d = x2.shape
    assert w.shape[2] == sum(SEG_WIDTHS)
    return pl.pallas_call(
        _in_proj_kernel,
        grid=(m // tm,),
        in_specs=[pl.BlockSpec((tm, d), lambda i: (i, 0)),
                  _resident(g, layer),
                  _resident(w, layer)],
        out_specs=[pl.BlockSpec((tm, n), lambda i: (i, 0)) for n in SEG_WIDTHS],
        out_shape=[jax.ShapeDtypeStruct((m, n), F32) for n in SEG_WIDTHS],
        compiler_params=_params("parallel"),
        name="in_proj",
    )(x2, g, w)


def _mla_prep_kernel(cq_ref, ckv_ref, kd_ref, qan_ref, wq_ref, kvan_ref, wkv_ref, qn_ref, kn_ref,
                     cos_ref, sa_ref, sb_ref, q_ref, k_ref, v_ref):
    scale = 1.0 / math.sqrt(QK_HEAD)
    hq = _rms(cq_ref[...], qan_ref[...]).astype(BF16)
    qf = jnp.dot(hq, wq_ref[...], preferred_element_type=F32)
    hkv = _rms(ckv_ref[...], kvan_ref[...]).astype(BF16)
    kvf = jnp.dot(hkv, wkv_ref[...], preferred_element_type=F32)

    cos = cos_ref[...]
    sa = sa_ref[...]
    sb = sb_ref[...]

    def rope(r):
        return r * cos + pltpu.roll(r, LANES - QK_ROPE // 2, 1) * sa + pltpu.roll(r, QK_ROPE // 2, 1) * sb

    kd = kd_ref[...]
    lane = lax.broadcasted_iota(jnp.int32, kd.shape, 1)
    kpe = jnp.where(lane < QK_ROPE, kd, 0.0)
    kpe_ss = jnp.sum(kpe * kpe, axis=-1, keepdims=True)

    qn_w = qn_ref[...]
    kn_w = kn_ref[...]
    for h in range(MLA_HEADS):
        qa = qf[:, h * QK_PAD:h * QK_PAD + LANES]
        qb = qf[:, h * QK_PAD + LANES:(h + 1) * QK_PAD]
        ss = jnp.sum(qa * qa, axis=-1, keepdims=True) + jnp.sum(qb * qb, axis=-1, keepdims=True)
        inv = lax.rsqrt(ss * (1.0 / QK_HEAD) + NORM_EPS)
        q_ref[0, h, :, 0:LANES] = (qa * inv * qn_w[:, 0:LANES] * scale).astype(q_ref.dtype)
        q_ref[0, h, :, LANES:QK_PAD] = (rope(qb * inv * qn_w[:, LANES:QK_PAD]) * scale).astype(q_ref.dtype)

        ka = kvf[:, h * LANES:(h + 1) * LANES]
        ss = jnp.sum(ka * ka, axis=-1, keepdims=True) + kpe_ss
        inv = lax.rsqrt(ss * (1.0 / QK_HEAD) + NORM_EPS)
        k_ref[0, h, :, 0:LANES] = (ka * inv * kn_w[:, 0:LANES]).astype(k_ref.dtype)
        k_ref[0, h, :, LANES:QK_PAD] = rope(kpe * inv * kn_w[:, LANES:QK_PAD]).astype(k_ref.dtype)

        v_ref[0, h, :, :] = kvf[:, MLA_WIDTH + h * V_HEAD:MLA_WIDTH + (h + 1) * V_HEAD].astype(v_ref.dtype)


def _mla_prep(layer, cq, ckv, kd, qan, wq, kvan, wkv, qn, kn, cos_t, sa_t, sb_t, bsz, seq, tm):
    nt = seq // tm
    row = lambda b, i: (b * nt + i, 0)
    pos = lambda b, i: (i, 0)
    hd = MLA_HEADS
    return pl.pallas_call(
        _mla_prep_kernel,
        grid=(bsz, nt),
        in_specs=[pl.BlockSpec((tm, Q_LORA), row),
                  pl.BlockSpec((tm, KV_LORA), row),
                  pl.BlockSpec((tm, KD_WIDTH), row),
                  _resident(qan, layer), _resident(wq, layer), _resident(kvan, layer), _resident(wkv, layer),
                  _resident(qn, layer), _resident(kn, layer),
                  pl.BlockSpec((tm, LANES), pos), pl.BlockSpec((tm, LANES), pos), pl.BlockSpec((tm, LANES), pos)],
        out_specs=[pl.BlockSpec((1, hd, tm, QK_PAD), lambda b, i: (b, 0, i, 0)),
                   pl.BlockSpec((1, hd, tm, QK_PAD), lambda b, i: (b, 0, i, 0)),
                   pl.BlockSpec((1, hd, tm, V_HEAD), lambda b, i: (b, 0, i, 0))],
        out_shape=[jax.ShapeDtypeStruct((bsz, hd, seq, QK_PAD), BF16),
                   jax.ShapeDtypeStruct((bsz, hd, seq, QK_PAD), BF16),
                   jax.ShapeDtypeStruct((bsz, hd, seq, V_HEAD), BF16)],
        compiler_params=_params("parallel", "parallel"),
        name="mla_prep",
    )(cq, ckv, kd, qan, wq, kvan, wkv, qn, kn, cos_t, sa_t, sb_t)


def _attn_kernel(q_ref, k_ref, v_ref, g_ref, o_ref, m_sc, l_sc, acc_sc):
    qi = pl.program_id(1)
    ki = pl.program_id(2)
    tq = q_ref.shape[2]
    tk = k_ref.shape[2]

    @pl.when(ki == 0)
    def _():
        m_sc[...] = jnp.full(m_sc.shape, -jnp.inf, F32)
        l_sc[...] = jnp.zeros(l_sc.shape, F32)
        acc_sc[...] = jnp.zeros(acc_sc.shape, F32)

    def step(masked):
        for h in range(MLA_HEADS):
            s = lax.dot_general(q_ref[0, h], k_ref[0, h], (((1,), (1,)), ((), ())),
                                preferred_element_type=F32)
            if masked:
                r = lax.broadcasted_iota(jnp.int32, (tq, tk), 0)
                c = lax.broadcasted_iota(jnp.int32, (tq, tk), 1)
                s = jnp.where(r >= c, s, -jnp.inf)
            m_prev = m_sc[h]
            m_new = jnp.maximum(m_prev, jnp.max(s, axis=-1, keepdims=True))
            alpha = jnp.exp(m_prev - m_new)
            p = jnp.exp(s - m_new)
            l_sc[h] = alpha * l_sc[h] + jnp.sum(p, axis=-1, keepdims=True)
            acc_sc[h] = alpha * acc_sc[h] + jnp.dot(p.astype(BF16), v_ref[0, h], preferred_element_type=F32)
            m_sc[h] = m_new

    @pl.when(ki < qi)
    def _():
        step(False)

    @pl.when(ki == qi)
    def _():
        step(True)
        for h in range(MLA_HEADS):
            o = acc_sc[h] / l_sc[h]
            g = g_ref[:, h * V_HEAD:(h + 1) * V_HEAD]
            o_ref[:, h * V_HEAD:(h + 1) * V_HEAD] = (o * _silu(g)).astype(o_ref.dtype)


def _attention(q, k, v, gmla, bsz, seq, tq):
    hd = MLA_HEADS
    nq = seq // tq
    kv_idx = lambda b, i, j: (b, 0, jnp.minimum(i, j), 0)
    row = lambda b, i, j: (b * nq + i, 0)
    return pl.pallas_call(
        _attn_kernel,
        grid=(bsz, nq, nq),
        in_specs=[pl.BlockSpec((1, hd, tq, QK_PAD), lambda b, i, j: (b, 0, i, 0)),
                  pl.BlockSpec((1, hd, tq, QK_PAD), kv_idx),
                  pl.BlockSpec((1, hd, tq, V_HEAD), kv_idx),
                  pl.BlockSpec((tq, MLA_WIDTH), row)],
        out_specs=pl.BlockSpec((tq, MLA_WIDTH), row),
        out_shape=jax.ShapeDtypeStruct((bsz * seq, MLA_WIDTH), BF16),
        scratch_shapes=[pltpu.VMEM((hd, tq, 1), F32), pltpu.VMEM((hd, tq, 1), F32),
                        pltpu.VMEM((hd, tq, V_HEAD), F32)],
        compiler_params=_params("parallel", "parallel", "arbitrary"),
        name="attention",
    )(q, k, v, gmla)


def _conv_kernel(u_ref, gc_ref, wdw_ref, bdw_ref, lng_ref, lnb_ref, wpw_ref, o_ref, hbuf, ybuf, *, strip):
    i = pl.program_id(1)
    tt = o_ref.shape[0]
    cw = CONV_WIDTH

    @pl.when(i == 0)
    def _():
        hbuf[0:CONV_HALO, :] = jnp.zeros((CONV_HALO, cw), F32)

    @pl.when(i > 0)
    def _():
        hbuf[0:CONV_HALO, :] = hbuf[tt:tt + CONV_HALO, :]

    hbuf[CONV_HALO:CONV_HALO + tt, :] = u_ref[:, 0:cw] * _sigmoid(u_ref[:, cw:2 * cw])

    base = CONV_HALO - (CONV_K - 1)

    def do_strip(s, carry):
        r0 = pl.multiple_of(s * strip, strip)
        win = hbuf[pl.ds(r0, strip + CONV_HALO), :]
        acc = jnp.broadcast_to(bdw_ref[...], (strip, cw))
        for res in range(SUBLANES):
            offs = [o for o in range(base, base + CONV_K) if o % SUBLANES == res]
            shifted = win[res:res + (offs[-1] - res) + strip, :]
            for o in offs:
                kk = o - base
                acc = acc + wdw_ref[kk:kk + 1, :] * shifted[o - res:o - res + strip, :]
        ybuf[pl.ds(r0, strip), :] = acc
        return carry

    lax.fori_loop(0, tt // strip, do_strip, 0)

    y = ybuf[...]
    mu = jnp.mean(y, axis=-1, keepdims=True)
    yc = y - mu
    var = jnp.mean(yc * yc, axis=-1, keepdims=True)
    hn = _silu(yc * lax.rsqrt(var + LN_EPS) * lng_ref[...] + lnb_ref[...])
    out = jnp.dot(hn.astype(BF16), wpw_ref[...], preferred_element_type=F32)
    o_ref[...] = (out * _silu(gc_ref[...])).astype(o_ref.dtype)


def _conv(layer, uconv, gconv, wdw, bdw, lng, lnb, wpw, bsz, seq, tt):
    nt = seq // tt
    row = lambda b, i: (b * nt + i, 0)
    strip = min(32, tt)
    return pl.pallas_call(
        functools.partial(_conv_kernel, strip=strip),
        grid=(bsz, nt),
        in_specs=[pl.BlockSpec((tt, 2 * CONV_WIDTH), row),
                  pl.BlockSpec((tt, CONV_WIDTH), row),
                  _resident(wdw, layer), _resident(bdw, layer), _resident(lng, layer), _resident(lnb, layer),
                  _resident(wpw, layer)],
        out_specs=pl.BlockSpec((tt, CONV_WIDTH), row),
        out_shape=jax.ShapeDtypeStruct((bsz * seq, CONV_WIDTH), BF16),
        scratch_shapes=[pltpu.VMEM((tt + CONV_HALO, CONV_WIDTH), F32), pltpu.VMEM((tt, CONV_WIDTH), F32)],
        compiler_params=_params("parallel", "arbitrary"),
        name="conformer_conv",
    )(uconv, gconv, wdw, bdw, lng, lnb, wpw)


def _split3(x):
    hi = x.astype(BF16)
    r1 = x - hi.astype(F32)
    mid = r1.astype(BF16)
    lo = (r1 - mid.astype(F32)).astype(BF16)
    return hi, mid, lo


def _ssd_kernel(z_ref, xbc_ref, kd_ref, cw_ref, cb_ref, dtb_ref, alog_ref, dsk_ref, ng_ref, o_ref,
                xbuf, xact, state):
    i = pl.program_id(1)
    ts = o_ref.shape[0]
    L = SSD_CHUNK
    hp = SSD_HEAD_DIM
    gw = SSD_WIDTH // SSD_GROUPS
    heads_per_group = SSD_HEADS // SSD_GROUPS

    @pl.when(i == 0)
    def _():
        xbuf[0:SSD_HALO, :] = jnp.zeros((SSD_HALO, SSD_XBC), F32)
        state[...] = jnp.zeros(state.shape, F32)

    @pl.when(i > 0)
    def _():
        xbuf[0:SSD_HALO, :] = xbuf[ts:ts + SSD_HALO, :]

    xbuf[SSD_HALO:SSD_HALO + ts, :] = xbc_ref[...]
    base = SSD_HALO - (SSD_CONV_K - 1)
    acc = jnp.broadcast_to(cb_ref[...], (ts, SSD_XBC))
    for kk in range(SSD_CONV_K):
        acc = acc + cw_ref[kk:kk + 1, :] * xbuf[base + kk:base + kk + ts, :]
    xact[...] = _silu(acc)

    lane = lax.broadcasted_iota(jnp.int32, (1, LANES), 1)
    is_dt = (lane >= DT_LANE0) & (lane < DT_LANE0 + SSD_HEADS)
    a_neg = jnp.where(is_dt, -jnp.exp(alog_ref[...]), 0.0)
    rr = lax.broadcasted_iota(jnp.int32, (L, L), 0)
    cc = lax.broadcasted_iota(jnp.int32, (L, L), 1)
    causal = rr >= cc
    tri = causal.astype(BF16)
    left = lax.broadcasted_iota(jnp.int32, (L, LANES), 1) < hp

    def chunk(c, carry):
        r0 = pl.multiple_of(c * L, L)
        xs = xact[pl.ds(r0, L), 0:SSD_WIDTH]
        dt = _softplus(kd_ref[pl.ds(r0, L), :] + dtb_ref[...])
        dta = dt * a_neg
        hi, mid, lo = _split3(dta)
        acs = (jnp.dot(tri, hi, preferred_element_type=F32)
               + jnp.dot(tri, mid, preferred_element_type=F32)
               + jnp.dot(tri, lo, preferred_element_type=F32))
        acs_t = acs.T

        def col(a, h):
            return jnp.broadcast_to(a[:, DT_LANE0 + h:DT_LANE0 + h + 1], (L, LANES))

        dt_cols = [col(dt, h) for h in range(SSD_HEADS)]
        acs_cols = [col(acs, h) for h in range(SSD_HEADS)]
        npair = SSD_HEADS // 2
        dt_exp = jnp.concatenate([jnp.where(left, dt_cols[2 * j], dt_cols[2 * j + 1]) for j in range(npair)], axis=1)
        acs_exp = jnp.concatenate([jnp.where(left, acs_cols[2 * j], acs_cols[2 * j + 1]) for j in range(npair)], axis=1)
        last = acs_exp[L - 1:L, :]
        xdt = xs * dt_exp
        xdt_b = xdt.astype(BF16)
        xdt_end = (xdt * jnp.exp(last - acs_exp)).astype(BF16)
        grow = jnp.exp(acs_exp)
        st_prev = state[...]
        st_prev_b = st_prev.astype(BF16)

        y_parts = []
        st_parts = []
        for g in range(SSD_GROUPS):
            bg = xact[pl.ds(r0, L), SSD_WIDTH + g * SSD_STATE:SSD_WIDTH + (g + 1) * SSD_STATE]
            cg = xact[pl.ds(r0, L), SSD_WIDTH + (SSD_GROUPS + g) * SSD_STATE:SSD_WIDTH + (SSD_GROUPS + g + 1) * SSD_STATE]
            bg_b = bg.astype(BF16)
            cg_b = cg.astype(BF16)
            cb = lax.dot_general(cg_b, bg_b, (((1,), (1,)), ((), ())), preferred_element_type=F32)
            st_parts.append(jnp.dot(bg.T.astype(BF16), xdt_end[:, g * gw:(g + 1) * gw],
                                    preferred_element_type=F32))
            y_off = jnp.dot(cg_b, st_prev_b[:, g * gw:(g + 1) * gw], preferred_element_type=F32)
            y_off = y_off * grow[:, g * gw:(g + 1) * gw]
            for jj in range(heads_per_group // 2):
                j = g * (heads_per_group // 2) + jj
                pair = xdt_b[:, j * LANES:(j + 1) * LANES]
                yd = []
                for h in (2 * j, 2 * j + 1):
                    seg = acs_cols[h] - acs_t[DT_LANE0 + h:DT_LANE0 + h + 1, :]
                    decay = jnp.exp(jnp.where(causal, seg, -jnp.inf))
                    yd.append(jnp.dot((cb * decay).astype(BF16), pair, preferred_element_type=F32))
                y_parts.append(jnp.where(left, yd[0], yd[1]) + y_off[:, jj * LANES:(jj + 1) * LANES])

        y = jnp.concatenate(y_parts, axis=1) + dsk_ref[...] * xs
        state[...] = st_prev * jnp.exp(last) + jnp.concatenate(st_parts, axis=1)
        yz = y * _silu(z_ref[pl.ds(r0, L), :])
        o_ref[pl.ds(r0, L), :] = _rms(yz, ng_ref[...]).astype(o_ref.dtype)
        return carry

    lax.fori_loop(0, ts // L, chunk, 0)


def _ssd(layer, z, xbc, kd, cw, cb, dtb, alog, dsk, ng, bsz, seq, ts):
    nt = seq // ts
    row = lambda b, i: (b * nt + i, 0)
    return pl.pallas_call(
        _ssd_kernel,
        grid=(bsz, nt),
        in_specs=[pl.BlockSpec((ts, SSD_WIDTH), row),
                  pl.BlockSpec((ts, SSD_XBC), row),
                  pl.BlockSpec((ts, KD_WIDTH), row),
                  _resident(cw, layer), _resident(cb, layer), _resident(dtb, layer), _resident(alog, layer),
                  _resident(dsk, layer), _resident(ng, layer)],
        out_specs=pl.BlockSpec((ts, SSD_WIDTH), row),
        out_shape=jax.ShapeDtypeStruct((bsz * seq, SSD_WIDTH), BF16),
        scratch_shapes=[pltpu.VMEM((ts + SSD_HALO, SSD_XBC), F32), pltpu.VMEM((ts, SSD_XBC), F32),
                        pltpu.VMEM((SSD_STATE, SSD_WIDTH), F32)],
        compiler_params=_params("parallel", "arbitrary"),
        name="ssd",
    )(z, xbc, kd, cw, cb, dtb, alog, dsk, ng)


def _out_proj_kernel(x_ref, a_ref, c_ref, s_ref, w_ref, o_ref):
    n0 = a_ref.shape[1]
    n1 = n0 + c_ref.shape[1]
    n2 = n1 + s_ref.shape[1]
    y = jnp.dot(a_ref[...], w_ref[0:n0, :], preferred_element_type=F32)
    y = y + jnp.dot(c_ref[...], w_ref[n0:n1, :], preferred_element_type=F32)
    y = y + jnp.dot(s_ref[...], w_ref[n1:n2, :], preferred_element_type=F32)
    o_ref[...] = x_ref[...] + y


def _out_proj(layer, x2, o_mla, o_conv, o_ssd, w, tm):
    m, d = x2.shape
    row = lambda i: (i, 0)
    return pl.pallas_call(
        _out_proj_kernel,
        grid=(m // tm,),
        in_specs=[pl.BlockSpec((tm, d), row),
                  pl.BlockSpec((tm, o_mla.shape[1]), row),
                  pl.BlockSpec((tm, o_conv.shape[1]), row),
                  pl.BlockSpec((tm, o_ssd.shape[1]), row),
                  _resident(w, layer)],
        out_specs=pl.BlockSpec((tm, d), row),
        out_shape=jax.ShapeDtypeStruct((m, d), F32),
        compiler_params=_params("parallel"),
        name="out_proj",
    )(x2, o_mla, o_conv, o_ssd, w)


def _rows(p):
    return p.astype(F32)[:, None, :]


def _pad_rows(p, lane0, width):
    return jnp.pad(p.astype(F32), ((0, 0), (lane0, width - lane0 - p.shape[1])))[:, None, :]


def _rope_tables(seq):
    half = QK_ROPE // 2
    inv_freq = ROPE_THETA ** (-jnp.arange(half, dtype=F32) / half)
    ang = jnp.arange(seq).astype(F32)[:, None] * inv_freq[None, :]
    cos, sin = jnp.cos(ang), jnp.sin(ang)
    z = lambda n: jnp.zeros((seq, n), F32)
    cos_t = jnp.concatenate([cos, cos, z(LANES - QK_ROPE)], axis=1)
    sa_t = jnp.concatenate([-sin, z(LANES - half)], axis=1)
    sb_t = jnp.concatenate([z(half), sin, z(LANES - QK_ROPE)], axis=1)
    return cos_t, sa_t, sb_t


def kernel(x, norm_g, w_in, q_a_norm, w_q_b, kv_a_norm, w_kv_b, q_norm, k_norm, conv_dw_w, conv_dw_b, conv_ln_g,
           conv_ln_b, conv_pw_w, ssd_conv_w, ssd_conv_b, ssd_dt_bias, ssd_A_log, ssd_D, ssd_norm_g, w_out):
    bsz, seq, d = x.shape
    m = bsz * seq
    x2 = x.reshape(m, d)
    tabs = _rope_tables(seq)

    w_in_k = _prep_w_in(w_in, min(256, d))
    wq, wkv, wpw = _prep_small(w_q_b, w_kv_b, conv_pw_w)
    w_out_k = _prep_cast(w_out, min(512, w_out.shape[1]))
    g_in, qan, kvan = _rows(norm_g), _rows(q_a_norm), _rows(kv_a_norm)
    qn, kn = _pad_rows(q_norm, 0, QK_PAD), _pad_rows(k_norm, 0, QK_PAD)
    wdw = jnp.pad(conv_dw_w.astype(F32), ((0, 0), (0, CONV_HALO - CONV_K), (0, 0)))
    bdw, lng, lnb = _rows(conv_dw_b), _rows(conv_ln_g), _rows(conv_ln_b)
    cw = jnp.pad(ssd_conv_w.astype(F32), ((0, 0), (0, SUBLANES - SSD_CONV_K), (0, 0)))
    cb, ng = _rows(ssd_conv_b), _rows(ssd_norm_g)
    dtb, alog = _pad_rows(ssd_dt_bias, DT_LANE0, LANES), _pad_rows(ssd_A_log, DT_LANE0, LANES)
    dsk = _rows(jnp.repeat(ssd_D, SSD_HEAD_DIM, axis=1))

    tm_in = min(256, m)
    tm_out = min(512, m)
    t_seq = min(512, seq)
    for layer in range(norm_g.shape[0]):
        cq, ckv, kd, gmla, uconv, gconv, z, xbc = _in_proj(layer, x2, g_in, w_in_k, tm_in)
        q, k, v = _mla_prep(layer, cq, ckv, kd, qan, wq, kvan, wkv, qn, kn, *tabs, bsz, seq, t_seq)
        o_mla = _attention(q, k, v, gmla, bsz, seq, t_seq)
        o_conv = _conv(layer, uconv, gconv, wdw, bdw, lng, lnb, wpw, bsz, seq, t_seq)
        o_ssd = _ssd(layer, z, xbc, kd, cw, cb, dtb, alog, dsk, ng, bsz, seq, t_seq)
        x2 = _out_proj(layer, x2, o_mla, o_conv, o_ssd, w_out_k, tm_out)
    return x2.reshape(bsz, seq, d)
```

```python
import functools
import math

import jax
import jax.numpy as jnp
from jax import lax
from jax.experimental import pallas as pl
from jax.experimental.pallas import tpu as pltpu

F32 = jnp.float32
BF16 = jnp.bfloat16

MLA_HEADS = 6
QK_NOPE = 128
QK_ROPE = 64
QK_HEAD = QK_NOPE + QK_ROPE
V_HEAD = 128
Q_LORA = 512
KV_LORA = 256
MLA_WIDTH = MLA_HEADS * V_HEAD
ROPE_THETA = 10000.0
CONV_WIDTH = 512
CONV_K = 31
SSD_HEADS = 12
SSD_HEAD_DIM = 64
SSD_WIDTH = SSD_HEADS * SSD_HEAD_DIM
SSD_GROUPS = 2
SSD_STATE = 128
SSD_CONV_K = 4
SSD_CHUNK = 128
SSD_XBC = SSD_WIDTH + 2 * SSD_GROUPS * SSD_STATE
NORM_EPS = 1e-6
LN_EPS = 1e-5

LANES = 128
SUBLANES = 8
QK_PAD = 2 * LANES
KD_WIDTH = LANES
DT_LANE0 = QK_ROPE
CONV_HALO = 32
SSD_HALO = SUBLANES
VMEM_LIMIT = 56 * 1024 * 1024

SEG_WIDTHS = (Q_LORA, KV_LORA, KD_WIDTH, MLA_WIDTH, 2 * CONV_WIDTH, CONV_WIDTH, SSD_WIDTH, SSD_XBC)


def _sigmoid(x):
    return 1.0 / (1.0 + jnp.exp(-x))


def _silu(x):
    return x * _sigmoid(x)


def _softplus(x):
    return jnp.maximum(x, 0.0) + jnp.log1p(jnp.exp(-jnp.abs(x)))


def _rms(x, g, eps=NORM_EPS):
    ms = jnp.mean(x * x, axis=-1, keepdims=True)
    return x * lax.rsqrt(ms + eps) * g


def _params(*sem):
    return pltpu.CompilerParams(dimension_semantics=sem, vmem_limit_bytes=VMEM_LIMIT)


def _resident(stacked, layer):
    _, a, b = stacked.shape
    return pl.BlockSpec((None, a, b), lambda *_: (layer, 0, 0), pipeline_mode=pl.Buffered(1))


def _prep_w_in_kernel(w_ref, o_ref):
    w = w_ref[...]
    o_gmla = Q_LORA + KV_LORA + QK_ROPE
    o_dt = w.shape[1] - SSD_HEADS
    pad = jnp.zeros((w.shape[0], KD_WIDTH - QK_ROPE - SSD_HEADS), F32)
    o_ref[...] = jnp.concatenate([w[:, :o_gmla], w[:, o_dt:], pad, w[:, o_gmla:o_dt]], axis=1).astype(o_ref.dtype)


def _prep_w_in(w_in, tk):
    depth, d, n = w_in.shape
    n_out = sum(SEG_WIDTHS)
    return pl.pallas_call(
        _prep_w_in_kernel,
        grid=(depth, d // tk),
        in_specs=[pl.BlockSpec((None, tk, n), lambda l, i: (l, i, 0))],
        out_specs=pl.BlockSpec((None, tk, n_out), lambda l, i: (l, i, 0)),
        out_shape=jax.ShapeDtypeStruct((depth, d, n_out), BF16),
        compiler_params=_params("parallel", "parallel"),
        name="prep_w_in",
    )(w_in)


def _prep_small_kernel(wq_ref, wkv_ref, wpw_ref, oq_ref, okv_ref, opw_ref):
    wq = wq_ref[...]
    zq = jnp.zeros((wq.shape[0], QK_PAD - QK_HEAD), F32)
    parts = []
    for h in range(MLA_HEADS):
        parts += [wq[:, h * QK_HEAD:(h + 1) * QK_HEAD], zq]
    oq_ref[...] = jnp.concatenate(parts, axis=1).astype(oq_ref.dtype)
    wkv = wkv_ref[...]
    per = QK_NOPE + V_HEAD
    k_cols = [wkv[:, h * per:h * per + QK_NOPE] for h in range(MLA_HEADS)]
    v_cols = [wkv[:, h * per + QK_NOPE:(h + 1) * per] for h in range(MLA_HEADS)]
    okv_ref[...] = jnp.concatenate(k_cols + v_cols, axis=1).astype(okv_ref.dtype)
    opw_ref[...] = wpw_ref[...].astype(opw_ref.dtype)


def _prep_small(w_q_b, w_kv_b, conv_pw_w):
    depth = w_q_b.shape[0]
    whole = lambda a: pl.BlockSpec((None,) + a.shape[1:], lambda l: (l, 0, 0))
    shapes = [(depth, Q_LORA, MLA_HEADS * QK_PAD), w_kv_b.shape, conv_pw_w.shape]
    return pl.pallas_call(
        _prep_small_kernel,
        grid=(depth,),
        in_specs=[whole(w_q_b), whole(w_kv_b), whole(conv_pw_w)],
        out_specs=[pl.BlockSpec((None,) + s[1:], lambda l: (l, 0, 0)) for s in shapes],
        out_shape=[jax.ShapeDtypeStruct(s, BF16) for s in shapes],
        compiler_params=_params("parallel"),
        name="prep_small",
    )(w_q_b, w_kv_b, conv_pw_w)


def _cast_kernel(w_ref, o_ref):
    o_ref[...] = w_ref[...].astype(o_ref.dtype)


def _prep_cast(w, tk):
    depth, k, n = w.shape
    return pl.pallas_call(
        _cast_kernel,
        grid=(depth, k // tk),
        in_specs=[pl.BlockSpec((None, tk, n), lambda l, i: (l, i, 0))],
        out_specs=pl.BlockSpec((None, tk, n), lambda l, i: (l, i, 0)),
        out_shape=jax.ShapeDtypeStruct(w.shape, BF16),
        compiler_params=_params("parallel", "parallel"),
        name="prep_cast",
    )(w)


def _in_proj_kernel(x_ref, g_ref, w_ref, *out_refs):
    h = _rms(x_ref[...], g_ref[...]).astype(BF16)
    off = 0
    for o_ref in out_refs:
        n = o_ref.shape[-1]
        o_ref[...] = jnp.dot(h, w_ref[:, off:off + n], preferred_element_type=F32).astype(o_ref.dtype)
        off += n


def _in_proj(layer, x2, g, w, tm):
    m, d = x2.shape
    assert w.shape[2] == sum(SEG_WIDTHS)
    return pl.pallas_call(
        _in_proj_kernel,
        grid=(m // tm,),
        in_specs=[pl.BlockSpec((tm, d), lambda i: (i, 0)),
                  _resident(g, layer),
                  _resident(w, layer)],
        out_specs=[pl.BlockSpec((tm, n), lambda i: (i, 0)) for n in SEG_WIDTHS],
        out_shape=[jax.ShapeDtypeStruct((m, n), F32) for n in SEG_WIDTHS],
        compiler_params=_params("parallel"),
        name="in_proj",
    )(x2, g, w)


def _mla_prep_kernel(cq_ref, ckv_ref, kd_ref, qan_ref, wq_ref, kvan_ref, wkv_ref, qn_ref, kn_ref,
                     cos_ref, sa_ref, sb_ref, q_ref, k_ref, v_ref):
    scale = 1.0 / math.sqrt(QK_HEAD)
    hq = _rms(cq_ref[...], qan_ref[...]).astype(BF16)
    qf = jnp.dot(hq, wq_ref[...], preferred_element_type=F32)
    hkv = _rms(ckv_ref[...], kvan_ref[...]).astype(BF16)
    kvf = jnp.dot(hkv, wkv_ref[...], preferred_element_type=F32)

    cos = cos_ref[...]
    sa = sa_ref[...]
    sb = sb_ref[...]

    def rope(r):
        return r * cos + pltpu.roll(r, LANES - QK_ROPE // 2, 1) * sa + pltpu.roll(r, QK_ROPE // 2, 1) * sb

    kd = kd_ref[...]
    lane = lax.broadcasted_iota(jnp.int32, kd.shape, 1)
    kpe = jnp.where(lane < QK_ROPE, kd, 0.0)
    kpe_ss = jnp.sum(kpe * kpe, axis=-1, keepdims=True)

    qn_w = qn_ref[...]
    kn_w = kn_ref[...]
    for h in range(MLA_HEADS):
        qa = qf[:, h * QK_PAD:h * QK_PAD + LANES]
        qb = qf[:, h * QK_PAD + LANES:(h + 1) * QK_PAD]
        ss = jnp.sum(qa * qa, axis=-1, keepdims=True) + jnp.sum(qb * qb, axis=-1, keepdims=True)
        inv = lax.rsqrt(ss * (1.0 / QK_HEAD) + NORM_EPS)
        q_ref[0, h, :, 0:LANES] = (qa * inv * qn_w[:, 0:LANES] * scale).astype(q_ref.dtype)
        q_ref[0, h, :, LANES:QK_PAD] = (rope(qb * inv * qn_w[:, LANES:QK_PAD]) * scale).astype(q_ref.dtype)

        ka = kvf[:, h * LANES:(h + 1) * LANES]
        ss = jnp.sum(ka * ka, axis=-1, keepdims=True) + kpe_ss
        inv = lax.rsqrt(ss * (1.0 / QK_HEAD) + NORM_EPS)
        k_ref[0, h, :, 0:LANES] = (ka * inv * kn_w[:, 0:LANES]).astype(k_ref.dtype)
        k_ref[0, h, :, LANES:QK_PAD] = rope(kpe * inv * kn_w[:, LANES:QK_PAD]).astype(k_ref.dtype)

        v_ref[0, h, :, :] = kvf[:, MLA_WIDTH + h * V_HEAD:MLA_WIDTH + (h + 1) * V_HEAD].astype(v_ref.dtype)


def _mla_prep(layer, cq, ckv, kd, qan, wq, kvan, wkv, qn, kn, cos_t, sa_t, sb_t, bsz, seq, tm):
    nt = seq // tm
    row = lambda b, i: (b * nt + i, 0)
    pos = lambda b, i: (i, 0)
    hd = MLA_HEADS
    return pl.pallas_call(
        _mla_prep_kernel,
        grid=(bsz, nt),
        in_specs=[pl.BlockSpec((tm, Q_LORA), row),
                  pl.BlockSpec((tm, KV_LORA), row),
                  pl.BlockSpec((tm, KD_WIDTH), row),
                  _resident(qan, layer), _resident(wq, layer), _resident(kvan, layer), _resident(wkv, layer),
                  _resident(qn, layer), _resident(kn, layer),
                  pl.BlockSpec((tm, LANES), pos), pl.BlockSpec((tm, LANES), pos), pl.BlockSpec((tm, LANES), pos)],
        out_specs=[pl.BlockSpec((1, hd, tm, QK_PAD), lambda b, i: (b, 0, i, 0)),
                   pl.BlockSpec((1, hd, tm, QK_PAD), lambda b, i: (b, 0, i, 0)),
                   pl.BlockSpec((1, hd, tm, V_HEAD), lambda b, i: (b, 0, i, 0))],
        out_shape=[jax.ShapeDtypeStruct((bsz, hd, seq, QK_PAD), BF16),
                   jax.ShapeDtypeStruct((bsz, hd, seq, QK_PAD), BF16),
                   jax.ShapeDtypeStruct((bsz, hd, seq, V_HEAD), BF16)],
        compiler_params=_params("parallel", "parallel"),
        name="mla_prep",
    )(cq, ckv, kd, qan, wq, kvan, wkv, qn, kn, cos_t, sa_t, sb_t)


def _attn_kernel(q_ref, k_ref, v_ref, g_ref, o_ref, m_sc, l_sc, acc_sc):
    qi = pl.program_id(1)
    ki = pl.program_id(2)
    tq = q_ref.shape[2]
    tk = k_ref.shape[2]

    @pl.when(ki == 0)
    def _():
        m_sc[...] = jnp.full(m_sc.shape, -jnp.inf, F32)
        l_sc[...] = jnp.zeros(l_sc.shape, F32)
        acc_sc[...] = jnp.zeros(acc_sc.shape, F32)

    def step(masked):
        if masked:
            keep = (lax.broadcasted_iota(jnp.int32, (tq, tk), 0) >= lax.broadcasted_iota(jnp.int32, (tq, tk), 1))
        for h in range(MLA_HEADS):
            s = lax.dot_general(q_ref[0, h], k_ref[0, h], (((1,), (1,)), ((), ())),
                                preferred_element_type=F32)
            if masked:
                s = jnp.where(keep, s, -jnp.inf)
            chunks = [s[:, c * LANES:(c + 1) * LANES] for c in range(tk // LANES)]
            m_loc = functools.reduce(jnp.maximum, chunks)
            m_prev = m_sc[h]
            m_new = jnp.maximum(m_prev, jnp.max(m_loc, axis=-1, keepdims=True))
            alpha = jnp.exp(m_prev - m_new)
            ps = [jnp.exp(c - m_new) for c in chunks]
            l_sc[h] = alpha * l_sc[h] + functools.reduce(jnp.add, ps)
            p = jnp.concatenate(ps, axis=1).astype(BF16)
            acc_sc[h] = alpha * acc_sc[h] + jnp.dot(p, v_ref[0, h], preferred_element_type=F32)
            m_sc[h] = m_new

    @pl.when(ki < qi)
    def _():
        step(False)

    @pl.when(ki == qi)
    def _():
        step(True)
        for h in range(MLA_HEADS):
            o = acc_sc[h] / jnp.sum(l_sc[h], axis=-1, keepdims=True)
            g = g_ref[:, h * V_HEAD:(h + 1) * V_HEAD]
            o_ref[:, h * V_HEAD:(h + 1) * V_HEAD] = (o * _silu(g)).astype(o_ref.dtype)


def _attention(q, k, v, gmla, bsz, seq, tq):
    hd = MLA_HEADS
    nq = seq // tq
    kv_idx = lambda b, i, j: (b, 0, jnp.minimum(i, j), 0)
    row = lambda b, i, j: (b * nq + i, 0)
    return pl.pallas_call(
        _attn_kernel,
        grid=(bsz, nq, nq),
        in_specs=[pl.BlockSpec((1, hd, tq, QK_PAD), lambda b, i, j: (b, 0, i, 0)),
                  pl.BlockSpec((1, hd, tq, QK_PAD), kv_idx),
                  pl.BlockSpec((1, hd, tq, V_HEAD), kv_idx),
                  pl.BlockSpec((tq, MLA_WIDTH), row)],
        out_specs=pl.BlockSpec((tq, MLA_WIDTH), row),
        out_shape=jax.ShapeDtypeStruct((bsz * seq, MLA_WIDTH), BF16),
        scratch_shapes=[pltpu.VMEM((hd, tq, LANES), F32), pltpu.VMEM((hd, tq, LANES), F32),
                        pltpu.VMEM((hd, tq, V_HEAD), F32)],
        compiler_params=_params("parallel", "parallel", "arbitrary"),
        name="attention",
    )(q, k, v, gmla)


def _conv_kernel(u_ref, gc_ref, wdw_ref, bdw_ref, lng_ref, lnb_ref, wpw_ref, o_ref, hbuf, ybuf, shbuf, *, strip):
    i = pl.program_id(1)
    tt = o_ref.shape[0]
    cw = CONV_WIDTH

    @pl.when(i == 0)
    def _():
        hbuf[0:CONV_HALO, :] = jnp.zeros((CONV_HALO, cw), F32)

    @pl.when(i > 0)
    def _():
        hbuf[0:CONV_HALO, :] = hbuf[tt:tt + CONV_HALO, :]

    hbuf[CONV_HALO:CONV_HALO + tt, :] = u_ref[:, 0:cw] * _sigmoid(u_ref[:, cw:2 * cw])

    base = CONV_HALO - (CONV_K - 1)

    span = tt + CONV_HALO - SUBLANES
    for res in range(1, SUBLANES):
        shbuf[res - 1, 0:span, :] = hbuf[res:res + span, :]

    def do_strip(s, carry):
        r0 = pl.multiple_of(s * strip, strip)
        acc = jnp.broadcast_to(bdw_ref[...], (strip, cw))
        for kk in range(CONV_K):
            res = (base + kk) % SUBLANES
            start = r0 + (base + kk - res)
            tap = hbuf[pl.ds(start, strip), :] if res == 0 else shbuf[res - 1, pl.ds(start, strip), :]
            acc = acc + wdw_ref[kk:kk + 1, :] * tap
        ybuf[pl.ds(r0, strip), :] = acc
        return carry

    lax.fori_loop(0, tt // strip, do_strip, 0)

    y = ybuf[...]
    mu = jnp.mean(y, axis=-1, keepdims=True)
    yc = y - mu
    var = jnp.mean(yc * yc, axis=-1, keepdims=True)
    hn = _silu(yc * lax.rsqrt(var + LN_EPS) * lng_ref[...] + lnb_ref[...])
    out = jnp.dot(hn.astype(BF16), wpw_ref[...], preferred_element_type=F32)
    o_ref[...] = (out * _silu(gc_ref[...])).astype(o_ref.dtype)


def _conv(layer, uconv, gconv, wdw, bdw, lng, lnb, wpw, bsz, seq, tt):
    nt = seq // tt
    row = lambda b, i: (b * nt + i, 0)
    strip = min(32, tt)
    return pl.pallas_call(
        functools.partial(_conv_kernel, strip=strip),
        grid=(bsz, nt),
        in_specs=[pl.BlockSpec((tt, 2 * CONV_WIDTH), row),
                  pl.BlockSpec((tt, CONV_WIDTH), row),
                  _resident(wdw, layer), _resident(bdw, layer), _resident(lng, layer), _resident(lnb, layer),
                  _resident(wpw, layer)],
        out_specs=pl.BlockSpec((tt, CONV_WIDTH), row),
        out_shape=jax.ShapeDtypeStruct((bsz * seq, CONV_WIDTH), BF16),
        scratch_shapes=[pltpu.VMEM((tt + CONV_HALO, CONV_WIDTH), F32), pltpu.VMEM((tt, CONV_WIDTH), F32),
                        pltpu.VMEM((SUBLANES - 1, tt + CONV_HALO, CONV_WIDTH), F32)],
        compiler_params=_params("parallel", "arbitrary"),
        name="conformer_conv",
    )(uconv, gconv, wdw, bdw, lng, lnb, wpw)


def _split3(x):
    hi = x.astype(BF16)
    r1 = x - hi.astype(F32)
    mid = r1.astype(BF16)
    lo = (r1 - mid.astype(F32)).astype(BF16)
    return hi, mid, lo


def _ssd_kernel(z_ref, xbc_ref, kd_ref, cw_ref, cb_ref, dtb_ref, alog_ref, dsk_ref, ng_ref, o_ref,
                xbuf, xact, state):
    i = pl.program_id(1)
    ts = o_ref.shape[0]
    L = SSD_CHUNK
    hp = SSD_HEAD_DIM
    gw = SSD_WIDTH // SSD_GROUPS
    heads_per_group = SSD_HEADS // SSD_GROUPS

    @pl.when(i == 0)
    def _():
        xbuf[0:SSD_HALO, :] = jnp.zeros((SSD_HALO, SSD_XBC), F32)
        state[...] = jnp.zeros(state.shape, F32)

    @pl.when(i > 0)
    def _():
        xbuf[0:SSD_HALO, :] = xbuf[ts:ts + SSD_HALO, :]

    xbuf[SSD_HALO:SSD_HALO + ts, :] = xbc_ref[...]
    base = SSD_HALO - (SSD_CONV_K - 1)
    acc = jnp.broadcast_to(cb_ref[...], (ts, SSD_XBC))
    for kk in range(SSD_CONV_K):
        acc = acc + cw_ref[kk:kk + 1, :] * xbuf[base + kk:base + kk + ts, :]
    xact[...] = _silu(acc)

    lane = lax.broadcasted_iota(jnp.int32, (1, LANES), 1)
    is_dt = (lane >= DT_LANE0) & (lane < DT_LANE0 + SSD_HEADS)
    a_neg = jnp.where(is_dt, -jnp.exp(alog_ref[...]), 0.0)
    rr = lax.broadcasted_iota(jnp.int32, (L, L), 0)
    cc = lax.broadcasted_iota(jnp.int32, (L, L), 1)
    causal = rr >= cc
    tri = causal.astype(BF16)
    left = lax.broadcasted_iota(jnp.int32, (L, LANES), 1) < hp

    def chunk(c, carry):
        r0 = pl.multiple_of(c * L, L)
        xs = xact[pl.ds(r0, L), 0:SSD_WIDTH]
        dt = _softplus(kd_ref[pl.ds(r0, L), :] + dtb_ref[...])
        dta = dt * a_neg
        hi, mid, lo = _split3(dta)
        acs = (jnp.dot(tri, hi, preferred_element_type=F32)
               + jnp.dot(tri, mid, preferred_element_type=F32)
               + jnp.dot(tri, lo, preferred_element_type=F32))
        acs_t = acs.T

        def col(a, h):
            return jnp.broadcast_to(a[:, DT_LANE0 + h:DT_LANE0 + h + 1], (L, LANES))

        dt_cols = [col(dt, h) for h in range(SSD_HEADS)]
        acs_cols = [col(acs, h) for h in range(SSD_HEADS)]
        npair = SSD_HEADS // 2
        dt_exp = jnp.concatenate([jnp.where(left, dt_cols[2 * j], dt_cols[2 * j + 1]) for j in range(npair)], axis=1)
        acs_exp = jnp.concatenate([jnp.where(left, acs_cols[2 * j], acs_cols[2 * j + 1]) for j in range(npair)], axis=1)
        last = acs_exp[L - 1:L, :]
        xdt = xs * dt_exp
        xdt_b = xdt.astype(BF16)
        xdt_end = (xdt * jnp.exp(last - acs_exp)).astype(BF16)
        grow = jnp.exp(acs_exp)
        st_prev = state[...]
        st_prev_b = st_prev.astype(BF16)

        y_parts = []
        st_parts = []
        for g in range(SSD_GROUPS):
            bg = xact[pl.ds(r0, L), SSD_WIDTH + g * SSD_STATE:SSD_WIDTH + (g + 1) * SSD_STATE]
            cg = xact[pl.ds(r0, L), SSD_WIDTH + (SSD_GROUPS + g) * SSD_STATE:SSD_WIDTH + (SSD_GROUPS + g + 1) * SSD_STATE]
            bg_b = bg.astype(BF16)
            cg_b = cg.astype(BF16)
            cb = lax.dot_general(cg_b, bg_b, (((1,), (1,)), ((), ())), preferred_element_type=F32)
            st_parts.append(jnp.dot(bg.T.astype(BF16), xdt_end[:, g * gw:(g + 1) * gw],
                                    preferred_element_type=F32))
            y_off = jnp.dot(cg_b, st_prev_b[:, g * gw:(g + 1) * gw], preferred_element_type=F32)
            y_off = y_off * grow[:, g * gw:(g + 1) * gw]
            for jj in range(heads_per_group // 2):
                j = g * (heads_per_group // 2) + jj
                pair = xdt_b[:, j * LANES:(j + 1) * LANES]
                yd = []
                for h in (2 * j, 2 * j + 1):
                    seg = acs_cols[h] - acs_t[DT_LANE0 + h:DT_LANE0 + h + 1, :]
                    decay = jnp.exp(jnp.where(causal, seg, -jnp.inf))
                    yd.append(jnp.dot((cb * decay).astype(BF16), pair, preferred_element_type=F32))
                y_parts.append(jnp.where(left, yd[0], yd[1]) + y_off[:, jj * LANES:(jj + 1) * LANES])

        y = jnp.concatenate(y_parts, axis=1) + dsk_ref[...] * xs
        state[...] = st_prev * jnp.exp(last) + jnp.concatenate(st_parts, axis=1)
        yz = y * _silu(z_ref[pl.ds(r0, L), :])
        o_ref[pl.ds(r0, L), :] = _rms(yz, ng_ref[...]).astype(o_ref.dtype)
        return carry

    lax.fori_loop(0, ts // L, chunk, 0)


def _ssd(layer, z, xbc, kd, cw, cb, dtb, alog, dsk, ng, bsz, seq, ts):
    nt = seq // ts
    row = lambda b, i: (b * nt + i, 0)
    return pl.pallas_call(
        _ssd_kernel,
        grid=(bsz, nt),
        in_specs=[pl.BlockSpec((ts, SSD_WIDTH), row),
                  pl.BlockSpec((ts, SSD_XBC), row),
                  pl.BlockSpec((ts, KD_WIDTH), row),
                  _resident(cw, layer), _resident(cb, layer), _resident(dtb, layer), _resident(alog, layer),
                  _resident(dsk, layer), _resident(ng, layer)],
        out_specs=pl.BlockSpec((ts, SSD_WIDTH), row),
        out_shape=jax.ShapeDtypeStruct((bsz * seq, SSD_WIDTH), BF16),
        scratch_shapes=[pltpu.VMEM((ts + SSD_HALO, SSD_XBC), F32), pltpu.VMEM((ts, SSD_XBC), F32),
                        pltpu.VMEM((SSD_STATE, SSD_WIDTH), F32)],
        compiler_params=_params("parallel", "arbitrary"),
        name="ssd",
    )(z, xbc, kd, cw, cb, dtb, alog, dsk, ng)


def _out_proj_kernel(x_ref, a_ref, c_ref, s_ref, w_ref, o_ref):
    n0 = a_ref.shape[1]
    n1 = n0 + c_ref.shape[1]
    n2 = n1 + s_ref.shape[1]
    y = jnp.dot(a_ref[...], w_ref[0:n0, :], preferred_element_type=F32)
    y = y + jnp.dot(c_ref[...], w_ref[n0:n1, :], preferred_element_type=F32)
    y = y + jnp.dot(s_ref[...], w_ref[n1:n2, :], preferred_element_type=F32)
    o_ref[...] = x_ref[...] + y


def _out_proj(layer, x2, o_mla, o_conv, o_ssd, w, tm):
    m, d = x2.shape
    row = lambda i: (i, 0)
    return pl.pallas_call(
        _out_proj_kernel,
        grid=(m // tm,),
        in_specs=[pl.BlockSpec((tm, d), row),
                  pl.BlockSpec((tm, o_mla.shape[1]), row),
                  pl.BlockSpec((tm, o_conv.shape[1]), row),
                  pl.BlockSpec((tm, o_ssd.shape[1]), row),
                  _resident(w, layer)],
        out_specs=pl.BlockSpec((tm, d), row),
        out_shape=jax.ShapeDtypeStruct((m, d), F32),
        compiler_params=_params("parallel"),
        name="out_proj",
    )(x2, o_mla, o_conv, o_ssd, w)


def _rows(p):
    return p.astype(F32)[:, None, :]


def _pad_rows(p, lane0, width):
    return jnp.pad(p.astype(F32), ((0, 0), (lane0, width - lane0 - p.shape[1])))[:, None, :]


def _rope_tables(seq):
    half = QK_ROPE // 2
    inv_freq = ROPE_THETA ** (-jnp.arange(half, dtype=F32) / half)
    ang = jnp.arange(seq).astype(F32)[:, None] * inv_freq[None, :]
    cos, sin = jnp.cos(ang), jnp.sin(ang)
    z = lambda n: jnp.zeros((seq, n), F32)
    cos_t = jnp.concatenate([cos, cos, z(LANES - QK_ROPE)], axis=1)
    sa_t = jnp.concatenate([-sin, z(LANES - half)], axis=1)
    sb_t = jnp.concatenate([z(half), sin, z(LANES - QK_ROPE)], axis=1)
    return cos_t, sa_t, sb_t


def kernel(x, norm_g, w_in, q_a_norm, w_q_b, kv_a_norm, w_kv_b, q_norm, k_norm, conv_dw_w, conv_dw_b, conv_ln_g,
           conv_ln_b, conv_pw_w, ssd_conv_w, ssd_conv_b, ssd_dt_bias, ssd_A_log, ssd_D, ssd_norm_g, w_out):
    bsz, seq, d = x.shape
    m = bsz * seq
    x2 = x.reshape(m, d)
    tabs = _rope_tables(seq)

    w_in_k = _prep_w_in(w_in, min(256, d))
    wq, wkv, wpw = _prep_small(w_q_b, w_kv_b, conv_pw_w)
    w_out_k = _prep_cast(w_out, min(512, w_out.shape[1]))
    g_in, qan, kvan = _rows(norm_g), _rows(q_a_norm), _rows(kv_a_norm)
    qn, kn = _pad_rows(q_norm, 0, QK_PAD), _pad_rows(k_norm, 0, QK_PAD)
    wdw = jnp.pad(conv_dw_w.astype(F32), ((0, 0), (0, CONV_HALO - CONV_K), (0, 0)))
    bdw, lng, lnb = _rows(conv_dw_b), _rows(conv_ln_g), _rows(conv_ln_b)
    cw = jnp.pad(ssd_conv_w.astype(F32), ((0, 0), (0, SUBLANES - SSD_CONV_K), (0, 0)))
    cb, ng = _rows(ssd_conv_b), _rows(ssd_norm_g)
    dtb, alog = _pad_rows(ssd_dt_bias, DT_LANE0, LANES), _pad_rows(ssd_A_log, DT_LANE0, LANES)
    dsk = _rows(jnp.repeat(ssd_D, SSD_HEAD_DIM, axis=1))

    tm_in = min(256, m)
    tm_out = min(512, m)
    t_seq = min(512, seq)
    for layer in range(norm_g.shape[0]):
        cq, ckv, kd, gmla, uconv, gconv, z, xbc = _in_proj(layer, x2, g_in, w_in_k, tm_in)
        q, k, v = _mla_prep(layer, cq, ckv, kd, qan, wq, kvan, wkv, qn, kn, *tabs, bsz, seq, t_seq)
        o_mla = _attention(q, k, v, gmla, bsz, seq, t_seq)
        o_conv = _conv(layer, uconv, gconv, wdw, bdw, lng, lnb, wpw, bsz, seq, t_seq)
        o_ssd = _ssd(layer, z, xbc, kd, cw, cb, dtb, alog, dsk, ng, bsz, seq, t_seq)
        x2 = _out_proj(layer, x2, o_mla, o_conv, o_ssd, w_out_k, tm_out)
    return x2.reshape(bsz, seq, d)
```

```python
import functools
import math

import jax
import jax.numpy as jnp
from jax import lax
from jax.experimental import pallas as pl
from jax.experimental.pallas import tpu as pltpu

F32 = jnp.float32
BF16 = jnp.bfloat16

MLA_HEADS = 6
QK_NOPE = 128
QK_ROPE = 64
QK_HEAD = QK_NOPE + QK_ROPE
V_HEAD = 128
Q_LORA = 512
KV_LORA = 256
MLA_WIDTH = MLA_HEADS * V_HEAD
ROPE_THETA = 10000.0
CONV_WIDTH = 512
CONV_K = 31
SSD_HEADS = 12
SSD_HEAD_DIM = 64
SSD_WIDTH = SSD_HEADS * SSD_HEAD_DIM
SSD_GROUPS = 2
SSD_STATE = 128
SSD_CONV_K = 4
SSD_CHUNK = 128
SSD_XBC = SSD_WIDTH + 2 * SSD_GROUPS * SSD_STATE
NORM_EPS = 1e-6
LN_EPS = 1e-5

LANES = 128
SUBLANES = 8
QK_PAD = 2 * LANES
KD_WIDTH = LANES
DT_LANE0 = QK_ROPE
CONV_HALO = 32
SSD_HALO = SUBLANES
VMEM_LIMIT = 56 * 1024 * 1024
ATTN_STRIP = 64

SEG_WIDTHS = (Q_LORA, KV_LORA, KD_WIDTH, MLA_WIDTH, 2 * CONV_WIDTH, CONV_WIDTH, SSD_WIDTH, SSD_XBC)


def _sigmoid(x):
    return 1.0 / (1.0 + jnp.exp(-x))


def _silu(x):
    return x * _sigmoid(x)


def _softplus(x):
    return jnp.maximum(x, 0.0) + jnp.log1p(jnp.exp(-jnp.abs(x)))


def _rms(x, g, eps=NORM_EPS):
    ms = jnp.mean(x * x, axis=-1, keepdims=True)
    return x * lax.rsqrt(ms + eps) * g


def _params(*sem):
    return pltpu.CompilerParams(dimension_semantics=sem, vmem_limit_bytes=VMEM_LIMIT)


def _resident(stacked, layer):
    _, a, b = stacked.shape
    return pl.BlockSpec((None, a, b), lambda *_: (layer, 0, 0), pipeline_mode=pl.Buffered(1))


def _prep_w_in_kernel(w_ref, o_ref):
    w = w_ref[...]
    o_gmla = Q_LORA + KV_LORA + QK_ROPE
    o_dt = w.shape[1] - SSD_HEADS
    pad = jnp.zeros((w.shape[0], KD_WIDTH - QK_ROPE - SSD_HEADS), F32)
    o_ref[...] = jnp.concatenate([w[:, :o_gmla], w[:, o_dt:], pad, w[:, o_gmla:o_dt]], axis=1).astype(o_ref.dtype)


def _prep_w_in(w_in, tk):
    depth, d, n = w_in.shape
    n_out = sum(SEG_WIDTHS)
    return pl.pallas_call(
        _prep_w_in_kernel,
        grid=(depth, d // tk),
        in_specs=[pl.BlockSpec((None, tk, n), lambda l, i: (l, i, 0))],
        out_specs=pl.BlockSpec((None, tk, n_out), lambda l, i: (l, i, 0)),
        out_shape=jax.ShapeDtypeStruct((depth, d, n_out), BF16),
        compiler_params=_params("parallel", "parallel"),
        name="prep_w_in",
    )(w_in)


def _prep_small_kernel(wq_ref, wkv_ref, wpw_ref, oq_ref, okv_ref, opw_ref):
    wq = wq_ref[...]
    zq = jnp.zeros((wq.shape[0], QK_PAD - QK_HEAD), F32)
    parts = []
    for h in range(MLA_HEADS):
        parts += [wq[:, h * QK_HEAD:(h + 1) * QK_HEAD], zq]
    oq_ref[...] = jnp.concatenate(parts, axis=1).astype(oq_ref.dtype)
    wkv = wkv_ref[...]
    per = QK_NOPE + V_HEAD
    k_cols = [wkv[:, h * per:h * per + QK_NOPE] for h in range(MLA_HEADS)]
    v_cols = [wkv[:, h * per + QK_NOPE:(h + 1) * per] for h in range(MLA_HEADS)]
    okv_ref[...] = jnp.concatenate(k_cols + v_cols, axis=1).astype(okv_ref.dtype)
    opw_ref[...] = wpw_ref[...].astype(opw_ref.dtype)


def _prep_small(w_q_b, w_kv_b, conv_pw_w):
    depth = w_q_b.shape[0]
    whole = lambda a: pl.BlockSpec((None,) + a.shape[1:], lambda l: (l, 0, 0))
    shapes = [(depth, Q_LORA, MLA_HEADS * QK_PAD), w_kv_b.shape, conv_pw_w.shape]
    return pl.pallas_call(
        _prep_small_kernel,
        grid=(depth,),
        in_specs=[whole(w_q_b), whole(w_kv_b), whole(conv_pw_w)],
        out_specs=[pl.BlockSpec((None,) + s[1:], lambda l: (l, 0, 0)) for s in shapes],
        out_shape=[jax.ShapeDtypeStruct(s, BF16) for s in shapes],
        compiler_params=_params("parallel"),
        name="prep_small",
    )(w_q_b, w_kv_b, conv_pw_w)


def _cast_kernel(w_ref, o_ref):
    o_ref[...] = w_ref[...].astype(o_ref.dtype)


def _prep_cast(w, tk):
    depth, k, n = w.shape
    return pl.pallas_call(
        _cast_kernel,
        grid=(depth, k // tk),
        in_specs=[pl.BlockSpec((None, tk, n), lambda l, i: (l, i, 0))],
        out_specs=pl.BlockSpec((None, tk, n), lambda l, i: (l, i, 0)),
        out_shape=jax.ShapeDtypeStruct(w.shape, BF16),
        compiler_params=_params("parallel", "parallel"),
        name="prep_cast",
    )(w)


def _in_proj_kernel(x_ref, g_ref, w_ref, *out_refs):
    h = _rms(x_ref[...], g_ref[...]).astype(BF16)
    off = 0
    for o_ref in out_refs:
        n = o_ref.shape[-1]
        o_ref[...] = jnp.dot(h, w_ref[:, off:off + n], preferred_element_type=F32).astype(o_ref.dtype)
        off += n


def _in_proj(layer, x2, g, w, tm):
    m, d = x2.shape
    assert w.shape[2] == sum(SEG_WIDTHS)
    return pl.pallas_call(
        _in_proj_kernel,
        grid=(m // tm,),
        in_specs=[pl.BlockSpec((tm, d), lambda i: (i, 0)),
                  _resident(g, layer),
                  _resident(w, layer)],
        out_specs=[pl.BlockSpec((tm, n), lambda i: (i, 0)) for n in SEG_WIDTHS],
        out_shape=[jax.ShapeDtypeStruct((m, n), F32) for n in SEG_WIDTHS],
        compiler_params=_params("parallel"),
        name="in_proj",
    )(x2, g, w)


def _mla_prep_kernel(cq_ref, ckv_ref, kd_ref, qan_ref, wq_ref, kvan_ref, wkv_ref, qn_ref, kn_ref,
                     cos_ref, sa_ref, sb_ref, q_ref, k_ref, v_ref):
    scale = math.log2(math.e) / math.sqrt(QK_HEAD)
    hq = _rms(cq_ref[...], qan_ref[...]).astype(BF16)
    qf = jnp.dot(hq, wq_ref[...], preferred_element_type=F32)
    hkv = _rms(ckv_ref[...], kvan_ref[...]).astype(BF16)
    kvf = jnp.dot(hkv, wkv_ref[...], preferred_element_type=F32)

    cos = cos_ref[...]
    sa = sa_ref[...]
    sb = sb_ref[...]

    def rope(r):
        return r * cos + pltpu.roll(r, LANES - QK_ROPE // 2, 1) * sa + pltpu.roll(r, QK_ROPE // 2, 1) * sb

    kd = kd_ref[...]
    lane = lax.broadcasted_iota(jnp.int32, kd.shape, 1)
    kpe = jnp.where(lane < QK_ROPE, kd, 0.0)
    kpe_ss = jnp.sum(kpe * kpe, axis=-1, keepdims=True)

    qn_w = qn_ref[...]
    kn_w = kn_ref[...]
    for h in range(MLA_HEADS):
        qa = qf[:, h * QK_PAD:h * QK_PAD + LANES]
        qb = qf[:, h * QK_PAD + LANES:(h + 1) * QK_PAD]
        ss = jnp.sum(qa * qa, axis=-1, keepdims=True) + jnp.sum(qb * qb, axis=-1, keepdims=True)
        inv = lax.rsqrt(ss * (1.0 / QK_HEAD) + NORM_EPS)
        q_ref[0, h, :, 0:LANES] = (qa * inv * qn_w[:, 0:LANES] * scale).astype(q_ref.dtype)
        q_ref[0, h, :, LANES:QK_PAD] = (rope(qb * inv * qn_w[:, LANES:QK_PAD]) * scale).astype(q_ref.dtype)

        ka = kvf[:, h * LANES:(h + 1) * LANES]
        ss = jnp.sum(ka * ka, axis=-1, keepdims=True) + kpe_ss
        inv = lax.rsqrt(ss * (1.0 / QK_HEAD) + NORM_EPS)
        k_ref[0, h, :, 0:LANES] = (ka * inv * kn_w[:, 0:LANES]).astype(k_ref.dtype)
        k_ref[0, h, :, LANES:QK_PAD] = rope(kpe * inv * kn_w[:, LANES:QK_PAD]).astype(k_ref.dtype)

        v_ref[0, h, :, :] = kvf[:, MLA_WIDTH + h * V_HEAD:MLA_WIDTH + (h + 1) * V_HEAD].astype(v_ref.dtype)


def _mla_prep(layer, cq, ckv, kd, qan, wq, kvan, wkv, qn, kn, cos_t, sa_t, sb_t, bsz, seq, tm):
    nt = seq // tm
    row = lambda b, i: (b * nt + i, 0)
    pos = lambda b, i: (i, 0)
    hd = MLA_HEADS
    return pl.pallas_call(
        _mla_prep_kernel,
        grid=(bsz, nt),
        in_specs=[pl.BlockSpec((tm, Q_LORA), row),
                  pl.BlockSpec((tm, KV_LORA), row),
                  pl.BlockSpec((tm, KD_WIDTH), row),
                  _resident(qan, layer), _resident(wq, layer), _resident(kvan, layer), _resident(wkv, layer),
                  _resident(qn, layer), _resident(kn, layer),
                  pl.BlockSpec((tm, LANES), pos), pl.BlockSpec((tm, LANES), pos), pl.BlockSpec((tm, LANES), pos)],
        out_specs=[pl.BlockSpec((1, hd, tm, QK_PAD), lambda b, i: (b, 0, i, 0)),
                   pl.BlockSpec((1, hd, tm, QK_PAD), lambda b, i: (b, 0, i, 0)),
                   pl.BlockSpec((1, hd, tm, V_HEAD), lambda b, i: (b, 0, i, 0))],
        out_shape=[jax.ShapeDtypeStruct((bsz, hd, seq, QK_PAD), BF16),
                   jax.ShapeDtypeStruct((bsz, hd, seq, QK_PAD), BF16),
                   jax.ShapeDtypeStruct((bsz, hd, seq, V_HEAD), BF16)],
        compiler_params=_params("parallel", "parallel"),
        name="mla_prep",
    )(cq, ckv, kd, qan, wq, kvan, wkv, qn, kn, cos_t, sa_t, sb_t)


def _attn_kernel(qi_ref, ki_ref, q_ref, k_ref, v_ref, g_ref, o_ref, m_sc, l_sc, acc_sc, s_sc, p_sc, a_sc):
    qi = qi_ref[pl.program_id(1)]
    ki = ki_ref[pl.program_id(1)]
    tq = q_ref.shape[2]
    tk = k_ref.shape[2]

    @pl.when(ki == 0)
    def _():
        m_sc[...] = jnp.full(m_sc.shape, -jnp.inf, F32)
        l_sc[...] = jnp.zeros(l_sc.shape, F32)
        acc_sc[...] = jnp.zeros(acc_sc.shape, F32)

    strip = min(ATTN_STRIP, tq)
    n_chunk = tk // LANES

    def scores(h):
        s_sc[h % 2] = lax.dot_general(q_ref[0, h], k_ref[0, h], (((1,), (1,)), ((), ())),
                                      preferred_element_type=F32)

    def softmax_strip(h, r0, masked):
        slot = h % 2
        rows = slice(r0, r0 + strip)
        live = [c for c in range(n_chunk) if not (masked and c * LANES >= r0 + strip)]
        chunks = []
        for c in live:
            x = s_sc[slot, rows, c * LANES:(c + 1) * LANES]
            if masked and (c + 1) * LANES - 1 > r0:
                rr = r0 + lax.broadcasted_iota(jnp.int32, (strip, LANES), 0)
                cc = c * LANES + lax.broadcasted_iota(jnp.int32, (strip, LANES), 1)
                x = jnp.where(rr >= cc, x, -jnp.inf)
            chunks.append(x)
        m_prev = m_sc[h, rows, :]
        m_loc = functools.reduce(jnp.maximum, chunks)
        m_new = jnp.maximum(m_prev, jnp.max(m_loc, axis=-1, keepdims=True))
        alpha = jnp.exp2(m_prev - m_new)
        ps = [jnp.exp2(x - m_new) for x in chunks]
        l_sc[h, rows, :] = alpha * l_sc[h, rows, :] + functools.reduce(jnp.add, ps)
        m_sc[h, rows, :] = m_new
        a_sc[slot, rows, :] = alpha
        dead = [jnp.zeros((strip, LANES), BF16)] * (n_chunk - len(live))
        p_sc[slot, rows, :] = jnp.concatenate([p.astype(BF16) for p in ps] + dead, axis=1)

    def step(masked):
        scores(0)
        for h in range(MLA_HEADS):
            if h + 1 < MLA_HEADS:
                scores(h + 1)
            for r0 in range(0, tq, strip):
                softmax_strip(h, r0, masked)
            pv = jnp.dot(p_sc[h % 2], v_ref[0, h], preferred_element_type=F32)
            acc_sc[h] = a_sc[h % 2] * acc_sc[h] + pv

    @pl.when(ki < qi)
    def _():
        step(False)

    @pl.when(ki == qi)
    def _():
        step(True)
        for h in range(MLA_HEADS):
            o = acc_sc[h] / jnp.sum(l_sc[h], axis=-1, keepdims=True)
            g = g_ref[:, h * V_HEAD:(h + 1) * V_HEAD]
            o_ref[:, h * V_HEAD:(h + 1) * V_HEAD] = (o * _silu(g)).astype(o_ref.dtype)


def _attention(q, k, v, gmla, bsz, seq, tq):
    hd = MLA_HEADS
    nq = seq // tq
    pairs = [(i, j) for i in range(nq) for j in range(i + 1)]
    qi_tab = jnp.asarray([i for i, _ in pairs], jnp.int32)
    ki_tab = jnp.asarray([j for _, j in pairs], jnp.int32)
    q_idx = lambda b, t, qi, ki: (b, 0, qi[t], 0)
    kv_idx = lambda b, t, qi, ki: (b, 0, ki[t], 0)
    row = lambda b, t, qi, ki: (b * nq + qi[t], 0)
    grid_spec = pltpu.PrefetchScalarGridSpec(
        num_scalar_prefetch=2,
        grid=(bsz, len(pairs)),
        in_specs=[pl.BlockSpec((1, hd, tq, QK_PAD), q_idx),
                  pl.BlockSpec((1, hd, tq, QK_PAD), kv_idx),
                  pl.BlockSpec((1, hd, tq, V_HEAD), kv_idx),
                  pl.BlockSpec((tq, MLA_WIDTH), row)],
        out_specs=pl.BlockSpec((tq, MLA_WIDTH), row),
        scratch_shapes=[pltpu.VMEM((hd, tq, LANES), F32), pltpu.VMEM((hd, tq, LANES), F32),
                        pltpu.VMEM((hd, tq, V_HEAD), F32),
                        pltpu.VMEM((2, tq, tq), F32), pltpu.VMEM((2, tq, tq), BF16),
                        pltpu.VMEM((2, tq, LANES), F32)])
    return pl.pallas_call(
        _attn_kernel,
        grid_spec=grid_spec,
        out_shape=jax.ShapeDtypeStruct((bsz * seq, MLA_WIDTH), BF16),
        compiler_params=_params("parallel", "arbitrary"),
        name="attention",
    )(qi_tab, ki_tab, q, k, v, gmla)


def _conv_kernel(u_ref, gc_ref, wdw_ref, bdw_ref, lng_ref, lnb_ref, wpw_ref, o_ref, hbuf, ybuf, shbuf, *, strip):
    i = pl.program_id(1)
    tt = o_ref.shape[0]
    cw = CONV_WIDTH

    @pl.when(i == 0)
    def _():
        hbuf[0:CONV_HALO, :] = jnp.zeros((CONV_HALO, cw), F32)

    @pl.when(i > 0)
    def _():
        hbuf[0:CONV_HALO, :] = hbuf[tt:tt + CONV_HALO, :]

    hbuf[CONV_HALO:CONV_HALO + tt, :] = u_ref[:, 0:cw] * _sigmoid(u_ref[:, cw:2 * cw])

    base = CONV_HALO - (CONV_K - 1)

    span = tt + CONV_HALO - SUBLANES
    for res in range(1, SUBLANES):
        shbuf[res - 1, 0:span, :] = hbuf[res:res + span, :]

    def do_strip(s, carry):
        r0 = pl.multiple_of(s * strip, strip)
        acc = jnp.broadcast_to(bdw_ref[...], (strip, cw))
        for kk in range(CONV_K):
            res = (base + kk) % SUBLANES
            start = r0 + (base + kk - res)
            tap = hbuf[pl.ds(start, strip), :] if res == 0 else shbuf[res - 1, pl.ds(start, strip), :]
            acc = acc + wdw_ref[kk:kk + 1, :] * tap
        ybuf[pl.ds(r0, strip), :] = acc
        return carry

    lax.fori_loop(0, tt // strip, do_strip, 0)

    y = ybuf[...]
    mu = jnp.mean(y, axis=-1, keepdims=True)
    yc = y - mu
    var = jnp.mean(yc * yc, axis=-1, keepdims=True)
    hn = _silu(yc * lax.rsqrt(var + LN_EPS) * lng_ref[...] + lnb_ref[...])
    out = jnp.dot(hn.astype(BF16), wpw_ref[...], preferred_element_type=F32)
    o_ref[...] = (out * _silu(gc_ref[...])).astype(o_ref.dtype)


def _conv(layer, uconv, gconv, wdw, bdw, lng, lnb, wpw, bsz, seq, tt):
    nt = seq // tt
    row = lambda b, i: (b * nt + i, 0)
    strip = min(32, tt)
    return pl.pallas_call(
        functools.partial(_conv_kernel, strip=strip),
        grid=(bsz, nt),
        in_specs=[pl.BlockSpec((tt, 2 * CONV_WIDTH), row),
                  pl.BlockSpec((tt, CONV_WIDTH), row),
                  _resident(wdw, layer), _resident(bdw, layer), _resident(lng, layer), _resident(lnb, layer),
                  _resident(wpw, layer)],
        out_specs=pl.BlockSpec((tt, CONV_WIDTH), row),
        out_shape=jax.ShapeDtypeStruct((bsz * seq, CONV_WIDTH), BF16),
        scratch_shapes=[pltpu.VMEM((tt + CONV_HALO, CONV_WIDTH), F32), pltpu.VMEM((tt, CONV_WIDTH), F32),
                        pltpu.VMEM((SUBLANES - 1, tt + CONV_HALO, CONV_WIDTH), F32)],
        compiler_params=_params("parallel", "arbitrary"),
        name="conformer_conv",
    )(uconv, gconv, wdw, bdw, lng, lnb, wpw)


def _split3(x):
    hi = x.astype(BF16)
    r1 = x - hi.astype(F32)
    mid = r1.astype(BF16)
    lo = (r1 - mid.astype(F32)).astype(BF16)
    return hi, mid, lo


def _ssd_kernel(z_ref, xbc_ref, kd_ref, cw_ref, cb_ref, dtb_ref, alog_ref, dsk_ref, ng_ref, o_ref,
                xbuf, xact, state):
    i = pl.program_id(1)
    ts = o_ref.shape[0]
    L = SSD_CHUNK
    hp = SSD_HEAD_DIM
    gw = SSD_WIDTH // SSD_GROUPS
    heads_per_group = SSD_HEADS // SSD_GROUPS

    @pl.when(i == 0)
    def _():
        xbuf[0:SSD_HALO, :] = jnp.zeros((SSD_HALO, SSD_XBC), F32)
        state[...] = jnp.zeros(state.shape, F32)

    @pl.when(i > 0)
    def _():
        xbuf[0:SSD_HALO, :] = xbuf[ts:ts + SSD_HALO, :]

    xbuf[SSD_HALO:SSD_HALO + ts, :] = xbc_ref[...]
    base = SSD_HALO - (SSD_CONV_K - 1)
    acc = jnp.broadcast_to(cb_ref[...], (ts, SSD_XBC))
    for kk in range(SSD_CONV_K):
        acc = acc + cw_ref[kk:kk + 1, :] * xbuf[base + kk:base + kk + ts, :]
    xact[...] = _silu(acc)

    lane = lax.broadcasted_iota(jnp.int32, (1, LANES), 1)
    is_dt = (lane >= DT_LANE0) & (lane < DT_LANE0 + SSD_HEADS)
    a_neg = jnp.where(is_dt, -jnp.exp(alog_ref[...]), 0.0)
    rr = lax.broadcasted_iota(jnp.int32, (L, L), 0)
    cc = lax.broadcasted_iota(jnp.int32, (L, L), 1)
    causal = rr >= cc
    tri = causal.astype(BF16)
    left = lax.broadcasted_iota(jnp.int32, (L, LANES), 1) < hp

    def chunk(c, carry):
        r0 = pl.multiple_of(c * L, L)
        xs = xact[pl.ds(r0, L), 0:SSD_WIDTH]
        dt = _softplus(kd_ref[pl.ds(r0, L), :] + dtb_ref[...])
        dta = dt * a_neg
        hi, mid, lo = _split3(dta)
        acs = (jnp.dot(tri, hi, preferred_element_type=F32)
               + jnp.dot(tri, mid, preferred_element_type=F32)
               + jnp.dot(tri, lo, preferred_element_type=F32))
        acs_t = acs.T

        def col(a, h):
            return jnp.broadcast_to(a[:, DT_LANE0 + h:DT_LANE0 + h + 1], (L, LANES))

        dt_cols = [col(dt, h) for h in range(SSD_HEADS)]
        acs_cols = [col(acs, h) for h in range(SSD_HEADS)]
        npair = SSD_HEADS // 2
        dt_exp = jnp.concatenate([jnp.where(left, dt_cols[2 * j], dt_cols[2 * j + 1]) for j in range(npair)], axis=1)
        acs_exp = jnp.concatenate([jnp.where(left, acs_cols[2 * j], acs_cols[2 * j + 1]) for j in range(npair)], axis=1)
        last = acs_exp[L - 1:L, :]
        xdt = xs * dt_exp
        xdt_b = xdt.astype(BF16)
        xdt_end = (xdt * jnp.exp(last - acs_exp)).astype(BF16)
        grow = jnp.exp(acs_exp)
        st_prev = state[...]
        st_prev_b = st_prev.astype(BF16)

        y_parts = []
        st_parts = []
        for g in range(SSD_GROUPS):
            bg = xact[pl.ds(r0, L), SSD_WIDTH + g * SSD_STATE:SSD_WIDTH + (g + 1) * SSD_STATE]
            cg = xact[pl.ds(r0, L), SSD_WIDTH + (SSD_GROUPS + g) * SSD_STATE:SSD_WIDTH + (SSD_GROUPS + g + 1) * SSD_STATE]
            bg_b = bg.astype(BF16)
            cg_b = cg.astype(BF16)
            cb = lax.dot_general(cg_b, bg_b, (((1,), (1,)), ((), ())), preferred_element_type=F32)
            st_parts.append(jnp.dot(bg.T.astype(BF16), xdt_end[:, g * gw:(g + 1) * gw],
                                    preferred_element_type=F32))
            y_off = jnp.dot(cg_b, st_prev_b[:, g * gw:(g + 1) * gw], preferred_element_type=F32)
            y_off = y_off * grow[:, g * gw:(g + 1) * gw]
            for jj in range(heads_per_group // 2):
                j = g * (heads_per_group // 2) + jj
                pair = xdt_b[:, j * LANES:(j + 1) * LANES]
                yd = []
                for h in (2 * j, 2 * j + 1):
                    seg = acs_cols[h] - acs_t[DT_LANE0 + h:DT_LANE0 + h + 1, :]
                    decay = jnp.exp(jnp.where(causal, seg, -jnp.inf))
                    yd.append(jnp.dot((cb * decay).astype(BF16), pair, preferred_element_type=F32))
                y_parts.append(jnp.where(left, yd[0], yd[1]) + y_off[:, jj * LANES:(jj + 1) * LANES])

        y = jnp.concatenate(y_parts, axis=1) + dsk_ref[...] * xs
        state[...] = st_prev * jnp.exp(last) + jnp.concatenate(st_parts, axis=1)
        yz = y * _silu(z_ref[pl.ds(r0, L), :])
        o_ref[pl.ds(r0, L), :] = _rms(yz, ng_ref[...]).astype(o_ref.dtype)
        return carry

    lax.fori_loop(0, ts // L, chunk, 0)


def _ssd(layer, z, xbc, kd, cw, cb, dtb, alog, dsk, ng, bsz, seq, ts):
    nt = seq // ts
    row = lambda b, i: (b * nt + i, 0)
    return pl.pallas_call(
        _ssd_kernel,
        grid=(bsz, nt),
        in_specs=[pl.BlockSpec((ts, SSD_WIDTH), row),
                  pl.BlockSpec((ts, SSD_XBC), row),
                  pl.BlockSpec((ts, KD_WIDTH), row),
                  _resident(cw, layer), _resident(cb, layer), _resident(dtb, layer), _resident(alog, layer),
                  _resident(dsk, layer), _resident(ng, layer)],
        out_specs=pl.BlockSpec((ts, SSD_WIDTH), row),
        out_shape=jax.ShapeDtypeStruct((bsz * seq, SSD_WIDTH), BF16),
        scratch_shapes=[pltpu.VMEM((ts + SSD_HALO, SSD_XBC), F32), pltpu.VMEM((ts, SSD_XBC), F32),
                        pltpu.VMEM((SSD_STATE, SSD_WIDTH), F32)],
        compiler_params=_params("parallel", "arbitrary"),
        name="ssd",
    )(z, xbc, kd, cw, cb, dtb, alog, dsk, ng)


def _out_proj_kernel(x_ref, a_ref, c_ref, s_ref, w_ref, o_ref):
    n0 = a_ref.shape[1]
    n1 = n0 + c_ref.shape[1]
    n2 = n1 + s_ref.shape[1]
    y = jnp.dot(a_ref[...], w_ref[0:n0, :], preferred_element_type=F32)
    y = y + jnp.dot(c_ref[...], w_ref[n0:n1, :], preferred_element_type=F32)
    y = y + jnp.dot(s_ref[...], w_ref[n1:n2, :], preferred_element_type=F32)
    o_ref[...] = x_ref[...] + y


def _out_proj(layer, x2, o_mla, o_conv, o_ssd, w, tm):
    m, d = x2.shape
    row = lambda i: (i, 0)
    return pl.pallas_call(
        _out_proj_kernel,
        grid=(m // tm,),
        in_specs=[pl.BlockSpec((tm, d), row),
                  pl.BlockSpec((tm, o_mla.shape[1]), row),
                  pl.BlockSpec((tm, o_conv.shape[1]), row),
                  pl.BlockSpec((tm, o_ssd.shape[1]), row),
                  _resident(w, layer)],
        out_specs=pl.BlockSpec((tm, d), row),
        out_shape=jax.ShapeDtypeStruct((m, d), F32),
        compiler_params=_params("parallel"),
        name="out_proj",
    )(x2, o_mla, o_conv, o_ssd, w)


def _rows(p):
    return p.astype(F32)[:, None, :]


def _pad_rows(p, lane0, width):
    return jnp.pad(p.astype(F32), ((0, 0), (lane0, width - lane0 - p.shape[1])))[:, None, :]


def _rope_tables(seq):
    half = QK_ROPE // 2
    inv_freq = ROPE_THETA ** (-jnp.arange(half, dtype=F32) / half)
    ang = jnp.arange(seq).astype(F32)[:, None] * inv_freq[None, :]
    cos, sin = jnp.cos(ang), jnp.sin(ang)
    z = lambda n: jnp.zeros((seq, n), F32)
    cos_t = jnp.concatenate([cos, cos, z(LANES - QK_ROPE)], axis=1)
    sa_t = jnp.concatenate([-sin, z(LANES - half)], axis=1)
    sb_t = jnp.concatenate([z(half), sin, z(LANES - QK_ROPE)], axis=1)
    return cos_t, sa_t, sb_t


def kernel(x, norm_g, w_in, q_a_norm, w_q_b, kv_a_norm, w_kv_b, q_norm, k_norm, conv_dw_w, conv_dw_b, conv_ln_g,
           conv_ln_b, conv_pw_w, ssd_conv_w, ssd_conv_b, ssd_dt_bias, ssd_A_log, ssd_D, ssd_norm_g, w_out):
    bsz, seq, d = x.shape
    m = bsz * seq
    x2 = x.reshape(m, d)
    tabs = _rope_tables(seq)

    w_in_k = _prep_w_in(w_in, min(256, d))
    wq, wkv, wpw = _prep_small(w_q_b, w_kv_b, conv_pw_w)
    w_out_k = _prep_cast(w_out, min(512, w_out.shape[1]))
    g_in, qan, kvan = _rows(norm_g), _rows(q_a_norm), _rows(kv_a_norm)
    qn, kn = _pad_rows(q_norm, 0, QK_PAD), _pad_rows(k_norm, 0, QK_PAD)
    wdw = jnp.pad(conv_dw_w.astype(F32), ((0, 0), (0, CONV_HALO - CONV_K), (0, 0)))
    bdw, lng, lnb = _rows(conv_dw_b), _rows(conv_ln_g), _rows(conv_ln_b)
    cw = jnp.pad(ssd_conv_w.astype(F32), ((0, 0), (0, SUBLANES - SSD_CONV_K), (0, 0)))
    cb, ng = _rows(ssd_conv_b), _rows(ssd_norm_g)
    dtb, alog = _pad_rows(ssd_dt_bias, DT_LANE0, LANES), _pad_rows(ssd_A_log, DT_LANE0, LANES)
    dsk = _rows(jnp.repeat(ssd_D, SSD_HEAD_DIM, axis=1))

    tm_in = min(256, m)
    tm_out = min(512, m)
    t_seq = min(512, seq)
    for layer in range(norm_g.shape[0]):
        cq, ckv, kd, gmla, uconv, gconv, z, xbc = _in_proj(layer, x2, g_in, w_in_k, tm_in)
        q, k, v = _mla_prep(layer, cq, ckv, kd, qan, wq, kvan, wkv, qn, kn, *tabs, bsz, seq, t_seq)
        o_mla = _attention(q, k, v, gmla, bsz, seq, t_seq)
        o_conv = _conv(layer, uconv, gconv, wdw, bdw, lng, lnb, wpw, bsz, seq, t_seq)
        o_ssd = _ssd(layer, z, xbc, kd, cw, cb, dtb, alog, dsk, ng, bsz, seq, t_seq)
        x2 = _out_proj(layer, x2, o_mla, o_conv, o_ssd, w_out_k, tm_out)
    return x2.reshape(bsz, seq, d)
```

```python
import functools
import math

import jax
import jax.numpy as jnp
from jax import lax
from jax.experimental import pallas as pl
from jax.experimental.pallas import tpu as pltpu

F32 = jnp.float32
BF16 = jnp.bfloat16

MLA_HEADS = 6
QK_NOPE = 128
QK_ROPE = 64
QK_HEAD = QK_NOPE + QK_ROPE
V_HEAD = 128
Q_LORA = 512
KV_LORA = 256
MLA_WIDTH = MLA_HEADS * V_HEAD
ROPE_THETA = 10000.0
CONV_WIDTH = 512
CONV_K = 31
SSD_HEADS = 12
SSD_HEAD_DIM = 64
SSD_WIDTH = SSD_HEADS * SSD_HEAD_DIM
SSD_GROUPS = 2
SSD_STATE = 128
SSD_CONV_K = 4
SSD_CHUNK = 128
SSD_XBC = SSD_WIDTH + 2 * SSD_GROUPS * SSD_STATE
NORM_EPS = 1e-6
LN_EPS = 1e-5

LANES = 128
SUBLANES = 8
QK_PAD = 2 * LANES
KD_WIDTH = LANES
DT_LANE0 = QK_ROPE
CONV_HALO = 32
SSD_HALO = SUBLANES
VMEM_LIMIT = 56 * 1024 * 1024
ATTN_STRIP = 64

SEG_WIDTHS = (Q_LORA, KV_LORA, KD_WIDTH, MLA_WIDTH, 2 * CONV_WIDTH, CONV_WIDTH, SSD_WIDTH, SSD_XBC)


def _sigmoid(x):
    return 1.0 / (1.0 + jnp.exp(-x))


def _silu(x):
    return x * _sigmoid(x)


def _softplus(x):
    return jnp.maximum(x, 0.0) + jnp.log1p(jnp.exp(-jnp.abs(x)))


def _rms(x, g, eps=NORM_EPS):
    ms = jnp.mean(x * x, axis=-1, keepdims=True)
    return x * lax.rsqrt(ms + eps) * g


def _params(*sem):
    return pltpu.CompilerParams(dimension_semantics=sem, vmem_limit_bytes=VMEM_LIMIT)


def _resident(stacked, layer):
    _, a, b = stacked.shape
    return pl.BlockSpec((None, a, b), lambda *_: (layer, 0, 0), pipeline_mode=pl.Buffered(1))


W_IN_HALF = LANES // 2


def _prep_w_in_kernel(a_ref, b_ref, o_ref, *, depth, n_kt, n_dt):
    j = pl.program_id(0)
    kd_tile = (Q_LORA + KV_LORA) // LANES
    stride = depth * n_kt
    row = lax.broadcasted_iota(jnp.int32, (W_IN_HALF, LANES), 0)
    keep_b = (j != kd_tile) | (row < n_dt)
    for l in range(depth):
        for kt in range(n_kt):
            xa = a_ref[pl.ds(kt * depth + l, W_IN_HALF, stride=stride), :]
            xb = b_ref[pl.ds(kt * depth + l, W_IN_HALF, stride=stride), :]
            x = jnp.concatenate([xa, jnp.where(keep_b, xb, 0.0)], axis=0)
            o_ref[l, kt * LANES:(kt + 1) * LANES, :] = x.T.astype(o_ref.dtype)


def _prep_w_in(w_in):
    depth, d, n = w_in.shape
    n_kt = d // LANES
    n_out = sum(SEG_WIDTHS)
    rows_per_group = W_IN_HALF * n_kt * depth
    view = w_in.transpose(2, 0, 1).reshape(n, depth, n_kt, LANES).transpose(0, 2, 1, 3).reshape(-1, LANES)
    kd_tile = (Q_LORA + KV_LORA) // LANES
    dt_group = (n - SSD_HEADS) // W_IN_HALF
    assert (n - SSD_HEADS) % W_IN_HALF == 0 and (Q_LORA + KV_LORA) % LANES == 0 and QK_ROPE == W_IN_HALF
    a_idx = lambda j: (jnp.where(j <= kd_tile, 2 * j, 2 * j - 1), 0)
    b_idx = lambda j: (jnp.where(j < kd_tile, 2 * j + 1, jnp.where(j == kd_tile, dt_group, 2 * j)), 0)
    return pl.pallas_call(
        functools.partial(_prep_w_in_kernel, depth=depth, n_kt=n_kt, n_dt=SSD_HEADS),
        grid=(n_out // LANES,),
        in_specs=[pl.BlockSpec((rows_per_group, LANES), a_idx),
                  pl.BlockSpec((rows_per_group, LANES), b_idx)],
        out_specs=pl.BlockSpec((depth, d, LANES), lambda j: (0, 0, j)),
        out_shape=jax.ShapeDtypeStruct((depth, d, n_out), BF16),
        compiler_params=_params("parallel"),
        name="prep_w_in",
    )(view, view)


def _prep_small_kernel(wq_ref, wkv_ref, wpw_ref, oq_ref, okv_ref, opw_ref):
    wq = wq_ref[...]
    zq = jnp.zeros((wq.shape[0], QK_PAD - QK_HEAD), F32)
    parts = []
    for h in range(MLA_HEADS):
        parts += [wq[:, h * QK_HEAD:(h + 1) * QK_HEAD], zq]
    oq_ref[...] = jnp.concatenate(parts, axis=1).astype(oq_ref.dtype)
    wkv = wkv_ref[...]
    per = QK_NOPE + V_HEAD
    k_cols = [wkv[:, h * per:h * per + QK_NOPE] for h in range(MLA_HEADS)]
    v_cols = [wkv[:, h * per + QK_NOPE:(h + 1) * per] for h in range(MLA_HEADS)]
    okv_ref[...] = jnp.concatenate(k_cols + v_cols, axis=1).astype(okv_ref.dtype)
    opw_ref[...] = wpw_ref[...].astype(opw_ref.dtype)


def _prep_small(w_q_b, w_kv_b, conv_pw_w):
    depth = w_q_b.shape[0]
    whole = lambda a: pl.BlockSpec((None,) + a.shape[1:], lambda l: (l, 0, 0))
    shapes = [(depth, Q_LORA, MLA_HEADS * QK_PAD), w_kv_b.shape, conv_pw_w.shape]
    return pl.pallas_call(
        _prep_small_kernel,
        grid=(depth,),
        in_specs=[whole(w_q_b), whole(w_kv_b), whole(conv_pw_w)],
        out_specs=[pl.BlockSpec((None,) + s[1:], lambda l: (l, 0, 0)) for s in shapes],
        out_shape=[jax.ShapeDtypeStruct(s, BF16) for s in shapes],
        compiler_params=_params("parallel"),
        name="prep_small",
    )(w_q_b, w_kv_b, conv_pw_w)


def _cast_kernel(w_ref, o_ref):
    o_ref[...] = w_ref[...].astype(o_ref.dtype)


def _prep_cast(w, tk):
    depth, k, n = w.shape
    return pl.pallas_call(
        _cast_kernel,
        grid=(depth, k // tk),
        in_specs=[pl.BlockSpec((None, tk, n), lambda l, i: (l, i, 0))],
        out_specs=pl.BlockSpec((None, tk, n), lambda l, i: (l, i, 0)),
        out_shape=jax.ShapeDtypeStruct(w.shape, BF16),
        compiler_params=_params("parallel", "parallel"),
        name="prep_cast",
    )(w)


def _in_proj_kernel(x_ref, g_ref, w_ref, *out_refs):
    h = _rms(x_ref[...], g_ref[...]).astype(BF16)
    off = 0
    for o_ref in out_refs:
        n = o_ref.shape[-1]
        o_ref[...] = jnp.dot(h, w_ref[:, off:off + n], preferred_element_type=F32).astype(o_ref.dtype)
        off += n


def _in_proj(layer, x2, g, w, tm):
    m, d = x2.shape
    assert w.shape[2] == sum(SEG_WIDTHS)
    return pl.pallas_call(
        _in_proj_kernel,
        grid=(m // tm,),
        in_specs=[pl.BlockSpec((tm, d), lambda i: (i, 0)),
                  _resident(g, layer),
                  _resident(w, layer)],
        out_specs=[pl.BlockSpec((tm, n), lambda i: (i, 0)) for n in SEG_WIDTHS],
        out_shape=[jax.ShapeDtypeStruct((m, n), F32) for n in SEG_WIDTHS],
        compiler_params=_params("parallel"),
        name="in_proj",
    )(x2, g, w)


def _mla_prep_kernel(cq_ref, ckv_ref, kd_ref, qan_ref, wq_ref, kvan_ref, wkv_ref, qn_ref, kn_ref,
                     cos_ref, sa_ref, sb_ref, q_ref, k_ref, v_ref):
    scale = math.log2(math.e) / math.sqrt(QK_HEAD)
    hq = _rms(cq_ref[...], qan_ref[...]).astype(BF16)
    qf = jnp.dot(hq, wq_ref[...], preferred_element_type=F32)
    hkv = _rms(ckv_ref[...], kvan_ref[...]).astype(BF16)
    kvf = jnp.dot(hkv, wkv_ref[...], preferred_element_type=F32)

    cos = cos_ref[...]
    sa = sa_ref[...]
    sb = sb_ref[...]

    def rope(r):
        return r * cos + pltpu.roll(r, LANES - QK_ROPE // 2, 1) * sa + pltpu.roll(r, QK_ROPE // 2, 1) * sb

    kd = kd_ref[...]
    lane = lax.broadcasted_iota(jnp.int32, kd.shape, 1)
    kpe = jnp.where(lane < QK_ROPE, kd, 0.0)
    kpe_ss = jnp.sum(kpe * kpe, axis=-1, keepdims=True)

    qn_a = qn_ref[:, 0:LANES] * scale
    qn_b = qn_ref[:, LANES:QK_PAD] * scale
    kn_w = kn_ref[...]
    kpe_rot = rope(kpe * kn_w[:, LANES:QK_PAD])
    for h in range(MLA_HEADS):
        qa = qf[:, h * QK_PAD:h * QK_PAD + LANES]
        qb = qf[:, h * QK_PAD + LANES:(h + 1) * QK_PAD]
        ss = jnp.sum(qa * qa + qb * qb, axis=-1, keepdims=True)
        inv = lax.rsqrt(ss * (1.0 / QK_HEAD) + NORM_EPS)
        q_ref[0, h, :, 0:LANES] = (qa * inv * qn_a).astype(q_ref.dtype)
        q_ref[0, h, :, LANES:QK_PAD] = rope(qb * inv * qn_b).astype(q_ref.dtype)

        ka = kvf[:, h * LANES:(h + 1) * LANES]
        ss = jnp.sum(ka * ka, axis=-1, keepdims=True) + kpe_ss
        inv = lax.rsqrt(ss * (1.0 / QK_HEAD) + NORM_EPS)
        k_ref[0, h, :, 0:LANES] = (ka * inv * kn_w[:, 0:LANES]).astype(k_ref.dtype)
        k_ref[0, h, :, LANES:QK_PAD] = (kpe_rot * inv).astype(k_ref.dtype)

        v_ref[0, h, :, :] = kvf[:, MLA_WIDTH + h * V_HEAD:MLA_WIDTH + (h + 1) * V_HEAD].astype(v_ref.dtype)


def _mla_prep(layer, cq, ckv, kd, qan, wq, kvan, wkv, qn, kn, cos_t, sa_t, sb_t, bsz, seq, tm):
    nt = seq // tm
    row = lambda b, i: (b * nt + i, 0)
    pos = lambda b, i: (i, 0)
    hd = MLA_HEADS
    return pl.pallas_call(
        _mla_prep_kernel,
        grid=(bsz, nt),
        in_specs=[pl.BlockSpec((tm, Q_LORA), row),
                  pl.BlockSpec((tm, KV_LORA), row),
                  pl.BlockSpec((tm, KD_WIDTH), row),
                  _resident(qan, layer), _resident(wq, layer), _resident(kvan, layer), _resident(wkv, layer),
                  _resident(qn, layer), _resident(kn, layer),
                  pl.BlockSpec((tm, LANES), pos), pl.BlockSpec((tm, LANES), pos), pl.BlockSpec((tm, LANES), pos)],
        out_specs=[pl.BlockSpec((1, hd, tm, QK_PAD), lambda b, i: (b, 0, i, 0)),
                   pl.BlockSpec((1, hd, tm, QK_PAD), lambda b, i: (b, 0, i, 0)),
                   pl.BlockSpec((1, hd, tm, V_HEAD), lambda b, i: (b, 0, i, 0))],
        out_shape=[jax.ShapeDtypeStruct((bsz, hd, seq, QK_PAD), BF16),
                   jax.ShapeDtypeStruct((bsz, hd, seq, QK_PAD), BF16),
                   jax.ShapeDtypeStruct((bsz, hd, seq, V_HEAD), BF16)],
        compiler_params=_params("parallel", "parallel"),
        name="mla_prep",
    )(cq, ckv, kd, qan, wq, kvan, wkv, qn, kn, cos_t, sa_t, sb_t)


def _attn_kernel(qi_ref, ki_ref, q_ref, k_ref, v_ref, g_ref, o_ref, m_sc, l_sc, acc_sc, s_sc, p_sc, a_sc):
    qi = qi_ref[pl.program_id(1)]
    ki = ki_ref[pl.program_id(1)]
    tq = q_ref.shape[2]
    tk = k_ref.shape[2]

    @pl.when(ki == 0)
    def _():
        m_sc[...] = jnp.full(m_sc.shape, -jnp.inf, F32)
        l_sc[...] = jnp.zeros(l_sc.shape, F32)
        acc_sc[...] = jnp.zeros(acc_sc.shape, F32)

    strip = min(ATTN_STRIP, tq)
    n_chunk = tk // LANES

    def scores(h):
        s_sc[h % 2] = lax.dot_general(q_ref[0, h], k_ref[0, h], (((1,), (1,)), ((), ())),
                                      preferred_element_type=F32)

    def softmax_strip(h, r0, masked):
        slot = h % 2
        rows = slice(r0, r0 + strip)
        live = [c for c in range(n_chunk) if not (masked and c * LANES >= r0 + strip)]
        chunks = []
        for c in live:
            x = s_sc[slot, rows, c * LANES:(c + 1) * LANES]
            if masked and (c + 1) * LANES - 1 > r0:
                rr = r0 + lax.broadcasted_iota(jnp.int32, (strip, LANES), 0)
                cc = c * LANES + lax.broadcasted_iota(jnp.int32, (strip, LANES), 1)
                x = jnp.where(rr >= cc, x, -jnp.inf)
            chunks.append(x)
        m_prev = m_sc[h, rows, :]
        m_loc = functools.reduce(jnp.maximum, chunks)
        m_new = jnp.maximum(m_prev, jnp.max(m_loc, axis=-1, keepdims=True))
        alpha = jnp.exp2(m_prev - m_new)
        ps = [jnp.exp2(x - m_new) for x in chunks]
        l_sc[h, rows, :] = alpha * l_sc[h, rows, :] + functools.reduce(jnp.add, ps)
        m_sc[h, rows, :] = m_new
        a_sc[slot, rows, :] = alpha
        dead = [jnp.zeros((strip, LANES), BF16)] * (n_chunk - len(live))
        p_sc[slot, rows, :] = jnp.concatenate([p.astype(BF16) for p in ps] + dead, axis=1)

    def step(masked):
        scores(0)
        for h in range(MLA_HEADS):
            if h + 1 < MLA_HEADS:
                scores(h + 1)
            for r0 in range(0, tq, strip):
                softmax_strip(h, r0, masked)
            pv = jnp.dot(p_sc[h % 2], v_ref[0, h], preferred_element_type=F32)
            acc_sc[h] = a_sc[h % 2] * acc_sc[h] + pv

    @pl.when(ki < qi)
    def _():
        step(False)

    @pl.when(ki == qi)
    def _():
        step(True)
        for h in range(MLA_HEADS):
            o = acc_sc[h] / jnp.sum(l_sc[h], axis=-1, keepdims=True)
            g = g_ref[:, h * V_HEAD:(h + 1) * V_HEAD]
            o_ref[:, h * V_HEAD:(h + 1) * V_HEAD] = (o * _silu(g)).astype(o_ref.dtype)


def _attention(q, k, v, gmla, bsz, seq, tq):
    hd = MLA_HEADS
    nq = seq // tq
    pairs = [(i, j) for i in range(nq) for j in range(i + 1)]
    qi_tab = jnp.asarray([i for i, _ in pairs], jnp.int32)
    ki_tab = jnp.asarray([j for _, j in pairs], jnp.int32)
    q_idx = lambda b, t, qi, ki: (b, 0, qi[t], 0)
    kv_idx = lambda b, t, qi, ki: (b, 0, ki[t], 0)
    row = lambda b, t, qi, ki: (b * nq + qi[t], 0)
    grid_spec = pltpu.PrefetchScalarGridSpec(
        num_scalar_prefetch=2,
        grid=(bsz, len(pairs)),
        in_specs=[pl.BlockSpec((1, hd, tq, QK_PAD), q_idx),
                  pl.BlockSpec((1, hd, tq, QK_PAD), kv_idx),
                  pl.BlockSpec((1, hd, tq, V_HEAD), kv_idx),
                  pl.BlockSpec((tq, MLA_WIDTH), row)],
        out_specs=pl.BlockSpec((tq, MLA_WIDTH), row),
        scratch_shapes=[pltpu.VMEM((hd, tq, LANES), F32), pltpu.VMEM((hd, tq, LANES), F32),
                        pltpu.VMEM((hd, tq, V_HEAD), F32),
                        pltpu.VMEM((2, tq, tq), F32), pltpu.VMEM((2, tq, tq), BF16),
                        pltpu.VMEM((2, tq, LANES), F32)])
    return pl.pallas_call(
        _attn_kernel,
        grid_spec=grid_spec,
        out_shape=jax.ShapeDtypeStruct((bsz * seq, MLA_WIDTH), BF16),
        compiler_params=_params("parallel", "arbitrary"),
        name="attention",
    )(qi_tab, ki_tab, q, k, v, gmla)


def _conv_kernel(u_ref, gc_ref, wdw_ref, bdw_ref, lng_ref, lnb_ref, wpw_ref, o_ref, hbuf, ybuf, shbuf, *, strip):
    i = pl.program_id(1)
    tt = o_ref.shape[0]
    cw = CONV_WIDTH

    @pl.when(i == 0)
    def _():
        hbuf[0:CONV_HALO, :] = jnp.zeros((CONV_HALO, cw), F32)

    @pl.when(i > 0)
    def _():
        hbuf[0:CONV_HALO, :] = hbuf[tt:tt + CONV_HALO, :]

    hbuf[CONV_HALO:CONV_HALO + tt, :] = u_ref[:, 0:cw] * _sigmoid(u_ref[:, cw:2 * cw])

    base = CONV_HALO - (CONV_K - 1)

    span = tt + CONV_HALO - SUBLANES
    for res in range(1, SUBLANES):
        shbuf[res - 1, 0:span, :] = hbuf[res:res + span, :]

    def do_strip(s, carry):
        r0 = pl.multiple_of(s * strip, strip)
        acc = jnp.broadcast_to(bdw_ref[...], (strip, cw))
        for kk in range(CONV_K):
            res = (base + kk) % SUBLANES
            start = r0 + (base + kk - res)
            tap = hbuf[pl.ds(start, strip), :] if res == 0 else shbuf[res - 1, pl.ds(start, strip), :]
            acc = acc + wdw_ref[kk:kk + 1, :] * tap
        ybuf[pl.ds(r0, strip), :] = acc
        return carry

    lax.fori_loop(0, tt // strip, do_strip, 0)

    y = ybuf[...]
    mu = jnp.mean(y, axis=-1, keepdims=True)
    yc = y - mu
    var = jnp.mean(yc * yc, axis=-1, keepdims=True)
    hn = _silu(yc * lax.rsqrt(var + LN_EPS) * lng_ref[...] + lnb_ref[...])
    out = jnp.dot(hn.astype(BF16), wpw_ref[...], preferred_element_type=F32)
    o_ref[...] = (out * _silu(gc_ref[...])).astype(o_ref.dtype)


def _conv(layer, uconv, gconv, wdw, bdw, lng, lnb, wpw, bsz, seq, tt):
    nt = seq // tt
    row = lambda b, i: (b * nt + i, 0)
    strip = min(32, tt)
    return pl.pallas_call(
        functools.partial(_conv_kernel, strip=strip),
        grid=(bsz, nt),
        in_specs=[pl.BlockSpec((tt, 2 * CONV_WIDTH), row),
                  pl.BlockSpec((tt, CONV_WIDTH), row),
                  _resident(wdw, layer), _resident(bdw, layer), _resident(lng, layer), _resident(lnb, layer),
                  _resident(wpw, layer)],
        out_specs=pl.BlockSpec((tt, CONV_WIDTH), row),
        out_shape=jax.ShapeDtypeStruct((bsz * seq, CONV_WIDTH), BF16),
        scratch_shapes=[pltpu.VMEM((tt + CONV_HALO, CONV_WIDTH), F32), pltpu.VMEM((tt, CONV_WIDTH), F32),
                        pltpu.VMEM((SUBLANES - 1, tt + CONV_HALO, CONV_WIDTH), F32)],
        compiler_params=_params("parallel", "arbitrary"),
        name="conformer_conv",
    )(uconv, gconv, wdw, bdw, lng, lnb, wpw)


def _split3(x):
    hi = x.astype(BF16)
    r1 = x - hi.astype(F32)
    mid = r1.astype(BF16)
    lo = (r1 - mid.astype(F32)).astype(BF16)
    return hi, mid, lo


def _ssd_kernel(z_ref, xbc_ref, kd_ref, cw_ref, cb_ref, dtb_ref, alog_ref, dsk_ref, ng_ref, o_ref,
                xbuf, xact, state):
    i = pl.program_id(1)
    ts = o_ref.shape[0]
    L = SSD_CHUNK
    hp = SSD_HEAD_DIM
    gw = SSD_WIDTH // SSD_GROUPS
    heads_per_group = SSD_HEADS // SSD_GROUPS

    @pl.when(i == 0)
    def _():
        xbuf[0:SSD_HALO, :] = jnp.zeros((SSD_HALO, SSD_XBC), F32)
        state[...] = jnp.zeros(state.shape, F32)

    @pl.when(i > 0)
    def _():
        xbuf[0:SSD_HALO, :] = xbuf[ts:ts + SSD_HALO, :]

    xbuf[SSD_HALO:SSD_HALO + ts, :] = xbc_ref[...]
    base = SSD_HALO - (SSD_CONV_K - 1)
    acc = jnp.broadcast_to(cb_ref[...], (ts, SSD_XBC))
    for kk in range(SSD_CONV_K):
        acc = acc + cw_ref[kk:kk + 1, :] * xbuf[base + kk:base + kk + ts, :]
    xact[...] = _silu(acc)

    lane = lax.broadcasted_iota(jnp.int32, (1, LANES), 1)
    is_dt = (lane >= DT_LANE0) & (lane < DT_LANE0 + SSD_HEADS)
    a_neg = jnp.where(is_dt, -jnp.exp(alog_ref[...]) * math.log2(math.e), 0.0)
    rr = lax.broadcasted_iota(jnp.int32, (L, L), 0)
    cc = lax.broadcasted_iota(jnp.int32, (L, L), 1)
    causal = rr >= cc
    tri = causal.astype(BF16)
    left = lax.broadcasted_iota(jnp.int32, (L, LANES), 1) < hp

    def chunk(c, carry):
        r0 = pl.multiple_of(c * L, L)
        xs = xact[pl.ds(r0, L), 0:SSD_WIDTH]
        dt = _softplus(kd_ref[pl.ds(r0, L), :] + dtb_ref[...])
        dta = dt * a_neg
        hi, mid, lo = _split3(dta)
        acs = (jnp.dot(tri, hi, preferred_element_type=F32)
               + jnp.dot(tri, mid, preferred_element_type=F32)
               + jnp.dot(tri, lo, preferred_element_type=F32))
        acs_t = acs.T

        def col(a, h):
            return jnp.broadcast_to(a[:, DT_LANE0 + h:DT_LANE0 + h + 1], (L, LANES))

        dt_cols = [col(dt, h) for h in range(SSD_HEADS)]
        acs_cols = [col(acs, h) for h in range(SSD_HEADS)]
        npair = SSD_HEADS // 2
        dt_exp = jnp.concatenate([jnp.where(left, dt_cols[2 * j], dt_cols[2 * j + 1]) for j in range(npair)], axis=1)
        acs_exp = jnp.concatenate([jnp.where(left, acs_cols[2 * j], acs_cols[2 * j + 1]) for j in range(npair)], axis=1)
        last = acs_exp[L - 1:L, :]
        xdt = xs * dt_exp
        xdt_b = xdt.astype(BF16)
        xdt_end = (xdt * jnp.exp2(last - acs_exp)).astype(BF16)
        grow = jnp.exp2(acs_exp)
        st_prev = state[...]
        st_prev_b = st_prev.astype(BF16)

        y_parts = []
        st_parts = []
        for g in range(SSD_GROUPS):
            bg = xact[pl.ds(r0, L), SSD_WIDTH + g * SSD_STATE:SSD_WIDTH + (g + 1) * SSD_STATE]
            cg = xact[pl.ds(r0, L), SSD_WIDTH + (SSD_GROUPS + g) * SSD_STATE:SSD_WIDTH + (SSD_GROUPS + g + 1) * SSD_STATE]
            bg_b = bg.astype(BF16)
            cg_b = cg.astype(BF16)
            cb = lax.dot_general(cg_b, bg_b, (((1,), (1,)), ((), ())), preferred_element_type=F32)
            st_parts.append(jnp.dot(bg.T.astype(BF16), xdt_end[:, g * gw:(g + 1) * gw],
                                    preferred_element_type=F32))
            y_off = jnp.dot(cg_b, st_prev_b[:, g * gw:(g + 1) * gw], preferred_element_type=F32)
            y_off = y_off * grow[:, g * gw:(g + 1) * gw]
            for jj in range(heads_per_group // 2):
                j = g * (heads_per_group // 2) + jj
                pair = xdt_b[:, j * LANES:(j + 1) * LANES]
                yd = []
                for h in (2 * j, 2 * j + 1):
                    seg = acs_cols[h] - acs_t[DT_LANE0 + h:DT_LANE0 + h + 1, :]
                    decay = jnp.exp2(jnp.where(causal, seg, -jnp.inf))
                    yd.append(jnp.dot((cb * decay).astype(BF16), pair, preferred_element_type=F32))
                y_parts.append(jnp.where(left, yd[0], yd[1]) + y_off[:, jj * LANES:(jj + 1) * LANES])

        y = jnp.concatenate(y_parts, axis=1) + dsk_ref[...] * xs
        state[...] = st_prev * jnp.exp2(last) + jnp.concatenate(st_parts, axis=1)
        yz = y * _silu(z_ref[pl.ds(r0, L), :])
        o_ref[pl.ds(r0, L), :] = _rms(yz, ng_ref[...]).astype(o_ref.dtype)
        return carry

    lax.fori_loop(0, ts // L, chunk, 0)


def _ssd(layer, z, xbc, kd, cw, cb, dtb, alog, dsk, ng, bsz, seq, ts):
    nt = seq // ts
    row = lambda b, i: (b * nt + i, 0)
    return pl.pallas_call(
        _ssd_kernel,
        grid=(bsz, nt),
        in_specs=[pl.BlockSpec((ts, SSD_WIDTH), row),
                  pl.BlockSpec((ts, SSD_XBC), row),
                  pl.BlockSpec((ts, KD_WIDTH), row),
                  _resident(cw, layer), _resident(cb, layer), _resident(dtb, layer), _resident(alog, layer),
                  _resident(dsk, layer), _resident(ng, layer)],
        out_specs=pl.BlockSpec((ts, SSD_WIDTH), row),
        out_shape=jax.ShapeDtypeStruct((bsz * seq, SSD_WIDTH), BF16),
        scratch_shapes=[pltpu.VMEM((ts + SSD_HALO, SSD_XBC), F32), pltpu.VMEM((ts, SSD_XBC), F32),
                        pltpu.VMEM((SSD_STATE, SSD_WIDTH), F32)],
        compiler_params=_params("parallel", "arbitrary"),
        name="ssd",
    )(z, xbc, kd, cw, cb, dtb, alog, dsk, ng)


def _out_proj_kernel(x_ref, a_ref, c_ref, s_ref, w_ref, o_ref):
    n0 = a_ref.shape[1]
    n1 = n0 + c_ref.shape[1]
    n2 = n1 + s_ref.shape[1]
    y = jnp.dot(a_ref[...], w_ref[0:n0, :], preferred_element_type=F32)
    y = y + jnp.dot(c_ref[...], w_ref[n0:n1, :], preferred_element_type=F32)
    y = y + jnp.dot(s_ref[...], w_ref[n1:n2, :], preferred_element_type=F32)
    o_ref[...] = x_ref[...] + y


def _out_proj(layer, x2, o_mla, o_conv, o_ssd, w, tm):
    m, d = x2.shape
    row = lambda i: (i, 0)
    return pl.pallas_call(
        _out_proj_kernel,
        grid=(m // tm,),
        in_specs=[pl.BlockSpec((tm, d), row),
                  pl.BlockSpec((tm, o_mla.shape[1]), row),
                  pl.BlockSpec((tm, o_conv.shape[1]), row),
                  pl.BlockSpec((tm, o_ssd.shape[1]), row),
                  _resident(w, layer)],
        out_specs=pl.BlockSpec((tm, d), row),
        out_shape=jax.ShapeDtypeStruct((m, d), F32),
        compiler_params=_params("parallel"),
        name="out_proj",
    )(x2, o_mla, o_conv, o_ssd, w)


def _rows(p):
    return p.astype(F32)[:, None, :]


def _pad_rows(p, lane0, width):
    return jnp.pad(p.astype(F32), ((0, 0), (lane0, width - lane0 - p.shape[1])))[:, None, :]


def _rope_tables(seq):
    half = QK_ROPE // 2
    inv_freq = ROPE_THETA ** (-jnp.arange(half, dtype=F32) / half)
    ang = jnp.arange(seq).astype(F32)[:, None] * inv_freq[None, :]
    cos, sin = jnp.cos(ang), jnp.sin(ang)
    z = lambda n: jnp.zeros((seq, n), F32)
    cos_t = jnp.concatenate([cos, cos, z(LANES - QK_ROPE)], axis=1)
    sa_t = jnp.concatenate([-sin, z(LANES - half)], axis=1)
    sb_t = jnp.concatenate([z(half), sin, z(LANES - QK_ROPE)], axis=1)
    return cos_t, sa_t, sb_t


def kernel(x, norm_g, w_in, q_a_norm, w_q_b, kv_a_norm, w_kv_b, q_norm, k_norm, conv_dw_w, conv_dw_b, conv_ln_g,
           conv_ln_b, conv_pw_w, ssd_conv_w, ssd_conv_b, ssd_dt_bias, ssd_A_log, ssd_D, ssd_norm_g, w_out):
    bsz, seq, d = x.shape
    m = bsz * seq
    x2 = x.reshape(m, d)
    tabs = _rope_tables(seq)

    w_in_k = _prep_w_in(w_in)
    wq, wkv, wpw = _prep_small(w_q_b, w_kv_b, conv_pw_w)
    w_out_k = _prep_cast(w_out, min(512, w_out.shape[1]))
    g_in, qan, kvan = _rows(norm_g), _rows(q_a_norm), _rows(kv_a_norm)
    qn, kn = _pad_rows(q_norm, 0, QK_PAD), _pad_rows(k_norm, 0, QK_PAD)
    wdw = jnp.pad(conv_dw_w.astype(F32), ((0, 0), (0, CONV_HALO - CONV_K), (0, 0)))
    bdw, lng, lnb = _rows(conv_dw_b), _rows(conv_ln_g), _rows(conv_ln_b)
    cw = jnp.pad(ssd_conv_w.astype(F32), ((0, 0), (0, SUBLANES - SSD_CONV_K), (0, 0)))
    cb, ng = _rows(ssd_conv_b), _rows(ssd_norm_g)
    dtb, alog = _pad_rows(ssd_dt_bias, DT_LANE0, LANES), _pad_rows(ssd_A_log, DT_LANE0, LANES)
    dsk = _rows(jnp.repeat(ssd_D, SSD_HEAD_DIM, axis=1))

    tm_in = min(256, m)
    tm_out = min(512, m)
    t_seq = min(512, seq)
    for layer in range(norm_g.shape[0]):
        cq, ckv, kd, gmla, uconv, gconv, z, xbc = _in_proj(layer, x2, g_in, w_in_k, tm_in)
        q, k, v = _mla_prep(layer, cq, ckv, kd, qan, wq, kvan, wkv, qn, kn, *tabs, bsz, seq, t_seq)
        o_mla = _attention(q, k, v, gmla, bsz, seq, t_seq)
        o_conv = _conv(layer, uconv, gconv, wdw, bdw, lng, lnb, wpw, bsz, seq, t_seq)
        o_ssd = _ssd(layer, z, xbc, kd, cw, cb, dtb, alog, dsk, ng, bsz, seq, t_seq)
        x2 = _out_proj(layer, x2, o_mla, o_conv, o_ssd, w_out_k, tm_out)
    return x2.reshape(bsz, seq, d)
```

```python
import functools
import math

import jax
import jax.numpy as jnp
from jax import lax
from jax.experimental import pallas as pl
from jax.experimental.pallas import tpu as pltpu

F32 = jnp.float32
BF16 = jnp.bfloat16

MLA_HEADS = 6
QK_NOPE = 128
QK_ROPE = 64
QK_HEAD = QK_NOPE + QK_ROPE
V_HEAD = 128
Q_LORA = 512
KV_LORA = 256
MLA_WIDTH = MLA_HEADS * V_HEAD
ROPE_THETA = 10000.0
CONV_WIDTH = 512
CONV_K = 31
SSD_HEADS = 12
SSD_HEAD_DIM = 64
SSD_WIDTH = SSD_HEADS * SSD_HEAD_DIM
SSD_GROUPS = 2
SSD_STATE = 128
SSD_CONV_K = 4
SSD_CHUNK = 128
SSD_XBC = SSD_WIDTH + 2 * SSD_GROUPS * SSD_STATE
NORM_EPS = 1e-6
LN_EPS = 1e-5

LANES = 128
SUBLANES = 8
QK_PAD = 2 * LANES
KD_WIDTH = LANES
DT_LANE0 = QK_ROPE
CONV_HALO = 32
SSD_HALO = SUBLANES
VMEM_LIMIT = 56 * 1024 * 1024
ATTN_STRIP = 64

SEG_WIDTHS = (Q_LORA, KV_LORA, KD_WIDTH, MLA_WIDTH, 2 * CONV_WIDTH, CONV_WIDTH, SSD_WIDTH, SSD_XBC)


def _sigmoid(x):
    return 1.0 / (1.0 + jnp.exp(-x))


def _silu(x):
    return x * _sigmoid(x)


def _softplus(x):
    return jnp.maximum(x, 0.0) + jnp.log1p(jnp.exp(-jnp.abs(x)))


def _rms(x, g, eps=NORM_EPS):
    ms = jnp.mean(x * x, axis=-1, keepdims=True)
    return x * lax.rsqrt(ms + eps) * g


def _params(*sem):
    return pltpu.CompilerParams(dimension_semantics=sem, vmem_limit_bytes=VMEM_LIMIT)


def _resident(stacked, layer):
    _, a, b = stacked.shape
    return pl.BlockSpec((None, a, b), lambda *_: (layer, 0, 0), pipeline_mode=pl.Buffered(1))


W_IN_HALF = LANES // 2


def _prep_w_in_kernel(a_ref, b_ref, o_ref, *, depth, n_kt, n_dt):
    j = pl.program_id(0)
    kd_tile = (Q_LORA + KV_LORA) // LANES
    stride = depth * n_kt
    row = lax.broadcasted_iota(jnp.int32, (W_IN_HALF, LANES), 0)
    keep_b = (j != kd_tile) | (row < n_dt)
    for l in range(depth):
        for kt in range(n_kt):
            xa = a_ref[pl.ds(kt * depth + l, W_IN_HALF, stride=stride), :]
            xb = b_ref[pl.ds(kt * depth + l, W_IN_HALF, stride=stride), :]
            x = jnp.concatenate([xa, jnp.where(keep_b, xb, 0.0)], axis=0)
            o_ref[l, kt * LANES:(kt + 1) * LANES, :] = x.T.astype(o_ref.dtype)


def _prep_w_in(w_in):
    depth, d, n = w_in.shape
    n_kt = d // LANES
    n_out = sum(SEG_WIDTHS)
    rows_per_group = W_IN_HALF * n_kt * depth
    view = w_in.transpose(2, 0, 1).reshape(n, depth, n_kt, LANES).transpose(0, 2, 1, 3).reshape(-1, LANES)
    kd_tile = (Q_LORA + KV_LORA) // LANES
    dt_group = (n - SSD_HEADS) // W_IN_HALF
    assert (n - SSD_HEADS) % W_IN_HALF == 0 and (Q_LORA + KV_LORA) % LANES == 0 and QK_ROPE == W_IN_HALF
    a_idx = lambda j: (jnp.where(j <= kd_tile, 2 * j, 2 * j - 1), 0)
    b_idx = lambda j: (jnp.where(j < kd_tile, 2 * j + 1, jnp.where(j == kd_tile, dt_group, 2 * j)), 0)
    return pl.pallas_call(
        functools.partial(_prep_w_in_kernel, depth=depth, n_kt=n_kt, n_dt=SSD_HEADS),
        grid=(n_out // LANES,),
        in_specs=[pl.BlockSpec((rows_per_group, LANES), a_idx),
                  pl.BlockSpec((rows_per_group, LANES), b_idx)],
        out_specs=pl.BlockSpec((depth, d, LANES), lambda j: (0, 0, j)),
        out_shape=jax.ShapeDtypeStruct((depth, d, n_out), BF16),
        compiler_params=_params("parallel"),
        name="prep_w_in",
    )(view, view)


def _prep_small_kernel(wq_ref, wkv_ref, wpw_ref, oq_ref, okv_ref, opw_ref):
    wq = wq_ref[...]
    zq = jnp.zeros((wq.shape[0], QK_PAD - QK_HEAD), F32)
    parts = []
    for h in range(MLA_HEADS):
        parts += [wq[:, h * QK_HEAD:(h + 1) * QK_HEAD], zq]
    oq_ref[...] = jnp.concatenate(parts, axis=1).astype(oq_ref.dtype)
    wkv = wkv_ref[...]
    per = QK_NOPE + V_HEAD
    k_cols = [wkv[:, h * per:h * per + QK_NOPE] for h in range(MLA_HEADS)]
    v_cols = [wkv[:, h * per + QK_NOPE:(h + 1) * per] for h in range(MLA_HEADS)]
    okv_ref[...] = jnp.concatenate(k_cols + v_cols, axis=1).astype(okv_ref.dtype)
    opw_ref[...] = wpw_ref[...].astype(opw_ref.dtype)


def _prep_small(w_q_b, w_kv_b, conv_pw_w):
    depth = w_q_b.shape[0]
    whole = lambda a: pl.BlockSpec((None,) + a.shape[1:], lambda l: (l, 0, 0))
    shapes = [(depth, Q_LORA, MLA_HEADS * QK_PAD), w_kv_b.shape, conv_pw_w.shape]
    return pl.pallas_call(
        _prep_small_kernel,
        grid=(depth,),
        in_specs=[whole(w_q_b), whole(w_kv_b), whole(conv_pw_w)],
        out_specs=[pl.BlockSpec((None,) + s[1:], lambda l: (l, 0, 0)) for s in shapes],
        out_shape=[jax.ShapeDtypeStruct(s, BF16) for s in shapes],
        compiler_params=_params("parallel"),
        name="prep_small",
    )(w_q_b, w_kv_b, conv_pw_w)


def _cast_kernel(w_ref, o_ref):
    o_ref[...] = w_ref[...].astype(o_ref.dtype)


def _prep_cast(w, tk):
    depth, k, n = w.shape
    return pl.pallas_call(
        _cast_kernel,
        grid=(depth, k // tk),
        in_specs=[pl.BlockSpec((None, tk, n), lambda l, i: (l, i, 0))],
        out_specs=pl.BlockSpec((None, tk, n), lambda l, i: (l, i, 0)),
        out_shape=jax.ShapeDtypeStruct(w.shape, BF16),
        compiler_params=_params("parallel", "parallel"),
        name="prep_cast",
    )(w)


def _in_proj_kernel(x_ref, g_ref, w_ref, *out_refs):
    h = _rms(x_ref[...], g_ref[...]).astype(BF16)
    off = 0
    for o_ref in out_refs:
        n = o_ref.shape[-1]
        o_ref[...] = jnp.dot(h, w_ref[:, off:off + n], preferred_element_type=F32).astype(o_ref.dtype)
        off += n


def _in_proj(layer, x2, g, w, tm):
    m, d = x2.shape
    assert w.shape[2] == sum(SEG_WIDTHS)
    return pl.pallas_call(
        _in_proj_kernel,
        grid=(m // tm,),
        in_specs=[pl.BlockSpec((tm, d), lambda i: (i, 0)),
                  _resident(g, layer),
                  _resident(w, layer)],
        out_specs=[pl.BlockSpec((tm, n), lambda i: (i, 0)) for n in SEG_WIDTHS],
        out_shape=[jax.ShapeDtypeStruct((m, n), F32) for n in SEG_WIDTHS],
        compiler_params=_params("parallel"),
        name="in_proj",
    )(x2, g, w)


def _mla_prep_kernel(cq_ref, ckv_ref, kd_ref, qan_ref, wq_ref, kvan_ref, wkv_ref, qn_ref, kn_ref,
                     cos_ref, sa_ref, sb_ref, q_ref, k_ref, v_ref):
    scale = math.log2(math.e) / math.sqrt(QK_HEAD)
    hq = _rms(cq_ref[...], qan_ref[...]).astype(BF16)
    qf = jnp.dot(hq, wq_ref[...], preferred_element_type=F32)
    hkv = _rms(ckv_ref[...], kvan_ref[...]).astype(BF16)
    kvf = jnp.dot(hkv, wkv_ref[...], preferred_element_type=F32)

    cos = cos_ref[...]
    sa = sa_ref[...]
    sb = sb_ref[...]

    def rope(r):
        return r * cos + pltpu.roll(r, LANES - QK_ROPE // 2, 1) * sa + pltpu.roll(r, QK_ROPE // 2, 1) * sb

    kd = kd_ref[...]
    lane = lax.broadcasted_iota(jnp.int32, kd.shape, 1)
    kpe = jnp.where(lane < QK_ROPE, kd, 0.0)
    kpe_ss = jnp.sum(kpe * kpe, axis=-1, keepdims=True)

    qn_a = qn_ref[:, 0:LANES] * scale
    qn_b = qn_ref[:, LANES:QK_PAD] * scale
    kn_w = kn_ref[...]
    kpe_rot = rope(kpe * kn_w[:, LANES:QK_PAD])
    for h in range(MLA_HEADS):
        qa = qf[:, h * QK_PAD:h * QK_PAD + LANES]
        qb = qf[:, h * QK_PAD + LANES:(h + 1) * QK_PAD]
        ss = jnp.sum(qa * qa + qb * qb, axis=-1, keepdims=True)
        inv = lax.rsqrt(ss * (1.0 / QK_HEAD) + NORM_EPS)
        q_ref[0, h, :, 0:LANES] = (qa * inv * qn_a).astype(q_ref.dtype)
        q_ref[0, h, :, LANES:QK_PAD] = rope(qb * inv * qn_b).astype(q_ref.dtype)

        ka = kvf[:, h * LANES:(h + 1) * LANES]
        ss = jnp.sum(ka * ka, axis=-1, keepdims=True) + kpe_ss
        inv = lax.rsqrt(ss * (1.0 / QK_HEAD) + NORM_EPS)
        k_ref[0, h, :, 0:LANES] = (ka * inv * kn_w[:, 0:LANES]).astype(k_ref.dtype)
        k_ref[0, h, :, LANES:QK_PAD] = (kpe_rot * inv).astype(k_ref.dtype)

        v_ref[0, h, :, :] = kvf[:, MLA_WIDTH + h * V_HEAD:MLA_WIDTH + (h + 1) * V_HEAD].astype(v_ref.dtype)


def _mla_prep(layer, cq, ckv, kd, qan, wq, kvan, wkv, qn, kn, cos_t, sa_t, sb_t, bsz, seq, tm):
    nt = seq // tm
    row = lambda b, i: (b * nt + i, 0)
    pos = lambda b, i: (i, 0)
    hd = MLA_HEADS
    return pl.pallas_call(
        _mla_prep_kernel,
        grid=(bsz, nt),
        in_specs=[pl.BlockSpec((tm, Q_LORA), row),
                  pl.BlockSpec((tm, KV_LORA), row),
                  pl.BlockSpec((tm, KD_WIDTH), row),
                  _resident(qan, layer), _resident(wq, layer), _resident(kvan, layer), _resident(wkv, layer),
                  _resident(qn, layer), _resident(kn, layer),
                  pl.BlockSpec((tm, LANES), pos), pl.BlockSpec((tm, LANES), pos), pl.BlockSpec((tm, LANES), pos)],
        out_specs=[pl.BlockSpec((1, hd, tm, QK_PAD), lambda b, i: (b, 0, i, 0)),
                   pl.BlockSpec((1, hd, tm, QK_PAD), lambda b, i: (b, 0, i, 0)),
                   pl.BlockSpec((1, hd, tm, V_HEAD), lambda b, i: (b, 0, i, 0))],
        out_shape=[jax.ShapeDtypeStruct((bsz, hd, seq, QK_PAD), BF16),
                   jax.ShapeDtypeStruct((bsz, hd, seq, QK_PAD), BF16),
                   jax.ShapeDtypeStruct((bsz, hd, seq, V_HEAD), BF16)],
        compiler_params=_params("parallel", "parallel"),
        name="mla_prep",
    )(cq, ckv, kd, qan, wq, kvan, wkv, qn, kn, cos_t, sa_t, sb_t)


def _attn_kernel(qi_ref, ki_ref, q_ref, k_ref, v_ref, g_ref, o_ref, m_sc, l_sc, acc_sc, s_sc, p_sc, a_sc):
    qi = qi_ref[pl.program_id(1)]
    ki = ki_ref[pl.program_id(1)]
    tq = q_ref.shape[2]
    tk = k_ref.shape[2]

    @pl.when(ki == 0)
    def _():
        m_sc[...] = jnp.full(m_sc.shape, -jnp.inf, F32)
        l_sc[...] = jnp.zeros(l_sc.shape, F32)
        acc_sc[...] = jnp.zeros(acc_sc.shape, F32)

    strip = min(ATTN_STRIP, tq)
    n_chunk = tk // LANES

    def scores(h):
        s_sc[h % 2] = lax.dot_general(q_ref[0, h], k_ref[0, h], (((1,), (1,)), ((), ())),
                                      preferred_element_type=F32)

    def softmax_strip(h, r0, masked):
        slot = h % 2
        rows = slice(r0, r0 + strip)
        live = [c for c in range(n_chunk) if not (masked and c * LANES >= r0 + strip)]
        chunks = []
        for c in live:
            x = s_sc[slot, rows, c * LANES:(c + 1) * LANES]
            if masked and (c + 1) * LANES - 1 > r0:
                rr = r0 + lax.broadcasted_iota(jnp.int32, (strip, LANES), 0)
                cc = c * LANES + lax.broadcasted_iota(jnp.int32, (strip, LANES), 1)
                x = jnp.where(rr >= cc, x, -jnp.inf)
            chunks.append(x)
        m_prev = m_sc[h, rows, :]
        m_loc = functools.reduce(jnp.maximum, chunks)
        m_new = jnp.maximum(m_prev, jnp.max(m_loc, axis=-1, keepdims=True))
        alpha = jnp.exp2(m_prev - m_new)
        ps = [jnp.exp2(x - m_new) for x in chunks]
        l_sc[h, rows, :] = alpha * l_sc[h, rows, :] + functools.reduce(jnp.add, ps)
        m_sc[h, rows, :] = m_new
        a_sc[slot, rows, :] = alpha
        dead = [jnp.zeros((strip, LANES), BF16)] * (n_chunk - len(live))
        p_sc[slot, rows, :] = jnp.concatenate([p.astype(BF16) for p in ps] + dead, axis=1)

    def step(masked):
        scores(0)
        for h in range(MLA_HEADS):
            if h + 1 < MLA_HEADS:
                scores(h + 1)
            for r0 in range(0, tq, strip):
                softmax_strip(h, r0, masked)
            pv = jnp.dot(p_sc[h % 2], v_ref[0, h], preferred_element_type=F32)
            acc_sc[h] = a_sc[h % 2] * acc_sc[h] + pv

    @pl.when(ki < qi)
    def _():
        step(False)

    @pl.when(ki == qi)
    def _():
        step(True)
        for h in range(MLA_HEADS):
            o = acc_sc[h] / jnp.sum(l_sc[h], axis=-1, keepdims=True)
            g = g_ref[:, h * V_HEAD:(h + 1) * V_HEAD]
            o_ref[:, h * V_HEAD:(h + 1) * V_HEAD] = (o * _silu(g)).astype(o_ref.dtype)


def _attention(q, k, v, gmla, bsz, seq, tq):
    hd = MLA_HEADS
    nq = seq // tq
    pairs = [(i, j) for i in range(nq) for j in range(i + 1)]
    qi_tab = jnp.asarray([i for i, _ in pairs], jnp.int32)
    ki_tab = jnp.asarray([j for _, j in pairs], jnp.int32)
    q_idx = lambda b, t, qi, ki: (b, 0, qi[t], 0)
    kv_idx = lambda b, t, qi, ki: (b, 0, ki[t], 0)
    row = lambda b, t, qi, ki: (b * nq + qi[t], 0)
    grid_spec = pltpu.PrefetchScalarGridSpec(
        num_scalar_prefetch=2,
        grid=(bsz, len(pairs)),
        in_specs=[pl.BlockSpec((1, hd, tq, QK_PAD), q_idx),
                  pl.BlockSpec((1, hd, tq, QK_PAD), kv_idx),
                  pl.BlockSpec((1, hd, tq, V_HEAD), kv_idx),
                  pl.BlockSpec((tq, MLA_WIDTH), row)],
        out_specs=pl.BlockSpec((tq, MLA_WIDTH), row),
        scratch_shapes=[pltpu.VMEM((hd, tq, LANES), F32), pltpu.VMEM((hd, tq, LANES), F32),
                        pltpu.VMEM((hd, tq, V_HEAD), F32),
                        pltpu.VMEM((2, tq, tq), F32), pltpu.VMEM((2, tq, tq), BF16),
                        pltpu.VMEM((2, tq, LANES), F32)])
    return pl.pallas_call(
        _attn_kernel,
        grid_spec=grid_spec,
        out_shape=jax.ShapeDtypeStruct((bsz * seq, MLA_WIDTH), BF16),
        compiler_params=_params("parallel", "arbitrary"),
        name="attention",
    )(qi_tab, ki_tab, q, k, v, gmla)


def _conv_kernel(u_ref, gc_ref, wdw_ref, bdw_ref, lng_ref, lnb_ref, wpw_ref, o_ref, hbuf, ybuf, shbuf, *, strip):
    i = pl.program_id(1)
    tt = o_ref.shape[0]
    cw = CONV_WIDTH

    @pl.when(i == 0)
    def _():
        hbuf[0:CONV_HALO, :] = jnp.zeros((CONV_HALO, cw), F32)

    @pl.when(i > 0)
    def _():
        hbuf[0:CONV_HALO, :] = hbuf[tt:tt + CONV_HALO, :]

    hbuf[CONV_HALO:CONV_HALO + tt, :] = u_ref[:, 0:cw] * _sigmoid(u_ref[:, cw:2 * cw])

    base = CONV_HALO - (CONV_K - 1)

    span = tt + CONV_HALO - SUBLANES
    for res in range(1, SUBLANES):
        shbuf[res - 1, 0:span, :] = hbuf[res:res + span, :]

    def do_strip(s, carry):
        r0 = pl.multiple_of(s * strip, strip)
        acc = jnp.broadcast_to(bdw_ref[...], (strip, cw))
        for kk in range(CONV_K):
            res = (base + kk) % SUBLANES
            start = r0 + (base + kk - res)
            tap = hbuf[pl.ds(start, strip), :] if res == 0 else shbuf[res - 1, pl.ds(start, strip), :]
            acc = acc + wdw_ref[kk:kk + 1, :] * tap
        ybuf[pl.ds(r0, strip), :] = acc
        return carry

    lax.fori_loop(0, tt // strip, do_strip, 0)

    y = ybuf[...]
    mu = jnp.mean(y, axis=-1, keepdims=True)
    yc = y - mu
    var = jnp.mean(yc * yc, axis=-1, keepdims=True)
    hn = _silu(yc * lax.rsqrt(var + LN_EPS) * lng_ref[...] + lnb_ref[...])
    out = jnp.dot(hn.astype(BF16), wpw_ref[...], preferred_element_type=F32)
    o_ref[...] = (out * _silu(gc_ref[...])).astype(o_ref.dtype)


def _conv(layer, uconv, gconv, wdw, bdw, lng, lnb, wpw, bsz, seq, tt):
    nt = seq // tt
    row = lambda b, i: (b * nt + i, 0)
    strip = min(32, tt)
    return pl.pallas_call(
        functools.partial(_conv_kernel, strip=strip),
        grid=(bsz, nt),
        in_specs=[pl.BlockSpec((tt, 2 * CONV_WIDTH), row),
                  pl.BlockSpec((tt, CONV_WIDTH), row),
                  _resident(wdw, layer), _resident(bdw, layer), _resident(lng, layer), _resident(lnb, layer),
                  _resident(wpw, layer)],
        out_specs=pl.BlockSpec((tt, CONV_WIDTH), row),
        out_shape=jax.ShapeDtypeStruct((bsz * seq, CONV_WIDTH), BF16),
        scratch_shapes=[pltpu.VMEM((tt + CONV_HALO, CONV_WIDTH), F32), pltpu.VMEM((tt, CONV_WIDTH), F32),
                        pltpu.VMEM((SUBLANES - 1, tt + CONV_HALO, CONV_WIDTH), F32)],
        compiler_params=_params("parallel", "arbitrary"),
        name="conformer_conv",
    )(uconv, gconv, wdw, bdw, lng, lnb, wpw)


def _split3(x):
    hi = x.astype(BF16)
    r1 = x - hi.astype(F32)
    mid = r1.astype(BF16)
    lo = (r1 - mid.astype(F32)).astype(BF16)
    return hi, mid, lo


def _ssd_kernel(z_ref, xbc_ref, kd_ref, cw_ref, cb_ref, dtb_ref, alog_ref, dsk_ref, ng_ref, o_ref,
                xbuf, xact, state, shx):
    i = pl.program_id(0)
    nb, ts = o_ref.shape[0], o_ref.shape[1]
    L = SSD_CHUNK
    hp = SSD_HEAD_DIM
    gw = SSD_WIDTH // SSD_GROUPS
    pairs_per_group = SSD_HEADS // SSD_GROUPS // 2
    npair = SSD_HEADS // 2

    @pl.when(i == 0)
    def _():
        xbuf[:, 0:SSD_HALO, :] = jnp.zeros((nb, SSD_HALO, SSD_XBC), F32)
        state[...] = jnp.zeros(state.shape, F32)

    @pl.when(i > 0)
    def _():
        xbuf[:, 0:SSD_HALO, :] = xbuf[:, ts:ts + SSD_HALO, :]

    for b in range(nb):
        xbuf[b, SSD_HALO:SSD_HALO + ts, :] = xbc_ref[b]
        for s in range(1, SSD_CONV_K):
            shx[s - 1, :, :] = xbuf[b, SSD_HALO - s:SSD_HALO - s + ts, :]
        acc = cb_ref[...] + cw_ref[SSD_CONV_K - 1:SSD_CONV_K, :] * xbc_ref[b]
        for s in range(1, SSD_CONV_K):
            acc = acc + cw_ref[SSD_CONV_K - 1 - s:SSD_CONV_K - s, :] * shx[s - 1]
        xact[b] = _silu(acc)

    lane = lax.broadcasted_iota(jnp.int32, (1, LANES), 1)
    is_dt = (lane >= DT_LANE0) & (lane < DT_LANE0 + SSD_HEADS)
    a_neg = jnp.where(is_dt, -jnp.exp(alog_ref[...]), 0.0)
    rr = lax.broadcasted_iota(jnp.int32, (L, L), 0)
    cc = lax.broadcasted_iota(jnp.int32, (L, L), 1)
    causal = rr >= cc
    tri = causal.astype(BF16)
    left = lax.broadcasted_iota(jnp.int32, (L, LANES), 1) < hp

    def decay_geometry(b, r0):
        dt = _softplus(kd_ref[b, pl.ds(r0, L), :] + dtb_ref[...])
        hi, mid, lo = _split3(dt * a_neg)
        acs = (jnp.dot(tri, hi, preferred_element_type=F32)
               + jnp.dot(tri, mid, preferred_element_type=F32)
               + jnp.dot(tri, lo, preferred_element_type=F32))
        return dict(dt=dt, acs=acs, acs_t=acs.T)

    def expand(b, r0, v):
        def col(a, h):
            return jnp.broadcast_to(a[:, DT_LANE0 + h:DT_LANE0 + h + 1], (L, LANES))

        dt_cols = [col(v["dt"], h) for h in range(SSD_HEADS)]
        acs_cols = [col(v["acs"], h) for h in range(SSD_HEADS)]
        dt_exp = jnp.concatenate([jnp.where(left, dt_cols[2 * j], dt_cols[2 * j + 1]) for j in range(npair)], axis=1)
        acs_exp = jnp.concatenate([jnp.where(left, acs_cols[2 * j], acs_cols[2 * j + 1]) for j in range(npair)], axis=1)
        last = acs_exp[L - 1:L, :]
        xs = xact[b, pl.ds(r0, L), 0:SSD_WIDTH]
        xdt = xs * dt_exp
        v.update(acs_cols=acs_cols, last=last, xs=xs, xdt_b=xdt.astype(BF16),
                 xdt_end=(xdt * jnp.exp(last - acs_exp)).astype(BF16), grow=jnp.exp(acs_exp))

    def mix(b, r0, v):
        st_prev = state[b]
        st_prev_b = st_prev.astype(BF16)
        y_parts = []
        st_parts = []
        for g in range(SSD_GROUPS):
            c0 = SSD_WIDTH + g * SSD_STATE
            c1 = SSD_WIDTH + (SSD_GROUPS + g) * SSD_STATE
            bg = xact[b, pl.ds(r0, L), c0:c0 + SSD_STATE]
            cg_b = xact[b, pl.ds(r0, L), c1:c1 + SSD_STATE].astype(BF16)
            cb = lax.dot_general(cg_b, bg.astype(BF16), (((1,), (1,)), ((), ())), preferred_element_type=F32)
            st_parts.append(jnp.dot(bg.T.astype(BF16), v["xdt_end"][:, g * gw:(g + 1) * gw],
                                    preferred_element_type=F32))
            y_off = jnp.dot(cg_b, st_prev_b[:, g * gw:(g + 1) * gw], preferred_element_type=F32)
            y_off = y_off * v["grow"][:, g * gw:(g + 1) * gw]
            for jj in range(pairs_per_group):
                j = g * pairs_per_group + jj
                pair = v["xdt_b"][:, j * LANES:(j + 1) * LANES]
                yd = []
                for h in (2 * j, 2 * j + 1):
                    seg = v["acs_cols"][h] - v["acs_t"][DT_LANE0 + h:DT_LANE0 + h + 1, :]
                    decay = jnp.exp(jnp.where(causal, seg, -jnp.inf))
                    yd.append(jnp.dot((cb * decay).astype(BF16), pair, preferred_element_type=F32))
                y_parts.append(jnp.where(left, yd[0], yd[1]) + y_off[:, jj * LANES:(jj + 1) * LANES])
        state[b] = st_prev * jnp.exp(v["last"]) + jnp.concatenate(st_parts, axis=1)
        v["y"] = jnp.concatenate(y_parts, axis=1) + dsk_ref[...] * v["xs"]

    def finish(b, r0, v):
        yz = v["y"] * _silu(z_ref[b, pl.ds(r0, L), :])
        o_ref[b, pl.ds(r0, L), :] = _rms(yz, ng_ref[...]).astype(o_ref.dtype)

    def chunk(c, carry):
        r0 = pl.multiple_of(c * L, L)
        vals = [decay_geometry(b, r0) for b in range(nb)]
        for stage in (expand, mix, finish):
            for b in range(nb):
                stage(b, r0, vals[b])
        return carry

    lax.fori_loop(0, ts // L, chunk, 0)


def _ssd(layer, z, xbc, kd, cw, cb, dtb, alog, dsk, ng, bsz, seq, ts):
    blk = lambda w: pl.BlockSpec((bsz, ts, w), lambda i: (0, i, 0))
    out = pl.pallas_call(
        _ssd_kernel,
        grid=(seq // ts,),
        in_specs=[blk(SSD_WIDTH), blk(SSD_XBC), blk(KD_WIDTH),
                  _resident(cw, layer), _resident(cb, layer), _resident(dtb, layer), _resident(alog, layer),
                  _resident(dsk, layer), _resident(ng, layer)],
        out_specs=blk(SSD_WIDTH),
        out_shape=jax.ShapeDtypeStruct((bsz, seq, SSD_WIDTH), BF16),
        scratch_shapes=[pltpu.VMEM((bsz, ts + SSD_HALO, SSD_XBC), F32), pltpu.VMEM((bsz, ts, SSD_XBC), F32),
                        pltpu.VMEM((bsz, SSD_STATE, SSD_WIDTH), F32),
                        pltpu.VMEM((SSD_CONV_K - 1, ts, SSD_XBC), F32)],
        compiler_params=_params("arbitrary"),
        name="ssd",
    )(z.reshape(bsz, seq, -1), xbc.reshape(bsz, seq, -1), kd.reshape(bsz, seq, -1), cw, cb, dtb, alog, dsk, ng)
    return out.reshape(bsz * seq, SSD_WIDTH)


def _out_proj_kernel(x_ref, a_ref, c_ref, s_ref, w_ref, o_ref):
    n0 = a_ref.shape[1]
    n1 = n0 + c_ref.shape[1]
    n2 = n1 + s_ref.shape[1]
    y = jnp.dot(a_ref[...], w_ref[0:n0, :], preferred_element_type=F32)
    y = y + jnp.dot(c_ref[...], w_ref[n0:n1, :], preferred_element_type=F32)
    y = y + jnp.dot(s_ref[...], w_ref[n1:n2, :], preferred_element_type=F32)
    o_ref[...] = x_ref[...] + y


def _out_proj(layer, x2, o_mla, o_conv, o_ssd, w, tm):
    m, d = x2.shape
    row = lambda i: (i, 0)
    return pl.pallas_call(
        _out_proj_kernel,
        grid=(m // tm,),
        in_specs=[pl.BlockSpec((tm, d), row),
                  pl.BlockSpec((tm, o_mla.shape[1]), row),
                  pl.BlockSpec((tm, o_conv.shape[1]), row),
                  pl.BlockSpec((tm, o_ssd.shape[1]), row),
                  _resident(w, layer)],
        out_specs=pl.BlockSpec((tm, d), row),
        out_shape=jax.ShapeDtypeStruct((m, d), F32),
        compiler_params=_params("parallel"),
        name="out_proj",
    )(x2, o_mla, o_conv, o_ssd, w)


def _rows(p):
    return p.astype(F32)[:, None, :]


def _pad_rows(p, lane0, width):
    return jnp.pad(p.astype(F32), ((0, 0), (lane0, width - lane0 - p.shape[1])))[:, None, :]


def _rope_tables(seq):
    half = QK_ROPE // 2
    inv_freq = ROPE_THETA ** (-jnp.arange(half, dtype=F32) / half)
    ang = jnp.arange(seq).astype(F32)[:, None] * inv_freq[None, :]
    cos, sin = jnp.cos(ang), jnp.sin(ang)
    z = lambda n: jnp.zeros((seq, n), F32)
    cos_t = jnp.concatenate([cos, cos, z(LANES - QK_ROPE)], axis=1)
    sa_t = jnp.concatenate([-sin, z(LANES - half)], axis=1)
    sb_t = jnp.concatenate([z(half), sin, z(LANES - QK_ROPE)], axis=1)
    return cos_t, sa_t, sb_t


def kernel(x, norm_g, w_in, q_a_norm, w_q_b, kv_a_norm, w_kv_b, q_norm, k_norm, conv_dw_w, conv_dw_b, conv_ln_g,
           conv_ln_b, conv_pw_w, ssd_conv_w, ssd_conv_b, ssd_dt_bias, ssd_A_log, ssd_D, ssd_norm_g, w_out):
    bsz, seq, d = x.shape
    m = bsz * seq
    x2 = x.reshape(m, d)
    tabs = _rope_tables(seq)

    w_in_k = _prep_w_in(w_in)
    wq, wkv, wpw = _prep_small(w_q_b, w_kv_b, conv_pw_w)
    w_out_k = _prep_cast(w_out, min(512, w_out.shape[1]))
    g_in, qan, kvan = _rows(norm_g), _rows(q_a_norm), _rows(kv_a_norm)
    qn, kn = _pad_rows(q_norm, 0, QK_PAD), _pad_rows(k_norm, 0, QK_PAD)
    wdw = jnp.pad(conv_dw_w.astype(F32), ((0, 0), (0, CONV_HALO - CONV_K), (0, 0)))
    bdw, lng, lnb = _rows(conv_dw_b), _rows(conv_ln_g), _rows(conv_ln_b)
    cw = jnp.pad(ssd_conv_w.astype(F32), ((0, 0), (0, SUBLANES - SSD_CONV_K), (0, 0)))
    cb, ng = _rows(ssd_conv_b), _rows(ssd_norm_g)
    dtb, alog = _pad_rows(ssd_dt_bias, DT_LANE0, LANES), _pad_rows(ssd_A_log, DT_LANE0, LANES)
    dsk = _rows(jnp.repeat(ssd_D, SSD_HEAD_DIM, axis=1))

    tm_in = min(256, m)
    tm_out = min(512, m)
    t_seq = min(512, seq)
    for layer in range(norm_g.shape[0]):
        cq, ckv, kd, gmla, uconv, gconv, z, xbc = _in_proj(layer, x2, g_in, w_in_k, tm_in)
        q, k, v = _mla_prep(layer, cq, ckv, kd, qan, wq, kvan, wkv, qn, kn, *tabs, bsz, seq, t_seq)
        o_mla = _attention(q, k, v, gmla, bsz, seq, t_seq)
        o_conv = _conv(layer, uconv, gconv, wdw, bdw, lng, lnb, wpw, bsz, seq, t_seq)
        o_ssd = _ssd(layer, z, xbc, kd, cw, cb, dtb, alog, dsk, ng, bsz, seq, t_seq)
        x2 = _out_proj(layer, x2, o_mla, o_conv, o_ssd, w_out_k, tm_out)
    return x2.reshape(bsz, seq, d)
```

```python
import functools
import math

import jax
import jax.numpy as jnp
from jax import lax
from jax.experimental import pallas as pl
from jax.experimental.pallas import tpu as pltpu

F32 = jnp.float32
BF16 = jnp.bfloat16

MLA_HEADS = 6
QK_NOPE = 128
QK_ROPE = 64
QK_HEAD = QK_NOPE + QK_ROPE
V_HEAD = 128
Q_LORA = 512
KV_LORA = 256
MLA_WIDTH = MLA_HEADS * V_HEAD
ROPE_THETA = 10000.0
CONV_WIDTH = 512
CONV_K = 31
SSD_HEADS = 12
SSD_HEAD_DIM = 64
SSD_WIDTH = SSD_HEADS * SSD_HEAD_DIM
SSD_GROUPS = 2
SSD_STATE = 128
SSD_CONV_K = 4
SSD_CHUNK = 128
SSD_XBC = SSD_WIDTH + 2 * SSD_GROUPS * SSD_STATE
NORM_EPS = 1e-6
LN_EPS = 1e-5

LANES = 128
SUBLANES = 8
QK_PAD = 2 * LANES
Q_UP = 3 * LANES
KD_WIDTH = LANES
DT_LANE0 = QK_ROPE
CONV_HALO = 32
SSD_HALO = SUBLANES
VMEM_LIMIT = 56 * 1024 * 1024
ATTN_STRIP = 64

SEG_WIDTHS = (Q_LORA, KV_LORA, KD_WIDTH, MLA_WIDTH, 2 * CONV_WIDTH, CONV_WIDTH, SSD_WIDTH, SSD_XBC)


def _sigmoid(x):
    return 1.0 / (1.0 + jnp.exp(-x))


def _silu(x):
    return x * _sigmoid(x)


def _softplus(x):
    return jnp.maximum(x, 0.0) + jnp.log1p(jnp.exp(-jnp.abs(x)))


def _rms(x, g, eps=NORM_EPS):
    ms = jnp.mean(x * x, axis=-1, keepdims=True)
    return x * lax.rsqrt(ms + eps) * g


def _params(*sem):
    return pltpu.CompilerParams(dimension_semantics=sem, vmem_limit_bytes=VMEM_LIMIT)


def _resident(stacked, layer):
    _, a, b = stacked.shape
    return pl.BlockSpec((None, a, b), lambda *_: (layer, 0, 0), pipeline_mode=pl.Buffered(1))


W_IN_HALF = LANES // 2


def _prep_w_in_kernel(a_ref, b_ref, o_ref, *, depth, n_kt, n_dt):
    j = pl.program_id(0)
    kd_tile = (Q_LORA + KV_LORA) // LANES
    stride = depth * n_kt
    row = lax.broadcasted_iota(jnp.int32, (W_IN_HALF, LANES), 0)
    keep_b = (j != kd_tile) | (row < n_dt)
    for l in range(depth):
        for kt in range(n_kt):
            xa = a_ref[pl.ds(kt * depth + l, W_IN_HALF, stride=stride), :]
            xb = b_ref[pl.ds(kt * depth + l, W_IN_HALF, stride=stride), :]
            x = jnp.concatenate([xa, jnp.where(keep_b, xb, 0.0)], axis=0)
            o_ref[l, kt * LANES:(kt + 1) * LANES, :] = x.T.astype(o_ref.dtype)


def _prep_w_in(w_in):
    depth, d, n = w_in.shape
    n_kt = d // LANES
    n_out = sum(SEG_WIDTHS)
    rows_per_group = W_IN_HALF * n_kt * depth
    view = w_in.transpose(2, 0, 1).reshape(n, depth, n_kt, LANES).transpose(0, 2, 1, 3).reshape(-1, LANES)
    kd_tile = (Q_LORA + KV_LORA) // LANES
    dt_group = (n - SSD_HEADS) // W_IN_HALF
    assert (n - SSD_HEADS) % W_IN_HALF == 0 and (Q_LORA + KV_LORA) % LANES == 0 and QK_ROPE == W_IN_HALF
    a_idx = lambda j: (jnp.where(j <= kd_tile, 2 * j, 2 * j - 1), 0)
    b_idx = lambda j: (jnp.where(j < kd_tile, 2 * j + 1, jnp.where(j == kd_tile, dt_group, 2 * j)), 0)
    return pl.pallas_call(
        functools.partial(_prep_w_in_kernel, depth=depth, n_kt=n_kt, n_dt=SSD_HEADS),
        grid=(n_out // LANES,),
        in_specs=[pl.BlockSpec((rows_per_group, LANES), a_idx),
                  pl.BlockSpec((rows_per_group, LANES), b_idx)],
        out_specs=pl.BlockSpec((depth, d, LANES), lambda j: (0, 0, j)),
        out_shape=jax.ShapeDtypeStruct((depth, d, n_out), BF16),
        compiler_params=_params("parallel"),
        name="prep_w_in",
    )(view, view)


def _prep_small_kernel(wq_ref, wkv_ref, wpw_ref, oq_ref, okv_ref, opw_ref):
    wq = wq_ref[...]
    zq = jnp.zeros((wq.shape[0], QK_PAD - QK_HEAD), F32)
    half = QK_ROPE // 2
    parts = []
    for h in range(MLA_HEADS):
        r0 = h * QK_HEAD + QK_NOPE
        parts += [wq[:, h * QK_HEAD:(h + 1) * QK_HEAD], zq,
                  wq[:, r0 + half:r0 + QK_ROPE], wq[:, r0:r0 + half], zq]
    oq_ref[...] = jnp.concatenate(parts, axis=1).astype(oq_ref.dtype)
    wkv = wkv_ref[...]
    per = QK_NOPE + V_HEAD
    k_cols = [wkv[:, h * per:h * per + QK_NOPE] for h in range(MLA_HEADS)]
    v_cols = [wkv[:, h * per + QK_NOPE:(h + 1) * per] for h in range(MLA_HEADS)]
    okv_ref[...] = jnp.concatenate(k_cols + v_cols, axis=1).astype(okv_ref.dtype)
    opw_ref[...] = wpw_ref[...].astype(opw_ref.dtype)


def _prep_small(w_q_b, w_kv_b, conv_pw_w):
    depth = w_q_b.shape[0]
    whole = lambda a: pl.BlockSpec((None,) + a.shape[1:], lambda l: (l, 0, 0))
    shapes = [(depth, Q_LORA, MLA_HEADS * Q_UP), w_kv_b.shape, conv_pw_w.shape]
    return pl.pallas_call(
        _prep_small_kernel,
        grid=(depth,),
        in_specs=[whole(w_q_b), whole(w_kv_b), whole(conv_pw_w)],
        out_specs=[pl.BlockSpec((None,) + s[1:], lambda l: (l, 0, 0)) for s in shapes],
        out_shape=[jax.ShapeDtypeStruct(s, BF16) for s in shapes],
        compiler_params=_params("parallel"),
        name="prep_small",
    )(w_q_b, w_kv_b, conv_pw_w)


def _cast_kernel(w_ref, o_ref):
    o_ref[...] = w_ref[...].astype(o_ref.dtype)


def _prep_cast(w, tk):
    depth, k, n = w.shape
    return pl.pallas_call(
        _cast_kernel,
        grid=(depth, k // tk),
        in_specs=[pl.BlockSpec((None, tk, n), lambda l, i: (l, i, 0))],
        out_specs=pl.BlockSpec((None, tk, n), lambda l, i: (l, i, 0)),
        out_shape=jax.ShapeDtypeStruct(w.shape, BF16),
        compiler_params=_params("parallel", "parallel"),
        name="prep_cast",
    )(w)


def _in_proj_kernel(x_ref, g_ref, w_ref, *out_refs):
    h = _rms(x_ref[...], g_ref[...]).astype(BF16)
    off = 0
    for o_ref in out_refs:
        n = o_ref.shape[-1]
        o_ref[...] = jnp.dot(h, w_ref[:, off:off + n], preferred_element_type=F32).astype(o_ref.dtype)
        off += n


def _in_proj(layer, x2, g, w, tm):
    m, d = x2.shape
    assert w.shape[2] == sum(SEG_WIDTHS)
    return pl.pallas_call(
        _in_proj_kernel,
        grid=(m // tm,),
        in_specs=[pl.BlockSpec((tm, d), lambda i: (i, 0)),
                  _resident(g, layer),
                  _resident(w, layer)],
        out_specs=[pl.BlockSpec((tm, n), lambda i: (i, 0)) for n in SEG_WIDTHS],
        out_shape=[jax.ShapeDtypeStruct((m, n), F32) for n in SEG_WIDTHS],
        compiler_params=_params("parallel"),
        name="in_proj",
    )(x2, g, w)


def _mla_prep_kernel(cq_ref, ckv_ref, kd_ref, qan_ref, wq_ref, kvan_ref, wkv_ref, qn_ref, kn_ref,
                     cos_ref, sa_ref, sb_ref, q_ref, k_ref, v_ref):
    scale = math.log2(math.e) / math.sqrt(QK_HEAD)
    hq = _rms(cq_ref[...], qan_ref[...]).astype(BF16)
    qf = jnp.dot(hq, wq_ref[...], preferred_element_type=F32)
    hkv = _rms(ckv_ref[...], kvan_ref[...]).astype(BF16)
    kvf = jnp.dot(hkv, wkv_ref[...], preferred_element_type=F32)

    cos = cos_ref[...]
    sa = sa_ref[...]
    sb = sb_ref[...]

    def rope(r):
        return r * cos + pltpu.roll(r, LANES - QK_ROPE // 2, 1) * sa + pltpu.roll(r, QK_ROPE // 2, 1) * sb

    kd = kd_ref[...]
    lane = lax.broadcasted_iota(jnp.int32, kd.shape, 1)
    kpe = jnp.where(lane < QK_ROPE, kd, 0.0)
    kpe_ss = jnp.sum(kpe * kpe, axis=-1, keepdims=True)

    qn_a = qn_ref[:, 0:LANES] * scale
    qn_b = qn_ref[:, LANES:QK_PAD] * scale
    qn_s = qn_ref[:, QK_PAD:Q_UP] * scale
    sin_signed = sa + sb
    kn_w = kn_ref[...]
    kpe_rot = rope(kpe * kn_w[:, LANES:QK_PAD])
    for h in range(MLA_HEADS):
        qa = qf[:, h * Q_UP:h * Q_UP + LANES]
        qb = qf[:, h * Q_UP + LANES:h * Q_UP + QK_PAD]
        qs = qf[:, h * Q_UP + QK_PAD:(h + 1) * Q_UP]
        ss = jnp.sum(qa * qa + qb * qb, axis=-1, keepdims=True)
        inv = lax.rsqrt(ss * (1.0 / QK_HEAD) + NORM_EPS)
        q_ref[0, h, :, 0:LANES] = (qa * inv * qn_a).astype(q_ref.dtype)
        q_ref[0, h, :, LANES:QK_PAD] = (inv * (qb * qn_b * cos + qs * qn_s * sin_signed)).astype(q_ref.dtype)

        ka = kvf[:, h * LANES:(h + 1) * LANES]
        ss = jnp.sum(ka * ka, axis=-1, keepdims=True) + kpe_ss
        inv = lax.rsqrt(ss * (1.0 / QK_HEAD) + NORM_EPS)
        k_ref[0, h, :, 0:LANES] = (ka * inv * kn_w[:, 0:LANES]).astype(k_ref.dtype)
        k_ref[0, h, :, LANES:QK_PAD] = (kpe_rot * inv).astype(k_ref.dtype)

        v_ref[0, h, :, :] = kvf[:, MLA_WIDTH + h * V_HEAD:MLA_WIDTH + (h + 1) * V_HEAD].astype(v_ref.dtype)


def _mla_prep(layer, cq, ckv, kd, qan, wq, kvan, wkv, qn, kn, cos_t, sa_t, sb_t, bsz, seq, tm):
    nt = seq // tm
    row = lambda b, i: (b * nt + i, 0)
    pos = lambda b, i: (i, 0)
    hd = MLA_HEADS
    return pl.pallas_call(
        _mla_prep_kernel,
        grid=(bsz, nt),
        in_specs=[pl.BlockSpec((tm, Q_LORA), row),
                  pl.BlockSpec((tm, KV_LORA), row),
                  pl.BlockSpec((tm, KD_WIDTH), row),
                  _resident(qan, layer), _resident(wq, layer), _resident(kvan, layer), _resident(wkv, layer),
                  _resident(qn, layer), _resident(kn, layer),
                  pl.BlockSpec((tm, LANES), pos), pl.BlockSpec((tm, LANES), pos), pl.BlockSpec((tm, LANES), pos)],
        out_specs=[pl.BlockSpec((1, hd, tm, QK_PAD), lambda b, i: (b, 0, i, 0)),
                   pl.BlockSpec((1, hd, tm, QK_PAD), lambda b, i: (b, 0, i, 0)),
                   pl.BlockSpec((1, hd, tm, V_HEAD), lambda b, i: (b, 0, i, 0))],
        out_shape=[jax.ShapeDtypeStruct((bsz, hd, seq, QK_PAD), BF16),
                   jax.ShapeDtypeStruct((bsz, hd, seq, QK_PAD), BF16),
                   jax.ShapeDtypeStruct((bsz, hd, seq, V_HEAD), BF16)],
        compiler_params=_params("parallel", "parallel"),
        name="mla_prep",
    )(cq, ckv, kd, qan, wq, kvan, wkv, qn, kn, cos_t, sa_t, sb_t)


def _attn_kernel(qi_ref, ki_ref, q_ref, k_ref, v_ref, g_ref, o_ref, m_sc, l_sc, acc_sc, s_sc, p_sc, a_sc):
    qi = qi_ref[pl.program_id(1)]
    ki = ki_ref[pl.program_id(1)]
    tq = q_ref.shape[2]
    tk = k_ref.shape[2]

    @pl.when(ki == 0)
    def _():
        m_sc[...] = jnp.full(m_sc.shape, -jnp.inf, F32)
        l_sc[...] = jnp.zeros(l_sc.shape, F32)
        acc_sc[...] = jnp.zeros(acc_sc.shape, F32)

    strip = min(ATTN_STRIP, tq)
    n_chunk = tk // LANES

    def scores(h):
        s_sc[h % 2] = lax.dot_general(q_ref[0, h], k_ref[0, h], (((1,), (1,)), ((), ())),
                                      preferred_element_type=F32)

    def softmax_strip(h, r0, masked):
        slot = h % 2
        rows = slice(r0, r0 + strip)
        live = [c for c in range(n_chunk) if not (masked and c * LANES >= r0 + strip)]
        chunks = []
        for c in live:
            x = s_sc[slot, rows, c * LANES:(c + 1) * LANES]
            if masked and (c + 1) * LANES - 1 > r0:
                rr = r0 + lax.broadcasted_iota(jnp.int32, (strip, LANES), 0)
                cc = c * LANES + lax.broadcasted_iota(jnp.int32, (strip, LANES), 1)
                x = jnp.where(rr >= cc, x, -jnp.inf)
            chunks.append(x)
        m_prev = m_sc[h, rows, :]
        m_loc = functools.reduce(jnp.maximum, chunks)
        m_new = jnp.maximum(m_prev, jnp.max(m_loc, axis=-1, keepdims=True))
        alpha = jnp.exp2(m_prev - m_new)
        ps = [jnp.exp2(x - m_new) for x in chunks]
        l_sc[h, rows, :] = alpha * l_sc[h, rows, :] + functools.reduce(jnp.add, ps)
        m_sc[h, rows, :] = m_new
        a_sc[slot, rows, :] = alpha
        dead = [jnp.zeros((strip, LANES), BF16)] * (n_chunk - len(live))
        p_sc[slot, rows, :] = jnp.concatenate([p.astype(BF16) for p in ps] + dead, axis=1)

    def step(masked):
        scores(0)
        for h in range(MLA_HEADS):
            if h + 1 < MLA_HEADS:
                scores(h + 1)
            for r0 in range(0, tq, strip):
                softmax_strip(h, r0, masked)
            pv = jnp.dot(p_sc[h % 2], v_ref[0, h], preferred_element_type=F32)
            acc_sc[h] = a_sc[h % 2] * acc_sc[h] + pv

    @pl.when(ki < qi)
    def _():
        step(False)

    @pl.when(ki == qi)
    def _():
        step(True)
        for h in range(MLA_HEADS):
            o = acc_sc[h] / jnp.sum(l_sc[h], axis=-1, keepdims=True)
            g = g_ref[:, h * V_HEAD:(h + 1) * V_HEAD]
            o_ref[:, h * V_HEAD:(h + 1) * V_HEAD] = (o * _silu(g)).astype(o_ref.dtype)


def _attention(q, k, v, gmla, bsz, seq, tq):
    hd = MLA_HEADS
    nq = seq // tq
    pairs = [(i, j) for i in range(nq) for j in range(i + 1)]
    qi_tab = jnp.asarray([i for i, _ in pairs], jnp.int32)
    ki_tab = jnp.asarray([j for _, j in pairs], jnp.int32)
    q_idx = lambda b, t, qi, ki: (b, 0, qi[t], 0)
    kv_idx = lambda b, t, qi, ki: (b, 0, ki[t], 0)
    row = lambda b, t, qi, ki: (b * nq + qi[t], 0)
    grid_spec = pltpu.PrefetchScalarGridSpec(
        num_scalar_prefetch=2,
        grid=(bsz, len(pairs)),
        in_specs=[pl.BlockSpec((1, hd, tq, QK_PAD), q_idx),
                  pl.BlockSpec((1, hd, tq, QK_PAD), kv_idx),
                  pl.BlockSpec((1, hd, tq, V_HEAD), kv_idx),
                  pl.BlockSpec((tq, MLA_WIDTH), row)],
        out_specs=pl.BlockSpec((tq, MLA_WIDTH), row),
        scratch_shapes=[pltpu.VMEM((hd, tq, LANES), F32), pltpu.VMEM((hd, tq, LANES), F32),
                        pltpu.VMEM((hd, tq, V_HEAD), F32),
                        pltpu.VMEM((2, tq, tq), F32), pltpu.VMEM((2, tq, tq), BF16),
                        pltpu.VMEM((2, tq, LANES), F32)])
    return pl.pallas_call(
        _attn_kernel,
        grid_spec=grid_spec,
        out_shape=jax.ShapeDtypeStruct((bsz * seq, MLA_WIDTH), BF16),
        compiler_params=_params("parallel", "arbitrary"),
        name="attention",
    )(qi_tab, ki_tab, q, k, v, gmla)


def _conv_kernel(u_ref, gc_ref, wdw_ref, bdw_ref, lng_ref, lnb_ref, wpw_ref, o_ref, hbuf, ybuf, shbuf, *, strip):
    i = pl.program_id(1)
    tt = o_ref.shape[0]
    cw = CONV_WIDTH

    @pl.when(i == 0)
    def _():
        hbuf[0:CONV_HALO, :] = jnp.zeros((CONV_HALO, cw), F32)

    @pl.when(i > 0)
    def _():
        hbuf[0:CONV_HALO, :] = hbuf[tt:tt + CONV_HALO, :]

    hbuf[CONV_HALO:CONV_HALO + tt, :] = u_ref[:, 0:cw] * _sigmoid(u_ref[:, cw:2 * cw])

    base = CONV_HALO - (CONV_K - 1)

    span = tt + CONV_HALO - SUBLANES
    for res in range(1, SUBLANES):
        shbuf[res - 1, 0:span, :] = hbuf[res:res + span, :]

    def do_strip(s, carry):
        r0 = pl.multiple_of(s * strip, strip)
        acc = jnp.broadcast_to(bdw_ref[...], (strip, cw))
        for kk in range(CONV_K):
            res = (base + kk) % SUBLANES
            start = r0 + (base + kk - res)
            tap = hbuf[pl.ds(start, strip), :] if res == 0 else shbuf[res - 1, pl.ds(start, strip), :]
            acc = acc + wdw_ref[kk:kk + 1, :] * tap
        ybuf[pl.ds(r0, strip), :] = acc
        return carry

    lax.fori_loop(0, tt // strip, do_strip, 0)

    y = ybuf[...]
    mu = jnp.mean(y, axis=-1, keepdims=True)
    yc = y - mu
    var = jnp.mean(yc * yc, axis=-1, keepdims=True)
    hn = _silu(yc * lax.rsqrt(var + LN_EPS) * lng_ref[...] + lnb_ref[...])
    out = jnp.dot(hn.astype(BF16), wpw_ref[...], preferred_element_type=F32)
    o_ref[...] = (out * _silu(gc_ref[...])).astype(o_ref.dtype)


def _conv(layer, uconv, gconv, wdw, bdw, lng, lnb, wpw, bsz, seq, tt):
    nt = seq // tt
    row = lambda b, i: (b * nt + i, 0)
    strip = min(64, tt)
    return pl.pallas_call(
        functools.partial(_conv_kernel, strip=strip),
        grid=(bsz, nt),
        in_specs=[pl.BlockSpec((tt, 2 * CONV_WIDTH), row),
                  pl.BlockSpec((tt, CONV_WIDTH), row),
                  _resident(wdw, layer), _resident(bdw, layer), _resident(lng, layer), _resident(lnb, layer),
                  _resident(wpw, layer)],
        out_specs=pl.BlockSpec((tt, CONV_WIDTH), row),
        out_shape=jax.ShapeDtypeStruct((bsz * seq, CONV_WIDTH), BF16),
        scratch_shapes=[pltpu.VMEM((tt + CONV_HALO, CONV_WIDTH), F32), pltpu.VMEM((tt, CONV_WIDTH), F32),
                        pltpu.VMEM((SUBLANES - 1, tt + CONV_HALO, CONV_WIDTH), F32)],
        compiler_params=_params("parallel", "arbitrary"),
        name="conformer_conv",
    )(uconv, gconv, wdw, bdw, lng, lnb, wpw)


def _split3(x):
    hi = x.astype(BF16)
    r1 = x - hi.astype(F32)
    mid = r1.astype(BF16)
    lo = (r1 - mid.astype(F32)).astype(BF16)
    return hi, mid, lo


def _ssd_kernel(z_ref, xbc_ref, kd_ref, cw_ref, cb_ref, dtb_ref, alog_ref, dsk_ref, ng_ref, o_ref,
                xbuf, xact, state, shx):
    i = pl.program_id(0)
    nb, ts = o_ref.shape[0], o_ref.shape[1]
    L = SSD_CHUNK
    hp = SSD_HEAD_DIM
    gw = SSD_WIDTH // SSD_GROUPS
    pairs_per_group = SSD_HEADS // SSD_GROUPS // 2
    npair = SSD_HEADS // 2

    @pl.when(i == 0)
    def _():
        xbuf[:, 0:SSD_HALO, :] = jnp.zeros((nb, SSD_HALO, SSD_XBC), F32)
        state[...] = jnp.zeros(state.shape, F32)

    @pl.when(i > 0)
    def _():
        xbuf[:, 0:SSD_HALO, :] = xbuf[:, ts:ts + SSD_HALO, :]

    for b in range(nb):
        xbuf[b, SSD_HALO:SSD_HALO + ts, :] = xbc_ref[b]
        for s in range(1, SSD_CONV_K):
            shx[s - 1, :, :] = xbuf[b, SSD_HALO - s:SSD_HALO - s + ts, :]
        acc = cb_ref[...] + cw_ref[SSD_CONV_K - 1:SSD_CONV_K, :] * xbc_ref[b]
        for s in range(1, SSD_CONV_K):
            acc = acc + cw_ref[SSD_CONV_K - 1 - s:SSD_CONV_K - s, :] * shx[s - 1]
        xact[b] = _silu(acc)

    lane = lax.broadcasted_iota(jnp.int32, (1, LANES), 1)
    is_dt = (lane >= DT_LANE0) & (lane < DT_LANE0 + SSD_HEADS)
    a_neg = jnp.where(is_dt, -jnp.exp(alog_ref[...]), 0.0)
    rr = lax.broadcasted_iota(jnp.int32, (L, L), 0)
    cc = lax.broadcasted_iota(jnp.int32, (L, L), 1)
    causal = rr >= cc
    tri = causal.astype(BF16)
    left = lax.broadcasted_iota(jnp.int32, (L, LANES), 1) < hp

    def decay_geometry(b, r0):
        dt = _softplus(kd_ref[b, pl.ds(r0, L), :] + dtb_ref[...])
        hi, mid, lo = _split3(dt * a_neg)
        acs = (jnp.dot(tri, hi, preferred_element_type=F32)
               + jnp.dot(tri, mid, preferred_element_type=F32)
               + jnp.dot(tri, lo, preferred_element_type=F32))
        return dict(dt=dt, acs=acs, acs_t=acs.T)

    def expand(b, r0, v):
        def col(a, h):
            return jnp.broadcast_to(a[:, DT_LANE0 + h:DT_LANE0 + h + 1], (L, LANES))

        dt_cols = [col(v["dt"], h) for h in range(SSD_HEADS)]
        acs_cols = [col(v["acs"], h) for h in range(SSD_HEADS)]
        dt_exp = jnp.concatenate([jnp.where(left, dt_cols[2 * j], dt_cols[2 * j + 1]) for j in range(npair)], axis=1)
        acs_exp = jnp.concatenate([jnp.where(left, acs_cols[2 * j], acs_cols[2 * j + 1]) for j in range(npair)], axis=1)
        last = acs_exp[L - 1:L, :]
        xs = xact[b, pl.ds(r0, L), 0:SSD_WIDTH]
        xdt = xs * dt_exp
        v.update(acs_cols=acs_cols, last=last, xs=xs, xdt_b=xdt.astype(BF16),
                 xdt_end=(xdt * jnp.exp(last - acs_exp)).astype(BF16), grow=jnp.exp(acs_exp))

    def mix(b, r0, v):
        st_prev = state[b]
        st_prev_b = st_prev.astype(BF16)
        y_parts = []
        st_parts = []
        for g in range(SSD_GROUPS):
            c0 = SSD_WIDTH + g * SSD_STATE
            c1 = SSD_WIDTH + (SSD_GROUPS + g) * SSD_STATE
            bg = xact[b, pl.ds(r0, L), c0:c0 + SSD_STATE]
            cg_b = xact[b, pl.ds(r0, L), c1:c1 + SSD_STATE].astype(BF16)
            cb = lax.dot_general(cg_b, bg.astype(BF16), (((1,), (1,)), ((), ())), preferred_element_type=F32)
            st_parts.append(jnp.dot(bg.T.astype(BF16), v["xdt_end"][:, g * gw:(g + 1) * gw],
                                    preferred_element_type=F32))
            y_off = jnp.dot(cg_b, st_prev_b[:, g * gw:(g + 1) * gw], preferred_element_type=F32)
            y_off = y_off * v["grow"][:, g * gw:(g + 1) * gw]
            for jj in range(pairs_per_group):
                j = g * pairs_per_group + jj
                pair = v["xdt_b"][:, j * LANES:(j + 1) * LANES]
                yd = []
                for h in (2 * j, 2 * j + 1):
                    seg = v["acs_cols"][h] - v["acs_t"][DT_LANE0 + h:DT_LANE0 + h + 1, :]
                    decay = jnp.exp(jnp.where(causal, seg, -jnp.inf))
                    yd.append(jnp.dot((cb * decay).astype(BF16), pair, preferred_element_type=F32))
                y_parts.append(jnp.where(left, yd[0], yd[1]) + y_off[:, jj * LANES:(jj + 1) * LANES])
        state[b] = st_prev * jnp.exp(v["last"]) + jnp.concatenate(st_parts, axis=1)
        v["y"] = jnp.concatenate(y_parts, axis=1) + dsk_ref[...] * v["xs"]

    def finish(b, r0, v):
        yz = v["y"] * _silu(z_ref[b, pl.ds(r0, L), :])
        o_ref[b, pl.ds(r0, L), :] = _rms(yz, ng_ref[...]).astype(o_ref.dtype)

    def chunk(c, carry):
        r0 = pl.multiple_of(c * L, L)
        vals = [decay_geometry(b, r0) for b in range(nb)]
        for stage in (expand, mix, finish):
            for b in range(nb):
                stage(b, r0, vals[b])
        return carry

    lax.fori_loop(0, ts // L, chunk, 0)


def _ssd(layer, z, xbc, kd, cw, cb, dtb, alog, dsk, ng, bsz, seq, ts):
    blk = lambda w: pl.BlockSpec((bsz, ts, w), lambda i: (0, i, 0))
    out = pl.pallas_call(
        _ssd_kernel,
        grid=(seq // ts,),
        in_specs=[blk(SSD_WIDTH), blk(SSD_XBC), blk(KD_WIDTH),
                  _resident(cw, layer), _resident(cb, layer), _resident(dtb, layer), _resident(alog, layer),
                  _resident(dsk, layer), _resident(ng, layer)],
        out_specs=blk(SSD_WIDTH),
        out_shape=jax.ShapeDtypeStruct((bsz, seq, SSD_WIDTH), BF16),
        scratch_shapes=[pltpu.VMEM((bsz, ts + SSD_HALO, SSD_XBC), F32), pltpu.VMEM((bsz, ts, SSD_XBC), F32),
                        pltpu.VMEM((bsz, SSD_STATE, SSD_WIDTH), F32),
                        pltpu.VMEM((SSD_CONV_K - 1, ts, SSD_XBC), F32)],
        compiler_params=_params("arbitrary"),
        name="ssd",
    )(z.reshape(bsz, seq, -1), xbc.reshape(bsz, seq, -1), kd.reshape(bsz, seq, -1), cw, cb, dtb, alog, dsk, ng)
    return out.reshape(bsz * seq, SSD_WIDTH)


def _out_proj_kernel(x_ref, a_ref, c_ref, s_ref, w_ref, o_ref):
    n0 = a_ref.shape[1]
    n1 = n0 + c_ref.shape[1]
    n2 = n1 + s_ref.shape[1]
    y = jnp.dot(a_ref[...], w_ref[0:n0, :], preferred_element_type=F32)
    y = y + jnp.dot(c_ref[...], w_ref[n0:n1, :], preferred_element_type=F32)
    y = y + jnp.dot(s_ref[...], w_ref[n1:n2, :], preferred_element_type=F32)
    o_ref[...] = x_ref[...] + y


def _out_proj(layer, x2, o_mla, o_conv, o_ssd, w, tm):
    m, d = x2.shape
    row = lambda i: (i, 0)
    return pl.pallas_call(
        _out_proj_kernel,
        grid=(m // tm,),
        in_specs=[pl.BlockSpec((tm, d), row),
                  pl.BlockSpec((tm, o_mla.shape[1]), row),
                  pl.BlockSpec((tm, o_conv.shape[1]), row),
                  pl.BlockSpec((tm, o_ssd.shape[1]), row),
                  _resident(w, layer)],
        out_specs=pl.BlockSpec((tm, d), row),
        out_shape=jax.ShapeDtypeStruct((m, d), F32),
        compiler_params=_params("parallel"),
        name="out_proj",
    )(x2, o_mla, o_conv, o_ssd, w)


def _rows(p):
    return p.astype(F32)[:, None, :]


def _pad_rows(p, lane0, width):
    return jnp.pad(p.astype(F32), ((0, 0), (lane0, width - lane0 - p.shape[1])))[:, None, :]


def _rope_tables(seq):
    half = QK_ROPE // 2
    inv_freq = ROPE_THETA ** (-jnp.arange(half, dtype=F32) / half)
    ang = jnp.arange(seq).astype(F32)[:, None] * inv_freq[None, :]
    cos, sin = jnp.cos(ang), jnp.sin(ang)
    z = lambda n: jnp.zeros((seq, n), F32)
    cos_t = jnp.concatenate([cos, cos, z(LANES - QK_ROPE)], axis=1)
    sa_t = jnp.concatenate([-sin, z(LANES - half)], axis=1)
    sb_t = jnp.concatenate([z(half), sin, z(LANES - QK_ROPE)], axis=1)
    return cos_t, sa_t, sb_t


def kernel(x, norm_g, w_in, q_a_norm, w_q_b, kv_a_norm, w_kv_b, q_norm, k_norm, conv_dw_w, conv_dw_b, conv_ln_g,
           conv_ln_b, conv_pw_w, ssd_conv_w, ssd_conv_b, ssd_dt_bias, ssd_A_log, ssd_D, ssd_norm_g, w_out):
    bsz, seq, d = x.shape
    m = bsz * seq
    x2 = x.reshape(m, d)
    tabs = _rope_tables(seq)

    w_in_k = _prep_w_in(w_in)
    wq, wkv, wpw = _prep_small(w_q_b, w_kv_b, conv_pw_w)
    w_out_k = _prep_cast(w_out, min(512, w_out.shape[1]))
    g_in, qan, kvan = _rows(norm_g), _rows(q_a_norm), _rows(kv_a_norm)
    kn = _pad_rows(k_norm, 0, QK_PAD)
    half = QK_ROPE // 2
    q_swapped = jnp.concatenate([q_norm[:, QK_NOPE + half:], q_norm[:, QK_NOPE:QK_NOPE + half]], axis=1)
    qn = jnp.concatenate([_pad_rows(q_norm, 0, QK_PAD), _pad_rows(q_swapped, 0, LANES)], axis=2)
    wdw = jnp.pad(conv_dw_w.astype(F32), ((0, 0), (0, CONV_HALO - CONV_K), (0, 0)))
    bdw, lng, lnb = _rows(conv_dw_b), _rows(conv_ln_g), _rows(conv_ln_b)
    cw = jnp.pad(ssd_conv_w.astype(F32), ((0, 0), (0, SUBLANES - SSD_CONV_K), (0, 0)))
    cb, ng = _rows(ssd_conv_b), _rows(ssd_norm_g)
    dtb, alog = _pad_rows(ssd_dt_bias, DT_LANE0, LANES), _pad_rows(ssd_A_log, DT_LANE0, LANES)
    dsk = _rows(jnp.repeat(ssd_D, SSD_HEAD_DIM, axis=1))

    tm_in = min(512, m)
    tm_out = min(512, m)
    t_seq = min(512, seq)
    for layer in range(norm_g.shape[0]):
        cq, ckv, kd, gmla, uconv, gconv, z, xbc = _in_proj(layer, x2, g_in, w_in_k, tm_in)
        q, k, v = _mla_prep(layer, cq, ckv, kd, qan, wq, kvan, wkv, qn, kn, *tabs, bsz, seq, t_seq)
        o_mla = _attention(q, k, v, gmla, bsz, seq, t_seq)
        o_conv = _conv(layer, uconv, gconv, wdw, bdw, lng, lnb, wpw, bsz, seq, t_seq)
        o_ssd = _ssd(layer, z, xbc, kd, cw, cb, dtb, alog, dsk, ng, bsz, seq, t_seq)
        x2 = _out_proj(layer, x2, o_mla, o_conv, o_ssd, w_out_k, tm_out)
    return x2.reshape(bsz, seq, d)
```

```python
import functools
import math

import jax
import jax.numpy as jnp
from jax import lax
from jax.experimental import pallas as pl
from jax.experimental.pallas import tpu as pltpu

F32 = jnp.float32
BF16 = jnp.bfloat16

MLA_HEADS = 6
QK_NOPE = 128
QK_ROPE = 64
QK_HEAD = QK_NOPE + QK_ROPE
V_HEAD = 128
Q_LORA = 512
KV_LORA = 256
MLA_WIDTH = MLA_HEADS * V_HEAD
ROPE_THETA = 10000.0
CONV_WIDTH = 512
CONV_K = 31
SSD_HEADS = 12
SSD_HEAD_DIM = 64
SSD_WIDTH = SSD_HEADS * SSD_HEAD_DIM
SSD_GROUPS = 2
SSD_STATE = 128
SSD_CONV_K = 4
SSD_CHUNK = 128
SSD_XBC = SSD_WIDTH + 2 * SSD_GROUPS * SSD_STATE
NORM_EPS = 1e-6
LN_EPS = 1e-5

LANES = 128
SUBLANES = 8
QK_PAD = 2 * LANES
Q_UP = 3 * LANES
KD_WIDTH = LANES
DT_LANE0 = QK_ROPE
CONV_HALO = 32
SSD_HALO = SUBLANES
VMEM_LIMIT = 56 * 1024 * 1024
ATTN_STRIP = 64

SEG_WIDTHS = (Q_LORA, KV_LORA, KD_WIDTH, MLA_WIDTH, 2 * CONV_WIDTH, CONV_WIDTH, SSD_WIDTH, SSD_XBC)


def _sigmoid(x):
    return 1.0 / (1.0 + jnp.exp(-x))


def _silu(x):
    return x * _sigmoid(x)


def _softplus(x):
    return jnp.maximum(x, 0.0) + jnp.log1p(jnp.exp(-jnp.abs(x)))


def _rms(x, g, eps=NORM_EPS):
    ms = jnp.mean(x * x, axis=-1, keepdims=True)
    return x * lax.rsqrt(ms + eps) * g


def _params(*sem):
    return pltpu.CompilerParams(dimension_semantics=sem, vmem_limit_bytes=VMEM_LIMIT)


def _resident(stacked, layer):
    _, a, b = stacked.shape
    return pl.BlockSpec((None, a, b), lambda *_: (layer, 0, 0), pipeline_mode=pl.Buffered(1))


W_IN_HALF = LANES // 2


def _prep_w_in_kernel(a_ref, b_ref, o_ref, *, depth, n_kt, n_dt):
    j = pl.program_id(0)
    kd_tile = (Q_LORA + KV_LORA) // LANES
    stride = depth * n_kt
    row = lax.broadcasted_iota(jnp.int32, (W_IN_HALF, LANES), 0)
    keep_b = (j != kd_tile) | (row < n_dt)
    for l in range(depth):
        for kt in range(n_kt):
            xa = a_ref[pl.ds(kt * depth + l, W_IN_HALF, stride=stride), :]
            xb = b_ref[pl.ds(kt * depth + l, W_IN_HALF, stride=stride), :]
            x = jnp.concatenate([xa, jnp.where(keep_b, xb, 0.0)], axis=0)
            o_ref[l, kt * LANES:(kt + 1) * LANES, :] = x.T.astype(o_ref.dtype)


def _prep_w_in(w_in):
    depth, d, n = w_in.shape
    n_kt = d // LANES
    n_out = sum(SEG_WIDTHS)
    rows_per_group = W_IN_HALF * n_kt * depth
    view = w_in.transpose(2, 0, 1).reshape(n, depth, n_kt, LANES).transpose(0, 2, 1, 3).reshape(-1, LANES)
    kd_tile = (Q_LORA + KV_LORA) // LANES
    dt_group = (n - SSD_HEADS) // W_IN_HALF
    assert (n - SSD_HEADS) % W_IN_HALF == 0 and (Q_LORA + KV_LORA) % LANES == 0 and QK_ROPE == W_IN_HALF
    a_idx = lambda j: (jnp.where(j <= kd_tile, 2 * j, 2 * j - 1), 0)
    b_idx = lambda j: (jnp.where(j < kd_tile, 2 * j + 1, jnp.where(j == kd_tile, dt_group, 2 * j)), 0)
    return pl.pallas_call(
        functools.partial(_prep_w_in_kernel, depth=depth, n_kt=n_kt, n_dt=SSD_HEADS),
        grid=(n_out // LANES,),
        in_specs=[pl.BlockSpec((rows_per_group, LANES), a_idx),
                  pl.BlockSpec((rows_per_group, LANES), b_idx)],
        out_specs=pl.BlockSpec((depth, d, LANES), lambda j: (0, 0, j)),
        out_shape=jax.ShapeDtypeStruct((depth, d, n_out), BF16),
        compiler_params=_params("parallel"),
        name="prep_w_in",
    )(view, view)


def _prep_small_kernel(wq_ref, wkv_ref, wpw_ref, oq_ref, okv_ref, opw_ref):
    wq = wq_ref[...]
    zq = jnp.zeros((wq.shape[0], QK_PAD - QK_HEAD), F32)
    half = QK_ROPE // 2
    parts = []
    for h in range(MLA_HEADS):
        r0 = h * QK_HEAD + QK_NOPE
        parts += [wq[:, h * QK_HEAD:(h + 1) * QK_HEAD], zq,
                  wq[:, r0 + half:r0 + QK_ROPE], wq[:, r0:r0 + half], zq]
    oq_ref[...] = jnp.concatenate(parts, axis=1).astype(oq_ref.dtype)
    wkv = wkv_ref[...]
    per = QK_NOPE + V_HEAD
    k_cols = [wkv[:, h * per:h * per + QK_NOPE] for h in range(MLA_HEADS)]
    v_cols = [wkv[:, h * per + QK_NOPE:(h + 1) * per] for h in range(MLA_HEADS)]
    okv_ref[...] = jnp.concatenate(k_cols + v_cols, axis=1).astype(okv_ref.dtype)
    opw_ref[...] = wpw_ref[...].astype(opw_ref.dtype)


def _prep_small(w_q_b, w_kv_b, conv_pw_w):
    depth = w_q_b.shape[0]
    whole = lambda a: pl.BlockSpec((None,) + a.shape[1:], lambda l: (l, 0, 0))
    shapes = [(depth, Q_LORA, MLA_HEADS * Q_UP), w_kv_b.shape, conv_pw_w.shape]
    return pl.pallas_call(
        _prep_small_kernel,
        grid=(depth,),
        in_specs=[whole(w_q_b), whole(w_kv_b), whole(conv_pw_w)],
        out_specs=[pl.BlockSpec((None,) + s[1:], lambda l: (l, 0, 0)) for s in shapes],
        out_shape=[jax.ShapeDtypeStruct(s, BF16) for s in shapes],
        compiler_params=_params("parallel"),
        name="prep_small",
    )(w_q_b, w_kv_b, conv_pw_w)


OUT_PIECE = 2 * LANES
CONV_SUB = 32
ANCHOR = 16


def _cast_kernel(w_ref, o_ref):
    for p in range(o_ref.shape[0]):
        o_ref[p] = w_ref[:, p * OUT_PIECE:(p + 1) * OUT_PIECE].astype(o_ref.dtype)


def _prep_cast(w, tk):
    depth, k, n = w.shape
    n_piece = n // OUT_PIECE
    return pl.pallas_call(
        _cast_kernel,
        grid=(depth, k // tk),
        in_specs=[pl.BlockSpec((None, tk, n), lambda l, i: (l, i, 0))],
        out_specs=pl.BlockSpec((None, n_piece, tk, OUT_PIECE), lambda l, i: (l, 0, i, 0)),
        out_shape=jax.ShapeDtypeStruct((depth, n_piece, k, OUT_PIECE), BF16),
        compiler_params=_params("parallel", "parallel"),
        name="prep_cast",
    )(w)


def _in_proj_kernel(x_ref, g_ref, w_ref, *out_refs):
    h = _rms(x_ref[...], g_ref[...]).astype(BF16)
    off = 0
    for o_ref in out_refs:
        n = o_ref.shape[-1]
        o_ref[...] = jnp.dot(h, w_ref[:, off:off + n], preferred_element_type=F32).astype(o_ref.dtype)
        off += n


def _in_proj(layer, x2, g, w, tm):
    m, d = x2.shape
    assert w.shape[2] == sum(SEG_WIDTHS)
    return pl.pallas_call(
        _in_proj_kernel,
        grid=(m // tm,),
        in_specs=[pl.BlockSpec((tm, d), lambda i: (i, 0)),
                  _resident(g, layer),
                  _resident(w, layer)],
        out_specs=[pl.BlockSpec((tm, n), lambda i: (i, 0)) for n in SEG_WIDTHS],
        out_shape=[jax.ShapeDtypeStruct((m, n), F32) for n in SEG_WIDTHS],
        compiler_params=_params("parallel"),
        name="in_proj",
    )(x2, g, w)


def _mla_prep_kernel(cq_ref, ckv_ref, kd_ref, qan_ref, wq_ref, kvan_ref, wkv_ref, qn_ref, kn_ref,
                     cos_ref, sa_ref, sb_ref, q_ref, k_ref, v_ref):
    scale = math.log2(math.e) / math.sqrt(QK_HEAD)
    hq = _rms(cq_ref[...], qan_ref[...]).astype(BF16)
    qf = jnp.dot(hq, wq_ref[...], preferred_element_type=F32)
    hkv = _rms(ckv_ref[...], kvan_ref[...]).astype(BF16)
    kvf = jnp.dot(hkv, wkv_ref[...], preferred_element_type=F32)

    cos = cos_ref[...]
    sa = sa_ref[...]
    sb = sb_ref[...]

    def rope(r):
        return r * cos + pltpu.roll(r, LANES - QK_ROPE // 2, 1) * sa + pltpu.roll(r, QK_ROPE // 2, 1) * sb

    kd = kd_ref[...]
    lane = lax.broadcasted_iota(jnp.int32, kd.shape, 1)
    kpe = jnp.where(lane < QK_ROPE, kd, 0.0)
    kpe_ss = jnp.sum(kpe * kpe, axis=-1, keepdims=True)

    qn_a = qn_ref[:, 0:LANES] * scale
    qn_b = qn_ref[:, LANES:QK_PAD] * scale
    qn_s = qn_ref[:, QK_PAD:Q_UP] * scale
    sin_signed = sa + sb
    kn_w = kn_ref[...]
    kpe_rot = rope(kpe * kn_w[:, LANES:QK_PAD])
    for h in range(MLA_HEADS):
        qa = qf[:, h * Q_UP:h * Q_UP + LANES]
        qb = qf[:, h * Q_UP + LANES:h * Q_UP + QK_PAD]
        qs = qf[:, h * Q_UP + QK_PAD:(h + 1) * Q_UP]
        ss = jnp.sum(qa * qa + qb * qb, axis=-1, keepdims=True)
        inv = lax.rsqrt(ss * (1.0 / QK_HEAD) + NORM_EPS)
        q_ref[0, h, :, 0:LANES] = (qa * inv * qn_a).astype(q_ref.dtype)
        q_ref[0, h, :, LANES:QK_PAD] = (inv * (qb * qn_b * cos + qs * qn_s * sin_signed)).astype(q_ref.dtype)

        ka = kvf[:, h * LANES:(h + 1) * LANES]
        ss = jnp.sum(ka * ka, axis=-1, keepdims=True) + kpe_ss
        inv = lax.rsqrt(ss * (1.0 / QK_HEAD) + NORM_EPS)
        k_ref[0, h, :, 0:LANES] = (ka * inv * kn_w[:, 0:LANES]).astype(k_ref.dtype)
        k_ref[0, h, :, LANES:QK_PAD] = (kpe_rot * inv).astype(k_ref.dtype)

        v_ref[0, h, :, :] = kvf[:, MLA_WIDTH + h * V_HEAD:MLA_WIDTH + (h + 1) * V_HEAD].astype(v_ref.dtype)


def _mla_prep(layer, cq, ckv, kd, qan, wq, kvan, wkv, qn, kn, cos_t, sa_t, sb_t, bsz, seq, tm):
    nt = seq // tm
    row = lambda b, i: (b * nt + i, 0)
    pos = lambda b, i: (i, 0)
    hd = MLA_HEADS
    return pl.pallas_call(
        _mla_prep_kernel,
        grid=(bsz, nt),
        in_specs=[pl.BlockSpec((tm, Q_LORA), row),
                  pl.BlockSpec((tm, KV_LORA), row),
                  pl.BlockSpec((tm, KD_WIDTH), row),
                  _resident(qan, layer), _resident(wq, layer), _resident(kvan, layer), _resident(wkv, layer),
                  _resident(qn, layer), _resident(kn, layer),
                  pl.BlockSpec((tm, LANES), pos), pl.BlockSpec((tm, LANES), pos), pl.BlockSpec((tm, LANES), pos)],
        out_specs=[pl.BlockSpec((1, hd, tm, QK_PAD), lambda b, i: (b, 0, i, 0)),
                   pl.BlockSpec((1, hd, tm, QK_PAD), lambda b, i: (b, 0, i, 0)),
                   pl.BlockSpec((1, hd, tm, V_HEAD), lambda b, i: (b, 0, i, 0))],
        out_shape=[jax.ShapeDtypeStruct((bsz, hd, seq, QK_PAD), BF16),
                   jax.ShapeDtypeStruct((bsz, hd, seq, QK_PAD), BF16),
                   jax.ShapeDtypeStruct((bsz, hd, seq, V_HEAD), BF16)],
        compiler_params=_params("parallel", "parallel"),
        name="mla_prep",
    )(cq, ckv, kd, qan, wq, kvan, wkv, qn, kn, cos_t, sa_t, sb_t)


def _attn_kernel(qi_ref, ki_ref, q_ref, k_ref, v_ref, g_ref, o_ref, m_sc, l_sc, acc_sc, s_sc, p_sc, a_sc):
    qi = qi_ref[pl.program_id(1)]
    ki = ki_ref[pl.program_id(1)]
    tq = q_ref.shape[2]
    tk = k_ref.shape[2]

    @pl.when(ki == 0)
    def _():
        m_sc[...] = jnp.full(m_sc.shape, -jnp.inf, F32)
        l_sc[...] = jnp.zeros(l_sc.shape, F32)
        acc_sc[...] = jnp.zeros(acc_sc.shape, F32)

    strip = min(ATTN_STRIP, tq)
    n_chunk = tk // LANES

    def scores(h):
        s_sc[h % 2] = lax.dot_general(q_ref[0, h], k_ref[0, h], (((1,), (1,)), ((), ())),
                                      preferred_element_type=F32)

    def softmax_strip(h, r0, masked):
        slot = h % 2
        rows = slice(r0, r0 + strip)
        live = [c for c in range(n_chunk) if not (masked and c * LANES >= r0 + strip)]
        chunks = []
        for c in live:
            x = s_sc[slot, rows, c * LANES:(c + 1) * LANES]
            if masked and (c + 1) * LANES - 1 > r0:
                rr = r0 + lax.broadcasted_iota(jnp.int32, (strip, LANES), 0)
                cc = c * LANES + lax.broadcasted_iota(jnp.int32, (strip, LANES), 1)
                x = jnp.where(rr >= cc, x, -jnp.inf)
            chunks.append(x)
        m_prev = m_sc[h, rows, :]
        m_loc = functools.reduce(jnp.maximum, chunks)
        m_new = jnp.maximum(m_prev, jnp.max(m_loc, axis=-1, keepdims=True))
        alpha = jnp.exp2(m_prev - m_new)
        ps = [jnp.exp2(x - m_new) for x in chunks]
        l_sc[h, rows, :] = alpha * l_sc[h, rows, :] + functools.reduce(jnp.add, ps)
        m_sc[h, rows, :] = m_new
        a_sc[slot, rows, :] = alpha
        dead = [jnp.zeros((strip, LANES), BF16)] * (n_chunk - len(live))
        p_sc[slot, rows, :] = jnp.concatenate([p.astype(BF16) for p in ps] + dead, axis=1)

    def step(masked):
        scores(0)
        for h in range(MLA_HEADS):
            if h + 1 < MLA_HEADS:
                scores(h + 1)
            for r0 in range(0, tq, strip):
                softmax_strip(h, r0, masked)
            pv = jnp.dot(p_sc[h % 2], v_ref[0, h], preferred_element_type=F32)
            acc_sc[h] = a_sc[h % 2] * acc_sc[h] + pv

    @pl.when(ki < qi)
    def _():
        step(False)

    @pl.when(ki == qi)
    def _():
        step(True)
        for h in range(MLA_HEADS):
            o = acc_sc[h] / jnp.sum(l_sc[h], axis=-1, keepdims=True)
            g = g_ref[:, h * V_HEAD:(h + 1) * V_HEAD]
            o_ref[:, h * V_HEAD:(h + 1) * V_HEAD] = (o * _silu(g)).astype(o_ref.dtype)


def _attention(q, k, v, gmla, bsz, seq, tq):
    hd = MLA_HEADS
    nq = seq // tq
    pairs = [(i, j) for i in range(nq) for j in range(i + 1)]
    qi_tab = jnp.asarray([i for i, _ in pairs], jnp.int32)
    ki_tab = jnp.asarray([j for _, j in pairs], jnp.int32)
    q_idx = lambda b, t, qi, ki: (b, 0, qi[t], 0)
    kv_idx = lambda b, t, qi, ki: (b, 0, ki[t], 0)
    row = lambda b, t, qi, ki: (b * nq + qi[t], 0)
    grid_spec = pltpu.PrefetchScalarGridSpec(
        num_scalar_prefetch=2,
        grid=(bsz, len(pairs)),
        in_specs=[pl.BlockSpec((1, hd, tq, QK_PAD), q_idx),
                  pl.BlockSpec((1, hd, tq, QK_PAD), kv_idx),
                  pl.BlockSpec((1, hd, tq, V_HEAD), kv_idx),
                  pl.BlockSpec((tq, MLA_WIDTH), row)],
        out_specs=pl.BlockSpec((tq, MLA_WIDTH), row),
        scratch_shapes=[pltpu.VMEM((hd, tq, LANES), F32), pltpu.VMEM((hd, tq, LANES), F32),
                        pltpu.VMEM((hd, tq, V_HEAD), F32),
                        pltpu.VMEM((2, tq, tq), F32), pltpu.VMEM((2, tq, tq), BF16),
                        pltpu.VMEM((2, tq, LANES), F32)])
    return pl.pallas_call(
        _attn_kernel,
        grid_spec=grid_spec,
        out_shape=jax.ShapeDtypeStruct((bsz * seq, MLA_WIDTH), BF16),
        compiler_params=_params("parallel", "arbitrary"),
        name="attention",
    )(qi_tab, ki_tab, q, k, v, gmla)


def _split3(x):
    hi = x.astype(BF16)
    r1 = x - hi.astype(F32)
    mid = r1.astype(BF16)
    lo = (r1 - mid.astype(F32)).astype(BF16)
    return hi, mid, lo


def _ssd_kernel(z_ref, xbc_ref, kd_ref, cw_ref, cb_ref, dtb_ref, alog_ref, dsk_ref, ng_ref, o_ref,
                xbuf, xact, state, shx):
    i = pl.program_id(0)
    nb, ts = o_ref.shape[0], o_ref.shape[1]
    L = SSD_CHUNK
    hp = SSD_HEAD_DIM
    gw = SSD_WIDTH // SSD_GROUPS
    pairs_per_group = SSD_HEADS // SSD_GROUPS // 2
    npair = SSD_HEADS // 2

    @pl.when(i == 0)
    def _():
        xbuf[:, 0:SSD_HALO, :] = jnp.zeros((nb, SSD_HALO, SSD_XBC), F32)
        state[...] = jnp.zeros(state.shape, F32)

    @pl.when(i > 0)
    def _():
        xbuf[:, 0:SSD_HALO, :] = xbuf[:, ts:ts + SSD_HALO, :]

    for b in range(nb):
        xbuf[b, SSD_HALO:SSD_HALO + ts, :] = xbc_ref[b]
        for s in range(1, SSD_CONV_K):
            shx[s - 1, :, :] = xbuf[b, SSD_HALO - s:SSD_HALO - s + ts, :]
        acc = cb_ref[...] + cw_ref[SSD_CONV_K - 1:SSD_CONV_K, :] * xbc_ref[b]
        for s in range(1, SSD_CONV_K):
            acc = acc + cw_ref[SSD_CONV_K - 1 - s:SSD_CONV_K - s, :] * shx[s - 1]
        xact[b] = _silu(acc)

    lane = lax.broadcasted_iota(jnp.int32, (1, LANES), 1)
    is_dt = (lane >= DT_LANE0) & (lane < DT_LANE0 + SSD_HEADS)
    a_neg = jnp.where(is_dt, -jnp.exp(alog_ref[...]), 0.0)
    rr = lax.broadcasted_iota(jnp.int32, (L, L), 0)
    cc = lax.broadcasted_iota(jnp.int32, (L, L), 1)
    causal = rr >= cc
    tri = causal.astype(BF16)
    left = lax.broadcasted_iota(jnp.int32, (L, LANES), 1) < hp

    def decay_geometry(b, r0):
        dt = _softplus(kd_ref[b, pl.ds(r0, L), :] + dtb_ref[...])
        hi, mid, lo = _split3(dt * a_neg)
        acs = (jnp.dot(tri, hi, preferred_element_type=F32)
               + jnp.dot(tri, mid, preferred_element_type=F32)
               + jnp.dot(tri, lo, preferred_element_type=F32))
        return dict(dt=dt, acs=acs, acs_t=acs.T)

    def expand(b, r0, v):
        def col(a, h):
            return jnp.broadcast_to(a[:, DT_LANE0 + h:DT_LANE0 + h + 1], (L, LANES))

        dt_cols = [col(v["dt"], h) for h in range(SSD_HEADS)]
        acs_cols = [col(v["acs"], h) for h in range(SSD_HEADS)]
        dt_exp = jnp.concatenate([jnp.where(left, dt_cols[2 * j], dt_cols[2 * j + 1]) for j in range(npair)], axis=1)
        acs_exp = jnp.concatenate([jnp.where(left, acs_cols[2 * j], acs_cols[2 * j + 1]) for j in range(npair)], axis=1)
        last = acs_exp[L - 1:L, :]
        xs = xact[b, pl.ds(r0, L), 0:SSD_WIDTH]
        xdt = xs * dt_exp
        v.update(acs_cols=acs_cols, last=last, xs=xs, xdt_b=xdt.astype(BF16),
                 xdt_end=(xdt * jnp.exp(last - acs_exp)).astype(BF16), grow=jnp.exp(acs_exp))

    def mix(b, r0, v):
        st_prev = state[b]
        st_prev_b = st_prev.astype(BF16)
        y_parts = []
        st_parts = []
        for g in range(SSD_GROUPS):
            c0 = SSD_WIDTH + g * SSD_STATE
            c1 = SSD_WIDTH + (SSD_GROUPS + g) * SSD_STATE
            bg = xact[b, pl.ds(r0, L), c0:c0 + SSD_STATE]
            cg_b = xact[b, pl.ds(r0, L), c1:c1 + SSD_STATE].astype(BF16)
            cb = lax.dot_general(cg_b, bg.astype(BF16), (((1,), (1,)), ((), ())), preferred_element_type=F32)
            st_parts.append(jnp.dot(bg.T.astype(BF16), v["xdt_end"][:, g * gw:(g + 1) * gw],
                                    preferred_element_type=F32))
            y_off = jnp.dot(cg_b, st_prev_b[:, g * gw:(g + 1) * gw], preferred_element_type=F32)
            y_off = y_off * v["grow"][:, g * gw:(g + 1) * gw]
            for jj in range(pairs_per_group):
                j = g * pairs_per_group + jj
                pair = v["xdt_b"][:, j * LANES:(j + 1) * LANES]
                yd = []
                for h in (2 * j, 2 * j + 1):
                    seg = v["acs_cols"][h] - v["acs_t"][DT_LANE0 + h:DT_LANE0 + h + 1, :]
                    decay = jnp.exp(jnp.where(causal, seg, -jnp.inf))
                    yd.append(jnp.dot((cb * decay).astype(BF16), pair, preferred_element_type=F32))
                y_parts.append(jnp.where(left, yd[0], yd[1]) + y_off[:, jj * LANES:(jj + 1) * LANES])
        state[b] = st_prev * jnp.exp(v["last"]) + jnp.concatenate(st_parts, axis=1)
        v["y"] = jnp.concatenate(y_parts, axis=1) + dsk_ref[...] * v["xs"]

    def finish(b, r0, v):
        yz = v["y"] * _silu(z_ref[b, pl.ds(r0, L), :])
        o_ref[b, pl.ds(r0, L), :] = _rms(yz, ng_ref[...]).astype(o_ref.dtype)

    def chunk(c, carry):
        r0 = pl.multiple_of(c * L, L)
        vals = [decay_geometry(b, r0) for b in range(nb)]
        for stage in (expand, mix, finish):
            for b in range(nb):
                stage(b, r0, vals[b])
        return carry

    lax.fori_loop(0, ts // L, chunk, 0)


def _ssd(layer, z, xbc, kd, cw, cb, dtb, alog, dsk, ng, bsz, seq, ts):
    blk = lambda w: pl.BlockSpec((bsz, ts, w), lambda i: (0, i, 0))
    out = pl.pallas_call(
        _ssd_kernel,
        grid=(seq // ts,),
        in_specs=[blk(SSD_WIDTH), blk(SSD_XBC), blk(KD_WIDTH),
                  _resident(cw, layer), _resident(cb, layer), _resident(dtb, layer), _resident(alog, layer),
                  _resident(dsk, layer), _resident(ng, layer)],
        out_specs=blk(SSD_WIDTH),
        out_shape=jax.ShapeDtypeStruct((bsz, seq, SSD_WIDTH), BF16),
        scratch_shapes=[pltpu.VMEM((bsz, ts + SSD_HALO, SSD_XBC), F32), pltpu.VMEM((bsz, ts, SSD_XBC), F32),
                        pltpu.VMEM((bsz, SSD_STATE, SSD_WIDTH), F32),
                        pltpu.VMEM((SSD_CONV_K - 1, ts, SSD_XBC), F32)],
        compiler_params=_params("arbitrary"),
        name="ssd",
    )(z.reshape(bsz, seq, -1), xbc.reshape(bsz, seq, -1), kd.reshape(bsz, seq, -1), cw, cb, dtb, alog, dsk, ng)
    return out.reshape(bsz * seq, SSD_WIDTH)


def _conv_out_kernel(u_ref, gc_ref, x_ref, a_ref, s_ref, wdw_ref, bdw_ref, lng_ref, lnb_ref, wpw_ref, w_ref,
                     o_ref, hbuf, shbuf, ybuf, oc_buf, *, tiles_per_seq, strip):
    i = pl.program_id(0)
    tt = u_ref.shape[0]
    cw = CONV_WIDTH
    n_piece = w_ref.shape[0]
    assert tt // strip == n_piece
    seq_start = lax.rem(i, tiles_per_seq) == 0

    @pl.when(i == 0)
    def _():
        oc_buf[...] = jnp.zeros(oc_buf.shape, oc_buf.dtype)

    @pl.when(seq_start)
    def _():
        hbuf[0:CONV_HALO, :] = jnp.zeros((CONV_HALO, cw), F32)

    @pl.when(jnp.logical_not(seq_start))
    def _():
        hbuf[0:CONV_HALO, :] = hbuf[tt:tt + CONV_HALO, :]

    hbuf[CONV_HALO:CONV_HALO + tt, :] = u_ref[:, 0:cw] * _sigmoid(u_ref[:, cw:2 * cw])

    base = CONV_HALO - (CONV_K - 1)
    span = tt + CONV_HALO - SUBLANES
    for res in range(1, SUBLANES):
        shbuf[res - 1, 0:span, :] = hbuf[res:res + span, :]

    n0 = a_ref.shape[1]
    n1 = n0 + cw
    n2 = n1 + s_ref.shape[1]

    dyn0 = pl.multiple_of(jnp.minimum(i, 0), ANCHOR)
    for p in range(n_piece):
        cols = slice(p * OUT_PIECE, (p + 1) * OUT_PIECE)
        y = jnp.dot(oc_buf[pl.ds(dyn0, tt), :], w_ref[p, n0:n1, :], preferred_element_type=F32)
        y = y + jnp.dot(a_ref[...], w_ref[p, 0:n0, :], preferred_element_type=F32)
        y = y + jnp.dot(s_ref[...], w_ref[p, n1:n2, :], preferred_element_type=F32)
        o_ref[:, cols] = x_ref[:, cols] + y
        mark = None
        for r0 in range(p * strip, (p + 1) * strip, CONV_SUB):
            acc = jnp.broadcast_to(bdw_ref[...], (CONV_SUB, cw))
            for kk in range(CONV_K):
                res = (base + kk) % SUBLANES
                start = r0 + (base + kk - res)
                tap = hbuf[start:start + CONV_SUB, :] if res == 0 else shbuf[res - 1, start:start + CONV_SUB, :]
                acc = acc + wdw_ref[kk:kk + 1, :] * tap
            ybuf[r0:r0 + CONV_SUB, :] = acc
            part = acc[0:ANCHOR, :] + acc[ANCHOR:2 * ANCHOR, :]
            mark = part if mark is None else mark + part
        oc_buf[pl.ds(tt + dyn0, ANCHOR), :] = mark.astype(oc_buf.dtype)

    y = ybuf[...]
    mu = jnp.mean(y, axis=-1, keepdims=True)
    yc = y - mu
    var = jnp.mean(yc * yc, axis=-1, keepdims=True)
    hn = _silu(yc * lax.rsqrt(var + LN_EPS) * lng_ref[...] + lnb_ref[...])
    out = jnp.dot(hn.astype(BF16), wpw_ref[...], preferred_element_type=F32)
    oc_buf[0:tt, :] = (out * _silu(gc_ref[...])).astype(oc_buf.dtype)


def _conv_out(layer, uconv, gconv, x2, o_mla, o_ssd, wdw, bdw, lng, lnb, wpw, w_out, seq, tt):
    m, d = x2.shape
    n = m // tt
    n_piece = w_out.shape[1]
    cur_row = lambda i: (jnp.minimum(i, n - 1), 0)
    prev_row = lambda i: (jnp.maximum(i - 1, 0), 0)
    w_spec = pl.BlockSpec((None,) + w_out.shape[1:], lambda i: (layer, 0, 0, 0), pipeline_mode=pl.Buffered(1))
    return pl.pallas_call(
        functools.partial(_conv_out_kernel, tiles_per_seq=seq // tt, strip=tt // n_piece),
        grid=(n + 1,),
        in_specs=[pl.BlockSpec((tt, 2 * CONV_WIDTH), cur_row),
                  pl.BlockSpec((tt, CONV_WIDTH), cur_row),
                  pl.BlockSpec((tt, d), prev_row),
                  pl.BlockSpec((tt, o_mla.shape[1]), prev_row),
                  pl.BlockSpec((tt, o_ssd.shape[1]), prev_row),
                  _resident(wdw, layer), _resident(bdw, layer), _resident(lng, layer), _resident(lnb, layer),
                  _resident(wpw, layer), w_spec],
        out_specs=pl.BlockSpec((tt, d), prev_row),
        out_shape=jax.ShapeDtypeStruct((m, d), F32),
        scratch_shapes=[pltpu.VMEM((tt + CONV_HALO, CONV_WIDTH), F32),
                        pltpu.VMEM((SUBLANES - 1, tt + CONV_HALO, CONV_WIDTH), F32),
                        pltpu.VMEM((tt, CONV_WIDTH), F32),
                        pltpu.VMEM((tt + ANCHOR, CONV_WIDTH), BF16)],
        compiler_params=_params("arbitrary"),
        name="conv_out_proj",
    )(uconv, gconv, x2, o_mla, o_ssd, wdw, bdw, lng, lnb, wpw, w_out)


def _rows(p):
    return p.astype(F32)[:, None, :]


def _pad_rows(p, lane0, width):
    return jnp.pad(p.astype(F32), ((0, 0), (lane0, width - lane0 - p.shape[1])))[:, None, :]


def _rope_tables(seq):
    half = QK_ROPE // 2
    inv_freq = ROPE_THETA ** (-jnp.arange(half, dtype=F32) / half)
    ang = jnp.arange(seq).astype(F32)[:, None] * inv_freq[None, :]
    cos, sin = jnp.cos(ang), jnp.sin(ang)
    z = lambda n: jnp.zeros((seq, n), F32)
    cos_t = jnp.concatenate([cos, cos, z(LANES - QK_ROPE)], axis=1)
    sa_t = jnp.concatenate([-sin, z(LANES - half)], axis=1)
    sb_t = jnp.concatenate([z(half), sin, z(LANES - QK_ROPE)], axis=1)
    return cos_t, sa_t, sb_t


def kernel(x, norm_g, w_in, q_a_norm, w_q_b, kv_a_norm, w_kv_b, q_norm, k_norm, conv_dw_w, conv_dw_b, conv_ln_g,
           conv_ln_b, conv_pw_w, ssd_conv_w, ssd_conv_b, ssd_dt_bias, ssd_A_log, ssd_D, ssd_norm_g, w_out):
    bsz, seq, d = x.shape
    m = bsz * seq
    x2 = x.reshape(m, d)
    tabs = _rope_tables(seq)

    w_in_k = _prep_w_in(w_in)
    wq, wkv, wpw = _prep_small(w_q_b, w_kv_b, conv_pw_w)
    w_out_k = _prep_cast(w_out, min(512, w_out.shape[1]))
    g_in, qan, kvan = _rows(norm_g), _rows(q_a_norm), _rows(kv_a_norm)
    kn = _pad_rows(k_norm, 0, QK_PAD)
    half = QK_ROPE // 2
    q_swapped = jnp.concatenate([q_norm[:, QK_NOPE + half:], q_norm[:, QK_NOPE:QK_NOPE + half]], axis=1)
    qn = jnp.concatenate([_pad_rows(q_norm, 0, QK_PAD), _pad_rows(q_swapped, 0, LANES)], axis=2)
    wdw = jnp.pad(conv_dw_w.astype(F32), ((0, 0), (0, CONV_HALO - CONV_K), (0, 0)))
    bdw, lng, lnb = _rows(conv_dw_b), _rows(conv_ln_g), _rows(conv_ln_b)
    cw = jnp.pad(ssd_conv_w.astype(F32), ((0, 0), (0, SUBLANES - SSD_CONV_K), (0, 0)))
    cb, ng = _rows(ssd_conv_b), _rows(ssd_norm_g)
    dtb, alog = _pad_rows(ssd_dt_bias, DT_LANE0, LANES), _pad_rows(ssd_A_log, DT_LANE0, LANES)
    dsk = _rows(jnp.repeat(ssd_D, SSD_HEAD_DIM, axis=1))

    tm_in = min(512, m)
    t_seq = min(512, seq)
    for layer in range(norm_g.shape[0]):
        cq, ckv, kd, gmla, uconv, gconv, z, xbc = _in_proj(layer, x2, g_in, w_in_k, tm_in)
        q, k, v = _mla_prep(layer, cq, ckv, kd, qan, wq, kvan, wkv, qn, kn, *tabs, bsz, seq, t_seq)
        o_mla = _attention(q, k, v, gmla, bsz, seq, t_seq)
        o_ssd = _ssd(layer, z, xbc, kd, cw, cb, dtb, alog, dsk, ng, bsz, seq, t_seq)
        x2 = _conv_out(layer, uconv, gconv, x2, o_mla, o_ssd, wdw, bdw, lng, lnb, wpw, w_out_k, seq, t_seq)
    return x2.reshape(bsz, seq, d)
```

```python
import functools
import math

import jax
import jax.numpy as jnp
from jax import lax
from jax.experimental import pallas as pl
from jax.experimental.pallas import tpu as pltpu

F32 = jnp.float32
BF16 = jnp.bfloat16

MLA_HEADS = 6
QK_NOPE = 128
QK_ROPE = 64
QK_HEAD = QK_NOPE + QK_ROPE
V_HEAD = 128
Q_LORA = 512
KV_LORA = 256
MLA_WIDTH = MLA_HEADS * V_HEAD
ROPE_THETA = 10000.0
CONV_WIDTH = 512
CONV_K = 31
SSD_HEADS = 12
SSD_HEAD_DIM = 64
SSD_WIDTH = SSD_HEADS * SSD_HEAD_DIM
SSD_GROUPS = 2
SSD_STATE = 128
SSD_CONV_K = 4
SSD_CHUNK = 128
SSD_XBC = SSD_WIDTH + 2 * SSD_GROUPS * SSD_STATE
NORM_EPS = 1e-6
LN_EPS = 1e-5

LANES = 128
SUBLANES = 8
QK_PAD = 2 * LANES
Q_UP = 3 * LANES
KD_WIDTH = LANES
DT_LANE0 = QK_ROPE
CONV_HALO = 32
SSD_HALO = SUBLANES
VMEM_LIMIT = 56 * 1024 * 1024
ATTN_STRIP = 64

SEG_WIDTHS = (Q_LORA, KV_LORA, KD_WIDTH, MLA_WIDTH, 2 * CONV_WIDTH, CONV_WIDTH, SSD_WIDTH, SSD_XBC)


def _sigmoid(x):
    return 1.0 / (1.0 + jnp.exp(-x))


def _silu(x):
    return x * _sigmoid(x)


def _softplus(x):
    return jnp.maximum(x, 0.0) + jnp.log1p(jnp.exp(-jnp.abs(x)))


def _rms(x, g, eps=NORM_EPS):
    ms = jnp.mean(x * x, axis=-1, keepdims=True)
    return x * lax.rsqrt(ms + eps) * g


def _params(*sem):
    return pltpu.CompilerParams(dimension_semantics=sem, vmem_limit_bytes=VMEM_LIMIT)


def _resident(stacked, layer):
    _, a, b = stacked.shape
    return pl.BlockSpec((None, a, b), lambda *_: (layer, 0, 0), pipeline_mode=pl.Buffered(1))


W_IN_HALF = LANES // 2


def _prep_w_in_kernel(a_ref, b_ref, o_ref, *, depth, n_kt, n_dt):
    j = pl.program_id(0)
    kd_tile = (Q_LORA + KV_LORA) // LANES
    stride = depth * n_kt
    row = lax.broadcasted_iota(jnp.int32, (W_IN_HALF, LANES), 0)
    keep_b = (j != kd_tile) | (row < n_dt)
    for l in range(depth):
        for kt in range(n_kt):
            xa = a_ref[pl.ds(kt * depth + l, W_IN_HALF, stride=stride), :]
            xb = b_ref[pl.ds(kt * depth + l, W_IN_HALF, stride=stride), :]
            x = jnp.concatenate([xa, jnp.where(keep_b, xb, 0.0)], axis=0)
            o_ref[l, kt * LANES:(kt + 1) * LANES, :] = x.T.astype(o_ref.dtype)


def _prep_w_in(w_in):
    depth, d, n = w_in.shape
    n_kt = d // LANES
    n_out = sum(SEG_WIDTHS)
    rows_per_group = W_IN_HALF * n_kt * depth
    view = w_in.transpose(2, 0, 1).reshape(n, depth, n_kt, LANES).transpose(0, 2, 1, 3).reshape(-1, LANES)
    kd_tile = (Q_LORA + KV_LORA) // LANES
    dt_group = (n - SSD_HEADS) // W_IN_HALF
    assert (n - SSD_HEADS) % W_IN_HALF == 0 and (Q_LORA + KV_LORA) % LANES == 0 and QK_ROPE == W_IN_HALF
    a_idx = lambda j: (jnp.where(j <= kd_tile, 2 * j, 2 * j - 1), 0)
    b_idx = lambda j: (jnp.where(j < kd_tile, 2 * j + 1, jnp.where(j == kd_tile, dt_group, 2 * j)), 0)
    return pl.pallas_call(
        functools.partial(_prep_w_in_kernel, depth=depth, n_kt=n_kt, n_dt=SSD_HEADS),
        grid=(n_out // LANES,),
        in_specs=[pl.BlockSpec((rows_per_group, LANES), a_idx),
                  pl.BlockSpec((rows_per_group, LANES), b_idx)],
        out_specs=pl.BlockSpec((depth, d, LANES), lambda j: (0, 0, j)),
        out_shape=jax.ShapeDtypeStruct((depth, d, n_out), BF16),
        compiler_params=_params("parallel"),
        name="prep_w_in",
    )(view, view)


def _prep_small_kernel(wq_ref, wkv_ref, wpw_ref, oq_ref, okv_ref, opw_ref):
    wq = wq_ref[...]
    zq = jnp.zeros((wq.shape[0], QK_PAD - QK_HEAD), F32)
    half = QK_ROPE // 2
    parts = []
    for h in range(MLA_HEADS):
        r0 = h * QK_HEAD + QK_NOPE
        parts += [wq[:, h * QK_HEAD:(h + 1) * QK_HEAD], zq,
                  wq[:, r0 + half:r0 + QK_ROPE], wq[:, r0:r0 + half], zq]
    oq_ref[...] = jnp.concatenate(parts, axis=1).astype(oq_ref.dtype)
    wkv = wkv_ref[...]
    per = QK_NOPE + V_HEAD
    k_cols = [wkv[:, h * per:h * per + QK_NOPE] for h in range(MLA_HEADS)]
    v_cols = [wkv[:, h * per + QK_NOPE:(h + 1) * per] for h in range(MLA_HEADS)]
    okv_ref[...] = jnp.concatenate(k_cols + v_cols, axis=1).astype(okv_ref.dtype)
    opw_ref[...] = wpw_ref[...].astype(opw_ref.dtype)


def _prep_small(w_q_b, w_kv_b, conv_pw_w):
    depth = w_q_b.shape[0]
    whole = lambda a: pl.BlockSpec((None,) + a.shape[1:], lambda l: (l, 0, 0))
    shapes = [(depth, Q_LORA, MLA_HEADS * Q_UP), w_kv_b.shape, conv_pw_w.shape]
    return pl.pallas_call(
        _prep_small_kernel,
        grid=(depth,),
        in_specs=[whole(w_q_b), whole(w_kv_b), whole(conv_pw_w)],
        out_specs=[pl.BlockSpec((None,) + s[1:], lambda l: (l, 0, 0)) for s in shapes],
        out_shape=[jax.ShapeDtypeStruct(s, BF16) for s in shapes],
        compiler_params=_params("parallel"),
        name="prep_small",
    )(w_q_b, w_kv_b, conv_pw_w)


OUT_PIECE = 2 * LANES
CONV_SUB = 32
ANCHOR = 16


def _cast_kernel(w_ref, o_ref):
    for p in range(o_ref.shape[0]):
        o_ref[p] = w_ref[:, p * OUT_PIECE:(p + 1) * OUT_PIECE].astype(o_ref.dtype)


def _prep_cast(w, tk):
    depth, k, n = w.shape
    n_piece = n // OUT_PIECE
    return pl.pallas_call(
        _cast_kernel,
        grid=(depth, k // tk),
        in_specs=[pl.BlockSpec((None, tk, n), lambda l, i: (l, i, 0))],
        out_specs=pl.BlockSpec((None, n_piece, tk, OUT_PIECE), lambda l, i: (l, 0, i, 0)),
        out_shape=jax.ShapeDtypeStruct((depth, n_piece, k, OUT_PIECE), BF16),
        compiler_params=_params("parallel", "parallel"),
        name="prep_cast",
    )(w)


def _in_proj_kernel(x_ref, g_ref, w_ref, *out_refs):
    h = _rms(x_ref[...], g_ref[...]).astype(BF16)
    off = 0
    for o_ref in out_refs:
        n = o_ref.shape[-1]
        o_ref[...] = jnp.dot(h, w_ref[:, off:off + n], preferred_element_type=F32).astype(o_ref.dtype)
        off += n


def _in_proj(layer, x2, g, w, tm):
    m, d = x2.shape
    assert w.shape[2] == sum(SEG_WIDTHS)
    return pl.pallas_call(
        _in_proj_kernel,
        grid=(m // tm,),
        in_specs=[pl.BlockSpec((tm, d), lambda i: (i, 0)),
                  _resident(g, layer),
                  _resident(w, layer)],
        out_specs=[pl.BlockSpec((tm, n), lambda i: (i, 0)) for n in SEG_WIDTHS],
        out_shape=[jax.ShapeDtypeStruct((m, n), F32) for n in SEG_WIDTHS],
        compiler_params=_params("parallel"),
        name="in_proj",
    )(x2, g, w)


def _mla_prep_kernel(cq_ref, ckv_ref, kd_ref, qan_ref, wq_ref, kvan_ref, wkv_ref, qn_ref, kn_ref,
                     cos_ref, sa_ref, sb_ref, q_ref, k_ref, v_ref):
    scale = math.log2(math.e) / math.sqrt(QK_HEAD)
    hq = _rms(cq_ref[...], qan_ref[...]).astype(BF16)
    qf = jnp.dot(hq, wq_ref[...], preferred_element_type=F32)
    hkv = _rms(ckv_ref[...], kvan_ref[...]).astype(BF16)
    kvf = jnp.dot(hkv, wkv_ref[...], preferred_element_type=F32)

    cos = cos_ref[...]
    sa = sa_ref[...]
    sb = sb_ref[...]

    def rope(r):
        return r * cos + pltpu.roll(r, LANES - QK_ROPE // 2, 1) * sa + pltpu.roll(r, QK_ROPE // 2, 1) * sb

    kd = kd_ref[...]
    lane = lax.broadcasted_iota(jnp.int32, kd.shape, 1)
    kpe = jnp.where(lane < QK_ROPE, kd, 0.0)
    kpe_ss = jnp.sum(kpe * kpe, axis=-1, keepdims=True)

    qn_a = qn_ref[:, 0:LANES] * scale
    qn_b = qn_ref[:, LANES:QK_PAD] * scale
    qn_s = qn_ref[:, QK_PAD:Q_UP] * scale
    sin_signed = sa + sb
    kn_w = kn_ref[...]
    kpe_rot = rope(kpe * kn_w[:, LANES:QK_PAD])
    for h in range(MLA_HEADS):
        qa = qf[:, h * Q_UP:h * Q_UP + LANES]
        qb = qf[:, h * Q_UP + LANES:h * Q_UP + QK_PAD]
        qs = qf[:, h * Q_UP + QK_PAD:(h + 1) * Q_UP]
        ss = jnp.sum(qa * qa + qb * qb, axis=-1, keepdims=True)
        inv = lax.rsqrt(ss * (1.0 / QK_HEAD) + NORM_EPS)
        q_ref[0, h, :, 0:LANES] = (qa * inv * qn_a).astype(q_ref.dtype)
        q_ref[0, h, :, LANES:QK_PAD] = (inv * (qb * qn_b * cos + qs * qn_s * sin_signed)).astype(q_ref.dtype)

        ka = kvf[:, h * LANES:(h + 1) * LANES]
        ss = jnp.sum(ka * ka, axis=-1, keepdims=True) + kpe_ss
        inv = lax.rsqrt(ss * (1.0 / QK_HEAD) + NORM_EPS)
        k_ref[0, h, :, 0:LANES] = (ka * inv * kn_w[:, 0:LANES]).astype(k_ref.dtype)
        k_ref[0, h, :, LANES:QK_PAD] = (kpe_rot * inv).astype(k_ref.dtype)

        v_ref[0, h, :, :] = kvf[:, MLA_WIDTH + h * V_HEAD:MLA_WIDTH + (h + 1) * V_HEAD].astype(v_ref.dtype)


def _mla_prep(layer, cq, ckv, kd, qan, wq, kvan, wkv, qn, kn, cos_t, sa_t, sb_t, bsz, seq, tm):
    nt = seq // tm
    row = lambda b, i: (b * nt + i, 0)
    pos = lambda b, i: (i, 0)
    hd = MLA_HEADS
    return pl.pallas_call(
        _mla_prep_kernel,
        grid=(bsz, nt),
        in_specs=[pl.BlockSpec((tm, Q_LORA), row),
                  pl.BlockSpec((tm, KV_LORA), row),
                  pl.BlockSpec((tm, KD_WIDTH), row),
                  _resident(qan, layer), _resident(wq, layer), _resident(kvan, layer), _resident(wkv, layer),
                  _resident(qn, layer), _resident(kn, layer),
                  pl.BlockSpec((tm, LANES), pos), pl.BlockSpec((tm, LANES), pos), pl.BlockSpec((tm, LANES), pos)],
        out_specs=[pl.BlockSpec((1, hd, tm, QK_PAD), lambda b, i: (b, 0, i, 0)),
                   pl.BlockSpec((1, hd, tm, QK_PAD), lambda b, i: (b, 0, i, 0)),
                   pl.BlockSpec((1, hd, tm, V_HEAD), lambda b, i: (b, 0, i, 0))],
        out_shape=[jax.ShapeDtypeStruct((bsz, hd, seq, QK_PAD), BF16),
                   jax.ShapeDtypeStruct((bsz, hd, seq, QK_PAD), BF16),
                   jax.ShapeDtypeStruct((bsz, hd, seq, V_HEAD), BF16)],
        compiler_params=_params("parallel", "parallel"),
        name="mla_prep",
    )(cq, ckv, kd, qan, wq, kvan, wkv, qn, kn, cos_t, sa_t, sb_t)


def _attn_kernel(qi_ref, ki_ref, q_ref, k_ref, v_ref, g_ref, o_ref, m_sc, l_sc, acc_sc, s_sc, p_sc, a_sc):
    qi = qi_ref[pl.program_id(1)]
    ki = ki_ref[pl.program_id(1)]
    tq = q_ref.shape[2]
    tk = k_ref.shape[2]

    @pl.when(ki == 0)
    def _():
        m_sc[...] = jnp.full(m_sc.shape, -jnp.inf, F32)
        l_sc[...] = jnp.zeros(l_sc.shape, F32)
        acc_sc[...] = jnp.zeros(acc_sc.shape, F32)

    strip = min(ATTN_STRIP, tq)
    n_chunk = tk // LANES

    def scores(h):
        s_sc[h % 2] = lax.dot_general(q_ref[0, h], k_ref[0, h], (((1,), (1,)), ((), ())),
                                      preferred_element_type=F32)

    def softmax_strip(h, r0, masked):
        slot = h % 2
        rows = slice(r0, r0 + strip)
        live = [c for c in range(n_chunk) if not (masked and c * LANES >= r0 + strip)]
        chunks = []
        for c in live:
            x = s_sc[slot, rows, c * LANES:(c + 1) * LANES]
            if masked and (c + 1) * LANES - 1 > r0:
                rr = r0 + lax.broadcasted_iota(jnp.int32, (strip, LANES), 0)
                cc = c * LANES + lax.broadcasted_iota(jnp.int32, (strip, LANES), 1)
                x = jnp.where(rr >= cc, x, -jnp.inf)
            chunks.append(x)
        m_prev = m_sc[h, rows, :]
        m_loc = functools.reduce(jnp.maximum, chunks)
        m_new = jnp.maximum(m_prev, jnp.max(m_loc, axis=-1, keepdims=True))
        alpha = jnp.exp2(m_prev - m_new)
        ps = [jnp.exp2(x - m_new) for x in chunks]
        l_sc[h, rows, :] = alpha * l_sc[h, rows, :] + functools.reduce(jnp.add, ps)
        m_sc[h, rows, :] = m_new
        a_sc[slot, rows, :] = alpha
        dead = [jnp.zeros((strip, LANES), BF16)] * (n_chunk - len(live))
        p_sc[slot, rows, :] = jnp.concatenate([p.astype(BF16) for p in ps] + dead, axis=1)

    def step(masked):
        scores(0)
        for h in range(MLA_HEADS):
            if h + 1 < MLA_HEADS:
                scores(h + 1)
            for r0 in range(0, tq, strip):
                softmax_strip(h, r0, masked)
            pv = jnp.dot(p_sc[h % 2], v_ref[0, h], preferred_element_type=F32)
            acc_sc[h] = a_sc[h % 2] * acc_sc[h] + pv

    @pl.when(ki < qi)
    def _():
        step(False)

    @pl.when(ki == qi)
    def _():
        step(True)
        for h in range(MLA_HEADS):
            o = acc_sc[h] / jnp.sum(l_sc[h], axis=-1, keepdims=True)
            g = g_ref[:, h * V_HEAD:(h + 1) * V_HEAD]
            o_ref[:, h * V_HEAD:(h + 1) * V_HEAD] = (o * _silu(g)).astype(o_ref.dtype)


def _attention(q, k, v, gmla, bsz, seq, tq):
    hd = MLA_HEADS
    nq = seq // tq
    pairs = [(i, j) for i in range(nq) for j in range(i + 1)]
    qi_tab = jnp.asarray([i for i, _ in pairs], jnp.int32)
    ki_tab = jnp.asarray([j for _, j in pairs], jnp.int32)
    q_idx = lambda b, t, qi, ki: (b, 0, qi[t], 0)
    kv_idx = lambda b, t, qi, ki: (b, 0, ki[t], 0)
    row = lambda b, t, qi, ki: (b * nq + qi[t], 0)
    grid_spec = pltpu.PrefetchScalarGridSpec(
        num_scalar_prefetch=2,
        grid=(bsz, len(pairs)),
        in_specs=[pl.BlockSpec((1, hd, tq, QK_PAD), q_idx),
                  pl.BlockSpec((1, hd, tq, QK_PAD), kv_idx),
                  pl.BlockSpec((1, hd, tq, V_HEAD), kv_idx),
                  pl.BlockSpec((tq, MLA_WIDTH), row)],
        out_specs=pl.BlockSpec((tq, MLA_WIDTH), row),
        scratch_shapes=[pltpu.VMEM((hd, tq, LANES), F32), pltpu.VMEM((hd, tq, LANES), F32),
                        pltpu.VMEM((hd, tq, V_HEAD), F32),
                        pltpu.VMEM((2, tq, tq), F32), pltpu.VMEM((2, tq, tq), BF16),
                        pltpu.VMEM((2, tq, LANES), F32)])
    return pl.pallas_call(
        _attn_kernel,
        grid_spec=grid_spec,
        out_shape=jax.ShapeDtypeStruct((bsz * seq, MLA_WIDTH), BF16),
        compiler_params=_params("parallel", "arbitrary"),
        name="attention",
    )(qi_tab, ki_tab, q, k, v, gmla)


def _split3(x):
    hi = x.astype(BF16)
    r1 = x - hi.astype(F32)
    mid = r1.astype(BF16)
    lo = (r1 - mid.astype(F32)).astype(BF16)
    return hi, mid, lo


def _ssd_kernel(z_ref, xbc_ref, kd_ref, cw_ref, cb_ref, dtb_ref, alog_ref, dsk_ref, ng_ref, o_ref,
                xbuf, xact, state, shx):
    i = pl.program_id(0)
    nb, ts = o_ref.shape[0], o_ref.shape[1]
    L = SSD_CHUNK
    hp = SSD_HEAD_DIM
    gw = SSD_WIDTH // SSD_GROUPS
    pairs_per_group = SSD_HEADS // SSD_GROUPS // 2
    npair = SSD_HEADS // 2

    @pl.when(i == 0)
    def _():
        xbuf[:, 0:SSD_HALO, :] = jnp.zeros((nb, SSD_HALO, SSD_XBC), F32)
        state[...] = jnp.zeros(state.shape, F32)

    @pl.when(i > 0)
    def _():
        xbuf[:, 0:SSD_HALO, :] = xbuf[:, ts:ts + SSD_HALO, :]

    for b in range(nb):
        xbuf[b, SSD_HALO:SSD_HALO + ts, :] = xbc_ref[b]
        for s in range(1, SSD_CONV_K):
            shx[s - 1, :, :] = xbuf[b, SSD_HALO - s:SSD_HALO - s + ts, :]
        acc = cb_ref[...] + cw_ref[SSD_CONV_K - 1:SSD_CONV_K, :] * xbc_ref[b]
        for s in range(1, SSD_CONV_K):
            acc = acc + cw_ref[SSD_CONV_K - 1 - s:SSD_CONV_K - s, :] * shx[s - 1]
        xact[b] = _silu(acc)

    lane = lax.broadcasted_iota(jnp.int32, (1, LANES), 1)
    is_dt = (lane >= DT_LANE0) & (lane < DT_LANE0 + SSD_HEADS)
    a_neg = jnp.where(is_dt, -jnp.exp(alog_ref[...]), 0.0)
    rr = lax.broadcasted_iota(jnp.int32, (L, L), 0)
    cc = lax.broadcasted_iota(jnp.int32, (L, L), 1)
    causal = rr >= cc
    tri = causal.astype(BF16)
    left = lax.broadcasted_iota(jnp.int32, (L, LANES), 1) < hp

    def decay_geometry(b, r0):
        dt = _softplus(kd_ref[b, pl.ds(r0, L), :] + dtb_ref[...])
        hi, mid, lo = _split3(dt * a_neg)
        acs = (jnp.dot(tri, hi, preferred_element_type=F32)
               + jnp.dot(tri, mid, preferred_element_type=F32)
               + jnp.dot(tri, lo, preferred_element_type=F32))
        return dict(dt=dt, acs=acs, acs_t=acs.T)

    def expand(b, r0, v):
        def col(a, h):
            return jnp.broadcast_to(a[:, DT_LANE0 + h:DT_LANE0 + h + 1], (L, LANES))

        dt_cols = [col(v["dt"], h) for h in range(SSD_HEADS)]
        acs_cols = [col(v["acs"], h) for h in range(SSD_HEADS)]
        dt_exp = jnp.concatenate([jnp.where(left, dt_cols[2 * j], dt_cols[2 * j + 1]) for j in range(npair)], axis=1)
        acs_exp = jnp.concatenate([jnp.where(left, acs_cols[2 * j], acs_cols[2 * j + 1]) for j in range(npair)], axis=1)
        last = acs_exp[L - 1:L, :]
        xs = xact[b, pl.ds(r0, L), 0:SSD_WIDTH]
        xdt = xs * dt_exp
        v.update(acs_cols=acs_cols, last=last, xs=xs, xdt_b=xdt.astype(BF16),
                 xdt_end=(xdt * jnp.exp(last - acs_exp)).astype(BF16), grow=jnp.exp(acs_exp))

    def mix(b, r0, v):
        st_prev = state[b]
        st_prev_b = st_prev.astype(BF16)
        y_parts = []
        st_parts = []
        for g in range(SSD_GROUPS):
            c0 = SSD_WIDTH + g * SSD_STATE
            c1 = SSD_WIDTH + (SSD_GROUPS + g) * SSD_STATE
            bg = xact[b, pl.ds(r0, L), c0:c0 + SSD_STATE]
            cg_b = xact[b, pl.ds(r0, L), c1:c1 + SSD_STATE].astype(BF16)
            cb = lax.dot_general(cg_b, bg.astype(BF16), (((1,), (1,)), ((), ())), preferred_element_type=F32)
            st_parts.append(jnp.dot(bg.T.astype(BF16), v["xdt_end"][:, g * gw:(g + 1) * gw],
                                    preferred_element_type=F32))
            y_off = jnp.dot(cg_b, st_prev_b[:, g * gw:(g + 1) * gw], preferred_element_type=F32)
            y_off = y_off * v["grow"][:, g * gw:(g + 1) * gw]
            for jj in range(pairs_per_group):
                j = g * pairs_per_group + jj
                pair = v["xdt_b"][:, j * LANES:(j + 1) * LANES]
                yd = []
                for h in (2 * j, 2 * j + 1):
                    seg = v["acs_cols"][h] - v["acs_t"][DT_LANE0 + h:DT_LANE0 + h + 1, :]
                    decay = jnp.exp(jnp.where(causal, seg, -jnp.inf))
                    yd.append(jnp.dot((cb * decay).astype(BF16), pair, preferred_element_type=F32))
                y_parts.append(jnp.where(left, yd[0], yd[1]) + y_off[:, jj * LANES:(jj + 1) * LANES])
        state[b] = st_prev * jnp.exp(v["last"]) + jnp.concatenate(st_parts, axis=1)
        v["y"] = jnp.concatenate(y_parts, axis=1) + dsk_ref[...] * v["xs"]

    def finish(b, r0, v):
        yz = v["y"] * _silu(z_ref[b, pl.ds(r0, L), :])
        o_ref[b, pl.ds(r0, L), :] = _rms(yz, ng_ref[...]).astype(o_ref.dtype)

    def chunk(c, carry):
        r0 = pl.multiple_of(c * L, L)
        vals = [decay_geometry(b, r0) for b in range(nb)]
        for stage in (expand, mix, finish):
            for b in range(nb):
                stage(b, r0, vals[b])
        return carry

    lax.fori_loop(0, ts // L, chunk, 0)


def _ssd(layer, z, xbc, kd, cw, cb, dtb, alog, dsk, ng, bsz, seq, ts):
    blk = lambda w: pl.BlockSpec((bsz, ts, w), lambda i: (0, i, 0))
    out = pl.pallas_call(
        _ssd_kernel,
        grid=(seq // ts,),
        in_specs=[blk(SSD_WIDTH), blk(SSD_XBC), blk(KD_WIDTH),
                  _resident(cw, layer), _resident(cb, layer), _resident(dtb, layer), _resident(alog, layer),
                  _resident(dsk, layer), _resident(ng, layer)],
        out_specs=blk(SSD_WIDTH),
        out_shape=jax.ShapeDtypeStruct((bsz, seq, SSD_WIDTH), BF16),
        scratch_shapes=[pltpu.VMEM((bsz, ts + SSD_HALO, SSD_XBC), F32), pltpu.VMEM((bsz, ts, SSD_XBC), F32),
                        pltpu.VMEM((bsz, SSD_STATE, SSD_WIDTH), F32),
                        pltpu.VMEM((SSD_CONV_K - 1, ts, SSD_XBC), F32)],
        compiler_params=_params("arbitrary"),
        name="ssd",
    )(z.reshape(bsz, seq, -1), xbc.reshape(bsz, seq, -1), kd.reshape(bsz, seq, -1), cw, cb, dtb, alog, dsk, ng)
    return out.reshape(bsz * seq, SSD_WIDTH)


def _conv_out_kernel(u_ref, gc_ref, x_ref, a_ref, s_ref, wdw_ref, bdw_ref, lng_ref, lnb_ref, wpw_ref, w_ref,
                     o_ref, hbuf, shbuf, ybuf, oc_buf, *, tiles_per_seq):
    i = pl.program_id(0)
    tt = u_ref.shape[0]
    cw = CONV_WIDTH
    n_piece = w_ref.shape[0]
    seq_start = lax.rem(i, tiles_per_seq) == 0

    @pl.when(i == 0)
    def _():
        oc_buf[...] = jnp.zeros(oc_buf.shape, oc_buf.dtype)

    @pl.when(seq_start)
    def _():
        hbuf[0:CONV_HALO, :] = jnp.zeros((CONV_HALO, cw), F32)

    @pl.when(jnp.logical_not(seq_start))
    def _():
        hbuf[0:CONV_HALO, :] = hbuf[tt:tt + CONV_HALO, :]

    n0 = a_ref.shape[1]
    n1 = n0 + cw
    n2 = n1 + s_ref.shape[1]
    base = CONV_HALO - (CONV_K - 1)
    span = tt + CONV_HALO - SUBLANES
    dyn0 = pl.multiple_of(jnp.minimum(i, 0), ANCHOR)

    def fold(v):
        rows = functools.reduce(jnp.add, [v[r:r + ANCHOR, :] for r in range(0, v.shape[0], ANCHOR)])
        return functools.reduce(jnp.add, [rows[:, c:c + LANES] for c in range(0, v.shape[1], LANES)])

    def piece(p):
        cols = slice(p * OUT_PIECE, (p + 1) * OUT_PIECE)
        y = jnp.dot(oc_buf[pl.ds(dyn0, tt), :], w_ref[p, n0:n1, :], preferred_element_type=F32)
        y = y + jnp.dot(a_ref[...], w_ref[p, 0:n0, :], preferred_element_type=F32)
        y = y + jnp.dot(s_ref[...], w_ref[p, n1:n2, :], preferred_element_type=F32)
        o_ref[:, cols] = x_ref[:, cols] + y

    def glu():
        hval = u_ref[:, 0:cw] * _sigmoid(u_ref[:, cw:2 * cw])
        hbuf[CONV_HALO:CONV_HALO + tt, :] = hval
        return fold(hval)

    def shifts(residues):
        for res in residues:
            for r0 in range(0, span, CONV_SUB):
                rows = min(CONV_SUB, span - r0)
                shbuf[res - 1, r0:r0 + rows, :] = hbuf[res + r0:res + r0 + rows, :]

    def strips(first, count):
        mark = None
        for r0 in range(first, first + count * CONV_SUB, CONV_SUB):
            acc = jnp.broadcast_to(bdw_ref[...], (CONV_SUB, cw))
            for kk in range(CONV_K):
                res = (base + kk) % SUBLANES
                start = r0 + (base + kk - res)
                tap = hbuf[start:start + CONV_SUB, :] if res == 0 else shbuf[res - 1, start:start + CONV_SUB, :]
                acc = acc + wdw_ref[kk:kk + 1, :] * tap
            ybuf[r0:r0 + CONV_SUB, :] = acc
            part = fold(acc)
            mark = part if mark is None else mark + part
        return mark

    def anchor(mark):
        oc_buf[pl.ds(tt + dyn0, ANCHOR), 0:LANES] = mark.astype(oc_buf.dtype)

    n_free = 3
    n_groups = n_piece - n_free - 1
    per = tt // CONV_SUB // n_groups
    piece(0)
    anchor(glu())
    piece(1)
    shifts(range(1, SUBLANES))
    for p in range(2, n_free + 1):
        piece(p)
    for k in range(n_groups):
        anchor(strips(k * per * CONV_SUB, per))
        piece(n_free + 1 + k)

    y = ybuf[...]
    mu = jnp.mean(y, axis=-1, keepdims=True)
    yc = y - mu
    var = jnp.mean(yc * yc, axis=-1, keepdims=True)
    hn = _silu(yc * lax.rsqrt(var + LN_EPS) * lng_ref[...] + lnb_ref[...])
    out = jnp.dot(hn.astype(BF16), wpw_ref[...], preferred_element_type=F32)
    oc_buf[0:tt, :] = (out * _silu(gc_ref[...])).astype(oc_buf.dtype)


def _conv_out(layer, uconv, gconv, x2, o_mla, o_ssd, wdw, bdw, lng, lnb, wpw, w_out, seq, tt):
    m, d = x2.shape
    n = m // tt
    n_piece = w_out.shape[1]
    cur_row = lambda i: (jnp.minimum(i, n - 1), 0)
    prev_row = lambda i: (jnp.maximum(i - 1, 0), 0)
    w_spec = pl.BlockSpec((None,) + w_out.shape[1:], lambda i: (layer, 0, 0, 0), pipeline_mode=pl.Buffered(1))
    return pl.pallas_call(
        functools.partial(_conv_out_kernel, tiles_per_seq=seq // tt),
        grid=(n + 1,),
        in_specs=[pl.BlockSpec((tt, 2 * CONV_WIDTH), cur_row),
                  pl.BlockSpec((tt, CONV_WIDTH), cur_row),
                  pl.BlockSpec((tt, d), prev_row),
                  pl.BlockSpec((tt, o_mla.shape[1]), prev_row),
                  pl.BlockSpec((tt, o_ssd.shape[1]), prev_row),
                  _resident(wdw, layer), _resident(bdw, layer), _resident(lng, layer), _resident(lnb, layer),
                  _resident(wpw, layer), w_spec],
        out_specs=pl.BlockSpec((tt, d), prev_row),
        out_shape=jax.ShapeDtypeStruct((m, d), F32),
        scratch_shapes=[pltpu.VMEM((tt + CONV_HALO, CONV_WIDTH), F32),
                        pltpu.VMEM((SUBLANES - 1, tt + CONV_HALO, CONV_WIDTH), F32),
                        pltpu.VMEM((tt, CONV_WIDTH), F32),
                        pltpu.VMEM((tt + ANCHOR, CONV_WIDTH), BF16)],
        compiler_params=_params("arbitrary"),
        name="conv_out_proj",
    )(uconv, gconv, x2, o_mla, o_ssd, wdw, bdw, lng, lnb, wpw, w_out)


def _rows(p):
    return p.astype(F32)[:, None, :]


def _pad_rows(p, lane0, width):
    return jnp.pad(p.astype(F32), ((0, 0), (lane0, width - lane0 - p.shape[1])))[:, None, :]


def _rope_tables(seq):
    half = QK_ROPE // 2
    inv_freq = ROPE_THETA ** (-jnp.arange(half, dtype=F32) / half)
    ang = jnp.arange(seq).astype(F32)[:, None] * inv_freq[None, :]
    cos, sin = jnp.cos(ang), jnp.sin(ang)
    z = lambda n: jnp.zeros((seq, n), F32)
    cos_t = jnp.concatenate([cos, cos, z(LANES - QK_ROPE)], axis=1)
    sa_t = jnp.concatenate([-sin, z(LANES - half)], axis=1)
    sb_t = jnp.concatenate([z(half), sin, z(LANES - QK_ROPE)], axis=1)
    return cos_t, sa_t, sb_t


def kernel(x, norm_g, w_in, q_a_norm, w_q_b, kv_a_norm, w_kv_b, q_norm, k_norm, conv_dw_w, conv_dw_b, conv_ln_g,
           conv_ln_b, conv_pw_w, ssd_conv_w, ssd_conv_b, ssd_dt_bias, ssd_A_log, ssd_D, ssd_norm_g, w_out):
    bsz, seq, d = x.shape
    m = bsz * seq
    x2 = x.reshape(m, d)
    tabs = _rope_tables(seq)

    w_in_k = _prep_w_in(w_in)
    wq, wkv, wpw = _prep_small(w_q_b, w_kv_b, conv_pw_w)
    w_out_k = _prep_cast(w_out, min(512, w_out.shape[1]))
    g_in, qan, kvan = _rows(norm_g), _rows(q_a_norm), _rows(kv_a_norm)
    kn = _pad_rows(k_norm, 0, QK_PAD)
    half = QK_ROPE // 2
    q_swapped = jnp.concatenate([q_norm[:, QK_NOPE + half:], q_norm[:, QK_NOPE:QK_NOPE + half]], axis=1)
    qn = jnp.concatenate([_pad_rows(q_norm, 0, QK_PAD), _pad_rows(q_swapped, 0, LANES)], axis=2)
    wdw = jnp.pad(conv_dw_w.astype(F32), ((0, 0), (0, CONV_HALO - CONV_K), (0, 0)))
    bdw, lng, lnb = _rows(conv_dw_b), _rows(conv_ln_g), _rows(conv_ln_b)
    cw = jnp.pad(ssd_conv_w.astype(F32), ((0, 0), (0, SUBLANES - SSD_CONV_K), (0, 0)))
    cb, ng = _rows(ssd_conv_b), _rows(ssd_norm_g)
    dtb, alog = _pad_rows(ssd_dt_bias, DT_LANE0, LANES), _pad_rows(ssd_A_log, DT_LANE0, LANES)
    dsk = _rows(jnp.repeat(ssd_D, SSD_HEAD_DIM, axis=1))

    tm_in = min(512, m)
    t_seq = min(512, seq)
    for layer in range(norm_g.shape[0]):
        cq, ckv, kd, gmla, uconv, gconv, z, xbc = _in_proj(layer, x2, g_in, w_in_k, tm_in)
        q, k, v = _mla_prep(layer, cq, ckv, kd, qan, wq, kvan, wkv, qn, kn, *tabs, bsz, seq, t_seq)
        o_mla = _attention(q, k, v, gmla, bsz, seq, t_seq)
        o_ssd = _ssd(layer, z, xbc, kd, cw, cb, dtb, alog, dsk, ng, bsz, seq, t_seq)
        x2 = _conv_out(layer, uconv, gconv, x2, o_mla, o_ssd, wdw, bdw, lng, lnb, wpw, w_out_k, seq, t_seq)
    return x2.reshape(bsz, seq, d)
```

```python
import functools
import math

import jax
import jax.numpy as jnp
from jax import lax
from jax.experimental import pallas as pl
from jax.experimental.pallas import tpu as pltpu

F32 = jnp.float32
BF16 = jnp.bfloat16

MLA_HEADS = 6
QK_NOPE = 128
QK_ROPE = 64
QK_HEAD = QK_NOPE + QK_ROPE
V_HEAD = 128
Q_LORA = 512
KV_LORA = 256
MLA_WIDTH = MLA_HEADS * V_HEAD
ROPE_THETA = 10000.0
CONV_WIDTH = 512
CONV_K = 31
SSD_HEADS = 12
SSD_HEAD_DIM = 64
SSD_WIDTH = SSD_HEADS * SSD_HEAD_DIM
SSD_GROUPS = 2
SSD_STATE = 128
SSD_CONV_K = 4
SSD_CHUNK = 128
SSD_XBC = SSD_WIDTH + 2 * SSD_GROUPS * SSD_STATE
NORM_EPS = 1e-6
LN_EPS = 1e-5

LANES = 128
SUBLANES = 8
QK_PAD = 2 * LANES
Q_UP = 3 * LANES
KD_WIDTH = LANES
DT_LANE0 = QK_ROPE
CONV_HALO = 32
SSD_HALO = SUBLANES
VMEM_LIMIT = 56 * 1024 * 1024
ATTN_STRIP = 64

SEG_WIDTHS = (Q_LORA, KV_LORA, KD_WIDTH, MLA_WIDTH, 2 * CONV_WIDTH, CONV_WIDTH, SSD_WIDTH, SSD_XBC)


def _sigmoid(x):
    return 1.0 / (1.0 + jnp.exp2(x * (-math.log2(math.e))))


def _silu(x):
    return x * _sigmoid(x)


def _softplus(x):
    return jnp.maximum(x, 0.0) + jnp.log1p(jnp.exp(-jnp.abs(x)))


def _rms(x, g, eps=NORM_EPS):
    ms = jnp.mean(x * x, axis=-1, keepdims=True)
    return x * lax.rsqrt(ms + eps) * g


def _params(*sem):
    return pltpu.CompilerParams(dimension_semantics=sem, vmem_limit_bytes=VMEM_LIMIT)


def _resident(stacked, layer):
    _, a, b = stacked.shape
    return pl.BlockSpec((None, a, b), lambda *_: (layer, 0, 0), pipeline_mode=pl.Buffered(1))


W_IN_HALF = LANES // 2


def _prep_w_in_kernel(a_ref, b_ref, o_ref, *, depth, n_kt, n_dt):
    j = pl.program_id(0)
    kd_tile = (Q_LORA + KV_LORA) // LANES
    stride = depth * n_kt
    row = lax.broadcasted_iota(jnp.int32, (W_IN_HALF, LANES), 0)
    keep_b = (j != kd_tile) | (row < n_dt)
    for l in range(depth):
        for kt in range(n_kt):
            xa = a_ref[pl.ds(kt * depth + l, W_IN_HALF, stride=stride), :]
            xb = b_ref[pl.ds(kt * depth + l, W_IN_HALF, stride=stride), :]
            x = jnp.concatenate([xa, jnp.where(keep_b, xb, 0.0)], axis=0)
            o_ref[l, kt * LANES:(kt + 1) * LANES, :] = x.T.astype(o_ref.dtype)


def _prep_w_in(w_in):
    depth, d, n = w_in.shape
    n_kt = d // LANES
    n_out = sum(SEG_WIDTHS)
    rows_per_group = W_IN_HALF * n_kt * depth
    view = w_in.transpose(2, 0, 1).reshape(n, depth, n_kt, LANES).transpose(0, 2, 1, 3).reshape(-1, LANES)
    kd_tile = (Q_LORA + KV_LORA) // LANES
    dt_group = (n - SSD_HEADS) // W_IN_HALF
    assert (n - SSD_HEADS) % W_IN_HALF == 0 and (Q_LORA + KV_LORA) % LANES == 0 and QK_ROPE == W_IN_HALF
    a_idx = lambda j: (jnp.where(j <= kd_tile, 2 * j, 2 * j - 1), 0)
    b_idx = lambda j: (jnp.where(j < kd_tile, 2 * j + 1, jnp.where(j == kd_tile, dt_group, 2 * j)), 0)
    return pl.pallas_call(
        functools.partial(_prep_w_in_kernel, depth=depth, n_kt=n_kt, n_dt=SSD_HEADS),
        grid=(n_out // LANES,),
        in_specs=[pl.BlockSpec((rows_per_group, LANES), a_idx),
                  pl.BlockSpec((rows_per_group, LANES), b_idx)],
        out_specs=pl.BlockSpec((depth, d, LANES), lambda j: (0, 0, j)),
        out_shape=jax.ShapeDtypeStruct((depth, d, n_out), BF16),
        compiler_params=_params("parallel"),
        name="prep_w_in",
    )(view, view)


def _prep_small_kernel(wq_ref, wkv_ref, wpw_ref, oq_ref, okv_ref, opw_ref):
    wq = wq_ref[...]
    zq = jnp.zeros((wq.shape[0], QK_PAD - QK_HEAD), F32)
    half = QK_ROPE // 2
    parts = []
    for h in range(MLA_HEADS):
        r0 = h * QK_HEAD + QK_NOPE
        parts += [wq[:, h * QK_HEAD:(h + 1) * QK_HEAD], zq,
                  wq[:, r0 + half:r0 + QK_ROPE], wq[:, r0:r0 + half], zq]
    oq_ref[...] = jnp.concatenate(parts, axis=1).astype(oq_ref.dtype)
    wkv = wkv_ref[...]
    per = QK_NOPE + V_HEAD
    k_cols = [wkv[:, h * per:h * per + QK_NOPE] for h in range(MLA_HEADS)]
    v_cols = [wkv[:, h * per + QK_NOPE:(h + 1) * per] for h in range(MLA_HEADS)]
    okv_ref[...] = jnp.concatenate(k_cols + v_cols, axis=1).astype(okv_ref.dtype)
    opw_ref[...] = wpw_ref[...].astype(opw_ref.dtype)


def _prep_small(w_q_b, w_kv_b, conv_pw_w):
    depth = w_q_b.shape[0]
    whole = lambda a: pl.BlockSpec((None,) + a.shape[1:], lambda l: (l, 0, 0))
    shapes = [(depth, Q_LORA, MLA_HEADS * Q_UP), w_kv_b.shape, conv_pw_w.shape]
    return pl.pallas_call(
        _prep_small_kernel,
        grid=(depth,),
        in_specs=[whole(w_q_b), whole(w_kv_b), whole(conv_pw_w)],
        out_specs=[pl.BlockSpec((None,) + s[1:], lambda l: (l, 0, 0)) for s in shapes],
        out_shape=[jax.ShapeDtypeStruct(s, BF16) for s in shapes],
        compiler_params=_params("parallel"),
        name="prep_small",
    )(w_q_b, w_kv_b, conv_pw_w)


OUT_PIECE = 2 * LANES
CONV_SUB = 32
ANCHOR = 16


def _cast_kernel(w_ref, o_ref):
    for p in range(o_ref.shape[0]):
        o_ref[p] = w_ref[:, p * OUT_PIECE:(p + 1) * OUT_PIECE].astype(o_ref.dtype)


def _prep_cast(w, tk):
    depth, k, n = w.shape
    n_piece = n // OUT_PIECE
    return pl.pallas_call(
        _cast_kernel,
        grid=(depth, k // tk),
        in_specs=[pl.BlockSpec((None, tk, n), lambda l, i: (l, i, 0))],
        out_specs=pl.BlockSpec((None, n_piece, tk, OUT_PIECE), lambda l, i: (l, 0, i, 0)),
        out_shape=jax.ShapeDtypeStruct((depth, n_piece, k, OUT_PIECE), BF16),
        compiler_params=_params("parallel", "parallel"),
        name="prep_cast",
    )(w)


def _in_proj_kernel(x_ref, g_ref, w_ref, *out_refs):
    h = _rms(x_ref[...], g_ref[...]).astype(BF16)
    off = 0
    for o_ref in out_refs:
        n = o_ref.shape[-1]
        o_ref[...] = jnp.dot(h, w_ref[:, off:off + n], preferred_element_type=F32).astype(o_ref.dtype)
        off += n


def _in_proj(layer, x2, g, w, tm):
    m, d = x2.shape
    assert w.shape[2] == sum(SEG_WIDTHS)
    return pl.pallas_call(
        _in_proj_kernel,
        grid=(m // tm,),
        in_specs=[pl.BlockSpec((tm, d), lambda i: (i, 0)),
                  _resident(g, layer),
                  _resident(w, layer)],
        out_specs=[pl.BlockSpec((tm, n), lambda i: (i, 0)) for n in SEG_WIDTHS],
        out_shape=[jax.ShapeDtypeStruct((m, n), F32) for n in SEG_WIDTHS],
        compiler_params=_params("parallel"),
        name="in_proj",
    )(x2, g, w)


def _mla_prep_kernel(cq_ref, ckv_ref, kd_ref, qan_ref, wq_ref, kvan_ref, wkv_ref, qn_ref, kn_ref,
                     cos_ref, sa_ref, sb_ref, q_ref, k_ref, v_ref):
    scale = math.log2(math.e) / math.sqrt(QK_HEAD)
    hq = _rms(cq_ref[...], qan_ref[...]).astype(BF16)
    qf = jnp.dot(hq, wq_ref[...], preferred_element_type=F32)
    hkv = _rms(ckv_ref[...], kvan_ref[...]).astype(BF16)
    kvf = jnp.dot(hkv, wkv_ref[...], preferred_element_type=F32)

    cos = cos_ref[...]
    sa = sa_ref[...]
    sb = sb_ref[...]

    def rope(r):
        return r * cos + pltpu.roll(r, LANES - QK_ROPE // 2, 1) * sa + pltpu.roll(r, QK_ROPE // 2, 1) * sb

    kd = kd_ref[...]
    lane = lax.broadcasted_iota(jnp.int32, kd.shape, 1)
    kpe = jnp.where(lane < QK_ROPE, kd, 0.0)
    kpe_ss = jnp.sum(kpe * kpe, axis=-1, keepdims=True)

    qn_a = qn_ref[:, 0:LANES] * scale
    qn_b = qn_ref[:, LANES:QK_PAD] * scale
    qn_s = qn_ref[:, QK_PAD:Q_UP] * scale
    sin_signed = sa + sb
    kn_w = kn_ref[...]
    kpe_rot = rope(kpe * kn_w[:, LANES:QK_PAD])
    for h in range(MLA_HEADS):
        qa = qf[:, h * Q_UP:h * Q_UP + LANES]
        qb = qf[:, h * Q_UP + LANES:h * Q_UP + QK_PAD]
        qs = qf[:, h * Q_UP + QK_PAD:(h + 1) * Q_UP]
        ss = jnp.sum(qa * qa + qb * qb, axis=-1, keepdims=True)
        inv = lax.rsqrt(ss * (1.0 / QK_HEAD) + NORM_EPS)
        q_ref[0, h, :, 0:LANES] = (qa * inv * qn_a).astype(q_ref.dtype)
        q_ref[0, h, :, LANES:QK_PAD] = (inv * (qb * qn_b * cos + qs * qn_s * sin_signed)).astype(q_ref.dtype)

        ka = kvf[:, h * LANES:(h + 1) * LANES]
        ss = jnp.sum(ka * ka, axis=-1, keepdims=True) + kpe_ss
        inv = lax.rsqrt(ss * (1.0 / QK_HEAD) + NORM_EPS)
        k_ref[0, h, :, 0:LANES] = (ka * inv * kn_w[:, 0:LANES]).astype(k_ref.dtype)
        k_ref[0, h, :, LANES:QK_PAD] = (kpe_rot * inv).astype(k_ref.dtype)

        v_ref[0, h, :, :] = kvf[:, MLA_WIDTH + h * V_HEAD:MLA_WIDTH + (h + 1) * V_HEAD].astype(v_ref.dtype)


def _mla_prep(layer, cq, ckv, kd, qan, wq, kvan, wkv, qn, kn, cos_t, sa_t, sb_t, bsz, seq, tm):
    nt = seq // tm
    row = lambda b, i: (b * nt + i, 0)
    pos = lambda b, i: (i, 0)
    hd = MLA_HEADS
    return pl.pallas_call(
        _mla_prep_kernel,
        grid=(bsz, nt),
        in_specs=[pl.BlockSpec((tm, Q_LORA), row),
                  pl.BlockSpec((tm, KV_LORA), row),
                  pl.BlockSpec((tm, KD_WIDTH), row),
                  _resident(qan, layer), _resident(wq, layer), _resident(kvan, layer), _resident(wkv, layer),
                  _resident(qn, layer), _resident(kn, layer),
                  pl.BlockSpec((tm, LANES), pos), pl.BlockSpec((tm, LANES), pos), pl.BlockSpec((tm, LANES), pos)],
        out_specs=[pl.BlockSpec((1, hd, tm, QK_PAD), lambda b, i: (b, 0, i, 0)),
                   pl.BlockSpec((1, hd, tm, QK_PAD), lambda b, i: (b, 0, i, 0)),
                   pl.BlockSpec((1, hd, tm, V_HEAD), lambda b, i: (b, 0, i, 0))],
        out_shape=[jax.ShapeDtypeStruct((bsz, hd, seq, QK_PAD), BF16),
                   jax.ShapeDtypeStruct((bsz, hd, seq, QK_PAD), BF16),
                   jax.ShapeDtypeStruct((bsz, hd, seq, V_HEAD), BF16)],
        compiler_params=_params("parallel", "parallel"),
        name="mla_prep",
    )(cq, ckv, kd, qan, wq, kvan, wkv, qn, kn, cos_t, sa_t, sb_t)


def _attn_kernel(qi_ref, ki_ref, q_ref, k_ref, v_ref, g_ref, o_ref, m_sc, l_sc, acc_sc, s_sc, p_sc, a_sc):
    qi = qi_ref[pl.program_id(1)]
    ki = ki_ref[pl.program_id(1)]
    tq = q_ref.shape[2]
    tk = k_ref.shape[2]

    @pl.when(ki == 0)
    def _():
        m_sc[...] = jnp.full(m_sc.shape, -jnp.inf, F32)
        l_sc[...] = jnp.zeros(l_sc.shape, F32)
        acc_sc[...] = jnp.zeros(acc_sc.shape, F32)

    strip = min(ATTN_STRIP, tq)
    n_chunk = tk // LANES

    def scores(h):
        s_sc[h % 2] = lax.dot_general(q_ref[0, h], k_ref[0, h], (((1,), (1,)), ((), ())),
                                      preferred_element_type=F32)

    def softmax_strip(h, r0, masked):
        slot = h % 2
        rows = slice(r0, r0 + strip)
        live = [c for c in range(n_chunk) if not (masked and c * LANES >= r0 + strip)]
        chunks = []
        for c in live:
            x = s_sc[slot, rows, c * LANES:(c + 1) * LANES]
            if masked and (c + 1) * LANES - 1 > r0:
                rr = r0 + lax.broadcasted_iota(jnp.int32, (strip, LANES), 0)
                cc = c * LANES + lax.broadcasted_iota(jnp.int32, (strip, LANES), 1)
                x = jnp.where(rr >= cc, x, -jnp.inf)
            chunks.append(x)
        m_prev = m_sc[h, rows, :]
        m_loc = functools.reduce(jnp.maximum, chunks)
        m_new = jnp.maximum(m_prev, jnp.max(m_loc, axis=-1, keepdims=True))
        alpha = jnp.exp2(m_prev - m_new)
        ps = [jnp.exp2(x - m_new) for x in chunks]
        l_sc[h, rows, :] = alpha * l_sc[h, rows, :] + functools.reduce(jnp.add, ps)
        m_sc[h, rows, :] = m_new
        a_sc[slot, rows, :] = alpha
        dead = [jnp.zeros((strip, LANES), BF16)] * (n_chunk - len(live))
        p_sc[slot, rows, :] = jnp.concatenate([p.astype(BF16) for p in ps] + dead, axis=1)

    def step(masked):
        scores(0)
        for h in range(MLA_HEADS):
            if h + 1 < MLA_HEADS:
                scores(h + 1)
            for r0 in range(0, tq, strip):
                softmax_strip(h, r0, masked)
            pv = jnp.dot(p_sc[h % 2], v_ref[0, h], preferred_element_type=F32)
            acc_sc[h] = a_sc[h % 2] * acc_sc[h] + pv

    @pl.when(ki < qi)
    def _():
        step(False)

    @pl.when(ki == qi)
    def _():
        step(True)
        for h in range(MLA_HEADS):
            o = acc_sc[h] / jnp.sum(l_sc[h], axis=-1, keepdims=True)
            g = g_ref[:, h * V_HEAD:(h + 1) * V_HEAD]
            o_ref[:, h * V_HEAD:(h + 1) * V_HEAD] = (o * _silu(g)).astype(o_ref.dtype)


def _attention(q, k, v, gmla, bsz, seq, tq):
    hd = MLA_HEADS
    nq = seq // tq
    pairs = [(i, j) for i in range(nq) for j in range(i + 1)]
    qi_tab = jnp.asarray([i for i, _ in pairs], jnp.int32)
    ki_tab = jnp.asarray([j for _, j in pairs], jnp.int32)
    q_idx = lambda b, t, qi, ki: (b, 0, qi[t], 0)
    kv_idx = lambda b, t, qi, ki: (b, 0, ki[t], 0)
    row = lambda b, t, qi, ki: (b * nq + qi[t], 0)
    grid_spec = pltpu.PrefetchScalarGridSpec(
        num_scalar_prefetch=2,
        grid=(bsz, len(pairs)),
        in_specs=[pl.BlockSpec((1, hd, tq, QK_PAD), q_idx),
                  pl.BlockSpec((1, hd, tq, QK_PAD), kv_idx),
                  pl.BlockSpec((1, hd, tq, V_HEAD), kv_idx),
                  pl.BlockSpec((tq, MLA_WIDTH), row)],
        out_specs=pl.BlockSpec((tq, MLA_WIDTH), row),
        scratch_shapes=[pltpu.VMEM((hd, tq, LANES), F32), pltpu.VMEM((hd, tq, LANES), F32),
                        pltpu.VMEM((hd, tq, V_HEAD), F32),
                        pltpu.VMEM((2, tq, tq), F32), pltpu.VMEM((2, tq, tq), BF16),
                        pltpu.VMEM((2, tq, LANES), F32)])
    return pl.pallas_call(
        _attn_kernel,
        grid_spec=grid_spec,
        out_shape=jax.ShapeDtypeStruct((bsz * seq, MLA_WIDTH), BF16),
        compiler_params=_params("parallel", "arbitrary"),
        name="attention",
    )(qi_tab, ki_tab, q, k, v, gmla)


def _split3(x):
    hi = x.astype(BF16)
    r1 = x - hi.astype(F32)
    mid = r1.astype(BF16)
    lo = (r1 - mid.astype(F32)).astype(BF16)
    return hi, mid, lo


def _ssd_kernel(z_ref, xbc_ref, kd_ref, cw_ref, cb_ref, dtb_ref, alog_ref, dsk_ref, ng_ref, o_ref,
                xbuf, xact, state, shx, acol_sc, xdtb_sc, xdte_sc, grow_sc, y_sc):
    i = pl.program_id(0)
    nb, ts = o_ref.shape[0], o_ref.shape[1]
    L = SSD_CHUNK
    hp = SSD_HEAD_DIM
    gw = SSD_WIDTH // SSD_GROUPS
    pairs_per_group = SSD_HEADS // SSD_GROUPS // 2
    npair = SSD_HEADS // 2

    @pl.when(i == 0)
    def _():
        xbuf[:, 0:SSD_HALO, :] = jnp.zeros((nb, SSD_HALO, SSD_XBC), F32)
        state[...] = jnp.zeros(state.shape, F32)

    @pl.when(i > 0)
    def _():
        xbuf[:, 0:SSD_HALO, :] = xbuf[:, ts:ts + SSD_HALO, :]

    for b in range(nb):
        xbuf[b, SSD_HALO:SSD_HALO + ts, :] = xbc_ref[b]
        for s in range(1, SSD_CONV_K):
            shx[s - 1, :, :] = xbuf[b, SSD_HALO - s:SSD_HALO - s + ts, :]
        acc = cb_ref[...] + cw_ref[SSD_CONV_K - 1:SSD_CONV_K, :] * xbc_ref[b]
        for s in range(1, SSD_CONV_K):
            acc = acc + cw_ref[SSD_CONV_K - 1 - s:SSD_CONV_K - s, :] * shx[s - 1]
        xact[b] = _silu(acc)

    lane = lax.broadcasted_iota(jnp.int32, (1, LANES), 1)
    is_dt = (lane >= DT_LANE0) & (lane < DT_LANE0 + SSD_HEADS)
    a_neg = jnp.where(is_dt, -jnp.exp(alog_ref[...]), 0.0)
    rr = lax.broadcasted_iota(jnp.int32, (L, L), 0)
    cc = lax.broadcasted_iota(jnp.int32, (L, L), 1)
    causal = rr >= cc
    tri = causal.astype(BF16)
    left = lax.broadcasted_iota(jnp.int32, (L, LANES), 1) < hp

    def decay_geometry(b, r0):
        dt = _softplus(kd_ref[b, pl.ds(r0, L), :] + dtb_ref[...])
        hi, mid, lo = _split3(dt * a_neg)
        acs = (jnp.dot(tri, hi, preferred_element_type=F32)
               + jnp.dot(tri, mid, preferred_element_type=F32)
               + jnp.dot(tri, lo, preferred_element_type=F32))
        return dict(dt=dt, acs=acs, acs_t=acs.T)

    def expand(b, r0, v):
        def col(a, h):
            return jnp.broadcast_to(a[:, DT_LANE0 + h:DT_LANE0 + h + 1], (L, LANES))

        lasts = []
        for j in range(npair):
            lanes = slice(j * LANES, (j + 1) * LANES)
            a0, a1 = col(v["acs"], 2 * j), col(v["acs"], 2 * j + 1)
            acol_sc[b, 2 * j] = a0
            acol_sc[b, 2 * j + 1] = a1
            acs_pair = jnp.where(left, a0, a1)
            dt_pair = jnp.where(left, col(v["dt"], 2 * j), col(v["dt"], 2 * j + 1))
            last = acs_pair[L - 1:L, :]
            xdt = xact[b, pl.ds(r0, L), lanes] * dt_pair
            xdtb_sc[b, :, lanes] = xdt.astype(BF16)
            xdte_sc[b, :, lanes] = (xdt * jnp.exp(last - acs_pair)).astype(BF16)
            grow_sc[b, :, lanes] = jnp.exp(acs_pair)
            lasts.append(last)
        v["lasts"] = lasts

    def mix(b, r0, v):
        ssq = jnp.zeros((L, LANES), F32)
        for g in range(SSD_GROUPS):
            c0 = SSD_WIDTH + g * SSD_STATE
            c1 = SSD_WIDTH + (SSD_GROUPS + g) * SSD_STATE
            grp = slice(g * gw, (g + 1) * gw)
            bg = xact[b, pl.ds(r0, L), c0:c0 + SSD_STATE]
            cg_b = xact[b, pl.ds(r0, L), c1:c1 + SSD_STATE].astype(BF16)
            cb = lax.dot_general(cg_b, bg.astype(BF16), (((1,), (1,)), ((), ())), preferred_element_type=F32)
            st_prev = state[b, :, grp]
            st_new = jnp.dot(bg.T.astype(BF16), xdte_sc[b, :, grp], preferred_element_type=F32)
            y_off = jnp.dot(cg_b, st_prev.astype(BF16), preferred_element_type=F32) * grow_sc[b, :, grp]
            last_g = jnp.concatenate(v["lasts"][g * pairs_per_group:(g + 1) * pairs_per_group], axis=1)
            state[b, :, grp] = st_prev * jnp.exp(last_g) + st_new
            for jj in range(pairs_per_group):
                j = g * pairs_per_group + jj
                lanes = slice(j * LANES, (j + 1) * LANES)
                pair = xdtb_sc[b, :, lanes]
                yd = []
                for h in (2 * j, 2 * j + 1):
                    seg = acol_sc[b, h] - v["acs_t"][DT_LANE0 + h:DT_LANE0 + h + 1, :]
                    decay = jnp.exp(jnp.where(causal, seg, -jnp.inf))
                    yd.append(jnp.dot((cb * decay).astype(BF16), pair, preferred_element_type=F32))
                y = (jnp.where(left, yd[0], yd[1]) + y_off[:, jj * LANES:(jj + 1) * LANES]
                     + dsk_ref[:, lanes] * xact[b, pl.ds(r0, L), lanes])
                yz = y * _silu(z_ref[b, pl.ds(r0, L), lanes])
                y_sc[b, :, lanes] = yz
                ssq = ssq + yz * yz
        v["inv"] = lax.rsqrt(jnp.sum(ssq, axis=-1, keepdims=True) * (1.0 / SSD_WIDTH) + NORM_EPS)

    def finish(b, r0, v):
        o_ref[b, pl.ds(r0, L), :] = (y_sc[b] * v["inv"] * ng_ref[...]).astype(o_ref.dtype)

    def chunk(c, carry):
        r0 = pl.multiple_of(c * L, L)
        vals = [decay_geometry(b, r0) for b in range(nb)]
        for stage in (expand, mix, finish):
            for b in range(nb):
                stage(b, r0, vals[b])
        return carry

    lax.fori_loop(0, ts // L, chunk, 0)


def _ssd(layer, z, xbc, kd, cw, cb, dtb, alog, dsk, ng, bsz, seq, ts):
    blk = lambda w: pl.BlockSpec((bsz, ts, w), lambda i: (0, i, 0))
    out = pl.pallas_call(
        _ssd_kernel,
        grid=(seq // ts,),
        in_specs=[blk(SSD_WIDTH), blk(SSD_XBC), blk(KD_WIDTH),
                  _resident(cw, layer), _resident(cb, layer), _resident(dtb, layer), _resident(alog, layer),
                  _resident(dsk, layer), _resident(ng, layer)],
        out_specs=blk(SSD_WIDTH),
        out_shape=jax.ShapeDtypeStruct((bsz, seq, SSD_WIDTH), BF16),
        scratch_shapes=[pltpu.VMEM((bsz, ts + SSD_HALO, SSD_XBC), F32), pltpu.VMEM((bsz, ts, SSD_XBC), F32),
                        pltpu.VMEM((bsz, SSD_STATE, SSD_WIDTH), F32),
                        pltpu.VMEM((SSD_CONV_K - 1, ts, SSD_XBC), F32),
                        pltpu.VMEM((bsz, SSD_HEADS, SSD_CHUNK, LANES), F32),
                        pltpu.VMEM((bsz, SSD_CHUNK, SSD_WIDTH), BF16), pltpu.VMEM((bsz, SSD_CHUNK, SSD_WIDTH), BF16),
                        pltpu.VMEM((bsz, SSD_CHUNK, SSD_WIDTH), F32), pltpu.VMEM((bsz, SSD_CHUNK, SSD_WIDTH), F32)],
        compiler_params=_params("arbitrary"),
        name="ssd",
    )(z.reshape(bsz, seq, -1), xbc.reshape(bsz, seq, -1), kd.reshape(bsz, seq, -1), cw, cb, dtb, alog, dsk, ng)
    return out.reshape(bsz * seq, SSD_WIDTH)


def _conv_out_kernel(u_ref, gc_ref, x_ref, a_ref, s_ref, wdw_ref, bdw_ref, lng_ref, lnb_ref, wpw_ref, w_ref,
                     o_ref, hbuf, shbuf, ybuf, oc_buf, *, tiles_per_seq):
    i = pl.program_id(0)
    tt = u_ref.shape[0]
    cw = CONV_WIDTH
    n_piece = w_ref.shape[0]
    seq_start = lax.rem(i, tiles_per_seq) == 0

    @pl.when(i == 0)
    def _():
        oc_buf[...] = jnp.zeros(oc_buf.shape, oc_buf.dtype)

    @pl.when(seq_start)
    def _():
        hbuf[0:CONV_HALO, :] = jnp.zeros((CONV_HALO, cw), F32)

    @pl.when(jnp.logical_not(seq_start))
    def _():
        hbuf[0:CONV_HALO, :] = hbuf[tt:tt + CONV_HALO, :]

    n0 = a_ref.shape[1]
    n1 = n0 + cw
    n2 = n1 + s_ref.shape[1]
    base = CONV_HALO - (CONV_K - 1)
    span = tt + CONV_HALO - SUBLANES
    dyn0 = pl.multiple_of(jnp.minimum(i, 0), ANCHOR)

    def fold(v):
        rows = functools.reduce(jnp.add, [v[r:r + ANCHOR, :] for r in range(0, v.shape[0], ANCHOR)])
        return functools.reduce(jnp.add, [rows[:, c:c + LANES] for c in range(0, v.shape[1], LANES)])

    def piece(p):
        cols = slice(p * OUT_PIECE, (p + 1) * OUT_PIECE)
        y = jnp.dot(oc_buf[pl.ds(dyn0, tt), :], w_ref[p, n0:n1, :], preferred_element_type=F32)
        y = y + jnp.dot(a_ref[...], w_ref[p, 0:n0, :], preferred_element_type=F32)
        y = y + jnp.dot(s_ref[...], w_ref[p, n1:n2, :], preferred_element_type=F32)
        o_ref[:, cols] = x_ref[:, cols] + y

    def glu():
        hval = u_ref[:, 0:cw] * _sigmoid(u_ref[:, cw:2 * cw])
        hbuf[CONV_HALO:CONV_HALO + tt, :] = hval
        return fold(hval)

    def shifts(residues):
        for res in residues:
            for r0 in range(0, span, CONV_SUB):
                rows = min(CONV_SUB, span - r0)
                shbuf[res - 1, r0:r0 + rows, :] = hbuf[res + r0:res + r0 + rows, :]

    def strips(first, count):
        mark = None
        for r0 in range(first, first + count * CONV_SUB, CONV_SUB):
            acc = jnp.broadcast_to(bdw_ref[...], (CONV_SUB, cw))
            for kk in range(CONV_K):
                res = (base + kk) % SUBLANES
                start = r0 + (base + kk - res)
                tap = hbuf[start:start + CONV_SUB, :] if res == 0 else shbuf[res - 1, start:start + CONV_SUB, :]
                acc = acc + wdw_ref[kk:kk + 1, :] * tap
            ybuf[r0:r0 + CONV_SUB, :] = acc
            part = fold(acc)
            mark = part if mark is None else mark + part
        return mark

    def anchor(mark):
        oc_buf[pl.ds(tt + dyn0, ANCHOR), 0:LANES] = mark.astype(oc_buf.dtype)

    n_free = 3
    n_groups = n_piece - n_free - 1
    per = tt // CONV_SUB // n_groups
    piece(0)
    anchor(glu())
    piece(1)
    shifts(range(1, SUBLANES))
    for p in range(2, n_free + 1):
        piece(p)
    for k in range(n_groups):
        anchor(strips(k * per * CONV_SUB, per))
        piece(n_free + 1 + k)

    y = ybuf[...]
    mu = jnp.mean(y, axis=-1, keepdims=True)
    yc = y - mu
    var = jnp.mean(yc * yc, axis=-1, keepdims=True)
    hn = _silu(yc * lax.rsqrt(var + LN_EPS) * lng_ref[...] + lnb_ref[...])
    out = jnp.dot(hn.astype(BF16), wpw_ref[...], preferred_element_type=F32)
    oc_buf[0:tt, :] = (out * _silu(gc_ref[...])).astype(oc_buf.dtype)


def _conv_out(layer, uconv, gconv, x2, o_mla, o_ssd, wdw, bdw, lng, lnb, wpw, w_out, seq, tt):
    m, d = x2.shape
    n = m // tt
    n_piece = w_out.shape[1]
    cur_row = lambda i: (jnp.minimum(i, n - 1), 0)
    prev_row = lambda i: (jnp.maximum(i - 1, 0), 0)
    w_spec = pl.BlockSpec((None,) + w_out.shape[1:], lambda i: (layer, 0, 0, 0), pipeline_mode=pl.Buffered(1))
    return pl.pallas_call(
        functools.partial(_conv_out_kernel, tiles_per_seq=seq // tt),
        grid=(n + 1,),
        in_specs=[pl.BlockSpec((tt, 2 * CONV_WIDTH), cur_row),
                  pl.BlockSpec((tt, CONV_WIDTH), cur_row),
                  pl.BlockSpec((tt, d), prev_row),
                  pl.BlockSpec((tt, o_mla.shape[1]), prev_row),
                  pl.BlockSpec((tt, o_ssd.shape[1]), prev_row),
                  _resident(wdw, layer), _resident(bdw, layer), _resident(lng, layer), _resident(lnb, layer),
                  _resident(wpw, layer), w_spec],
        out_specs=pl.BlockSpec((tt, d), prev_row),
        out_shape=jax.ShapeDtypeStruct((m, d), F32),
        scratch_shapes=[pltpu.VMEM((tt + CONV_HALO, CONV_WIDTH), F32),
                        pltpu.VMEM((SUBLANES - 1, tt + CONV_HALO, CONV_WIDTH), F32),
                        pltpu.VMEM((tt, CONV_WIDTH), F32),
                        pltpu.VMEM((tt + ANCHOR, CONV_WIDTH), BF16)],
        compiler_params=_params("arbitrary"),
        name="conv_out_proj",
    )(uconv, gconv, x2, o_mla, o_ssd, wdw, bdw, lng, lnb, wpw, w_out)


def _rows(p):
    return p.astype(F32)[:, None, :]


def _pad_rows(p, lane0, width):
    return jnp.pad(p.astype(F32), ((0, 0), (lane0, width - lane0 - p.shape[1])))[:, None, :]


def _rope_tables(seq):
    half = QK_ROPE // 2
    inv_freq = ROPE_THETA ** (-jnp.arange(half, dtype=F32) / half)
    ang = jnp.arange(seq).astype(F32)[:, None] * inv_freq[None, :]
    cos, sin = jnp.cos(ang), jnp.sin(ang)
    z = lambda n: jnp.zeros((seq, n), F32)
    cos_t = jnp.concatenate([cos, cos, z(LANES - QK_ROPE)], axis=1)
    sa_t = jnp.concatenate([-sin, z(LANES - half)], axis=1)
    sb_t = jnp.concatenate([z(half), sin, z(LANES - QK_ROPE)], axis=1)
    return cos_t, sa_t, sb_t


def kernel(x, norm_g, w_in, q_a_norm, w_q_b, kv_a_norm, w_kv_b, q_norm, k_norm, conv_dw_w, conv_dw_b, conv_ln_g,
           conv_ln_b, conv_pw_w, ssd_conv_w, ssd_conv_b, ssd_dt_bias, ssd_A_log, ssd_D, ssd_norm_g, w_out):
    bsz, seq, d = x.shape
    m = bsz * seq
    x2 = x.reshape(m, d)
    tabs = _rope_tables(seq)

    w_in_k = _prep_w_in(w_in)
    wq, wkv, wpw = _prep_small(w_q_b, w_kv_b, conv_pw_w)
    w_out_k = _prep_cast(w_out, min(512, w_out.shape[1]))
    g_in, qan, kvan = _rows(norm_g), _rows(q_a_norm), _rows(kv_a_norm)
    kn = _pad_rows(k_norm, 0, QK_PAD)
    half = QK_ROPE // 2
    q_swapped = jnp.concatenate([q_norm[:, QK_NOPE + half:], q_norm[:, QK_NOPE:QK_NOPE + half]], axis=1)
    qn = jnp.concatenate([_pad_rows(q_norm, 0, QK_PAD), _pad_rows(q_swapped, 0, LANES)], axis=2)
    wdw = jnp.pad(conv_dw_w.astype(F32), ((0, 0), (0, CONV_HALO - CONV_K), (0, 0)))
    bdw, lng, lnb = _rows(conv_dw_b), _rows(conv_ln_g), _rows(conv_ln_b)
    cw = jnp.pad(ssd_conv_w.astype(F32), ((0, 0), (0, SUBLANES - SSD_CONV_K), (0, 0)))
    cb, ng = _rows(ssd_conv_b), _rows(ssd_norm_g)
    dtb, alog = _pad_rows(ssd_dt_bias, DT_LANE0, LANES), _pad_rows(ssd_A_log, DT_LANE0, LANES)
    dsk = _rows(jnp.repeat(ssd_D, SSD_HEAD_DIM, axis=1))

    tm_in = min(512, m)
    t_seq = min(512, seq)
    for layer in range(norm_g.shape[0]):
        cq, ckv, kd, gmla, uconv, gconv, z, xbc = _in_proj(layer, x2, g_in, w_in_k, tm_in)
        q, k, v = _mla_prep(layer, cq, ckv, kd, qan, wq, kvan, wkv, qn, kn, *tabs, bsz, seq, t_seq)
        o_mla = _attention(q, k, v, gmla, bsz, seq, t_seq)
        o_ssd = _ssd(layer, z, xbc, kd, cw, cb, dtb, alog, dsk, ng, bsz, seq, t_seq)
        x2 = _conv_out(layer, uconv, gconv, x2, o_mla, o_ssd, wdw, bdw, lng, lnb, wpw, w_out_k, seq, t_seq)
    return x2.reshape(bsz, seq, d)
```

```python
import functools
import math

import jax
import jax.numpy as jnp
from jax import lax
from jax.experimental import pallas as pl
from jax.experimental.pallas import tpu as pltpu

F32 = jnp.float32
BF16 = jnp.bfloat16

MLA_HEADS = 6
QK_NOPE = 128
QK_ROPE = 64
QK_HEAD = QK_NOPE + QK_ROPE
V_HEAD = 128
Q_LORA = 512
KV_LORA = 256
MLA_WIDTH = MLA_HEADS * V_HEAD
ROPE_THETA = 10000.0
CONV_WIDTH = 512
CONV_K = 31
SSD_HEADS = 12
SSD_HEAD_DIM = 64
SSD_WIDTH = SSD_HEADS * SSD_HEAD_DIM
SSD_GROUPS = 2
SSD_STATE = 128
SSD_CONV_K = 4
SSD_CHUNK = 128
SSD_XBC = SSD_WIDTH + 2 * SSD_GROUPS * SSD_STATE
NORM_EPS = 1e-6
LN_EPS = 1e-5

LANES = 128
SUBLANES = 8
QK_PAD = 2 * LANES
Q_UP = 3 * LANES
KD_WIDTH = LANES
DT_LANE0 = QK_ROPE
CONV_HALO = 32
SSD_HALO = SUBLANES
VMEM_LIMIT = 56 * 1024 * 1024
ATTN_STRIP = 64

SEG_WIDTHS = (Q_LORA, KV_LORA, KD_WIDTH, MLA_WIDTH, 2 * CONV_WIDTH, CONV_WIDTH, SSD_WIDTH, SSD_XBC)


def _sigmoid(x):
    return 1.0 / (1.0 + jnp.exp2(x * (-math.log2(math.e))))


def _silu(x):
    return x * _sigmoid(x)


def _softplus(x):
    return jnp.maximum(x, 0.0) + jnp.log1p(jnp.exp(-jnp.abs(x)))


def _rms(x, g, eps=NORM_EPS):
    ms = jnp.mean(x * x, axis=-1, keepdims=True)
    return x * lax.rsqrt(ms + eps) * g


def _params(*sem):
    return pltpu.CompilerParams(dimension_semantics=sem, vmem_limit_bytes=VMEM_LIMIT)


def _resident(stacked, layer):
    _, a, b = stacked.shape
    return pl.BlockSpec((None, a, b), lambda *_: (layer, 0, 0), pipeline_mode=pl.Buffered(1))


W_IN_HALF = LANES // 2


def _prep_w_in_kernel(a_ref, b_ref, o_ref, *, depth, n_kt, n_dt):
    j = pl.program_id(0)
    kd_tile = (Q_LORA + KV_LORA) // LANES
    stride = depth * n_kt
    row = lax.broadcasted_iota(jnp.int32, (W_IN_HALF, LANES), 0)
    keep_b = (j != kd_tile) | (row < n_dt)
    for l in range(depth):
        for kt in range(n_kt):
            xa = a_ref[pl.ds(kt * depth + l, W_IN_HALF, stride=stride), :]
            xb = b_ref[pl.ds(kt * depth + l, W_IN_HALF, stride=stride), :]
            x = jnp.concatenate([xa, jnp.where(keep_b, xb, 0.0)], axis=0)
            o_ref[l, kt * LANES:(kt + 1) * LANES, :] = x.T.astype(o_ref.dtype)


def _prep_w_in(w_in):
    depth, d, n = w_in.shape
    n_kt = d // LANES
    n_out = sum(SEG_WIDTHS)
    rows_per_group = W_IN_HALF * n_kt * depth
    view = w_in.transpose(2, 0, 1).reshape(n, depth, n_kt, LANES).transpose(0, 2, 1, 3).reshape(-1, LANES)
    kd_tile = (Q_LORA + KV_LORA) // LANES
    dt_group = (n - SSD_HEADS) // W_IN_HALF
    assert (n - SSD_HEADS) % W_IN_HALF == 0 and (Q_LORA + KV_LORA) % LANES == 0 and QK_ROPE == W_IN_HALF
    a_idx = lambda j: (jnp.where(j <= kd_tile, 2 * j, 2 * j - 1), 0)
    b_idx = lambda j: (jnp.where(j < kd_tile, 2 * j + 1, jnp.where(j == kd_tile, dt_group, 2 * j)), 0)
    return pl.pallas_call(
        functools.partial(_prep_w_in_kernel, depth=depth, n_kt=n_kt, n_dt=SSD_HEADS),
        grid=(n_out // LANES,),
        in_specs=[pl.BlockSpec((rows_per_group, LANES), a_idx),
                  pl.BlockSpec((rows_per_group, LANES), b_idx)],
        out_specs=pl.BlockSpec((depth, d, LANES), lambda j: (0, 0, j)),
        out_shape=jax.ShapeDtypeStruct((depth, d, n_out), BF16),
        compiler_params=_params("parallel"),
        name="prep_w_in",
    )(view, view)


def _prep_small_kernel(wq_ref, wkv_ref, wpw_ref, oq_ref, okv_ref, opw_ref):
    wq = wq_ref[...]
    zq = jnp.zeros((wq.shape[0], QK_PAD - QK_HEAD), F32)
    half = QK_ROPE // 2
    parts = []
    for h in range(MLA_HEADS):
        r0 = h * QK_HEAD + QK_NOPE
        parts += [wq[:, h * QK_HEAD:(h + 1) * QK_HEAD], zq,
                  wq[:, r0 + half:r0 + QK_ROPE], wq[:, r0:r0 + half], zq]
    oq_ref[...] = jnp.concatenate(parts, axis=1).astype(oq_ref.dtype)
    wkv = wkv_ref[...]
    per = QK_NOPE + V_HEAD
    k_cols = [wkv[:, h * per:h * per + QK_NOPE] for h in range(MLA_HEADS)]
    v_cols = [wkv[:, h * per + QK_NOPE:(h + 1) * per] for h in range(MLA_HEADS)]
    okv_ref[...] = jnp.concatenate(k_cols + v_cols, axis=1).astype(okv_ref.dtype)
    opw_ref[...] = wpw_ref[...].astype(opw_ref.dtype)


def _prep_small(w_q_b, w_kv_b, conv_pw_w):
    depth = w_q_b.shape[0]
    whole = lambda a: pl.BlockSpec((None,) + a.shape[1:], lambda l: (l, 0, 0))
    shapes = [(depth, Q_LORA, MLA_HEADS * Q_UP), w_kv_b.shape, conv_pw_w.shape]
    return pl.pallas_call(
        _prep_small_kernel,
        grid=(depth,),
        in_specs=[whole(w_q_b), whole(w_kv_b), whole(conv_pw_w)],
        out_specs=[pl.BlockSpec((None,) + s[1:], lambda l: (l, 0, 0)) for s in shapes],
        out_shape=[jax.ShapeDtypeStruct(s, BF16) for s in shapes],
        compiler_params=_params("parallel"),
        name="prep_small",
    )(w_q_b, w_kv_b, conv_pw_w)


OUT_PIECE = 2 * LANES
CONV_SUB = 32
ANCHOR = 16


def _cast_kernel(w_ref, o_ref):
    for p in range(o_ref.shape[0]):
        o_ref[p] = w_ref[:, p * OUT_PIECE:(p + 1) * OUT_PIECE].astype(o_ref.dtype)


def _prep_cast(w, tk):
    depth, k, n = w.shape
    n_piece = n // OUT_PIECE
    return pl.pallas_call(
        _cast_kernel,
        grid=(depth, k // tk),
        in_specs=[pl.BlockSpec((None, tk, n), lambda l, i: (l, i, 0))],
        out_specs=pl.BlockSpec((None, n_piece, tk, OUT_PIECE), lambda l, i: (l, 0, i, 0)),
        out_shape=jax.ShapeDtypeStruct((depth, n_piece, k, OUT_PIECE), BF16),
        compiler_params=_params("parallel", "parallel"),
        name="prep_cast",
    )(w)


def _in_proj_kernel(x_ref, g_ref, w_ref, *out_refs):
    h = _rms(x_ref[...], g_ref[...]).astype(BF16)
    off = 0
    for o_ref in out_refs:
        n = o_ref.shape[-1]
        o_ref[...] = jnp.dot(h, w_ref[:, off:off + n], preferred_element_type=F32).astype(o_ref.dtype)
        off += n


def _in_proj(layer, x2, g, w, tm):
    m, d = x2.shape
    assert w.shape[2] == sum(SEG_WIDTHS)
    return pl.pallas_call(
        _in_proj_kernel,
        grid=(m // tm,),
        in_specs=[pl.BlockSpec((tm, d), lambda i: (i, 0)),
                  _resident(g, layer),
                  _resident(w, layer)],
        out_specs=[pl.BlockSpec((tm, n), lambda i: (i, 0)) for n in SEG_WIDTHS],
        out_shape=[jax.ShapeDtypeStruct((m, n), F32) for n in SEG_WIDTHS],
        compiler_params=_params("parallel"),
        name="in_proj",
    )(x2, g, w)


def _mla_prep_kernel(cq_ref, ckv_ref, kd_ref, qan_ref, wq_ref, kvan_ref, wkv_ref, qn_ref, kn_ref,
                     cos_ref, sa_ref, sb_ref, q_ref, k_ref, v_ref):
    scale = math.log2(math.e) / math.sqrt(QK_HEAD)
    hq = _rms(cq_ref[...], qan_ref[...]).astype(BF16)
    qf = jnp.dot(hq, wq_ref[...], preferred_element_type=F32)
    hkv = _rms(ckv_ref[...], kvan_ref[...]).astype(BF16)
    kvf = jnp.dot(hkv, wkv_ref[...], preferred_element_type=F32)

    cos = cos_ref[...]
    sa = sa_ref[...]
    sb = sb_ref[...]

    def rope(r):
        return r * cos + pltpu.roll(r, LANES - QK_ROPE // 2, 1) * sa + pltpu.roll(r, QK_ROPE // 2, 1) * sb

    kd = kd_ref[...]
    lane = lax.broadcasted_iota(jnp.int32, kd.shape, 1)
    kpe = jnp.where(lane < QK_ROPE, kd, 0.0)
    kpe_ss = jnp.sum(kpe * kpe, axis=-1, keepdims=True)

    qn_a = qn_ref[:, 0:LANES] * scale
    qn_b = qn_ref[:, LANES:QK_PAD] * scale
    qn_s = qn_ref[:, QK_PAD:Q_UP] * scale
    sin_signed = sa + sb
    kn_w = kn_ref[...]
    kpe_rot = rope(kpe * kn_w[:, LANES:QK_PAD])
    for h in range(MLA_HEADS):
        qa = qf[:, h * Q_UP:h * Q_UP + LANES]
        qb = qf[:, h * Q_UP + LANES:h * Q_UP + QK_PAD]
        qs = qf[:, h * Q_UP + QK_PAD:(h + 1) * Q_UP]
        ss = jnp.sum(qa * qa + qb * qb, axis=-1, keepdims=True)
        inv = lax.rsqrt(ss * (1.0 / QK_HEAD) + NORM_EPS)
        q_ref[0, h, :, 0:LANES] = (qa * inv * qn_a).astype(q_ref.dtype)
        q_ref[0, h, :, LANES:QK_PAD] = (inv * (qb * qn_b * cos + qs * qn_s * sin_signed)).astype(q_ref.dtype)

        ka = kvf[:, h * LANES:(h + 1) * LANES]
        ss = jnp.sum(ka * ka, axis=-1, keepdims=True) + kpe_ss
        inv = lax.rsqrt(ss * (1.0 / QK_HEAD) + NORM_EPS)
        k_ref[0, h, :, 0:LANES] = (ka * inv * kn_w[:, 0:LANES]).astype(k_ref.dtype)
        k_ref[0, h, :, LANES:QK_PAD] = (kpe_rot * inv).astype(k_ref.dtype)

        v_ref[0, h, :, :] = kvf[:, MLA_WIDTH + h * V_HEAD:MLA_WIDTH + (h + 1) * V_HEAD].astype(v_ref.dtype)


def _mla_prep(layer, cq, ckv, kd, qan, wq, kvan, wkv, qn, kn, cos_t, sa_t, sb_t, bsz, seq, tm):
    nt = seq // tm
    row = lambda b, i: (b * nt + i, 0)
    pos = lambda b, i: (i, 0)
    hd = MLA_HEADS
    return pl.pallas_call(
        _mla_prep_kernel,
        grid=(bsz, nt),
        in_specs=[pl.BlockSpec((tm, Q_LORA), row),
                  pl.BlockSpec((tm, KV_LORA), row),
                  pl.BlockSpec((tm, KD_WIDTH), row),
                  _resident(qan, layer), _resident(wq, layer), _resident(kvan, layer), _resident(wkv, layer),
                  _resident(qn, layer), _resident(kn, layer),
                  pl.BlockSpec((tm, LANES), pos), pl.BlockSpec((tm, LANES), pos), pl.BlockSpec((tm, LANES), pos)],
        out_specs=[pl.BlockSpec((1, hd, tm, QK_PAD), lambda b, i: (b, 0, i, 0)),
                   pl.BlockSpec((1, hd, tm, QK_PAD), lambda b, i: (b, 0, i, 0)),
                   pl.BlockSpec((1, hd, tm, V_HEAD), lambda b, i: (b, 0, i, 0))],
        out_shape=[jax.ShapeDtypeStruct((bsz, hd, seq, QK_PAD), BF16),
                   jax.ShapeDtypeStruct((bsz, hd, seq, QK_PAD), BF16),
                   jax.ShapeDtypeStruct((bsz, hd, seq, V_HEAD), BF16)],
        compiler_params=_params("parallel", "parallel"),
        name="mla_prep",
    )(cq, ckv, kd, qan, wq, kvan, wkv, qn, kn, cos_t, sa_t, sb_t)


def _attn_kernel(qi_ref, ki_ref, q_ref, k_ref, v_ref, g_ref, o_ref, m_sc, l_sc, acc_sc, s_sc, p_sc, a_sc):
    qi = qi_ref[pl.program_id(1)]
    ki = ki_ref[pl.program_id(1)]
    tq = q_ref.shape[2]
    tk = k_ref.shape[2]

    @pl.when(ki == 0)
    def _():
        m_sc[...] = jnp.full(m_sc.shape, -jnp.inf, F32)
        l_sc[...] = jnp.zeros(l_sc.shape, F32)
        acc_sc[...] = jnp.zeros(acc_sc.shape, F32)

    strip = min(ATTN_STRIP, tq)
    n_chunk = tk // LANES

    def scores(h):
        s_sc[h % 2] = lax.dot_general(q_ref[0, h], k_ref[0, h], (((1,), (1,)), ((), ())),
                                      preferred_element_type=F32)

    def softmax_strip(h, r0, masked):
        slot = h % 2
        rows = slice(r0, r0 + strip)
        live = [c for c in range(n_chunk) if not (masked and c * LANES >= r0 + strip)]
        chunks = []
        for c in live:
            x = s_sc[slot, rows, c * LANES:(c + 1) * LANES]
            if masked and (c + 1) * LANES - 1 > r0:
                rr = r0 + lax.broadcasted_iota(jnp.int32, (strip, LANES), 0)
                cc = c * LANES + lax.broadcasted_iota(jnp.int32, (strip, LANES), 1)
                x = jnp.where(rr >= cc, x, -jnp.inf)
            chunks.append(x)
        m_prev = m_sc[h, rows, :]
        m_loc = functools.reduce(jnp.maximum, chunks)
        m_new = jnp.maximum(m_prev, jnp.max(m_loc, axis=-1, keepdims=True))
        alpha = jnp.exp2(m_prev - m_new)
        ps = [jnp.exp2(x - m_new) for x in chunks]
        l_sc[h, rows, :] = alpha * l_sc[h, rows, :] + functools.reduce(jnp.add, ps)
        m_sc[h, rows, :] = m_new
        a_sc[slot, rows, :] = alpha
        dead = [jnp.zeros((strip, LANES), BF16)] * (n_chunk - len(live))
        p_sc[slot, rows, :] = jnp.concatenate([p.astype(BF16) for p in ps] + dead, axis=1)

    def step(masked):
        scores(0)
        for h in range(MLA_HEADS):
            if h + 1 < MLA_HEADS:
                scores(h + 1)
            for r0 in range(0, tq, strip):
                softmax_strip(h, r0, masked)
            pv = jnp.dot(p_sc[h % 2], v_ref[0, h], preferred_element_type=F32)
            acc_sc[h] = a_sc[h % 2] * acc_sc[h] + pv

    @pl.when(ki < qi)
    def _():
        step(False)

    @pl.when(ki == qi)
    def _():
        step(True)
        for h in range(MLA_HEADS):
            o = acc_sc[h] / jnp.sum(l_sc[h], axis=-1, keepdims=True)
            g = g_ref[:, h * V_HEAD:(h + 1) * V_HEAD]
            o_ref[:, h * V_HEAD:(h + 1) * V_HEAD] = (o * _silu(g)).astype(o_ref.dtype)


def _attention(q, k, v, gmla, bsz, seq, tq):
    hd = MLA_HEADS
    nq = seq // tq
    pairs = [(i, j) for i in range(nq) for j in range(i + 1)]
    qi_tab = jnp.asarray([i for i, _ in pairs], jnp.int32)
    ki_tab = jnp.asarray([j for _, j in pairs], jnp.int32)
    q_idx = lambda b, t, qi, ki: (b, 0, qi[t], 0)
    kv_idx = lambda b, t, qi, ki: (b, 0, ki[t], 0)
    row = lambda b, t, qi, ki: (b * nq + qi[t], 0)
    grid_spec = pltpu.PrefetchScalarGridSpec(
        num_scalar_prefetch=2,
        grid=(bsz, len(pairs)),
        in_specs=[pl.BlockSpec((1, hd, tq, QK_PAD), q_idx),
                  pl.BlockSpec((1, hd, tq, QK_PAD), kv_idx),
                  pl.BlockSpec((1, hd, tq, V_HEAD), kv_idx),
                  pl.BlockSpec((tq, MLA_WIDTH), row)],
        out_specs=pl.BlockSpec((tq, MLA_WIDTH), row),
        scratch_shapes=[pltpu.VMEM((hd, tq, LANES), F32), pltpu.VMEM((hd, tq, LANES), F32),
                        pltpu.VMEM((hd, tq, V_HEAD), F32),
                        pltpu.VMEM((2, tq, tq), F32), pltpu.VMEM((2, tq, tq), BF16),
                        pltpu.VMEM((2, tq, LANES), F32)])
    return pl.pallas_call(
        _attn_kernel,
        grid_spec=grid_spec,
        out_shape=jax.ShapeDtypeStruct((bsz * seq, MLA_WIDTH), BF16),
        compiler_params=_params("parallel", "arbitrary"),
        name="attention",
    )(qi_tab, ki_tab, q, k, v, gmla)


def _split3(x):
    hi = x.astype(BF16)
    r1 = x - hi.astype(F32)
    mid = r1.astype(BF16)
    lo = (r1 - mid.astype(F32)).astype(BF16)
    return hi, mid, lo


def _ssd_kernel(z_ref, xbc_ref, kd_ref, cw_ref, cb_ref, dtb_ref, alog_ref, dsk_ref, ng_ref, o_ref,
                xbuf, xact, state, shx, acol_sc, xdtb_sc, xdte_sc, grow_sc, y_sc):
    i = pl.program_id(0)
    nb, ts = o_ref.shape[0], o_ref.shape[1]
    L = SSD_CHUNK
    hp = SSD_HEAD_DIM
    gw = SSD_WIDTH // SSD_GROUPS
    pairs_per_group = SSD_HEADS // SSD_GROUPS // 2
    npair = SSD_HEADS // 2

    @pl.when(i == 0)
    def _():
        xbuf[:, 0:SSD_HALO, :] = jnp.zeros((nb, SSD_HALO, SSD_XBC), F32)
        state[...] = jnp.zeros(state.shape, F32)

    @pl.when(i > 0)
    def _():
        xbuf[:, 0:SSD_HALO, :] = xbuf[:, ts:ts + SSD_HALO, :]

    for b in range(nb):
        xbuf[b, SSD_HALO:SSD_HALO + ts, :] = xbc_ref[b]
        for s in range(1, SSD_CONV_K):
            shx[s - 1, :, :] = xbuf[b, SSD_HALO - s:SSD_HALO - s + ts, :]
        acc = cb_ref[...] + cw_ref[SSD_CONV_K - 1:SSD_CONV_K, :] * xbc_ref[b]
        for s in range(1, SSD_CONV_K):
            acc = acc + cw_ref[SSD_CONV_K - 1 - s:SSD_CONV_K - s, :] * shx[s - 1]
        xact[b] = _silu(acc)

    lane = lax.broadcasted_iota(jnp.int32, (1, LANES), 1)
    is_dt = (lane >= DT_LANE0) & (lane < DT_LANE0 + SSD_HEADS)
    a_neg = jnp.where(is_dt, -jnp.exp(alog_ref[...]), 0.0)
    rr = lax.broadcasted_iota(jnp.int32, (L, L), 0)
    cc = lax.broadcasted_iota(jnp.int32, (L, L), 1)
    causal = rr >= cc
    tri = causal.astype(BF16)
    left = lax.broadcasted_iota(jnp.int32, (L, LANES), 1) < hp

    def decay_geometry(b, r0):
        dt = _softplus(kd_ref[b, pl.ds(r0, L), :] + dtb_ref[...])
        hi, mid, lo = _split3(dt * a_neg)
        acs = (jnp.dot(tri, hi, preferred_element_type=F32)
               + jnp.dot(tri, mid, preferred_element_type=F32)
               + jnp.dot(tri, lo, preferred_element_type=F32))
        return dict(dt=dt, acs=acs, acs_t=acs.T)

    def expand(b, r0, v):
        def col(a, h):
            return jnp.broadcast_to(a[:, DT_LANE0 + h:DT_LANE0 + h + 1], (L, LANES))

        lasts = []
        for j in range(npair):
            lanes = slice(j * LANES, (j + 1) * LANES)
            a0, a1 = col(v["acs"], 2 * j), col(v["acs"], 2 * j + 1)
            acol_sc[b, 2 * j] = a0
            acol_sc[b, 2 * j + 1] = a1
            acs_pair = jnp.where(left, a0, a1)
            dt_pair = jnp.where(left, col(v["dt"], 2 * j), col(v["dt"], 2 * j + 1))
            last = acs_pair[L - 1:L, :]
            xdt = xact[b, pl.ds(r0, L), lanes] * dt_pair
            xdtb_sc[b, :, lanes] = xdt.astype(BF16)
            xdte_sc[b, :, lanes] = (xdt * jnp.exp(last - acs_pair)).astype(BF16)
            grow_sc[b, :, lanes] = jnp.exp(acs_pair)
            lasts.append(last)
        v["lasts"] = lasts

    def mix(b, r0, v):
        ssq = jnp.zeros((L, LANES), F32)
        for g in range(SSD_GROUPS):
            c0 = SSD_WIDTH + g * SSD_STATE
            c1 = SSD_WIDTH + (SSD_GROUPS + g) * SSD_STATE
            grp = slice(g * gw, (g + 1) * gw)
            bg = xact[b, pl.ds(r0, L), c0:c0 + SSD_STATE]
            cg_b = xact[b, pl.ds(r0, L), c1:c1 + SSD_STATE].astype(BF16)
            cb = lax.dot_general(cg_b, bg.astype(BF16), (((1,), (1,)), ((), ())), preferred_element_type=F32)
            st_prev = state[b, :, grp]
            st_new = jnp.dot(bg.T.astype(BF16), xdte_sc[b, :, grp], preferred_element_type=F32)
            y_off = jnp.dot(cg_b, st_prev.astype(BF16), preferred_element_type=F32) * grow_sc[b, :, grp]
            last_g = jnp.concatenate(v["lasts"][g * pairs_per_group:(g + 1) * pairs_per_group], axis=1)
            state[b, :, grp] = st_prev * jnp.exp(last_g) + st_new
            for jj in range(pairs_per_group):
                j = g * pairs_per_group + jj
                lanes = slice(j * LANES, (j + 1) * LANES)
                pair = xdtb_sc[b, :, lanes]
                yd = []
                for h in (2 * j, 2 * j + 1):
                    seg = acol_sc[b, h] - v["acs_t"][DT_LANE0 + h:DT_LANE0 + h + 1, :]
                    decay = jnp.exp(jnp.where(causal, seg, -jnp.inf))
                    yd.append(jnp.dot((cb * decay).astype(BF16), pair, preferred_element_type=F32))
                y = (jnp.where(left, yd[0], yd[1]) + y_off[:, jj * LANES:(jj + 1) * LANES]
                     + dsk_ref[:, lanes] * xact[b, pl.ds(r0, L), lanes])
                yz = y * _silu(z_ref[b, pl.ds(r0, L), lanes])
                y_sc[b, :, lanes] = yz
                ssq = ssq + yz * yz
        v["inv"] = lax.rsqrt(jnp.sum(ssq, axis=-1, keepdims=True) * (1.0 / SSD_WIDTH) + NORM_EPS)

    def finish(b, r0, v):
        o_ref[b, pl.ds(r0, L), :] = (y_sc[b] * v["inv"] * ng_ref[...]).astype(o_ref.dtype)

    def chunk(c, carry):
        r0 = pl.multiple_of(c * L, L)
        vals = [decay_geometry(b, r0) for b in range(nb)]
        for stage in (expand, mix, finish):
            for b in range(nb):
                stage(b, r0, vals[b])
        return carry

    lax.fori_loop(0, ts // L, chunk, 0)


def _ssd(layer, z, xbc, kd, cw, cb, dtb, alog, dsk, ng, bsz, seq, ts):
    blk = lambda w: pl.BlockSpec((bsz, ts, w), lambda i: (0, i, 0))
    out = pl.pallas_call(
        _ssd_kernel,
        grid=(seq // ts,),
        in_specs=[blk(SSD_WIDTH), blk(SSD_XBC), blk(KD_WIDTH),
                  _resident(cw, layer), _resident(cb, layer), _resident(dtb, layer), _resident(alog, layer),
                  _resident(dsk, layer), _resident(ng, layer)],
        out_specs=blk(SSD_WIDTH),
        out_shape=jax.ShapeDtypeStruct((bsz, seq, SSD_WIDTH), BF16),
        scratch_shapes=[pltpu.VMEM((bsz, ts + SSD_HALO, SSD_XBC), F32), pltpu.VMEM((bsz, ts, SSD_XBC), F32),
                        pltpu.VMEM((bsz, SSD_STATE, SSD_WIDTH), F32),
                        pltpu.VMEM((SSD_CONV_K - 1, ts, SSD_XBC), F32),
                        pltpu.VMEM((bsz, SSD_HEADS, SSD_CHUNK, LANES), F32),
                        pltpu.VMEM((bsz, SSD_CHUNK, SSD_WIDTH), BF16), pltpu.VMEM((bsz, SSD_CHUNK, SSD_WIDTH), BF16),
                        pltpu.VMEM((bsz, SSD_CHUNK, SSD_WIDTH), F32), pltpu.VMEM((bsz, SSD_CHUNK, SSD_WIDTH), F32)],
        compiler_params=_params("arbitrary"),
        name="ssd",
    )(z.reshape(bsz, seq, -1), xbc.reshape(bsz, seq, -1), kd.reshape(bsz, seq, -1), cw, cb, dtb, alog, dsk, ng)
    return out.reshape(bsz * seq, SSD_WIDTH)


def _conv_out_kernel(u_ref, gc_ref, x_ref, a_ref, s_ref, wdw_ref, bdw_ref, lng_ref, lnb_ref, wpw_ref, w_ref,
                     o_ref, hbuf, shbuf, ybuf, oc_buf, *, tiles_per_seq):
    i = pl.program_id(0)
    tt = u_ref.shape[0]
    cw = CONV_WIDTH
    n_piece = w_ref.shape[0]
    seq_start = lax.rem(i, tiles_per_seq) == 0

    @pl.when(i == 0)
    def _():
        oc_buf[...] = jnp.zeros(oc_buf.shape, oc_buf.dtype)

    @pl.when(seq_start)
    def _():
        hbuf[0:CONV_HALO, :] = jnp.zeros((CONV_HALO, cw), F32)

    @pl.when(jnp.logical_not(seq_start))
    def _():
        hbuf[0:CONV_HALO, :] = hbuf[tt:tt + CONV_HALO, :]

    n0 = a_ref.shape[1]
    n1 = n0 + cw
    n2 = n1 + s_ref.shape[1]
    base = CONV_HALO - (CONV_K - 1)
    span = tt + CONV_HALO - SUBLANES
    dyn0 = pl.multiple_of(jnp.minimum(i, 0), ANCHOR)

    def fold(v):
        rows = functools.reduce(jnp.add, [v[r:r + ANCHOR, :] for r in range(0, v.shape[0], ANCHOR)])
        return functools.reduce(jnp.add, [rows[:, c:c + LANES] for c in range(0, v.shape[1], LANES)])

    def piece(p):
        cols = slice(p * OUT_PIECE, (p + 1) * OUT_PIECE)
        y = jnp.dot(oc_buf[pl.ds(dyn0, tt), :], w_ref[p, n0:n1, :], preferred_element_type=F32)
        y = y + jnp.dot(a_ref[...], w_ref[p, 0:n0, :], preferred_element_type=F32)
        y = y + jnp.dot(s_ref[...], w_ref[p, n1:n2, :], preferred_element_type=F32)
        o_ref[:, cols] = x_ref[:, cols] + y

    def glu():
        hval = u_ref[:, 0:cw] * _sigmoid(u_ref[:, cw:2 * cw])
        hbuf[CONV_HALO:CONV_HALO + tt, :] = hval
        return fold(hval)

    def shifts(residues):
        for res in residues:
            for r0 in range(0, span, CONV_SUB):
                rows = min(CONV_SUB, span - r0)
                shbuf[res - 1, r0:r0 + rows, :] = hbuf[res + r0:res + r0 + rows, :]

    def strips(first, count):
        mark = None
        for r0 in range(first, first + count * CONV_SUB, CONV_SUB):
            acc = jnp.broadcast_to(bdw_ref[...], (CONV_SUB, cw))
            for kk in range(CONV_K):
                res = (base + kk) % SUBLANES
                start = r0 + (base + kk - res)
                tap = hbuf[start:start + CONV_SUB, :] if res == 0 else shbuf[res - 1, start:start + CONV_SUB, :]
                acc = acc + wdw_ref[kk:kk + 1, :] * tap
            ybuf[r0:r0 + CONV_SUB, :] = acc
            part = fold(acc)
            mark = part if mark is None else mark + part
        return mark

    def anchor(mark):
        oc_buf[pl.ds(tt + dyn0, ANCHOR), 0:LANES] = mark.astype(oc_buf.dtype)

    n_free = 3
    n_groups = n_piece - n_free - 1
    per = tt // CONV_SUB // n_groups
    piece(0)
    anchor(glu())
    piece(1)
    shifts(range(1, SUBLANES))
    for p in range(2, n_free + 1):
        piece(p)
    for k in range(n_groups):
        anchor(strips(k * per * CONV_SUB, per))
        piece(n_free + 1 + k)

    y = ybuf[...]
    mu = jnp.mean(y, axis=-1, keepdims=True)
    yc = y - mu
    var = jnp.mean(yc * yc, axis=-1, keepdims=True)
    hn = _silu(yc * lax.rsqrt(var + LN_EPS) * lng_ref[...] + lnb_ref[...])
    out = jnp.dot(hn.astype(BF16), wpw_ref[...], preferred_element_type=F32)
    oc_buf[0:tt, :] = (out * _silu(gc_ref[...])).astype(oc_buf.dtype)


def _conv_out(layer, uconv, gconv, x2, o_mla, o_ssd, wdw, bdw, lng, lnb, wpw, w_out, seq, tt):
    m, d = x2.shape
    n = m // tt
    n_piece = w_out.shape[1]
    cur_row = lambda i: (jnp.minimum(i, n - 1), 0)
    prev_row = lambda i: (jnp.maximum(i - 1, 0), 0)
    w_spec = pl.BlockSpec((None,) + w_out.shape[1:], lambda i: (layer, 0, 0, 0), pipeline_mode=pl.Buffered(1))
    return pl.pallas_call(
        functools.partial(_conv_out_kernel, tiles_per_seq=seq // tt),
        grid=(n + 1,),
        in_specs=[pl.BlockSpec((tt, 2 * CONV_WIDTH), cur_row),
                  pl.BlockSpec((tt, CONV_WIDTH), cur_row),
                  pl.BlockSpec((tt, d), prev_row),
                  pl.BlockSpec((tt, o_mla.shape[1]), prev_row),
                  pl.BlockSpec((tt, o_ssd.shape[1]), prev_row),
                  _resident(wdw, layer), _resident(bdw, layer), _resident(lng, layer), _resident(lnb, layer),
                  _resident(wpw, layer), w_spec],
        out_specs=pl.BlockSpec((tt, d), prev_row),
        out_shape=jax.ShapeDtypeStruct((m, d), F32),
        scratch_shapes=[pltpu.VMEM((tt + CONV_HALO, CONV_WIDTH), F32),
                        pltpu.VMEM((SUBLANES - 1, tt + CONV_HALO, CONV_WIDTH), F32),
                        pltpu.VMEM((tt, CONV_WIDTH), F32),
                        pltpu.VMEM((tt + ANCHOR, CONV_WIDTH), BF16)],
        compiler_params=_params("arbitrary"),
        name="conv_out_proj",
    )(uconv, gconv, x2, o_mla, o_ssd, wdw, bdw, lng, lnb, wpw, w_out)


def _rows(p):
    return p.astype(F32)[:, None, :]


def _pad_rows(p, lane0, width):
    return jnp.pad(p.astype(F32), ((0, 0), (lane0, width - lane0 - p.shape[1])))[:, None, :]


ROW_TILE = 512


def _tile_sizes(m, seq):
    tm_in, t_seq = min(ROW_TILE, m), min(ROW_TILE, seq)
    assert m % tm_in == 0 and seq % t_seq == 0 and t_seq % SSD_CHUNK == 0 and t_seq % (4 * CONV_SUB) == 0
    return tm_in, t_seq


def _rope_tables(seq):
    half = QK_ROPE // 2
    inv_freq = ROPE_THETA ** (-jnp.arange(half, dtype=F32) / half)
    ang = jnp.arange(seq).astype(F32)[:, None] * inv_freq[None, :]
    cos, sin = jnp.cos(ang), jnp.sin(ang)
    z = lambda n: jnp.zeros((seq, n), F32)
    cos_t = jnp.concatenate([cos, cos, z(LANES - QK_ROPE)], axis=1)
    sa_t = jnp.concatenate([-sin, z(LANES - half)], axis=1)
    sb_t = jnp.concatenate([z(half), sin, z(LANES - QK_ROPE)], axis=1)
    return cos_t, sa_t, sb_t


def kernel(x, norm_g, w_in, q_a_norm, w_q_b, kv_a_norm, w_kv_b, q_norm, k_norm, conv_dw_w, conv_dw_b, conv_ln_g,
           conv_ln_b, conv_pw_w, ssd_conv_w, ssd_conv_b, ssd_dt_bias, ssd_A_log, ssd_D, ssd_norm_g, w_out):
    bsz, seq, d = x.shape
    m = bsz * seq
    x2 = x.reshape(m, d)
    tabs = _rope_tables(seq)

    w_in_k = _prep_w_in(w_in)
    wq, wkv, wpw = _prep_small(w_q_b, w_kv_b, conv_pw_w)
    w_out_k = _prep_cast(w_out, min(512, w_out.shape[1]))
    g_in, qan, kvan = _rows(norm_g), _rows(q_a_norm), _rows(kv_a_norm)
    kn = _pad_rows(k_norm, 0, QK_PAD)
    half = QK_ROPE // 2
    q_swapped = jnp.concatenate([q_norm[:, QK_NOPE + half:], q_norm[:, QK_NOPE:QK_NOPE + half]], axis=1)
    qn = jnp.concatenate([_pad_rows(q_norm, 0, QK_PAD), _pad_rows(q_swapped, 0, LANES)], axis=2)
    wdw = jnp.pad(conv_dw_w.astype(F32), ((0, 0), (0, CONV_HALO - CONV_K), (0, 0)))
    bdw, lng, lnb = _rows(conv_dw_b), _rows(conv_ln_g), _rows(conv_ln_b)
    cw = jnp.pad(ssd_conv_w.astype(F32), ((0, 0), (0, SUBLANES - SSD_CONV_K), (0, 0)))
    cb, ng = _rows(ssd_conv_b), _rows(ssd_norm_g)
    dtb, alog = _pad_rows(ssd_dt_bias, DT_LANE0, LANES), _pad_rows(ssd_A_log, DT_LANE0, LANES)
    dsk = _rows(jnp.repeat(ssd_D, SSD_HEAD_DIM, axis=1))

    tm_in, t_seq = _tile_sizes(m, seq)
    for layer in range(norm_g.shape[0]):
        cq, ckv, kd, gmla, uconv, gconv, z, xbc = _in_proj(layer, x2, g_in, w_in_k, tm_in)
        q, k, v = _mla_prep(layer, cq, ckv, kd, qan, wq, kvan, wkv, qn, kn, *tabs, bsz, seq, t_seq)
        o_mla = _attention(q, k, v, gmla, bsz, seq, t_seq)
        o_ssd = _ssd(layer, z, xbc, kd, cw, cb, dtb, alog, dsk, ng, bsz, seq, t_seq)
        x2 = _conv_out(layer, uconv, gconv, x2, o_mla, o_ssd, wdw, bdw, lng, lnb, wpw, w_out_k, seq, t_seq)
    return x2.reshape(bsz, seq, d)
```

```python
import functools
import math

import jax
import jax.numpy as jnp
from jax import lax
from jax.experimental import pallas as pl
from jax.experimental.pallas import tpu as pltpu

F32 = jnp.float32
BF16 = jnp.bfloat16

MLA_HEADS = 6
QK_NOPE = 128
QK_ROPE = 64
QK_HEAD = QK_NOPE + QK_ROPE
V_HEAD = 128
Q_LORA = 512
KV_LORA = 256
MLA_WIDTH = MLA_HEADS * V_HEAD
ROPE_THETA = 10000.0
CONV_WIDTH = 512
CONV_K = 31
SSD_HEADS = 12
SSD_HEAD_DIM = 64
SSD_WIDTH = SSD_HEADS * SSD_HEAD_DIM
SSD_GROUPS = 2
SSD_STATE = 128
SSD_CONV_K = 4
SSD_CHUNK = 128
SSD_XBC = SSD_WIDTH + 2 * SSD_GROUPS * SSD_STATE
NORM_EPS = 1e-6
LN_EPS = 1e-5

LANES = 128
SUBLANES = 8
QK_PAD = 2 * LANES
Q_UP = 3 * LANES
KD_WIDTH = LANES
DT_LANE0 = QK_ROPE
CONV_HALO = 32
SSD_HALO = SUBLANES
VMEM_LIMIT = 56 * 1024 * 1024
ATTN_STRIP = 64

SEG_WIDTHS = (Q_LORA, KV_LORA, KD_WIDTH, MLA_WIDTH, 2 * CONV_WIDTH, CONV_WIDTH, SSD_WIDTH, SSD_XBC)


def _sigmoid(x):
    return 1.0 / (1.0 + jnp.exp2(x * (-math.log2(math.e))))


def _silu(x):
    return x * _sigmoid(x)


def _softplus(x):
    return jnp.maximum(x, 0.0) + jnp.log1p(jnp.exp(-jnp.abs(x)))


def _rms(x, g, eps=NORM_EPS):
    ms = jnp.mean(x * x, axis=-1, keepdims=True)
    return x * lax.rsqrt(ms + eps) * g


def _params(*sem):
    return pltpu.CompilerParams(dimension_semantics=sem, vmem_limit_bytes=VMEM_LIMIT)


def _resident(stacked, layer):
    _, a, b = stacked.shape
    return pl.BlockSpec((None, a, b), lambda *_: (layer, 0, 0), pipeline_mode=pl.Buffered(1))


W_IN_HALF = LANES // 2


def _prep_w_in_kernel(a_ref, b_ref, o_ref, *, depth, n_kt, n_dt):
    j = pl.program_id(0)
    kd_tile = (Q_LORA + KV_LORA) // LANES
    stride = depth * n_kt
    row = lax.broadcasted_iota(jnp.int32, (W_IN_HALF, LANES), 0)
    keep_b = (j != kd_tile) | (row < n_dt)
    for l in range(depth):
        for kt in range(n_kt):
            xa = a_ref[pl.ds(kt * depth + l, W_IN_HALF, stride=stride), :]
            xb = b_ref[pl.ds(kt * depth + l, W_IN_HALF, stride=stride), :]
            x = jnp.concatenate([xa, jnp.where(keep_b, xb, 0.0)], axis=0)
            o_ref[l, kt * LANES:(kt + 1) * LANES, :] = x.T.astype(o_ref.dtype)


def _prep_w_in(w_in):
    depth, d, n = w_in.shape
    n_kt = d // LANES
    n_out = sum(SEG_WIDTHS)
    rows_per_group = W_IN_HALF * n_kt * depth
    view = w_in.transpose(2, 0, 1).reshape(n, depth, n_kt, LANES).transpose(0, 2, 1, 3).reshape(-1, LANES)
    kd_tile = (Q_LORA + KV_LORA) // LANES
    dt_group = (n - SSD_HEADS) // W_IN_HALF
    assert (n - SSD_HEADS) % W_IN_HALF == 0 and (Q_LORA + KV_LORA) % LANES == 0 and QK_ROPE == W_IN_HALF
    a_idx = lambda j: (jnp.where(j <= kd_tile, 2 * j, 2 * j - 1), 0)
    b_idx = lambda j: (jnp.where(j < kd_tile, 2 * j + 1, jnp.where(j == kd_tile, dt_group, 2 * j)), 0)
    return pl.pallas_call(
        functools.partial(_prep_w_in_kernel, depth=depth, n_kt=n_kt, n_dt=SSD_HEADS),
        grid=(n_out // LANES,),
        in_specs=[pl.BlockSpec((rows_per_group, LANES), a_idx),
                  pl.BlockSpec((rows_per_group, LANES), b_idx)],
        out_specs=pl.BlockSpec((depth, d, LANES), lambda j: (0, 0, j)),
        out_shape=jax.ShapeDtypeStruct((depth, d, n_out), BF16),
        compiler_params=_params("parallel"),
        name="prep_w_in",
    )(view, view)


def _prep_small_kernel(wq_ref, wkv_ref, wpw_ref, oq_ref, okv_ref, opw_ref):
    wq = wq_ref[...]
    zq = jnp.zeros((wq.shape[0], QK_PAD - QK_HEAD), F32)
    half = QK_ROPE // 2
    parts = []
    for h in range(MLA_HEADS):
        r0 = h * QK_HEAD + QK_NOPE
        parts += [wq[:, h * QK_HEAD:(h + 1) * QK_HEAD], zq,
                  wq[:, r0 + half:r0 + QK_ROPE], wq[:, r0:r0 + half], zq]
    oq_ref[...] = jnp.concatenate(parts, axis=1).astype(oq_ref.dtype)
    wkv = wkv_ref[...]
    per = QK_NOPE + V_HEAD
    k_cols = [wkv[:, h * per:h * per + QK_NOPE] for h in range(MLA_HEADS)]
    v_cols = [wkv[:, h * per + QK_NOPE:(h + 1) * per] for h in range(MLA_HEADS)]
    okv_ref[...] = jnp.concatenate(k_cols + v_cols, axis=1).astype(okv_ref.dtype)
    opw_ref[...] = wpw_ref[...].astype(opw_ref.dtype)


def _prep_small(w_q_b, w_kv_b, conv_pw_w):
    depth = w_q_b.shape[0]
    whole = lambda a: pl.BlockSpec((None,) + a.shape[1:], lambda l: (l, 0, 0))
    shapes = [(depth, Q_LORA, MLA_HEADS * Q_UP), w_kv_b.shape, conv_pw_w.shape]
    return pl.pallas_call(
        _prep_small_kernel,
        grid=(depth,),
        in_specs=[whole(w_q_b), whole(w_kv_b), whole(conv_pw_w)],
        out_specs=[pl.BlockSpec((None,) + s[1:], lambda l: (l, 0, 0)) for s in shapes],
        out_shape=[jax.ShapeDtypeStruct(s, BF16) for s in shapes],
        compiler_params=_params("parallel"),
        name="prep_small",
    )(w_q_b, w_kv_b, conv_pw_w)


OUT_PIECE = 2 * LANES
CONV_SUB = 32
ANCHOR = 16


def _cast_kernel(w_ref, o_ref):
    for p in range(o_ref.shape[0]):
        o_ref[p] = w_ref[:, p * OUT_PIECE:(p + 1) * OUT_PIECE].astype(o_ref.dtype)


def _prep_cast(w, tk):
    depth, k, n = w.shape
    n_piece = n // OUT_PIECE
    return pl.pallas_call(
        _cast_kernel,
        grid=(depth, k // tk),
        in_specs=[pl.BlockSpec((None, tk, n), lambda l, i: (l, i, 0))],
        out_specs=pl.BlockSpec((None, n_piece, tk, OUT_PIECE), lambda l, i: (l, 0, i, 0)),
        out_shape=jax.ShapeDtypeStruct((depth, n_piece, k, OUT_PIECE), BF16),
        compiler_params=_params("parallel", "parallel"),
        name="prep_cast",
    )(w)


def _in_proj_kernel(x_ref, g_ref, w_ref, *out_refs):
    h = _rms(x_ref[...], g_ref[...]).astype(BF16)
    off = 0
    for o_ref in out_refs:
        n = o_ref.shape[-1]
        o_ref[...] = jnp.dot(h, w_ref[:, off:off + n], preferred_element_type=F32).astype(o_ref.dtype)
        off += n


def _in_proj(layer, x2, g, w, tm):
    m, d = x2.shape
    assert w.shape[2] == sum(SEG_WIDTHS)
    return pl.pallas_call(
        _in_proj_kernel,
        grid=(m // tm,),
        in_specs=[pl.BlockSpec((tm, d), lambda i: (i, 0)),
                  _resident(g, layer),
                  _resident(w, layer)],
        out_specs=[pl.BlockSpec((tm, n), lambda i: (i, 0)) for n in SEG_WIDTHS],
        out_shape=[jax.ShapeDtypeStruct((m, n), F32) for n in SEG_WIDTHS],
        compiler_params=_params("parallel"),
        name="in_proj",
    )(x2, g, w)


def _mla_prep_kernel(cq_ref, ckv_ref, kd_ref, qan_ref, wq_ref, kvan_ref, wkv_ref, qn_ref, kn_ref,
                     cos_ref, sa_ref, sb_ref, q_ref, k_ref, v_ref):
    scale = math.log2(math.e) / math.sqrt(QK_HEAD)
    hq = _rms(cq_ref[...], qan_ref[...]).astype(BF16)
    qf = jnp.dot(hq, wq_ref[...], preferred_element_type=F32)
    hkv = _rms(ckv_ref[...], kvan_ref[...]).astype(BF16)
    kvf = jnp.dot(hkv, wkv_ref[...], preferred_element_type=F32)

    cos = cos_ref[...]
    sa = sa_ref[...]
    sb = sb_ref[...]

    def rope(r):
        return r * cos + pltpu.roll(r, LANES - QK_ROPE // 2, 1) * sa + pltpu.roll(r, QK_ROPE // 2, 1) * sb

    kd = kd_ref[...]
    lane = lax.broadcasted_iota(jnp.int32, kd.shape, 1)
    kpe = jnp.where(lane < QK_ROPE, kd, 0.0)
    kpe_ss = jnp.sum(kpe * kpe, axis=-1, keepdims=True)

    qn_a = qn_ref[:, 0:LANES] * scale
    qn_b = qn_ref[:, LANES:QK_PAD] * scale
    qn_s = qn_ref[:, QK_PAD:Q_UP] * scale
    sin_signed = sa + sb
    kn_w = kn_ref[...]
    kpe_rot = rope(kpe * kn_w[:, LANES:QK_PAD])
    for h in range(MLA_HEADS):
        qa = qf[:, h * Q_UP:h * Q_UP + LANES]
        qb = qf[:, h * Q_UP + LANES:h * Q_UP + QK_PAD]
        qs = qf[:, h * Q_UP + QK_PAD:(h + 1) * Q_UP]
        ss = jnp.sum(qa * qa + qb * qb, axis=-1, keepdims=True)
        inv = lax.rsqrt(ss * (1.0 / QK_HEAD) + NORM_EPS)
        q_ref[0, h, :, 0:LANES] = (qa * inv * qn_a).astype(q_ref.dtype)
        q_ref[0, h, :, LANES:QK_PAD] = (inv * (qb * qn_b * cos + qs * qn_s * sin_signed)).astype(q_ref.dtype)

        ka = kvf[:, h * LANES:(h + 1) * LANES]
        ss = jnp.sum(ka * ka, axis=-1, keepdims=True) + kpe_ss
        inv = lax.rsqrt(ss * (1.0 / QK_HEAD) + NORM_EPS)
        k_ref[0, h, :, 0:LANES] = (ka * inv * kn_w[:, 0:LANES]).astype(k_ref.dtype)
        k_ref[0, h, :, LANES:QK_PAD] = (kpe_rot * inv).astype(k_ref.dtype)

        v_ref[0, h, :, :] = kvf[:, MLA_WIDTH + h * V_HEAD:MLA_WIDTH + (h + 1) * V_HEAD].astype(v_ref.dtype)


def _mla_prep(layer, cq, ckv, kd, qan, wq, kvan, wkv, qn, kn, cos_t, sa_t, sb_t, bsz, seq, tm):
    nt = seq // tm
    row = lambda b, i: (b * nt + i, 0)
    pos = lambda b, i: (i, 0)
    hd = MLA_HEADS
    return pl.pallas_call(
        _mla_prep_kernel,
        grid=(bsz, nt),
        in_specs=[pl.BlockSpec((tm, Q_LORA), row),
                  pl.BlockSpec((tm, KV_LORA), row),
                  pl.BlockSpec((tm, KD_WIDTH), row),
                  _resident(qan, layer), _resident(wq, layer), _resident(kvan, layer), _resident(wkv, layer),
                  _resident(qn, layer), _resident(kn, layer),
                  pl.BlockSpec((tm, LANES), pos), pl.BlockSpec((tm, LANES), pos), pl.BlockSpec((tm, LANES), pos)],
        out_specs=[pl.BlockSpec((1, hd, tm, QK_PAD), lambda b, i: (b, 0, i, 0)),
                   pl.BlockSpec((1, hd, tm, QK_PAD), lambda b, i: (b, 0, i, 0)),
                   pl.BlockSpec((1, hd, tm, V_HEAD), lambda b, i: (b, 0, i, 0))],
        out_shape=[jax.ShapeDtypeStruct((bsz, hd, seq, QK_PAD), BF16),
                   jax.ShapeDtypeStruct((bsz, hd, seq, QK_PAD), BF16),
                   jax.ShapeDtypeStruct((bsz, hd, seq, V_HEAD), BF16)],
        compiler_params=_params("parallel", "parallel"),
        name="mla_prep",
    )(cq, ckv, kd, qan, wq, kvan, wkv, qn, kn, cos_t, sa_t, sb_t)


def _attn_kernel(qi_ref, ki_ref, q_ref, k_ref, v_ref, g_ref, o_ref, m_sc, l_sc, acc_sc, s_sc, p_sc, a_sc):
    qi = qi_ref[pl.program_id(1)]
    ki = ki_ref[pl.program_id(1)]
    tq = q_ref.shape[2]
    tk = k_ref.shape[2]

    @pl.when(ki == 0)
    def _():
        m_sc[...] = jnp.full(m_sc.shape, -jnp.inf, F32)
        l_sc[...] = jnp.zeros(l_sc.shape, F32)
        acc_sc[...] = jnp.zeros(acc_sc.shape, F32)

    strip = min(ATTN_STRIP, tq)
    n_chunk = tk // LANES

    def scores(h):
        s_sc[h % 2] = lax.dot_general(q_ref[0, h], k_ref[0, h], (((1,), (1,)), ((), ())),
                                      preferred_element_type=F32)

    def softmax_strip(h, r0, masked):
        slot = h % 2
        rows = slice(r0, r0 + strip)
        live = [c for c in range(n_chunk) if not (masked and c * LANES >= r0 + strip)]
        chunks = []
        for c in live:
            x = s_sc[slot, rows, c * LANES:(c + 1) * LANES]
            if masked and (c + 1) * LANES - 1 > r0:
                rr = r0 + lax.broadcasted_iota(jnp.int32, (strip, LANES), 0)
                cc = c * LANES + lax.broadcasted_iota(jnp.int32, (strip, LANES), 1)
                x = jnp.where(rr >= cc, x, -jnp.inf)
            chunks.append(x)
        m_prev = m_sc[h, rows, :]
        m_loc = functools.reduce(jnp.maximum, chunks)
        m_new = jnp.maximum(m_prev, jnp.max(m_loc, axis=-1, keepdims=True))
        alpha = jnp.exp2(m_prev - m_new)
        ps = [jnp.exp2(x - m_new) for x in chunks]
        l_sc[h, rows, :] = alpha * l_sc[h, rows, :] + functools.reduce(jnp.add, ps)
        m_sc[h, rows, :] = m_new
        a_sc[slot, rows, :] = alpha
        dead = [jnp.zeros((strip, LANES), BF16)] * (n_chunk - len(live))
        p_sc[slot, rows, :] = jnp.concatenate([p.astype(BF16) for p in ps] + dead, axis=1)

    def step(masked):
        scores(0)
        for h in range(MLA_HEADS):
            if h + 1 < MLA_HEADS:
                scores(h + 1)
            for r0 in range(0, tq, strip):
                softmax_strip(h, r0, masked)
            pv = jnp.dot(p_sc[h % 2], v_ref[0, h], preferred_element_type=F32)
            acc_sc[h] = a_sc[h % 2] * acc_sc[h] + pv

    @pl.when(ki < qi)
    def _():
        step(False)

    @pl.when(ki == qi)
    def _():
        step(True)
        for h in range(MLA_HEADS):
            o = acc_sc[h] / jnp.sum(l_sc[h], axis=-1, keepdims=True)
            g = g_ref[:, h * V_HEAD:(h + 1) * V_HEAD]
            o_ref[:, h * V_HEAD:(h + 1) * V_HEAD] = (o * _silu(g)).astype(o_ref.dtype)


def _attention(q, k, v, gmla, bsz, seq, tq):
    hd = MLA_HEADS
    nq = seq // tq
    pairs = [(i, j) for i in range(nq) for j in range(i + 1)]
    qi_tab = jnp.asarray([i for i, _ in pairs], jnp.int32)
    ki_tab = jnp.asarray([j for _, j in pairs], jnp.int32)
    q_idx = lambda b, t, qi, ki: (b, 0, qi[t], 0)
    kv_idx = lambda b, t, qi, ki: (b, 0, ki[t], 0)
    row = lambda b, t, qi, ki: (b * nq + qi[t], 0)
    grid_spec = pltpu.PrefetchScalarGridSpec(
        num_scalar_prefetch=2,
        grid=(bsz, len(pairs)),
        in_specs=[pl.BlockSpec((1, hd, tq, QK_PAD), q_idx),
                  pl.BlockSpec((1, hd, tq, QK_PAD), kv_idx),
                  pl.BlockSpec((1, hd, tq, V_HEAD), kv_idx),
                  pl.BlockSpec((tq, MLA_WIDTH), row)],
        out_specs=pl.BlockSpec((tq, MLA_WIDTH), row),
        scratch_shapes=[pltpu.VMEM((hd, tq, LANES), F32), pltpu.VMEM((hd, tq, LANES), F32),
                        pltpu.VMEM((hd, tq, V_HEAD), F32),
                        pltpu.VMEM((2, tq, tq), F32), pltpu.VMEM((2, tq, tq), BF16),
                        pltpu.VMEM((2, tq, LANES), F32)])
    return pl.pallas_call(
        _attn_kernel,
        grid_spec=grid_spec,
        out_shape=jax.ShapeDtypeStruct((bsz * seq, MLA_WIDTH), BF16),
        compiler_params=_params("parallel", "arbitrary"),
        name="attention",
    )(qi_tab, ki_tab, q, k, v, gmla)


def _split3(x):
    hi = x.astype(BF16)
    r1 = x - hi.astype(F32)
    mid = r1.astype(BF16)
    lo = (r1 - mid.astype(F32)).astype(BF16)
    return hi, mid, lo


def _ssd_kernel(z_ref, xbc_ref, kd_ref, cw_ref, cb_ref, dtb_ref, alog_ref, dsk_ref, ng_ref, o_ref,
                xbuf, xact, state, shx, acol_sc, xdtb_sc, xdte_sc, grow_sc, y_sc):
    i = pl.program_id(0)
    nb, ts = o_ref.shape[0], o_ref.shape[1]
    L = SSD_CHUNK
    hp = SSD_HEAD_DIM
    gw = SSD_WIDTH // SSD_GROUPS
    pairs_per_group = SSD_HEADS // SSD_GROUPS // 2
    npair = SSD_HEADS // 2

    @pl.when(i == 0)
    def _():
        xbuf[:, 0:SSD_HALO, :] = jnp.zeros((nb, SSD_HALO, SSD_XBC), F32)
        state[...] = jnp.zeros(state.shape, F32)

    @pl.when(i > 0)
    def _():
        xbuf[:, 0:SSD_HALO, :] = xbuf[:, ts:ts + SSD_HALO, :]

    for b in range(nb):
        xbuf[b, SSD_HALO:SSD_HALO + ts, :] = xbc_ref[b]
        for s in range(1, SSD_CONV_K):
            shx[s - 1, :, :] = xbuf[b, SSD_HALO - s:SSD_HALO - s + ts, :]
        acc = cb_ref[...] + cw_ref[SSD_CONV_K - 1:SSD_CONV_K, :] * xbc_ref[b]
        for s in range(1, SSD_CONV_K):
            acc = acc + cw_ref[SSD_CONV_K - 1 - s:SSD_CONV_K - s, :] * shx[s - 1]
        xact[b] = _silu(acc)

    lane = lax.broadcasted_iota(jnp.int32, (1, LANES), 1)
    is_dt = (lane >= DT_LANE0) & (lane < DT_LANE0 + SSD_HEADS)
    a_neg = jnp.where(is_dt, -jnp.exp(alog_ref[...]), 0.0)
    rr = lax.broadcasted_iota(jnp.int32, (L, L), 0)
    cc = lax.broadcasted_iota(jnp.int32, (L, L), 1)
    causal = rr >= cc
    tri = causal.astype(BF16)
    left = lax.broadcasted_iota(jnp.int32, (L, LANES), 1) < hp

    def decay_geometry(b, r0):
        dt = _softplus(kd_ref[b, pl.ds(r0, L), :] + dtb_ref[...])
        hi, mid, lo = _split3(dt * a_neg)
        acs = (jnp.dot(tri, hi, preferred_element_type=F32)
               + jnp.dot(tri, mid, preferred_element_type=F32)
               + jnp.dot(tri, lo, preferred_element_type=F32))
        return dict(dt=dt, acs=acs, acs_t=acs.T)

    def expand(b, r0, v):
        def col(a, h):
            return jnp.broadcast_to(a[:, DT_LANE0 + h:DT_LANE0 + h + 1], (L, LANES))

        lasts = []
        for j in range(npair):
            lanes = slice(j * LANES, (j + 1) * LANES)
            a0, a1 = col(v["acs"], 2 * j), col(v["acs"], 2 * j + 1)
            acol_sc[b, 2 * j] = a0
            acol_sc[b, 2 * j + 1] = a1
            acs_pair = jnp.where(left, a0, a1)
            dt_pair = jnp.where(left, col(v["dt"], 2 * j), col(v["dt"], 2 * j + 1))
            last = acs_pair[L - 1:L, :]
            xdt = xact[b, pl.ds(r0, L), lanes] * dt_pair
            xdtb_sc[b, :, lanes] = xdt.astype(BF16)
            xdte_sc[b, :, lanes] = (xdt * jnp.exp(last - acs_pair)).astype(BF16)
            grow_sc[b, :, lanes] = jnp.exp(acs_pair)
            lasts.append(last)
        v["lasts"] = lasts

    def mix(b, r0, v):
        ssq = jnp.zeros((L, LANES), F32)
        for g in range(SSD_GROUPS):
            c0 = SSD_WIDTH + g * SSD_STATE
            c1 = SSD_WIDTH + (SSD_GROUPS + g) * SSD_STATE
            grp = slice(g * gw, (g + 1) * gw)
            bg = xact[b, pl.ds(r0, L), c0:c0 + SSD_STATE]
            cg_b = xact[b, pl.ds(r0, L), c1:c1 + SSD_STATE].astype(BF16)
            cb = lax.dot_general(cg_b, bg.astype(BF16), (((1,), (1,)), ((), ())), preferred_element_type=F32)
            st_prev = state[b, :, grp]
            st_new = jnp.dot(bg.T.astype(BF16), xdte_sc[b, :, grp], preferred_element_type=F32)
            y_off = jnp.dot(cg_b, st_prev.astype(BF16), preferred_element_type=F32) * grow_sc[b, :, grp]
            last_g = jnp.concatenate(v["lasts"][g * pairs_per_group:(g + 1) * pairs_per_group], axis=1)
            state[b, :, grp] = st_prev * jnp.exp(last_g) + st_new
            for jj in range(pairs_per_group):
                j = g * pairs_per_group + jj
                lanes = slice(j * LANES, (j + 1) * LANES)
                pair = xdtb_sc[b, :, lanes]
                yd = []
                for h in (2 * j, 2 * j + 1):
                    seg = acol_sc[b, h] - v["acs_t"][DT_LANE0 + h:DT_LANE0 + h + 1, :]
                    decay = jnp.exp(jnp.where(causal, seg, -jnp.inf))
                    yd.append(jnp.dot((cb * decay).astype(BF16), pair, preferred_element_type=F32))
                y = (jnp.where(left, yd[0], yd[1]) + y_off[:, jj * LANES:(jj + 1) * LANES]
                     + dsk_ref[:, lanes] * xact[b, pl.ds(r0, L), lanes])
                yz = y * _silu(z_ref[b, pl.ds(r0, L), lanes])
                y_sc[b, :, lanes] = yz
                ssq = ssq + yz * yz
        v["inv"] = lax.rsqrt(jnp.sum(ssq, axis=-1, keepdims=True) * (1.0 / SSD_WIDTH) + NORM_EPS)

    def finish(b, r0, v):
        o_ref[b, pl.ds(r0, L), :] = (y_sc[b] * v["inv"] * ng_ref[...]).astype(o_ref.dtype)

    def chunk(c, carry):
        r0 = pl.multiple_of(c * L, L)
        vals = [decay_geometry(b, r0) for b in range(nb)]
        for stage in (expand, mix, finish):
            for b in range(nb):
                stage(b, r0, vals[b])
        return carry

    lax.fori_loop(0, ts // L, chunk, 0)


def _ssd(layer, z, xbc, kd, cw, cb, dtb, alog, dsk, ng, bsz, seq, ts):
    blk = lambda w: pl.BlockSpec((bsz, ts, w), lambda i: (0, i, 0))
    out = pl.pallas_call(
        _ssd_kernel,
        grid=(seq // ts,),
        in_specs=[blk(SSD_WIDTH), blk(SSD_XBC), blk(KD_WIDTH),
                  _resident(cw, layer), _resident(cb, layer), _resident(dtb, layer), _resident(alog, layer),
                  _resident(dsk, layer), _resident(ng, layer)],
        out_specs=blk(SSD_WIDTH),
        out_shape=jax.ShapeDtypeStruct((bsz, seq, SSD_WIDTH), BF16),
        scratch_shapes=[pltpu.VMEM((bsz, ts + SSD_HALO, SSD_XBC), F32), pltpu.VMEM((bsz, ts, SSD_XBC), F32),
                        pltpu.VMEM((bsz, SSD_STATE, SSD_WIDTH), F32),
                        pltpu.VMEM((SSD_CONV_K - 1, ts, SSD_XBC), F32),
                        pltpu.VMEM((bsz, SSD_HEADS, SSD_CHUNK, LANES), F32),
                        pltpu.VMEM((bsz, SSD_CHUNK, SSD_WIDTH), BF16), pltpu.VMEM((bsz, SSD_CHUNK, SSD_WIDTH), BF16),
                        pltpu.VMEM((bsz, SSD_CHUNK, SSD_WIDTH), F32), pltpu.VMEM((bsz, SSD_CHUNK, SSD_WIDTH), F32)],
        compiler_params=_params("arbitrary"),
        name="ssd",
    )(z.reshape(bsz, seq, -1), xbc.reshape(bsz, seq, -1), kd.reshape(bsz, seq, -1), cw, cb, dtb, alog, dsk, ng)
    return out.reshape(bsz * seq, SSD_WIDTH)


def _conv_out_kernel(u_ref, gc_ref, x_ref, a_ref, s_ref, wdw_ref, bdw_ref, lng_ref, lnb_ref, wpw_ref, w_ref,
                     o_ref, hbuf, shbuf, ybuf, oc_buf, *, tiles_per_seq):
    i = pl.program_id(0)
    tt = u_ref.shape[0]
    cw = CONV_WIDTH
    n_piece = w_ref.shape[0]
    seq_start = lax.rem(i, tiles_per_seq) == 0

    @pl.when(i == 0)
    def _():
        oc_buf[...] = jnp.zeros(oc_buf.shape, oc_buf.dtype)

    @pl.when(seq_start)
    def _():
        hbuf[0:CONV_HALO, :] = jnp.zeros((CONV_HALO, cw), F32)

    @pl.when(jnp.logical_not(seq_start))
    def _():
        hbuf[0:CONV_HALO, :] = hbuf[tt:tt + CONV_HALO, :]

    n0 = a_ref.shape[1]
    n1 = n0 + cw
    n2 = n1 + s_ref.shape[1]
    base = CONV_HALO - (CONV_K - 1)
    span = tt + CONV_HALO - SUBLANES
    dyn0 = pl.multiple_of(jnp.minimum(i, 0), ANCHOR)

    def fold(v):
        rows = functools.reduce(jnp.add, [v[r:r + ANCHOR, :] for r in range(0, v.shape[0], ANCHOR)])
        return functools.reduce(jnp.add, [rows[:, c:c + LANES] for c in range(0, v.shape[1], LANES)])

    def piece(p):
        cols = slice(p * OUT_PIECE, (p + 1) * OUT_PIECE)
        y = jnp.dot(oc_buf[pl.ds(dyn0, tt), :], w_ref[p, n0:n1, :], preferred_element_type=F32)
        y = y + jnp.dot(a_ref[...], w_ref[p, 0:n0, :], preferred_element_type=F32)
        y = y + jnp.dot(s_ref[...], w_ref[p, n1:n2, :], preferred_element_type=F32)
        o_ref[:, cols] = x_ref[:, cols] + y

    def glu():
        hval = u_ref[:, 0:cw] * _sigmoid(u_ref[:, cw:2 * cw])
        hbuf[CONV_HALO:CONV_HALO + tt, :] = hval
        return fold(hval)

    def shifts(residues):
        for res in residues:
            for r0 in range(0, span, CONV_SUB):
                rows = min(CONV_SUB, span - r0)
                shbuf[res - 1, r0:r0 + rows, :] = hbuf[res + r0:res + r0 + rows, :]

    def strips(first, count):
        mark = None
        for r0 in range(first, first + count * CONV_SUB, CONV_SUB):
            acc = jnp.broadcast_to(bdw_ref[...], (CONV_SUB, cw))
            for kk in range(CONV_K):
                res = (base + kk) % SUBLANES
                start = r0 + (base + kk - res)
                tap = hbuf[start:start + CONV_SUB, :] if res == 0 else shbuf[res - 1, start:start + CONV_SUB, :]
                acc = acc + wdw_ref[kk:kk + 1, :] * tap
            ybuf[r0:r0 + CONV_SUB, :] = acc
            part = fold(acc)
            mark = part if mark is None else mark + part
        return mark

    def anchor(mark):
        oc_buf[pl.ds(tt + dyn0, ANCHOR), 0:LANES] = mark.astype(oc_buf.dtype)

    n_free = 3
    n_groups = n_piece - n_free - 1
    per = tt // CONV_SUB // n_groups
    piece(0)
    anchor(glu())
    piece(1)
    shifts(range(1, SUBLANES))
    for p in range(2, n_free + 1):
        piece(p)
    for k in range(n_groups):
        anchor(strips(k * per * CONV_SUB, per))
        piece(n_free + 1 + k)

    y = ybuf[...]
    mu = jnp.mean(y, axis=-1, keepdims=True)
    yc = y - mu
    var = jnp.mean(yc * yc, axis=-1, keepdims=True)
    hn = _silu(yc * lax.rsqrt(var + LN_EPS) * lng_ref[...] + lnb_ref[...])
    out = jnp.dot(hn.astype(BF16), wpw_ref[...], preferred_element_type=F32)
    oc_buf[0:tt, :] = (out * _silu(gc_ref[...])).astype(oc_buf.dtype)


def _conv_out(layer, uconv, gconv, x2, o_mla, o_ssd, wdw, bdw, lng, lnb, wpw, w_out, seq, tt):
    m, d = x2.shape
    n = m // tt
    n_piece = w_out.shape[1]
    cur_row = lambda i: (jnp.minimum(i, n - 1), 0)
    prev_row = lambda i: (jnp.maximum(i - 1, 0), 0)
    w_spec = pl.BlockSpec((None,) + w_out.shape[1:], lambda i: (layer, 0, 0, 0), pipeline_mode=pl.Buffered(1))
    return pl.pallas_call(
        functools.partial(_conv_out_kernel, tiles_per_seq=seq // tt),
        grid=(n + 1,),
        in_specs=[pl.BlockSpec((tt, 2 * CONV_WIDTH), cur_row),
                  pl.BlockSpec((tt, CONV_WIDTH), cur_row),
                  pl.BlockSpec((tt, d), prev_row),
                  pl.BlockSpec((tt, o_mla.shape[1]), prev_row),
                  pl.BlockSpec((tt, o_ssd.shape[1]), prev_row),
                  _resident(wdw, layer), _resident(bdw, layer), _resident(lng, layer), _resident(lnb, layer),
                  _resident(wpw, layer), w_spec],
        out_specs=pl.BlockSpec((tt, d), prev_row),
        out_shape=jax.ShapeDtypeStruct((m, d), F32),
        scratch_shapes=[pltpu.VMEM((tt + CONV_HALO, CONV_WIDTH), F32),
                        pltpu.VMEM((SUBLANES - 1, tt + CONV_HALO, CONV_WIDTH), F32),
                        pltpu.VMEM((tt, CONV_WIDTH), F32),
                        pltpu.VMEM((tt + ANCHOR, CONV_WIDTH), BF16)],
        compiler_params=_params("arbitrary"),
        name="conv_out_proj",
    )(uconv, gconv, x2, o_mla, o_ssd, wdw, bdw, lng, lnb, wpw, w_out)


def _rows(p):
    return p.astype(F32)[:, None, :]


def _pad_rows(p, lane0, width):
    return jnp.pad(p.astype(F32), ((0, 0), (lane0, width - lane0 - p.shape[1])))[:, None, :]


ROW_TILE = 512


def _tile_sizes(m, seq):
    tm_in, t_seq = min(ROW_TILE, m), min(ROW_TILE, seq)
    t_prep = min(2 * ROW_TILE, seq)
    assert m % tm_in == 0 and seq % t_seq == 0 and t_seq % SSD_CHUNK == 0 and t_seq % (4 * CONV_SUB) == 0
    assert seq % t_prep == 0
    return tm_in, t_seq, t_prep


def _rope_tables(seq):
    half = QK_ROPE // 2
    inv_freq = ROPE_THETA ** (-jnp.arange(half, dtype=F32) / half)
    ang = jnp.arange(seq).astype(F32)[:, None] * inv_freq[None, :]
    cos, sin = jnp.cos(ang), jnp.sin(ang)
    z = lambda n: jnp.zeros((seq, n), F32)
    cos_t = jnp.concatenate([cos, cos, z(LANES - QK_ROPE)], axis=1)
    sa_t = jnp.concatenate([-sin, z(LANES - half)], axis=1)
    sb_t = jnp.concatenate([z(half), sin, z(LANES - QK_ROPE)], axis=1)
    return cos_t, sa_t, sb_t


def kernel(x, norm_g, w_in, q_a_norm, w_q_b, kv_a_norm, w_kv_b, q_norm, k_norm, conv_dw_w, conv_dw_b, conv_ln_g,
           conv_ln_b, conv_pw_w, ssd_conv_w, ssd_conv_b, ssd_dt_bias, ssd_A_log, ssd_D, ssd_norm_g, w_out):
    bsz, seq, d = x.shape
    m = bsz * seq
    x2 = x.reshape(m, d)
    tabs = _rope_tables(seq)

    w_in_k = _prep_w_in(w_in)
    wq, wkv, wpw = _prep_small(w_q_b, w_kv_b, conv_pw_w)
    w_out_k = _prep_cast(w_out, min(512, w_out.shape[1]))
    g_in, qan, kvan = _rows(norm_g), _rows(q_a_norm), _rows(kv_a_norm)
    kn = _pad_rows(k_norm, 0, QK_PAD)
    half = QK_ROPE // 2
    q_swapped = jnp.concatenate([q_norm[:, QK_NOPE + half:], q_norm[:, QK_NOPE:QK_NOPE + half]], axis=1)
    qn = jnp.concatenate([_pad_rows(q_norm, 0, QK_PAD), _pad_rows(q_swapped, 0, LANES)], axis=2)
    wdw = jnp.pad(conv_dw_w.astype(F32), ((0, 0), (0, CONV_HALO - CONV_K), (0, 0)))
    bdw, lng, lnb = _rows(conv_dw_b), _rows(conv_ln_g), _rows(conv_ln_b)
    cw = jnp.pad(ssd_conv_w.astype(F32), ((0, 0), (0, SUBLANES - SSD_CONV_K), (0, 0)))
    cb, ng = _rows(ssd_conv_b), _rows(ssd_norm_g)
    dtb, alog = _pad_rows(ssd_dt_bias, DT_LANE0, LANES), _pad_rows(ssd_A_log, DT_LANE0, LANES)
    dsk = _rows(jnp.repeat(ssd_D, SSD_HEAD_DIM, axis=1))

    tm_in, t_seq, t_prep = _tile_sizes(m, seq)
    for layer in range(norm_g.shape[0]):
        cq, ckv, kd, gmla, uconv, gconv, z, xbc = _in_proj(layer, x2, g_in, w_in_k, tm_in)
        q, k, v = _mla_prep(layer, cq, ckv, kd, qan, wq, kvan, wkv, qn, kn, *tabs, bsz, seq, t_prep)
        o_mla = _attention(q, k, v, gmla, bsz, seq, t_seq)
        o_ssd = _ssd(layer, z, xbc, kd, cw, cb, dtb, alog, dsk, ng, bsz, seq, t_seq)
        x2 = _conv_out(layer, uconv, gconv, x2, o_mla, o_ssd, wdw, bdw, lng, lnb, wpw, w_out_k, seq, t_seq)
    return x2.reshape(bsz, seq, d)
```

```python
import functools
import math

import jax
import jax.numpy as jnp
from jax import lax
from jax.experimental import pallas as pl
from jax.experimental.pallas import tpu as pltpu

F32 = jnp.float32
BF16 = jnp.bfloat16

MLA_HEADS = 6
QK_NOPE = 128
QK_ROPE = 64
QK_HEAD = QK_NOPE + QK_ROPE
V_HEAD = 128
Q_LORA = 512
KV_LORA = 256
MLA_WIDTH = MLA_HEADS * V_HEAD
ROPE_THETA = 10000.0
CONV_WIDTH = 512
CONV_K = 31
SSD_HEADS = 12
SSD_HEAD_DIM = 64
SSD_WIDTH = SSD_HEADS * SSD_HEAD_DIM
SSD_GROUPS = 2
SSD_STATE = 128
SSD_CONV_K = 4
SSD_CHUNK = 128
SSD_XBC = SSD_WIDTH + 2 * SSD_GROUPS * SSD_STATE
NORM_EPS = 1e-6
LN_EPS = 1e-5

LANES = 128
SUBLANES = 8
QK_PAD = 2 * LANES
Q_UP = 3 * LANES
KD_WIDTH = LANES
DT_LANE0 = QK_ROPE
CONV_HALO = 32
SSD_HALO = SUBLANES
VMEM_LIMIT = 56 * 1024 * 1024
ATTN_STRIP = 64

SEG_WIDTHS = (Q_LORA, KV_LORA, KD_WIDTH, MLA_WIDTH, 2 * CONV_WIDTH, CONV_WIDTH, SSD_WIDTH, SSD_XBC)


def _sigmoid(x):
    return 1.0 / (1.0 + jnp.exp2(x * (-math.log2(math.e))))


def _silu(x):
    return x * _sigmoid(x)


def _softplus(x):
    return jnp.maximum(x, 0.0) + jnp.log1p(jnp.exp(-jnp.abs(x)))


def _rms(x, g, eps=NORM_EPS):
    ms = jnp.mean(x * x, axis=-1, keepdims=True)
    return x * lax.rsqrt(ms + eps) * g


def _params(*sem):
    return pltpu.CompilerParams(dimension_semantics=sem, vmem_limit_bytes=VMEM_LIMIT)


def _resident(stacked, layer):
    _, a, b = stacked.shape
    return pl.BlockSpec((None, a, b), lambda *_: (layer, 0, 0), pipeline_mode=pl.Buffered(1))


W_IN_HALF = LANES // 2


def _prep_w_in_kernel(a_ref, b_ref, o_ref, *, depth, n_kt, n_dt):
    j = pl.program_id(0)
    kd_tile = (Q_LORA + KV_LORA) // LANES
    stride = depth * n_kt
    row = lax.broadcasted_iota(jnp.int32, (W_IN_HALF, LANES), 0)
    keep_b = (j != kd_tile) | (row < n_dt)
    for l in range(depth):
        for kt in range(n_kt):
            xa = a_ref[pl.ds(kt * depth + l, W_IN_HALF, stride=stride), :]
            xb = b_ref[pl.ds(kt * depth + l, W_IN_HALF, stride=stride), :]
            x = jnp.concatenate([xa, jnp.where(keep_b, xb, 0.0)], axis=0)
            o_ref[l, kt * LANES:(kt + 1) * LANES, :] = x.T.astype(o_ref.dtype)


def _prep_w_in(w_in):
    depth, d, n = w_in.shape
    n_kt = d // LANES
    n_out = sum(SEG_WIDTHS)
    rows_per_group = W_IN_HALF * n_kt * depth
    view = w_in.transpose(2, 0, 1).reshape(n, depth, n_kt, LANES).transpose(0, 2, 1, 3).reshape(-1, LANES)
    kd_tile = (Q_LORA + KV_LORA) // LANES
    dt_group = (n - SSD_HEADS) // W_IN_HALF
    assert (n - SSD_HEADS) % W_IN_HALF == 0 and (Q_LORA + KV_LORA) % LANES == 0 and QK_ROPE == W_IN_HALF
    a_idx = lambda j: (jnp.where(j <= kd_tile, 2 * j, 2 * j - 1), 0)
    b_idx = lambda j: (jnp.where(j < kd_tile, 2 * j + 1, jnp.where(j == kd_tile, dt_group, 2 * j)), 0)
    return pl.pallas_call(
        functools.partial(_prep_w_in_kernel, depth=depth, n_kt=n_kt, n_dt=SSD_HEADS),
        grid=(n_out // LANES,),
        in_specs=[pl.BlockSpec((rows_per_group, LANES), a_idx),
                  pl.BlockSpec((rows_per_group, LANES), b_idx)],
        out_specs=pl.BlockSpec((depth, d, LANES), lambda j: (0, 0, j)),
        out_shape=jax.ShapeDtypeStruct((depth, d, n_out), BF16),
        compiler_params=_params("parallel"),
        name="prep_w_in",
    )(view, view)


def _prep_small_kernel(wq_ref, wkv_ref, wpw_ref, oq_ref, okv_ref, opw_ref):
    wq = wq_ref[...]
    zq = jnp.zeros((wq.shape[0], QK_PAD - QK_HEAD), F32)
    half = QK_ROPE // 2
    parts = []
    for h in range(MLA_HEADS):
        r0 = h * QK_HEAD + QK_NOPE
        parts += [wq[:, h * QK_HEAD:(h + 1) * QK_HEAD], zq,
                  wq[:, r0 + half:r0 + QK_ROPE], wq[:, r0:r0 + half], zq]
    oq_ref[...] = jnp.concatenate(parts, axis=1).astype(oq_ref.dtype)
    wkv = wkv_ref[...]
    per = QK_NOPE + V_HEAD
    k_cols = [wkv[:, h * per:h * per + QK_NOPE] for h in range(MLA_HEADS)]
    v_cols = [wkv[:, h * per + QK_NOPE:(h + 1) * per] for h in range(MLA_HEADS)]
    okv_ref[...] = jnp.concatenate(k_cols + v_cols, axis=1).astype(okv_ref.dtype)
    opw_ref[...] = wpw_ref[...].astype(opw_ref.dtype)


def _prep_small(w_q_b, w_kv_b, conv_pw_w):
    depth = w_q_b.shape[0]
    whole = lambda a: pl.BlockSpec((None,) + a.shape[1:], lambda l: (l, 0, 0))
    shapes = [(depth, Q_LORA, MLA_HEADS * Q_UP), w_kv_b.shape, conv_pw_w.shape]
    return pl.pallas_call(
        _prep_small_kernel,
        grid=(depth,),
        in_specs=[whole(w_q_b), whole(w_kv_b), whole(conv_pw_w)],
        out_specs=[pl.BlockSpec((None,) + s[1:], lambda l: (l, 0, 0)) for s in shapes],
        out_shape=[jax.ShapeDtypeStruct(s, BF16) for s in shapes],
        compiler_params=_params("parallel"),
        name="prep_small",
    )(w_q_b, w_kv_b, conv_pw_w)


OUT_PIECE = 2 * LANES
CONV_SUB = 32
ANCHOR = 16


def _cast_kernel(w_ref, o_ref):
    for p in range(o_ref.shape[0]):
        o_ref[p] = w_ref[:, p * OUT_PIECE:(p + 1) * OUT_PIECE].astype(o_ref.dtype)


def _prep_cast(w, tk):
    depth, k, n = w.shape
    n_piece = n // OUT_PIECE
    return pl.pallas_call(
        _cast_kernel,
        grid=(depth, k // tk),
        in_specs=[pl.BlockSpec((None, tk, n), lambda l, i: (l, i, 0))],
        out_specs=pl.BlockSpec((None, n_piece, tk, OUT_PIECE), lambda l, i: (l, 0, i, 0)),
        out_shape=jax.ShapeDtypeStruct((depth, n_piece, k, OUT_PIECE), BF16),
        compiler_params=_params("parallel", "parallel"),
        name="prep_cast",
    )(w)


def _in_proj_kernel(x_ref, g_ref, w_ref, *out_refs):
    h = _rms(x_ref[...], g_ref[...]).astype(BF16)
    off = 0
    for o_ref in out_refs:
        n = o_ref.shape[-1]
        o_ref[...] = jnp.dot(h, w_ref[:, off:off + n], preferred_element_type=F32).astype(o_ref.dtype)
        off += n


def _in_proj(layer, x2, g, w, tm):
    m, d = x2.shape
    assert w.shape[2] == sum(SEG_WIDTHS)
    return pl.pallas_call(
        _in_proj_kernel,
        grid=(m // tm,),
        in_specs=[pl.BlockSpec((tm, d), lambda i: (i, 0)),
                  _resident(g, layer),
                  _resident(w, layer)],
        out_specs=[pl.BlockSpec((tm, n), lambda i: (i, 0)) for n in SEG_WIDTHS],
        out_shape=[jax.ShapeDtypeStruct((m, n), F32) for n in SEG_WIDTHS],
        compiler_params=_params("parallel"),
        name="in_proj",
    )(x2, g, w)


def _mla_prep_kernel(cq_ref, ckv_ref, kd_ref, qan_ref, wq_ref, kvan_ref, wkv_ref, qn_ref, kn_ref,
                     cos_ref, sa_ref, sb_ref, q_ref, k_ref, v_ref):
    scale = math.log2(math.e) / math.sqrt(QK_HEAD)
    hq = _rms(cq_ref[...], qan_ref[...]).astype(BF16)
    qf = jnp.dot(hq, wq_ref[...], preferred_element_type=F32)
    hkv = _rms(ckv_ref[...], kvan_ref[...]).astype(BF16)
    kvf = jnp.dot(hkv, wkv_ref[...], preferred_element_type=F32)

    cos = cos_ref[...]
    sa = sa_ref[...]
    sb = sb_ref[...]

    def rope(r):
        return r * cos + pltpu.roll(r, LANES - QK_ROPE // 2, 1) * sa + pltpu.roll(r, QK_ROPE // 2, 1) * sb

    kd = kd_ref[...]
    lane = lax.broadcasted_iota(jnp.int32, kd.shape, 1)
    kpe = jnp.where(lane < QK_ROPE, kd, 0.0)
    kpe_ss = jnp.sum(kpe * kpe, axis=-1, keepdims=True)

    qn_a = qn_ref[:, 0:LANES] * scale
    qn_b = qn_ref[:, LANES:QK_PAD] * scale
    qn_s = qn_ref[:, QK_PAD:Q_UP] * scale
    sin_signed = sa + sb
    kn_w = kn_ref[...]
    kpe_rot = rope(kpe * kn_w[:, LANES:QK_PAD])
    for h in range(MLA_HEADS):
        qa = qf[:, h * Q_UP:h * Q_UP + LANES]
        qb = qf[:, h * Q_UP + LANES:h * Q_UP + QK_PAD]
        qs = qf[:, h * Q_UP + QK_PAD:(h + 1) * Q_UP]
        ss = jnp.sum(qa * qa + qb * qb, axis=-1, keepdims=True)
        inv = lax.rsqrt(ss * (1.0 / QK_HEAD) + NORM_EPS)
        q_ref[0, h, :, 0:LANES] = (qa * inv * qn_a).astype(q_ref.dtype)
        q_ref[0, h, :, LANES:QK_PAD] = (inv * (qb * qn_b * cos + qs * qn_s * sin_signed)).astype(q_ref.dtype)

        ka = kvf[:, h * LANES:(h + 1) * LANES]
        ss = jnp.sum(ka * ka, axis=-1, keepdims=True) + kpe_ss
        inv = lax.rsqrt(ss * (1.0 / QK_HEAD) + NORM_EPS)
        k_ref[0, h, :, 0:LANES] = (ka * inv * kn_w[:, 0:LANES]).astype(k_ref.dtype)
        k_ref[0, h, :, LANES:QK_PAD] = (kpe_rot * inv).astype(k_ref.dtype)

        v_ref[0, h, :, :] = kvf[:, MLA_WIDTH + h * V_HEAD:MLA_WIDTH + (h + 1) * V_HEAD].astype(v_ref.dtype)


def _mla_prep(layer, cq, ckv, kd, qan, wq, kvan, wkv, qn, kn, cos_t, sa_t, sb_t, bsz, seq, tm):
    nt = seq // tm
    row = lambda b, i: (b * nt + i, 0)
    pos = lambda b, i: (i, 0)
    hd = MLA_HEADS
    return pl.pallas_call(
        _mla_prep_kernel,
        grid=(bsz, nt),
        in_specs=[pl.BlockSpec((tm, Q_LORA), row),
                  pl.BlockSpec((tm, KV_LORA), row),
                  pl.BlockSpec((tm, KD_WIDTH), row),
                  _resident(qan, layer), _resident(wq, layer), _resident(kvan, layer), _resident(wkv, layer),
                  _resident(qn, layer), _resident(kn, layer),
                  pl.BlockSpec((tm, LANES), pos), pl.BlockSpec((tm, LANES), pos), pl.BlockSpec((tm, LANES), pos)],
        out_specs=[pl.BlockSpec((1, hd, tm, QK_PAD), lambda b, i: (b, 0, i, 0)),
                   pl.BlockSpec((1, hd, tm, QK_PAD), lambda b, i: (b, 0, i, 0)),
                   pl.BlockSpec((1, hd, tm, V_HEAD), lambda b, i: (b, 0, i, 0))],
        out_shape=[jax.ShapeDtypeStruct((bsz, hd, seq, QK_PAD), BF16),
                   jax.ShapeDtypeStruct((bsz, hd, seq, QK_PAD), BF16),
                   jax.ShapeDtypeStruct((bsz, hd, seq, V_HEAD), BF16)],
        compiler_params=_params("parallel", "parallel"),
        name="mla_prep",
    )(cq, ckv, kd, qan, wq, kvan, wkv, qn, kn, cos_t, sa_t, sb_t)


def _attn_kernel(qi_ref, ki_ref, q_ref, k_ref, v_ref, g_ref, o_ref, m_sc, l_sc, acc_sc, s_sc, p_sc, a_sc):
    qi = qi_ref[pl.program_id(1)]
    kg = ki_ref[pl.program_id(1)]
    tq = q_ref.shape[2]
    tk = tq
    kb = k_ref.shape[2] // tk

    @pl.when(kg == 0)
    def _():
        m_sc[...] = jnp.full(m_sc.shape, -jnp.inf, F32)
        l_sc[...] = jnp.zeros(l_sc.shape, F32)
        acc_sc[...] = jnp.zeros(acc_sc.shape, F32)

    strip = min(ATTN_STRIP, tq)
    n_chunk = tk // LANES

    def scores(h, sub):
        s_sc[h % 2] = lax.dot_general(q_ref[0, h], k_ref[0, h, sub * tk:(sub + 1) * tk, :], (((1,), (1,)), ((), ())),
                                      preferred_element_type=F32)

    def softmax_strip(h, r0, masked):
        slot = h % 2
        rows = slice(r0, r0 + strip)
        live = [c for c in range(n_chunk) if not (masked and c * LANES >= r0 + strip)]
        chunks = []
        for c in live:
            x = s_sc[slot, rows, c * LANES:(c + 1) * LANES]
            if masked and (c + 1) * LANES - 1 > r0:
                rr = r0 + lax.broadcasted_iota(jnp.int32, (strip, LANES), 0)
                cc = c * LANES + lax.broadcasted_iota(jnp.int32, (strip, LANES), 1)
                x = jnp.where(rr >= cc, x, -jnp.inf)
            chunks.append(x)
        m_prev = m_sc[h, rows, :]
        m_loc = functools.reduce(jnp.maximum, chunks)
        m_new = jnp.maximum(m_prev, jnp.max(m_loc, axis=-1, keepdims=True))
        alpha = jnp.exp2(m_prev - m_new)
        ps = [jnp.exp2(x - m_new) for x in chunks]
        l_sc[h, rows, :] = alpha * l_sc[h, rows, :] + functools.reduce(jnp.add, ps)
        m_sc[h, rows, :] = m_new
        a_sc[slot, rows, :] = alpha
        dead = [jnp.zeros((strip, LANES), BF16)] * (n_chunk - len(live))
        p_sc[slot, rows, :] = jnp.concatenate([p.astype(BF16) for p in ps] + dead, axis=1)

    def step(masked, sub):
        scores(0, sub)
        for h in range(MLA_HEADS):
            if h + 1 < MLA_HEADS:
                scores(h + 1, sub)
            for r0 in range(0, tq, strip):
                softmax_strip(h, r0, masked)
            pv = jnp.dot(p_sc[h % 2], v_ref[0, h, sub * tk:(sub + 1) * tk, :], preferred_element_type=F32)
            acc_sc[h] = a_sc[h % 2] * acc_sc[h] + pv

    for sub in range(kb):
        ki = kg * kb + sub

        @pl.when(ki < qi)
        def _(sub=sub):
            step(False, sub)

        @pl.when(ki == qi)
        def _(sub=sub):
            step(True, sub)

    @pl.when(kg == qi // kb)
    def _():
        for h in range(MLA_HEADS):
            o = acc_sc[h] / jnp.sum(l_sc[h], axis=-1, keepdims=True)
            g = g_ref[:, h * V_HEAD:(h + 1) * V_HEAD]
            o_ref[:, h * V_HEAD:(h + 1) * V_HEAD] = (o * _silu(g)).astype(o_ref.dtype)


def _attention(q, k, v, gmla, bsz, seq, tq):
    hd = MLA_HEADS
    nq = seq // tq
    kb = 2 if nq % 2 == 0 else 1
    pairs = [(i, j) for i in range(nq) for j in range(i // kb + 1)]
    qi_tab = jnp.asarray([i for i, _ in pairs], jnp.int32)
    ki_tab = jnp.asarray([j for _, j in pairs], jnp.int32)
    q_idx = lambda b, t, qi, ki: (b, 0, qi[t], 0)
    kv_idx = lambda b, t, qi, ki: (b, 0, ki[t], 0)
    row = lambda b, t, qi, ki: (b * nq + qi[t], 0)
    grid_spec = pltpu.PrefetchScalarGridSpec(
        num_scalar_prefetch=2,
        grid=(bsz, len(pairs)),
        in_specs=[pl.BlockSpec((1, hd, tq, QK_PAD), q_idx),
                  pl.BlockSpec((1, hd, kb * tq, QK_PAD), kv_idx),
                  pl.BlockSpec((1, hd, kb * tq, V_HEAD), kv_idx),
                  pl.BlockSpec((tq, MLA_WIDTH), row)],
        out_specs=pl.BlockSpec((tq, MLA_WIDTH), row),
        scratch_shapes=[pltpu.VMEM((hd, tq, LANES), F32), pltpu.VMEM((hd, tq, LANES), F32),
                        pltpu.VMEM((hd, tq, V_HEAD), F32),
                        pltpu.VMEM((2, tq, tq), F32), pltpu.VMEM((2, tq, tq), BF16),
                        pltpu.VMEM((2, tq, LANES), F32)])
    return pl.pallas_call(
        _attn_kernel,
        grid_spec=grid_spec,
        out_shape=jax.ShapeDtypeStruct((bsz * seq, MLA_WIDTH), BF16),
        compiler_params=_params("parallel", "arbitrary"),
        name="attention",
    )(qi_tab, ki_tab, q, k, v, gmla)


def _split3(x):
    hi = x.astype(BF16)
    r1 = x - hi.astype(F32)
    mid = r1.astype(BF16)
    lo = (r1 - mid.astype(F32)).astype(BF16)
    return hi, mid, lo


def _ssd_kernel(z_ref, xbc_ref, kd_ref, cw_ref, cb_ref, dtb_ref, alog_ref, dsk_ref, ng_ref, o_ref,
                xbuf, xact, state, shx, acol_sc, xdtb_sc, xdte_sc, grow_sc, y_sc):
    i = pl.program_id(0)
    nb, ts = o_ref.shape[0], o_ref.shape[1]
    L = SSD_CHUNK
    hp = SSD_HEAD_DIM
    gw = SSD_WIDTH // SSD_GROUPS
    pairs_per_group = SSD_HEADS // SSD_GROUPS // 2
    npair = SSD_HEADS // 2

    @pl.when(i == 0)
    def _():
        xbuf[:, 0:SSD_HALO, :] = jnp.zeros((nb, SSD_HALO, SSD_XBC), F32)
        state[...] = jnp.zeros(state.shape, F32)

    @pl.when(i > 0)
    def _():
        xbuf[:, 0:SSD_HALO, :] = xbuf[:, ts:ts + SSD_HALO, :]

    for b in range(nb):
        xbuf[b, SSD_HALO:SSD_HALO + ts, :] = xbc_ref[b]
        for s in range(1, SSD_CONV_K):
            shx[s - 1, :, :] = xbuf[b, SSD_HALO - s:SSD_HALO - s + ts, :]
        acc = cb_ref[...] + cw_ref[SSD_CONV_K - 1:SSD_CONV_K, :] * xbc_ref[b]
        for s in range(1, SSD_CONV_K):
            acc = acc + cw_ref[SSD_CONV_K - 1 - s:SSD_CONV_K - s, :] * shx[s - 1]
        xact[b] = _silu(acc)

    lane = lax.broadcasted_iota(jnp.int32, (1, LANES), 1)
    is_dt = (lane >= DT_LANE0) & (lane < DT_LANE0 + SSD_HEADS)
    a_neg = jnp.where(is_dt, -jnp.exp(alog_ref[...]), 0.0)
    rr = lax.broadcasted_iota(jnp.int32, (L, L), 0)
    cc = lax.broadcasted_iota(jnp.int32, (L, L), 1)
    causal = rr >= cc
    tri = causal.astype(BF16)
    left = lax.broadcasted_iota(jnp.int32, (L, LANES), 1) < hp

    def decay_geometry(b, r0):
        dt = _softplus(kd_ref[b, pl.ds(r0, L), :] + dtb_ref[...])
        hi, mid, lo = _split3(dt * a_neg)
        acs = (jnp.dot(tri, hi, preferred_element_type=F32)
               + jnp.dot(tri, mid, preferred_element_type=F32)
               + jnp.dot(tri, lo, preferred_element_type=F32))
        return dict(dt=dt, acs=acs, acs_t=acs.T)

    def expand(b, r0, v):
        def col(a, h):
            return jnp.broadcast_to(a[:, DT_LANE0 + h:DT_LANE0 + h + 1], (L, LANES))

        lasts = []
        for j in range(npair):
            lanes = slice(j * LANES, (j + 1) * LANES)
            a0, a1 = col(v["acs"], 2 * j), col(v["acs"], 2 * j + 1)
            acol_sc[b, 2 * j] = a0
            acol_sc[b, 2 * j + 1] = a1
            acs_pair = jnp.where(left, a0, a1)
            dt_pair = jnp.where(left, col(v["dt"], 2 * j), col(v["dt"], 2 * j + 1))
            last = acs_pair[L - 1:L, :]
            xdt = xact[b, pl.ds(r0, L), lanes] * dt_pair
            xdtb_sc[b, :, lanes] = xdt.astype(BF16)
            xdte_sc[b, :, lanes] = (xdt * jnp.exp(last - acs_pair)).astype(BF16)
            grow_sc[b, :, lanes] = jnp.exp(acs_pair)
            lasts.append(last)
        v["lasts"] = lasts

    def mix(b, r0, v):
        ssq = jnp.zeros((L, LANES), F32)
        for g in range(SSD_GROUPS):
            c0 = SSD_WIDTH + g * SSD_STATE
            c1 = SSD_WIDTH + (SSD_GROUPS + g) * SSD_STATE
            grp = slice(g * gw, (g + 1) * gw)
            bg = xact[b, pl.ds(r0, L), c0:c0 + SSD_STATE]
            cg_b = xact[b, pl.ds(r0, L), c1:c1 + SSD_STATE].astype(BF16)
            cb = lax.dot_general(cg_b, bg.astype(BF16), (((1,), (1,)), ((), ())), preferred_element_type=F32)
            st_prev = state[b, :, grp]
            st_new = jnp.dot(bg.T.astype(BF16), xdte_sc[b, :, grp], preferred_element_type=F32)
            y_off = jnp.dot(cg_b, st_prev.astype(BF16), preferred_element_type=F32) * grow_sc[b, :, grp]
            last_g = jnp.concatenate(v["lasts"][g * pairs_per_group:(g + 1) * pairs_per_group], axis=1)
            state[b, :, grp] = st_prev * jnp.exp(last_g) + st_new
            for jj in range(pairs_per_group):
                j = g * pairs_per_group + jj
                lanes = slice(j * LANES, (j + 1) * LANES)
                pair = xdtb_sc[b, :, lanes]
                yd = []
                for h in (2 * j, 2 * j + 1):
                    seg = acol_sc[b, h] - v["acs_t"][DT_LANE0 + h:DT_LANE0 + h + 1, :]
                    decay = jnp.exp(jnp.where(causal, seg, -jnp.inf))
                    yd.append(jnp.dot((cb * decay).astype(BF16), pair, preferred_element_type=F32))
                y = (jnp.where(left, yd[0], yd[1]) + y_off[:, jj * LANES:(jj + 1) * LANES]
                     + dsk_ref[:, lanes] * xact[b, pl.ds(r0, L), lanes])
                yz = y * _silu(z_ref[b, pl.ds(r0, L), lanes])
                y_sc[b, :, lanes] = yz
                ssq = ssq + yz * yz
        v["inv"] = lax.rsqrt(jnp.sum(ssq, axis=-1, keepdims=True) * (1.0 / SSD_WIDTH) + NORM_EPS)

    def finish(b, r0, v):
        o_ref[b, pl.ds(r0, L), :] = (y_sc[b] * v["inv"] * ng_ref[...]).astype(o_ref.dtype)

    def chunk(c, carry):
        r0 = pl.multiple_of(c * L, L)
        vals = [decay_geometry(b, r0) for b in range(nb)]
        for stage in (expand, mix, finish):
            for b in range(nb):
                stage(b, r0, vals[b])
        return carry

    lax.fori_loop(0, ts // L, chunk, 0)


def _ssd(layer, z, xbc, kd, cw, cb, dtb, alog, dsk, ng, bsz, seq, ts):
    blk = lambda w: pl.BlockSpec((bsz, ts, w), lambda i: (0, i, 0))
    out = pl.pallas_call(
        _ssd_kernel,
        grid=(seq // ts,),
        in_specs=[blk(SSD_WIDTH), blk(SSD_XBC), blk(KD_WIDTH),
                  _resident(cw, layer), _resident(cb, layer), _resident(dtb, layer), _resident(alog, layer),
                  _resident(dsk, layer), _resident(ng, layer)],
        out_specs=blk(SSD_WIDTH),
        out_shape=jax.ShapeDtypeStruct((bsz, seq, SSD_WIDTH), BF16),
        scratch_shapes=[pltpu.VMEM((bsz, ts + SSD_HALO, SSD_XBC), F32), pltpu.VMEM((bsz, ts, SSD_XBC), F32),
                        pltpu.VMEM((bsz, SSD_STATE, SSD_WIDTH), F32),
                        pltpu.VMEM((SSD_CONV_K - 1, ts, SSD_XBC), F32),
                        pltpu.VMEM((bsz, SSD_HEADS, SSD_CHUNK, LANES), F32),
                        pltpu.VMEM((bsz, SSD_CHUNK, SSD_WIDTH), BF16), pltpu.VMEM((bsz, SSD_CHUNK, SSD_WIDTH), BF16),
                        pltpu.VMEM((bsz, SSD_CHUNK, SSD_WIDTH), F32), pltpu.VMEM((bsz, SSD_CHUNK, SSD_WIDTH), F32)],
        compiler_params=_params("arbitrary"),
        name="ssd",
    )(z.reshape(bsz, seq, -1), xbc.reshape(bsz, seq, -1), kd.reshape(bsz, seq, -1), cw, cb, dtb, alog, dsk, ng)
    return out.reshape(bsz * seq, SSD_WIDTH)


def _conv_out_kernel(u_ref, gc_ref, x_ref, a_ref, s_ref, wdw_ref, bdw_ref, lng_ref, lnb_ref, wpw_ref, w_ref,
                     o_ref, hbuf, shbuf, ybuf, oc_buf, *, tiles_per_seq):
    i = pl.program_id(0)
    tt = u_ref.shape[0]
    cw = CONV_WIDTH
    n_piece = w_ref.shape[0]
    seq_start = lax.rem(i, tiles_per_seq) == 0

    @pl.when(i == 0)
    def _():
        oc_buf[...] = jnp.zeros(oc_buf.shape, oc_buf.dtype)

    @pl.when(seq_start)
    def _():
        hbuf[0:CONV_HALO, :] = jnp.zeros((CONV_HALO, cw), F32)

    @pl.when(jnp.logical_not(seq_start))
    def _():
        hbuf[0:CONV_HALO, :] = hbuf[tt:tt + CONV_HALO, :]

    n0 = a_ref.shape[1]
    n1 = n0 + cw
    n2 = n1 + s_ref.shape[1]
    base = CONV_HALO - (CONV_K - 1)
    span = tt + CONV_HALO - SUBLANES
    dyn0 = pl.multiple_of(jnp.minimum(i, 0), ANCHOR)

    def fold(v):
        rows = functools.reduce(jnp.add, [v[r:r + ANCHOR, :] for r in range(0, v.shape[0], ANCHOR)])
        return functools.reduce(jnp.add, [rows[:, c:c + LANES] for c in range(0, v.shape[1], LANES)])

    def piece(p):
        cols = slice(p * OUT_PIECE, (p + 1) * OUT_PIECE)
        y = jnp.dot(oc_buf[pl.ds(dyn0, tt), :], w_ref[p, n0:n1, :], preferred_element_type=F32)
        y = y + jnp.dot(a_ref[...], w_ref[p, 0:n0, :], preferred_element_type=F32)
        y = y + jnp.dot(s_ref[...], w_ref[p, n1:n2, :], preferred_element_type=F32)
        o_ref[:, cols] = x_ref[:, cols] + y

    def glu():
        hval = u_ref[:, 0:cw] * _sigmoid(u_ref[:, cw:2 * cw])
        hbuf[CONV_HALO:CONV_HALO + tt, :] = hval
        return fold(hval)

    def shifts(residues):
        for res in residues:
            for r0 in range(0, span, CONV_SUB):
                rows = min(CONV_SUB, span - r0)
                shbuf[res - 1, r0:r0 + rows, :] = hbuf[res + r0:res + r0 + rows, :]

    def strips(first, count):
        mark = None
        for r0 in range(first, first + count * CONV_SUB, CONV_SUB):
            acc = jnp.broadcast_to(bdw_ref[...], (CONV_SUB, cw))
            for kk in range(CONV_K):
                res = (base + kk) % SUBLANES
                start = r0 + (base + kk - res)
                tap = hbuf[start:start + CONV_SUB, :] if res == 0 else shbuf[res - 1, start:start + CONV_SUB, :]
                acc = acc + wdw_ref[kk:kk + 1, :] * tap
            ybuf[r0:r0 + CONV_SUB, :] = acc
            part = fold(acc)
            mark = part if mark is None else mark + part
        return mark

    def anchor(mark):
        oc_buf[pl.ds(tt + dyn0, ANCHOR), 0:LANES] = mark.astype(oc_buf.dtype)

    n_free = 3
    n_groups = n_piece - n_free - 1
    per = tt // CONV_SUB // n_groups
    piece(0)
    anchor(glu())
    piece(1)
    shifts(range(1, SUBLANES))
    for p in range(2, n_free + 1):
        piece(p)
    for k in range(n_groups):
        anchor(strips(k * per * CONV_SUB, per))
        piece(n_free + 1 + k)

    y = ybuf[...]
    mu = jnp.mean(y, axis=-1, keepdims=True)
    yc = y - mu
    var = jnp.mean(yc * yc, axis=-1, keepdims=True)
    hn = _silu(yc * lax.rsqrt(var + LN_EPS) * lng_ref[...] + lnb_ref[...])
    out = jnp.dot(hn.astype(BF16), wpw_ref[...], preferred_element_type=F32)
    oc_buf[0:tt, :] = (out * _silu(gc_ref[...])).astype(oc_buf.dtype)


def _conv_out(layer, uconv, gconv, x2, o_mla, o_ssd, wdw, bdw, lng, lnb, wpw, w_out, seq, tt):
    m, d = x2.shape
    n = m // tt
    n_piece = w_out.shape[1]
    cur_row = lambda i: (jnp.minimum(i, n - 1), 0)
    prev_row = lambda i: (jnp.maximum(i - 1, 0), 0)
    w_spec = pl.BlockSpec((None,) + w_out.shape[1:], lambda i: (layer, 0, 0, 0), pipeline_mode=pl.Buffered(1))
    return pl.pallas_call(
        functools.partial(_conv_out_kernel, tiles_per_seq=seq // tt),
        grid=(n + 1,),
        in_specs=[pl.BlockSpec((tt, 2 * CONV_WIDTH), cur_row),
                  pl.BlockSpec((tt, CONV_WIDTH), cur_row),
                  pl.BlockSpec((tt, d), prev_row),
                  pl.BlockSpec((tt, o_mla.shape[1]), prev_row),
                  pl.BlockSpec((tt, o_ssd.shape[1]), prev_row),
                  _resident(wdw, layer), _resident(bdw, layer), _resident(lng, layer), _resident(lnb, layer),
                  _resident(wpw, layer), w_spec],
        out_specs=pl.BlockSpec((tt, d), prev_row),
        out_shape=jax.ShapeDtypeStruct((m, d), F32),
        scratch_shapes=[pltpu.VMEM((tt + CONV_HALO, CONV_WIDTH), F32),
                        pltpu.VMEM((SUBLANES - 1, tt + CONV_HALO, CONV_WIDTH), F32),
                        pltpu.VMEM((tt, CONV_WIDTH), F32),
                        pltpu.VMEM((tt + ANCHOR, CONV_WIDTH), BF16)],
        compiler_params=_params("arbitrary"),
        name="conv_out_proj",
    )(uconv, gconv, x2, o_mla, o_ssd, wdw, bdw, lng, lnb, wpw, w_out)


def _rows(p):
    return p.astype(F32)[:, None, :]


def _pad_rows(p, lane0, width):
    return jnp.pad(p.astype(F32), ((0, 0), (lane0, width - lane0 - p.shape[1])))[:, None, :]


ROW_TILE = 512


def _tile_sizes(m, seq):
    tm_in, t_seq = min(ROW_TILE, m), min(ROW_TILE, seq)
    t_prep = min(2 * ROW_TILE, seq)
    assert m % tm_in == 0 and seq % t_seq == 0 and t_seq % SSD_CHUNK == 0 and t_seq % (4 * CONV_SUB) == 0
    assert seq % t_prep == 0
    return tm_in, t_seq, t_prep


def _rope_tables(seq):
    half = QK_ROPE // 2
    inv_freq = ROPE_THETA ** (-jnp.arange(half, dtype=F32) / half)
    ang = jnp.arange(seq).astype(F32)[:, None] * inv_freq[None, :]
    cos, sin = jnp.cos(ang), jnp.sin(ang)
    z = lambda n: jnp.zeros((seq, n), F32)
    cos_t = jnp.concatenate([cos, cos, z(LANES - QK_ROPE)], axis=1)
    sa_t = jnp.concatenate([-sin, z(LANES - half)], axis=1)
    sb_t = jnp.concatenate([z(half), sin, z(LANES - QK_ROPE)], axis=1)
    return cos_t, sa_t, sb_t


def kernel(x, norm_g, w_in, q_a_norm, w_q_b, kv_a_norm, w_kv_b, q_norm, k_norm, conv_dw_w, conv_dw_b, conv_ln_g,
           conv_ln_b, conv_pw_w, ssd_conv_w, ssd_conv_b, ssd_dt_bias, ssd_A_log, ssd_D, ssd_norm_g, w_out):
    bsz, seq, d = x.shape
    m = bsz * seq
    x2 = x.reshape(m, d)
    tabs = _rope_tables(seq)

    w_in_k = _prep_w_in(w_in)
    wq, wkv, wpw = _prep_small(w_q_b, w_kv_b, conv_pw_w)
    w_out_k = _prep_cast(w_out, min(512, w_out.shape[1]))
    g_in, qan, kvan = _rows(norm_g), _rows(q_a_norm), _rows(kv_a_norm)
    kn = _pad_rows(k_norm, 0, QK_PAD)
    half = QK_ROPE // 2
    q_swapped = jnp.concatenate([q_norm[:, QK_NOPE + half:], q_norm[:, QK_NOPE:QK_NOPE + half]], axis=1)
    qn = jnp.concatenate([_pad_rows(q_norm, 0, QK_PAD), _pad_rows(q_swapped, 0, LANES)], axis=2)
    wdw = jnp.pad(conv_dw_w.astype(F32), ((0, 0), (0, CONV_HALO - CONV_K), (0, 0)))
    bdw, lng, lnb = _rows(conv_dw_b), _rows(conv_ln_g), _rows(conv_ln_b)
    cw = jnp.pad(ssd_conv_w.astype(F32), ((0, 0), (0, SUBLANES - SSD_CONV_K), (0, 0)))
    cb, ng = _rows(ssd_conv_b), _rows(ssd_norm_g)
    dtb, alog = _pad_rows(ssd_dt_bias, DT_LANE0, LANES), _pad_rows(ssd_A_log, DT_LANE0, LANES)
    dsk = _rows(jnp.repeat(ssd_D, SSD_HEAD_DIM, axis=1))

    tm_in, t_seq, t_prep = _tile_sizes(m, seq)
    for layer in range(norm_g.shape[0]):
        cq, ckv, kd, gmla, uconv, gconv, z, xbc = _in_proj(layer, x2, g_in, w_in_k, tm_in)
        q, k, v = _mla_prep(layer, cq, ckv, kd, qan, wq, kvan, wkv, qn, kn, *tabs, bsz, seq, t_prep)
        o_mla = _attention(q, k, v, gmla, bsz, seq, t_seq)
        o_ssd = _ssd(layer, z, xbc, kd, cw, cb, dtb, alog, dsk, ng, bsz, seq, t_seq)
        x2 = _conv_out(layer, uconv, gconv, x2, o_mla, o_ssd, wdw, bdw, lng, lnb, wpw, w_out_k, seq, t_seq)
    return x2.reshape(bsz, seq, d)
```

```python
import functools
import math

import jax
import jax.numpy as jnp
from jax import lax
from jax.experimental import pallas as pl
from jax.experimental.pallas import tpu as pltpu

F32 = jnp.float32
BF16 = jnp.bfloat16

MLA_HEADS = 6
QK_NOPE = 128
QK_ROPE = 64
QK_HEAD = QK_NOPE + QK_ROPE
V_HEAD = 128
Q_LORA = 512
KV_LORA = 256
MLA_WIDTH = MLA_HEADS * V_HEAD
ROPE_THETA = 10000.0
CONV_WIDTH = 512
CONV_K = 31
SSD_HEADS = 12
SSD_HEAD_DIM = 64
SSD_WIDTH = SSD_HEADS * SSD_HEAD_DIM
SSD_GROUPS = 2
SSD_STATE = 128
SSD_CONV_K = 4
SSD_CHUNK = 128
SSD_XBC = SSD_WIDTH + 2 * SSD_GROUPS * SSD_STATE
NORM_EPS = 1e-6
LN_EPS = 1e-5

LANES = 128
SUBLANES = 8
QK_PAD = 2 * LANES
Q_UP = 3 * LANES
KD_WIDTH = LANES
DT_LANE0 = QK_ROPE
CONV_HALO = 32
SSD_HALO = SUBLANES
VMEM_LIMIT = 56 * 1024 * 1024
ATTN_STRIP = 64

SEG_WIDTHS = (Q_LORA, KV_LORA, KD_WIDTH, MLA_WIDTH, 2 * CONV_WIDTH, CONV_WIDTH, SSD_WIDTH, SSD_XBC)


def _sigmoid(x):
    return 1.0 / (1.0 + jnp.exp2(x * (-math.log2(math.e))))


def _silu(x):
    return x * _sigmoid(x)


def _softplus(x):
    return jnp.maximum(x, 0.0) + jnp.log1p(jnp.exp(-jnp.abs(x)))


def _rms(x, g, eps=NORM_EPS):
    ms = jnp.mean(x * x, axis=-1, keepdims=True)
    return x * lax.rsqrt(ms + eps) * g


def _params(*sem):
    return pltpu.CompilerParams(dimension_semantics=sem, vmem_limit_bytes=VMEM_LIMIT)


def _resident(stacked, layer):
    _, a, b = stacked.shape
    return pl.BlockSpec((None, a, b), lambda *_: (layer, 0, 0), pipeline_mode=pl.Buffered(1))


W_IN_HALF = LANES // 2


def _prep_w_in_kernel(a_ref, b_ref, o_ref, *, depth, n_kt, n_dt):
    j = pl.program_id(0)
    kd_tile = (Q_LORA + KV_LORA) // LANES
    stride = depth * n_kt
    row = lax.broadcasted_iota(jnp.int32, (W_IN_HALF, LANES), 0)
    keep_b = (j != kd_tile) | (row < n_dt)
    for l in range(depth):
        for kt in range(n_kt):
            xa = a_ref[pl.ds(kt * depth + l, W_IN_HALF, stride=stride), :]
            xb = b_ref[pl.ds(kt * depth + l, W_IN_HALF, stride=stride), :]
            x = jnp.concatenate([xa, jnp.where(keep_b, xb, 0.0)], axis=0)
            o_ref[l, kt * LANES:(kt + 1) * LANES, :] = x.T.astype(o_ref.dtype)


def _prep_w_in(w_in):
    depth, d, n = w_in.shape
    n_kt = d // LANES
    n_out = sum(SEG_WIDTHS)
    rows_per_group = W_IN_HALF * n_kt * depth
    view = w_in.transpose(2, 0, 1).reshape(n, depth, n_kt, LANES).transpose(0, 2, 1, 3).reshape(-1, LANES)
    kd_tile = (Q_LORA + KV_LORA) // LANES
    dt_group = (n - SSD_HEADS) // W_IN_HALF
    assert (n - SSD_HEADS) % W_IN_HALF == 0 and (Q_LORA + KV_LORA) % LANES == 0 and QK_ROPE == W_IN_HALF
    a_idx = lambda j: (jnp.where(j <= kd_tile, 2 * j, 2 * j - 1), 0)
    b_idx = lambda j: (jnp.where(j < kd_tile, 2 * j + 1, jnp.where(j == kd_tile, dt_group, 2 * j)), 0)
    return pl.pallas_call(
        functools.partial(_prep_w_in_kernel, depth=depth, n_kt=n_kt, n_dt=SSD_HEADS),
        grid=(n_out // LANES,),
        in_specs=[pl.BlockSpec((rows_per_group, LANES), a_idx),
                  pl.BlockSpec((rows_per_group, LANES), b_idx)],
        out_specs=pl.BlockSpec((depth, d, LANES), lambda j: (0, 0, j)),
        out_shape=jax.ShapeDtypeStruct((depth, d, n_out), BF16),
        compiler_params=_params("parallel"),
        name="prep_w_in",
    )(view, view)


def _prep_small_kernel(wq_ref, wkv_ref, wpw_ref, oq_ref, okv_ref, opw_ref):
    wq = wq_ref[...]
    zq = jnp.zeros((wq.shape[0], QK_PAD - QK_HEAD), F32)
    half = QK_ROPE // 2
    parts = []
    for h in range(MLA_HEADS):
        r0 = h * QK_HEAD + QK_NOPE
        parts += [wq[:, h * QK_HEAD:(h + 1) * QK_HEAD], zq,
                  wq[:, r0 + half:r0 + QK_ROPE], wq[:, r0:r0 + half], zq]
    oq_ref[...] = jnp.concatenate(parts, axis=1).astype(oq_ref.dtype)
    wkv = wkv_ref[...]
    per = QK_NOPE + V_HEAD
    k_cols = [wkv[:, h * per:h * per + QK_NOPE] for h in range(MLA_HEADS)]
    v_cols = [wkv[:, h * per + QK_NOPE:(h + 1) * per] for h in range(MLA_HEADS)]
    okv_ref[...] = jnp.concatenate(k_cols + v_cols, axis=1).astype(okv_ref.dtype)
    opw_ref[...] = wpw_ref[...].astype(opw_ref.dtype)


def _prep_small(w_q_b, w_kv_b, conv_pw_w):
    depth = w_q_b.shape[0]
    whole = lambda a: pl.BlockSpec((None,) + a.shape[1:], lambda l: (l, 0, 0))
    shapes = [(depth, Q_LORA, MLA_HEADS * Q_UP), w_kv_b.shape, conv_pw_w.shape]
    return pl.pallas_call(
        _prep_small_kernel,
        grid=(depth,),
        in_specs=[whole(w_q_b), whole(w_kv_b), whole(conv_pw_w)],
        out_specs=[pl.BlockSpec((None,) + s[1:], lambda l: (l, 0, 0)) for s in shapes],
        out_shape=[jax.ShapeDtypeStruct(s, BF16) for s in shapes],
        compiler_params=_params("parallel"),
        name="prep_small",
    )(w_q_b, w_kv_b, conv_pw_w)


OUT_PIECE = 2 * LANES
CONV_SUB = 32
ANCHOR = 16


def _cast_kernel(w_ref, o_ref):
    for p in range(o_ref.shape[0]):
        o_ref[p] = w_ref[:, p * OUT_PIECE:(p + 1) * OUT_PIECE].astype(o_ref.dtype)


def _prep_cast(w, tk):
    depth, k, n = w.shape
    n_piece = n // OUT_PIECE
    return pl.pallas_call(
        _cast_kernel,
        grid=(depth, k // tk),
        in_specs=[pl.BlockSpec((None, tk, n), lambda l, i: (l, i, 0))],
        out_specs=pl.BlockSpec((None, n_piece, tk, OUT_PIECE), lambda l, i: (l, 0, i, 0)),
        out_shape=jax.ShapeDtypeStruct((depth, n_piece, k, OUT_PIECE), BF16),
        compiler_params=_params("parallel", "parallel"),
        name="prep_cast",
    )(w)


def _in_proj_kernel(x_ref, g_ref, w_ref, *out_refs):
    h = _rms(x_ref[...], g_ref[...]).astype(BF16)
    off = 0
    for o_ref in out_refs:
        n = o_ref.shape[-1]
        o_ref[...] = jnp.dot(h, w_ref[:, off:off + n], preferred_element_type=F32).astype(o_ref.dtype)
        off += n


def _in_proj(layer, x2, g, w, tm):
    m, d = x2.shape
    assert w.shape[2] == sum(SEG_WIDTHS)
    return pl.pallas_call(
        _in_proj_kernel,
        grid=(m // tm,),
        in_specs=[pl.BlockSpec((tm, d), lambda i: (i, 0)),
                  _resident(g, layer),
                  _resident(w, layer)],
        out_specs=[pl.BlockSpec((tm, n), lambda i: (i, 0)) for n in SEG_WIDTHS],
        out_shape=[jax.ShapeDtypeStruct((m, n), F32) for n in SEG_WIDTHS],
        compiler_params=_params("parallel"),
        name="in_proj",
    )(x2, g, w)


def _mla_prep_kernel(cq_ref, ckv_ref, kd_ref, qan_ref, wq_ref, kvan_ref, wkv_ref, qn_ref, kn_ref,
                     cos_ref, sa_ref, sb_ref, q_ref, k_ref, v_ref):
    scale = math.log2(math.e) / math.sqrt(QK_HEAD)
    hq = _rms(cq_ref[...], qan_ref[...]).astype(BF16)
    qf = jnp.dot(hq, wq_ref[...], preferred_element_type=F32)
    hkv = _rms(ckv_ref[...], kvan_ref[...]).astype(BF16)
    kvf = jnp.dot(hkv, wkv_ref[...], preferred_element_type=F32)

    cos = cos_ref[...]
    sa = sa_ref[...]
    sb = sb_ref[...]

    def rope(r):
        return r * cos + pltpu.roll(r, LANES - QK_ROPE // 2, 1) * sa + pltpu.roll(r, QK_ROPE // 2, 1) * sb

    kd = kd_ref[...]
    lane = lax.broadcasted_iota(jnp.int32, kd.shape, 1)
    kpe = jnp.where(lane < QK_ROPE, kd, 0.0)
    kpe_ss = jnp.sum(kpe * kpe, axis=-1, keepdims=True)

    qn_a = qn_ref[:, 0:LANES] * scale
    qn_b = qn_ref[:, LANES:QK_PAD] * scale
    qn_s = qn_ref[:, QK_PAD:Q_UP] * scale
    sin_signed = sa + sb
    kn_w = kn_ref[...]
    kpe_rot = rope(kpe * kn_w[:, LANES:QK_PAD])
    for h in range(MLA_HEADS):
        qa = qf[:, h * Q_UP:h * Q_UP + LANES]
        qb = qf[:, h * Q_UP + LANES:h * Q_UP + QK_PAD]
        qs = qf[:, h * Q_UP + QK_PAD:(h + 1) * Q_UP]
        ss = jnp.sum(qa * qa + qb * qb, axis=-1, keepdims=True)
        inv = lax.rsqrt(ss * (1.0 / QK_HEAD) + NORM_EPS)
        q_ref[0, h, :, 0:LANES] = (qa * inv * qn_a).astype(q_ref.dtype)
        q_ref[0, h, :, LANES:QK_PAD] = (inv * (qb * qn_b * cos + qs * qn_s * sin_signed)).astype(q_ref.dtype)

        ka = kvf[:, h * LANES:(h + 1) * LANES]
        ss = jnp.sum(ka * ka, axis=-1, keepdims=True) + kpe_ss
        inv = lax.rsqrt(ss * (1.0 / QK_HEAD) + NORM_EPS)
        k_ref[0, h, :, 0:LANES] = (ka * inv * kn_w[:, 0:LANES]).astype(k_ref.dtype)
        k_ref[0, h, :, LANES:QK_PAD] = (kpe_rot * inv).astype(k_ref.dtype)

        v_ref[0, h, :, :] = kvf[:, MLA_WIDTH + h * V_HEAD:MLA_WIDTH + (h + 1) * V_HEAD].astype(v_ref.dtype)


def _mla_prep(layer, cq, ckv, kd, qan, wq, kvan, wkv, qn, kn, cos_t, sa_t, sb_t, bsz, seq, tm):
    nt = seq // tm
    row = lambda b, i: (b * nt + i, 0)
    pos = lambda b, i: (i, 0)
    hd = MLA_HEADS
    return pl.pallas_call(
        _mla_prep_kernel,
        grid=(bsz, nt),
        in_specs=[pl.BlockSpec((tm, Q_LORA), row),
                  pl.BlockSpec((tm, KV_LORA), row),
                  pl.BlockSpec((tm, KD_WIDTH), row),
                  _resident(qan, layer), _resident(wq, layer), _resident(kvan, layer), _resident(wkv, layer),
                  _resident(qn, layer), _resident(kn, layer),
                  pl.BlockSpec((tm, LANES), pos), pl.BlockSpec((tm, LANES), pos), pl.BlockSpec((tm, LANES), pos)],
        out_specs=[pl.BlockSpec((1, hd, tm, QK_PAD), lambda b, i: (b, 0, i, 0)),
                   pl.BlockSpec((1, hd, tm, QK_PAD), lambda b, i: (b, 0, i, 0)),
                   pl.BlockSpec((1, hd, tm, V_HEAD), lambda b, i: (b, 0, i, 0))],
        out_shape=[jax.ShapeDtypeStruct((bsz, hd, seq, QK_PAD), BF16),
                   jax.ShapeDtypeStruct((bsz, hd, seq, QK_PAD), BF16),
                   jax.ShapeDtypeStruct((bsz, hd, seq, V_HEAD), BF16)],
        compiler_params=_params("parallel", "parallel"),
        name="mla_prep",
    )(cq, ckv, kd, qan, wq, kvan, wkv, qn, kn, cos_t, sa_t, sb_t)


def _attn_kernel(qi_ref, ki_ref, q_ref, k_ref, v_ref, g_ref, o_ref, m_sc, l_sc, acc_sc, s_sc, p_sc, a_sc):
    qi = qi_ref[pl.program_id(1)]
    ki = ki_ref[pl.program_id(1)]
    tq = q_ref.shape[2]
    tk = k_ref.shape[2]

    @pl.when(ki == 0)
    def _():
        m_sc[...] = jnp.full(m_sc.shape, -jnp.inf, F32)
        l_sc[...] = jnp.zeros(l_sc.shape, F32)
        acc_sc[...] = jnp.zeros(acc_sc.shape, F32)

    strip = min(ATTN_STRIP, tq)
    n_chunk = tk // LANES

    def scores(h):
        s_sc[h % 2] = lax.dot_general(q_ref[0, h], k_ref[0, h], (((1,), (1,)), ((), ())),
                                      preferred_element_type=F32)

    def softmax_strip(h, r0, masked):
        slot = h % 2
        rows = slice(r0, r0 + strip)
        live = [c for c in range(n_chunk) if not (masked and c * LANES >= r0 + strip)]
        chunks = []
        for c in live:
            x = s_sc[slot, rows, c * LANES:(c + 1) * LANES]
            if masked and (c + 1) * LANES - 1 > r0:
                rr = r0 + lax.broadcasted_iota(jnp.int32, (strip, LANES), 0)
                cc = c * LANES + lax.broadcasted_iota(jnp.int32, (strip, LANES), 1)
                x = jnp.where(rr >= cc, x, -jnp.inf)
            chunks.append(x)
        m_prev = m_sc[h, rows, :]
        m_loc = functools.reduce(jnp.maximum, chunks)
        m_new = jnp.maximum(m_prev, jnp.max(m_loc, axis=-1, keepdims=True))
        alpha = jnp.exp2(m_prev - m_new)
        ps = [jnp.exp2(x - m_new) for x in chunks]
        l_sc[h, rows, :] = alpha * l_sc[h, rows, :] + functools.reduce(jnp.add, ps)
        m_sc[h, rows, :] = m_new
        a_sc[slot, rows, :] = alpha
        dead = [jnp.zeros((strip, LANES), BF16)] * (n_chunk - len(live))
        p_sc[slot, rows, :] = jnp.concatenate([p.astype(BF16) for p in ps] + dead, axis=1)

    def step(masked):
        scores(0)
        for h in range(MLA_HEADS):
            if h + 1 < MLA_HEADS:
                scores(h + 1)
            for r0 in range(0, tq, strip):
                softmax_strip(h, r0, masked)
            pv = jnp.dot(p_sc[h % 2], v_ref[0, h], preferred_element_type=F32)
            acc_sc[h] = a_sc[h % 2] * acc_sc[h] + pv

    @pl.when(ki < qi)
    def _():
        step(False)

    @pl.when(ki == qi)
    def _():
        step(True)
        for h in range(MLA_HEADS):
            o = acc_sc[h] / jnp.sum(l_sc[h], axis=-1, keepdims=True)
            g = g_ref[:, h * V_HEAD:(h + 1) * V_HEAD]
            o_ref[:, h * V_HEAD:(h + 1) * V_HEAD] = (o * _silu(g)).astype(o_ref.dtype)


def _attention(q, k, v, gmla, bsz, seq, tq):
    hd = MLA_HEADS
    nq = seq // tq
    pairs = [(i, j) for i in range(nq) for j in range(i + 1)]
    qi_tab = jnp.asarray([i for i, _ in pairs], jnp.int32)
    ki_tab = jnp.asarray([j for _, j in pairs], jnp.int32)
    q_idx = lambda b, t, qi, ki: (b, 0, qi[t], 0)
    kv_idx = lambda b, t, qi, ki: (b, 0, ki[t], 0)
    row = lambda b, t, qi, ki: (b * nq + qi[t], 0)
    grid_spec = pltpu.PrefetchScalarGridSpec(
        num_scalar_prefetch=2,
        grid=(bsz, len(pairs)),
        in_specs=[pl.BlockSpec((1, hd, tq, QK_PAD), q_idx),
                  pl.BlockSpec((1, hd, tq, QK_PAD), kv_idx),
                  pl.BlockSpec((1, hd, tq, V_HEAD), kv_idx),
                  pl.BlockSpec((tq, MLA_WIDTH), row)],
        out_specs=pl.BlockSpec((tq, MLA_WIDTH), row),
        scratch_shapes=[pltpu.VMEM((hd, tq, LANES), F32), pltpu.VMEM((hd, tq, LANES), F32),
                        pltpu.VMEM((hd, tq, V_HEAD), F32),
                        pltpu.VMEM((2, tq, tq), F32), pltpu.VMEM((2, tq, tq), BF16),
                        pltpu.VMEM((2, tq, LANES), F32)])
    return pl.pallas_call(
        _attn_kernel,
        grid_spec=grid_spec,
        out_shape=jax.ShapeDtypeStruct((bsz * seq, MLA_WIDTH), BF16),
        compiler_params=_params("parallel", "arbitrary"),
        name="attention",
    )(qi_tab, ki_tab, q, k, v, gmla)


def _split3(x):
    hi = x.astype(BF16)
    r1 = x - hi.astype(F32)
    mid = r1.astype(BF16)
    lo = (r1 - mid.astype(F32)).astype(BF16)
    return hi, mid, lo


def _ssd_kernel(z_ref, xbc_ref, kd_ref, cw_ref, cb_ref, dtb_ref, alog_ref, dsk_ref, ng_ref, o_ref,
                xbuf, xact, state, shx, acol_sc, xdtb_sc, xdte_sc, grow_sc, y_sc):
    i = pl.program_id(0)
    nb, ts = o_ref.shape[0], o_ref.shape[1]
    L = SSD_CHUNK
    hp = SSD_HEAD_DIM
    gw = SSD_WIDTH // SSD_GROUPS
    pairs_per_group = SSD_HEADS // SSD_GROUPS // 2
    npair = SSD_HEADS // 2

    @pl.when(i == 0)
    def _():
        xbuf[:, 0:SSD_HALO, :] = jnp.zeros((nb, SSD_HALO, SSD_XBC), F32)
        state[...] = jnp.zeros(state.shape, F32)

    @pl.when(i > 0)
    def _():
        xbuf[:, 0:SSD_HALO, :] = xbuf[:, ts:ts + SSD_HALO, :]

    for b in range(nb):
        xbuf[b, SSD_HALO:SSD_HALO + ts, :] = xbc_ref[b]
        for s in range(1, SSD_CONV_K):
            shx[s - 1, :, :] = xbuf[b, SSD_HALO - s:SSD_HALO - s + ts, :]
        acc = cb_ref[...] + cw_ref[SSD_CONV_K - 1:SSD_CONV_K, :] * xbc_ref[b]
        for s in range(1, SSD_CONV_K):
            acc = acc + cw_ref[SSD_CONV_K - 1 - s:SSD_CONV_K - s, :] * shx[s - 1]
        xact[b] = _silu(acc)

    lane = lax.broadcasted_iota(jnp.int32, (1, LANES), 1)
    is_dt = (lane >= DT_LANE0) & (lane < DT_LANE0 + SSD_HEADS)
    a_neg = jnp.where(is_dt, -jnp.exp(alog_ref[...]), 0.0)
    rr = lax.broadcasted_iota(jnp.int32, (L, L), 0)
    cc = lax.broadcasted_iota(jnp.int32, (L, L), 1)
    causal = rr >= cc
    tri = causal.astype(BF16)
    left = lax.broadcasted_iota(jnp.int32, (L, LANES), 1) < hp

    def decay_geometry(b, r0):
        dt = _softplus(kd_ref[b, pl.ds(r0, L), :] + dtb_ref[...])
        hi, mid, lo = _split3(dt * a_neg)
        acs = (jnp.dot(tri, hi, preferred_element_type=F32)
               + jnp.dot(tri, mid, preferred_element_type=F32)
               + jnp.dot(tri, lo, preferred_element_type=F32))
        return dict(dt=dt, acs=acs, acs_t=acs.T)

    def expand(b, r0, v, sl):
        def col(a, h):
            return jnp.broadcast_to(a[:, DT_LANE0 + h:DT_LANE0 + h + 1], (L, LANES))

        lasts = []
        for j in range(npair):
            lanes = slice(j * LANES, (j + 1) * LANES)
            a0, a1 = col(v["acs"], 2 * j), col(v["acs"], 2 * j + 1)
            acol_sc[sl,2 * j] = a0
            acol_sc[sl,2 * j + 1] = a1
            acs_pair = jnp.where(left, a0, a1)
            dt_pair = jnp.where(left, col(v["dt"], 2 * j), col(v["dt"], 2 * j + 1))
            last = acs_pair[L - 1:L, :]
            xdt = xact[b, pl.ds(r0, L), lanes] * dt_pair
            xdtb_sc[sl,:, lanes] = xdt.astype(BF16)
            xdte_sc[sl,:, lanes] = (xdt * jnp.exp(last - acs_pair)).astype(BF16)
            grow_sc[sl,:, lanes] = jnp.exp(acs_pair)
            lasts.append(last)
        v["lasts"] = lasts

    def mix(b, r0, v, sl):
        ssq = jnp.zeros((L, LANES), F32)
        for g in range(SSD_GROUPS):
            c0 = SSD_WIDTH + g * SSD_STATE
            c1 = SSD_WIDTH + (SSD_GROUPS + g) * SSD_STATE
            grp = slice(g * gw, (g + 1) * gw)
            bg = xact[b, pl.ds(r0, L), c0:c0 + SSD_STATE]
            cg_b = xact[b, pl.ds(r0, L), c1:c1 + SSD_STATE].astype(BF16)
            cb = lax.dot_general(cg_b, bg.astype(BF16), (((1,), (1,)), ((), ())), preferred_element_type=F32)
            st_prev = state[b, :, grp]
            st_new = jnp.dot(bg.T.astype(BF16), xdte_sc[sl,:, grp], preferred_element_type=F32)
            y_off = jnp.dot(cg_b, st_prev.astype(BF16), preferred_element_type=F32) * grow_sc[sl,:, grp]
            last_g = jnp.concatenate(v["lasts"][g * pairs_per_group:(g + 1) * pairs_per_group], axis=1)
            state[b, :, grp] = st_prev * jnp.exp(last_g) + st_new
            for jj in range(pairs_per_group):
                j = g * pairs_per_group + jj
                lanes = slice(j * LANES, (j + 1) * LANES)
                pair = xdtb_sc[sl,:, lanes]
                yd = []
                for h in (2 * j, 2 * j + 1):
                    seg = acol_sc[sl,h] - v["acs_t"][DT_LANE0 + h:DT_LANE0 + h + 1, :]
                    decay = jnp.exp(jnp.where(causal, seg, -jnp.inf))
                    yd.append(jnp.dot((cb * decay).astype(BF16), pair, preferred_element_type=F32))
                y = (jnp.where(left, yd[0], yd[1]) + y_off[:, jj * LANES:(jj + 1) * LANES]
                     + dsk_ref[:, lanes] * xact[b, pl.ds(r0, L), lanes])
                yz = y * _silu(z_ref[b, pl.ds(r0, L), lanes])
                y_sc[sl, :, lanes] = yz
                ssq = ssq + yz * yz
        v["inv"] = lax.rsqrt(jnp.sum(ssq, axis=-1, keepdims=True) * (1.0 / SSD_WIDTH) + NORM_EPS)

    def finish(b, r0, v, sl):
        o_ref[b, pl.ds(r0, L), :] = (y_sc[sl] * v["inv"] * ng_ref[...]).astype(o_ref.dtype)

    jobs = [(b, k * L, k * nb + b) for k in range(ts // L) for b in range(nb)]
    vals = [decay_geometry(b, r0) for b, r0, _ in jobs]
    for stage in (expand, mix, finish):
        for (b, r0, sl), v in zip(jobs, vals):
            stage(b, r0, v, sl)


def _ssd(layer, z, xbc, kd, cw, cb, dtb, alog, dsk, ng, bsz, seq, ts):
    blk = lambda w: pl.BlockSpec((bsz, ts, w), lambda i: (0, i, 0))
    slots = bsz * (ts // SSD_CHUNK)
    out = pl.pallas_call(
        _ssd_kernel,
        grid=(seq // ts,),
        in_specs=[blk(SSD_WIDTH), blk(SSD_XBC), blk(KD_WIDTH),
                  _resident(cw, layer), _resident(cb, layer), _resident(dtb, layer), _resident(alog, layer),
                  _resident(dsk, layer), _resident(ng, layer)],
        out_specs=blk(SSD_WIDTH),
        out_shape=jax.ShapeDtypeStruct((bsz, seq, SSD_WIDTH), BF16),
        scratch_shapes=[pltpu.VMEM((bsz, ts + SSD_HALO, SSD_XBC), F32), pltpu.VMEM((bsz, ts, SSD_XBC), F32),
                        pltpu.VMEM((bsz, SSD_STATE, SSD_WIDTH), F32),
                        pltpu.VMEM((SSD_CONV_K - 1, ts, SSD_XBC), F32),
                        pltpu.VMEM((slots, SSD_HEADS, SSD_CHUNK, LANES), F32),
                        pltpu.VMEM((slots, SSD_CHUNK, SSD_WIDTH), BF16), pltpu.VMEM((slots, SSD_CHUNK, SSD_WIDTH), BF16),
                        pltpu.VMEM((slots, SSD_CHUNK, SSD_WIDTH), F32), pltpu.VMEM((slots, SSD_CHUNK, SSD_WIDTH), F32)],
        compiler_params=_params("arbitrary"),
        name="ssd",
    )(z.reshape(bsz, seq, -1), xbc.reshape(bsz, seq, -1), kd.reshape(bsz, seq, -1), cw, cb, dtb, alog, dsk, ng)
    return out.reshape(bsz * seq, SSD_WIDTH)


def _conv_out_kernel(u_ref, gc_ref, x_ref, a_ref, s_ref, wdw_ref, bdw_ref, lng_ref, lnb_ref, wpw_ref, w_ref,
                     o_ref, hbuf, shbuf, ybuf, oc_buf, *, tiles_per_seq):
    i = pl.program_id(0)
    tt = u_ref.shape[0]
    cw = CONV_WIDTH
    n_piece = w_ref.shape[0]
    seq_start = lax.rem(i, tiles_per_seq) == 0

    @pl.when(i == 0)
    def _():
        oc_buf[...] = jnp.zeros(oc_buf.shape, oc_buf.dtype)

    @pl.when(seq_start)
    def _():
        hbuf[0:CONV_HALO, :] = jnp.zeros((CONV_HALO, cw), F32)

    @pl.when(jnp.logical_not(seq_start))
    def _():
        hbuf[0:CONV_HALO, :] = hbuf[tt:tt + CONV_HALO, :]

    n0 = a_ref.shape[1]
    n1 = n0 + cw
    n2 = n1 + s_ref.shape[1]
    base = CONV_HALO - (CONV_K - 1)
    span = tt + CONV_HALO - SUBLANES
    dyn0 = pl.multiple_of(jnp.minimum(i, 0), ANCHOR)

    def fold(v):
        rows = functools.reduce(jnp.add, [v[r:r + ANCHOR, :] for r in range(0, v.shape[0], ANCHOR)])
        return functools.reduce(jnp.add, [rows[:, c:c + LANES] for c in range(0, v.shape[1], LANES)])

    def piece(p):
        cols = slice(p * OUT_PIECE, (p + 1) * OUT_PIECE)
        y = jnp.dot(oc_buf[pl.ds(dyn0, tt), :], w_ref[p, n0:n1, :], preferred_element_type=F32)
        y = y + jnp.dot(a_ref[...], w_ref[p, 0:n0, :], preferred_element_type=F32)
        y = y + jnp.dot(s_ref[...], w_ref[p, n1:n2, :], preferred_element_type=F32)
        o_ref[:, cols] = x_ref[:, cols] + y

    def glu():
        hval = u_ref[:, 0:cw] * _sigmoid(u_ref[:, cw:2 * cw])
        hbuf[CONV_HALO:CONV_HALO + tt, :] = hval
        return fold(hval)

    def shifts(residues):
        for res in residues:
            for r0 in range(0, span, CONV_SUB):
                rows = min(CONV_SUB, span - r0)
                shbuf[res - 1, r0:r0 + rows, :] = hbuf[res + r0:res + r0 + rows, :]

    def strips(first, count):
        mark = None
        for r0 in range(first, first + count * CONV_SUB, CONV_SUB):
            acc = jnp.broadcast_to(bdw_ref[...], (CONV_SUB, cw))
            for kk in range(CONV_K):
                res = (base + kk) % SUBLANES
                start = r0 + (base + kk - res)
                tap = hbuf[start:start + CONV_SUB, :] if res == 0 else shbuf[res - 1, start:start + CONV_SUB, :]
                acc = acc + wdw_ref[kk:kk + 1, :] * tap
            ybuf[r0:r0 + CONV_SUB, :] = acc
            part = fold(acc)
            mark = part if mark is None else mark + part
        return mark

    def anchor(mark):
        oc_buf[pl.ds(tt + dyn0, ANCHOR), 0:LANES] = mark.astype(oc_buf.dtype)

    n_free = 3
    n_groups = n_piece - n_free - 1
    per = tt // CONV_SUB // n_groups
    piece(0)
    anchor(glu())
    piece(1)
    shifts(range(1, SUBLANES))
    for p in range(2, n_free + 1):
        piece(p)
    for k in range(n_groups):
        anchor(strips(k * per * CONV_SUB, per))
        piece(n_free + 1 + k)

    y = ybuf[...]
    mu = jnp.mean(y, axis=-1, keepdims=True)
    yc = y - mu
    var = jnp.mean(yc * yc, axis=-1, keepdims=True)
    hn = _silu(yc * lax.rsqrt(var + LN_EPS) * lng_ref[...] + lnb_ref[...])
    out = jnp.dot(hn.astype(BF16), wpw_ref[...], preferred_element_type=F32)
    oc_buf[0:tt, :] = (out * _silu(gc_ref[...])).astype(oc_buf.dtype)


def _conv_out(layer, uconv, gconv, x2, o_mla, o_ssd, wdw, bdw, lng, lnb, wpw, w_out, seq, tt):
    m, d = x2.shape
    n = m // tt
    n_piece = w_out.shape[1]
    cur_row = lambda i: (jnp.minimum(i, n - 1), 0)
    prev_row = lambda i: (jnp.maximum(i - 1, 0), 0)
    w_spec = pl.BlockSpec((None,) + w_out.shape[1:], lambda i: (layer, 0, 0, 0), pipeline_mode=pl.Buffered(1))
    return pl.pallas_call(
        functools.partial(_conv_out_kernel, tiles_per_seq=seq // tt),
        grid=(n + 1,),
        in_specs=[pl.BlockSpec((tt, 2 * CONV_WIDTH), cur_row),
                  pl.BlockSpec((tt, CONV_WIDTH), cur_row),
                  pl.BlockSpec((tt, d), prev_row),
                  pl.BlockSpec((tt, o_mla.shape[1]), prev_row),
                  pl.BlockSpec((tt, o_ssd.shape[1]), prev_row),
                  _resident(wdw, layer), _resident(bdw, layer), _resident(lng, layer), _resident(lnb, layer),
                  _resident(wpw, layer), w_spec],
        out_specs=pl.BlockSpec((tt, d), prev_row),
        out_shape=jax.ShapeDtypeStruct((m, d), F32),
        scratch_shapes=[pltpu.VMEM((tt + CONV_HALO, CONV_WIDTH), F32),
                        pltpu.VMEM((SUBLANES - 1, tt + CONV_HALO, CONV_WIDTH), F32),
                        pltpu.VMEM((tt, CONV_WIDTH), F32),
                        pltpu.VMEM((tt + ANCHOR, CONV_WIDTH), BF16)],
        compiler_params=_params("arbitrary"),
        name="conv_out_proj",
    )(uconv, gconv, x2, o_mla, o_ssd, wdw, bdw, lng, lnb, wpw, w_out)


def _rows(p):
    return p.astype(F32)[:, None, :]


def _pad_rows(p, lane0, width):
    return jnp.pad(p.astype(F32), ((0, 0), (lane0, width - lane0 - p.shape[1])))[:, None, :]


ROW_TILE = 512


def _tile_sizes(m, seq):
    tm_in, t_seq = min(ROW_TILE, m), min(ROW_TILE, seq)
    t_prep = min(2 * ROW_TILE, seq)
    assert m % tm_in == 0 and seq % t_seq == 0 and t_seq % SSD_CHUNK == 0 and t_seq % (4 * CONV_SUB) == 0
    assert seq % t_prep == 0
    return tm_in, t_seq, t_prep


def _rope_tables(seq):
    half = QK_ROPE // 2
    inv_freq = ROPE_THETA ** (-jnp.arange(half, dtype=F32) / half)
    ang = jnp.arange(seq).astype(F32)[:, None] * inv_freq[None, :]
    cos, sin = jnp.cos(ang), jnp.sin(ang)
    z = lambda n: jnp.zeros((seq, n), F32)
    cos_t = jnp.concatenate([cos, cos, z(LANES - QK_ROPE)], axis=1)
    sa_t = jnp.concatenate([-sin, z(LANES - half)], axis=1)
    sb_t = jnp.concatenate([z(half), sin, z(LANES - QK_ROPE)], axis=1)
    return cos_t, sa_t, sb_t


def kernel(x, norm_g, w_in, q_a_norm, w_q_b, kv_a_norm, w_kv_b, q_norm, k_norm, conv_dw_w, conv_dw_b, conv_ln_g,
           conv_ln_b, conv_pw_w, ssd_conv_w, ssd_conv_b, ssd_dt_bias, ssd_A_log, ssd_D, ssd_norm_g, w_out):
    bsz, seq, d = x.shape
    m = bsz * seq
    x2 = x.reshape(m, d)
    tabs = _rope_tables(seq)

    w_in_k = _prep_w_in(w_in)
    wq, wkv, wpw = _prep_small(w_q_b, w_kv_b, conv_pw_w)
    w_out_k = _prep_cast(w_out, min(512, w_out.shape[1]))
    g_in, qan, kvan = _rows(norm_g), _rows(q_a_norm), _rows(kv_a_norm)
    kn = _pad_rows(k_norm, 0, QK_PAD)
    half = QK_ROPE // 2
    q_swapped = jnp.concatenate([q_norm[:, QK_NOPE + half:], q_norm[:, QK_NOPE:QK_NOPE + half]], axis=1)
    qn = jnp.concatenate([_pad_rows(q_norm, 0, QK_PAD), _pad_rows(q_swapped, 0, LANES)], axis=2)
    wdw = jnp.pad(conv_dw_w.astype(F32), ((0, 0), (0, CONV_HALO - CONV_K), (0, 0)))
    bdw, lng, lnb = _rows(conv_dw_b), _rows(conv_ln_g), _rows(conv_ln_b)
    cw = jnp.pad(ssd_conv_w.astype(F32), ((0, 0), (0, SUBLANES - SSD_CONV_K), (0, 0)))
    cb, ng = _rows(ssd_conv_b), _rows(ssd_norm_g)
    dtb, alog = _pad_rows(ssd_dt_bias, DT_LANE0, LANES), _pad_rows(ssd_A_log, DT_LANE0, LANES)
    dsk = _rows(jnp.repeat(ssd_D, SSD_HEAD_DIM, axis=1))

    tm_in, t_seq, t_prep = _tile_sizes(m, seq)
    for layer in range(norm_g.shape[0]):
        cq, ckv, kd, gmla, uconv, gconv, z, xbc = _in_proj(layer, x2, g_in, w_in_k, tm_in)
        q, k, v = _mla_prep(layer, cq, ckv, kd, qan, wq, kvan, wkv, qn, kn, *tabs, bsz, seq, t_prep)
        o_mla = _attention(q, k, v, gmla, bsz, seq, t_seq)
        o_ssd = _ssd(layer, z, xbc, kd, cw, cb, dtb, alog, dsk, ng, bsz, seq, t_seq)
        x2 = _conv_out(layer, uconv, gconv, x2, o_mla, o_ssd, wdw, bdw, lng, lnb, wpw, w_out_k, seq, t_seq)
    return x2.reshape(bsz, seq, d)
```

```python
import functools
import math

import jax
import jax.numpy as jnp
from jax import lax
from jax.experimental import pallas as pl
from jax.experimental.pallas import tpu as pltpu

F32 = jnp.float32
BF16 = jnp.bfloat16

MLA_HEADS = 6
QK_NOPE = 128
QK_ROPE = 64
QK_HEAD = QK_NOPE + QK_ROPE
V_HEAD = 128
Q_LORA = 512
KV_LORA = 256
MLA_WIDTH = MLA_HEADS * V_HEAD
ROPE_THETA = 10000.0
CONV_WIDTH = 512
CONV_K = 31
SSD_HEADS = 12
SSD_HEAD_DIM = 64
SSD_WIDTH = SSD_HEADS * SSD_HEAD_DIM
SSD_GROUPS = 2
SSD_STATE = 128
SSD_CONV_K = 4
SSD_CHUNK = 128
SSD_XBC = SSD_WIDTH + 2 * SSD_GROUPS * SSD_STATE
NORM_EPS = 1e-6
LN_EPS = 1e-5

LANES = 128
SUBLANES = 8
QK_PAD = 2 * LANES
Q_UP = 3 * LANES
KD_WIDTH = LANES
DT_LANE0 = QK_ROPE
CONV_HALO = 32
SSD_HALO = SUBLANES
VMEM_LIMIT = 56 * 1024 * 1024
ATTN_STRIP = 64

SEG_WIDTHS = (Q_LORA, KV_LORA, KD_WIDTH, MLA_WIDTH, 2 * CONV_WIDTH, CONV_WIDTH, SSD_WIDTH, SSD_XBC)


def _sigmoid(x):
    return 1.0 / (1.0 + jnp.exp2(x * (-math.log2(math.e))))


def _silu(x):
    return x * _sigmoid(x)


def _softplus(x):
    return jnp.maximum(x, 0.0) + jnp.log1p(jnp.exp(-jnp.abs(x)))


def _rms(x, g, eps=NORM_EPS):
    ms = jnp.mean(x * x, axis=-1, keepdims=True)
    return x * lax.rsqrt(ms + eps) * g


def _params(*sem):
    return pltpu.CompilerParams(dimension_semantics=sem, vmem_limit_bytes=VMEM_LIMIT)


def _resident(stacked, layer):
    if stacked.ndim == 2:
        return pl.BlockSpec(stacked.shape, lambda *_: (0, 0), pipeline_mode=pl.Buffered(1))
    _, a, b = stacked.shape
    return pl.BlockSpec((None, a, b), lambda *_: (layer, 0, 0), pipeline_mode=pl.Buffered(1))


W_IN_HALF = LANES // 2


def _prep_w_in_kernel(a_ref, b_ref, o_ref, *, depth, n_kt, n_dt):
    j = pl.program_id(0)
    kd_tile = (Q_LORA + KV_LORA) // LANES
    stride = depth * n_kt
    row = lax.broadcasted_iota(jnp.int32, (W_IN_HALF, LANES), 0)
    keep_b = (j != kd_tile) | (row < n_dt)
    for l in range(depth):
        for kt in range(n_kt):
            xa = a_ref[pl.ds(kt * depth + l, W_IN_HALF, stride=stride), :]
            xb = b_ref[pl.ds(kt * depth + l, W_IN_HALF, stride=stride), :]
            x = jnp.concatenate([xa, jnp.where(keep_b, xb, 0.0)], axis=0)
            o_ref[l, kt * LANES:(kt + 1) * LANES, :] = x.T.astype(o_ref.dtype)


def _prep_w_in(w_in):
    depth, d, n = w_in.shape
    n_kt = d // LANES
    n_out = sum(SEG_WIDTHS)
    rows_per_group = W_IN_HALF * n_kt * depth
    view = w_in.transpose(2, 0, 1).reshape(n, depth, n_kt, LANES).transpose(0, 2, 1, 3).reshape(-1, LANES)
    kd_tile = (Q_LORA + KV_LORA) // LANES
    dt_group = (n - SSD_HEADS) // W_IN_HALF
    assert (n - SSD_HEADS) % W_IN_HALF == 0 and (Q_LORA + KV_LORA) % LANES == 0 and QK_ROPE == W_IN_HALF
    a_idx = lambda j: (jnp.where(j <= kd_tile, 2 * j, 2 * j - 1), 0)
    b_idx = lambda j: (jnp.where(j < kd_tile, 2 * j + 1, jnp.where(j == kd_tile, dt_group, 2 * j)), 0)
    return pl.pallas_call(
        functools.partial(_prep_w_in_kernel, depth=depth, n_kt=n_kt, n_dt=SSD_HEADS),
        grid=(n_out // LANES,),
        in_specs=[pl.BlockSpec((rows_per_group, LANES), a_idx),
                  pl.BlockSpec((rows_per_group, LANES), b_idx)],
        out_specs=pl.BlockSpec((depth, d, LANES), lambda j: (0, 0, j)),
        out_shape=jax.ShapeDtypeStruct((depth, d, n_out), BF16),
        compiler_params=_params("parallel"),
        name="prep_w_in",
    )(view, view)


def _prep_small_kernel(wq_ref, wkv_ref, wpw_ref, oq_ref, okv_ref, opw_ref):
    wq = wq_ref[...]
    zq = jnp.zeros((wq.shape[0], QK_PAD - QK_HEAD), F32)
    half = QK_ROPE // 2
    parts = []
    for h in range(MLA_HEADS):
        r0 = h * QK_HEAD + QK_NOPE
        parts += [wq[:, h * QK_HEAD:(h + 1) * QK_HEAD], zq,
                  wq[:, r0 + half:r0 + QK_ROPE], wq[:, r0:r0 + half], zq]
    oq_ref[...] = jnp.concatenate(parts, axis=1).astype(oq_ref.dtype)
    wkv = wkv_ref[...]
    per = QK_NOPE + V_HEAD
    k_cols = [wkv[:, h * per:h * per + QK_NOPE] for h in range(MLA_HEADS)]
    v_cols = [wkv[:, h * per + QK_NOPE:(h + 1) * per] for h in range(MLA_HEADS)]
    okv_ref[...] = jnp.concatenate(k_cols + v_cols, axis=1).astype(okv_ref.dtype)
    opw_ref[...] = wpw_ref[...].astype(opw_ref.dtype)


def _prep_small(w_q_b, w_kv_b, conv_pw_w):
    depth = w_q_b.shape[0]
    whole = lambda a: pl.BlockSpec((None,) + a.shape[1:], lambda l: (l, 0, 0))
    shapes = [(depth, Q_LORA, MLA_HEADS * Q_UP), w_kv_b.shape, conv_pw_w.shape]
    return pl.pallas_call(
        _prep_small_kernel,
        grid=(depth,),
        in_specs=[whole(w_q_b), whole(w_kv_b), whole(conv_pw_w)],
        out_specs=[pl.BlockSpec((None,) + s[1:], lambda l: (l, 0, 0)) for s in shapes],
        out_shape=[jax.ShapeDtypeStruct(s, BF16) for s in shapes],
        compiler_params=_params("parallel"),
        name="prep_small",
    )(w_q_b, w_kv_b, conv_pw_w)


OUT_PIECE = 2 * LANES
CONV_SUB = 32
ANCHOR = 16


def _cast_kernel(w_ref, o_ref):
    for p in range(o_ref.shape[0]):
        o_ref[p] = w_ref[:, p * OUT_PIECE:(p + 1) * OUT_PIECE].astype(o_ref.dtype)


def _prep_cast(w, tk):
    depth, k, n = w.shape
    n_piece = n // OUT_PIECE
    return pl.pallas_call(
        _cast_kernel,
        grid=(depth, k // tk),
        in_specs=[pl.BlockSpec((None, tk, n), lambda l, i: (l, i, 0))],
        out_specs=pl.BlockSpec((None, n_piece, tk, OUT_PIECE), lambda l, i: (l, 0, i, 0)),
        out_shape=jax.ShapeDtypeStruct((depth, n_piece, k, OUT_PIECE), BF16),
        compiler_params=_params("parallel", "parallel"),
        name="prep_cast",
    )(w)


def _in_proj_kernel(x_ref, g_ref, w_ref, *out_refs, layer):
    h = _rms(x_ref[...], g_ref[layer:layer + 1, :]).astype(BF16)
    off = 0
    for o_ref in out_refs:
        n = o_ref.shape[-1]
        o_ref[...] = jnp.dot(h, w_ref[:, off:off + n], preferred_element_type=F32).astype(o_ref.dtype)
        off += n


def _in_proj(layer, x2, g, w, tm):
    m, d = x2.shape
    assert w.shape[2] == sum(SEG_WIDTHS)
    return pl.pallas_call(
        functools.partial(_in_proj_kernel, layer=layer),
        grid=(m // tm,),
        in_specs=[pl.BlockSpec((tm, d), lambda i: (i, 0)),
                  _resident(g, layer),
                  _resident(w, layer)],
        out_specs=[pl.BlockSpec((tm, n), lambda i: (i, 0)) for n in SEG_WIDTHS],
        out_shape=[jax.ShapeDtypeStruct((m, n), F32) for n in SEG_WIDTHS],
        compiler_params=_params("parallel"),
        name="in_proj",
    )(x2, g, w)


def _mla_prep_kernel(cq_ref, ckv_ref, kd_ref, qan_ref, wq_ref, kvan_ref, wkv_ref, qn_ref, kn_ref,
                     cos_ref, sa_ref, sb_ref, q_ref, k_ref, v_ref, *, layer):
    scale = math.log2(math.e) / math.sqrt(QK_HEAD)
    hq = _rms(cq_ref[...], qan_ref[layer:layer + 1, :]).astype(BF16)
    qf = jnp.dot(hq, wq_ref[...], preferred_element_type=F32)
    hkv = _rms(ckv_ref[...], kvan_ref[layer:layer + 1, :]).astype(BF16)
    kvf = jnp.dot(hkv, wkv_ref[...], preferred_element_type=F32)

    cos = cos_ref[...]
    sa = sa_ref[...]
    sb = sb_ref[...]

    def rope(r):
        return r * cos + pltpu.roll(r, LANES - QK_ROPE // 2, 1) * sa + pltpu.roll(r, QK_ROPE // 2, 1) * sb

    kd = kd_ref[...]
    lane = lax.broadcasted_iota(jnp.int32, kd.shape, 1)
    kpe = jnp.where(lane < QK_ROPE, kd, 0.0)
    kpe_ss = jnp.sum(kpe * kpe, axis=-1, keepdims=True)

    qn_a = qn_ref[:, 0:LANES] * scale
    qn_b = qn_ref[:, LANES:QK_PAD] * scale
    qn_s = qn_ref[:, QK_PAD:Q_UP] * scale
    sin_signed = sa + sb
    kn_w = kn_ref[...]
    kpe_rot = rope(kpe * kn_w[:, LANES:QK_PAD])
    for h in range(MLA_HEADS):
        qa = qf[:, h * Q_UP:h * Q_UP + LANES]
        qb = qf[:, h * Q_UP + LANES:h * Q_UP + QK_PAD]
        qs = qf[:, h * Q_UP + QK_PAD:(h + 1) * Q_UP]
        ss = jnp.sum(qa * qa + qb * qb, axis=-1, keepdims=True)
        inv = lax.rsqrt(ss * (1.0 / QK_HEAD) + NORM_EPS)
        q_ref[0, h, :, 0:LANES] = (qa * inv * qn_a).astype(q_ref.dtype)
        q_ref[0, h, :, LANES:QK_PAD] = (inv * (qb * qn_b * cos + qs * qn_s * sin_signed)).astype(q_ref.dtype)

        ka = kvf[:, h * LANES:(h + 1) * LANES]
        ss = jnp.sum(ka * ka, axis=-1, keepdims=True) + kpe_ss
        inv = lax.rsqrt(ss * (1.0 / QK_HEAD) + NORM_EPS)
        k_ref[0, h, :, 0:LANES] = (ka * inv * kn_w[:, 0:LANES]).astype(k_ref.dtype)
        k_ref[0, h, :, LANES:QK_PAD] = (kpe_rot * inv).astype(k_ref.dtype)

        v_ref[0, h, :, :] = kvf[:, MLA_WIDTH + h * V_HEAD:MLA_WIDTH + (h + 1) * V_HEAD].astype(v_ref.dtype)


def _mla_prep(layer, cq, ckv, kd, qan, wq, kvan, wkv, qn, kn, cos_t, sa_t, sb_t, bsz, seq, tm):
    nt = seq // tm
    row = lambda b, i: (b * nt + i, 0)
    pos = lambda b, i: (i, 0)
    hd = MLA_HEADS
    return pl.pallas_call(
        functools.partial(_mla_prep_kernel, layer=layer),
        grid=(bsz, nt),
        in_specs=[pl.BlockSpec((tm, Q_LORA), row),
                  pl.BlockSpec((tm, KV_LORA), row),
                  pl.BlockSpec((tm, KD_WIDTH), row),
                  _resident(qan, layer), _resident(wq, layer), _resident(kvan, layer), _resident(wkv, layer),
                  _resident(qn, layer), _resident(kn, layer),
                  pl.BlockSpec((tm, LANES), pos), pl.BlockSpec((tm, LANES), pos), pl.BlockSpec((tm, LANES), pos)],
        out_specs=[pl.BlockSpec((1, hd, tm, QK_PAD), lambda b, i: (b, 0, i, 0)),
                   pl.BlockSpec((1, hd, tm, QK_PAD), lambda b, i: (b, 0, i, 0)),
                   pl.BlockSpec((1, hd, tm, V_HEAD), lambda b, i: (b, 0, i, 0))],
        out_shape=[jax.ShapeDtypeStruct((bsz, hd, seq, QK_PAD), BF16),
                   jax.ShapeDtypeStruct((bsz, hd, seq, QK_PAD), BF16),
                   jax.ShapeDtypeStruct((bsz, hd, seq, V_HEAD), BF16)],
        compiler_params=_params("parallel", "parallel"),
        name="mla_prep",
    )(cq, ckv, kd, qan, wq, kvan, wkv, qn, kn, cos_t, sa_t, sb_t)


def _attn_kernel(qi_ref, ki_ref, q_ref, k_ref, v_ref, g_ref, o_ref, m_sc, l_sc, acc_sc, s_sc, p_sc, a_sc):
    qi = qi_ref[pl.program_id(1)]
    ki = ki_ref[pl.program_id(1)]
    tq = q_ref.shape[2]
    tk = k_ref.shape[2]

    @pl.when(ki == 0)
    def _():
        m_sc[...] = jnp.full(m_sc.shape, -jnp.inf, F32)
        l_sc[...] = jnp.zeros(l_sc.shape, F32)
        acc_sc[...] = jnp.zeros(acc_sc.shape, F32)

    strip = min(ATTN_STRIP, tq)
    n_chunk = tk // LANES

    def scores(h):
        s_sc[h % 2] = lax.dot_general(q_ref[0, h], k_ref[0, h], (((1,), (1,)), ((), ())),
                                      preferred_element_type=F32)

    def softmax_strip(h, r0, masked):
        slot = h % 2
        rows = slice(r0, r0 + strip)
        live = [c for c in range(n_chunk) if not (masked and c * LANES >= r0 + strip)]
        chunks = []
        for c in live:
            x = s_sc[slot, rows, c * LANES:(c + 1) * LANES]
            if masked and (c + 1) * LANES - 1 > r0:
                rr = r0 + lax.broadcasted_iota(jnp.int32, (strip, LANES), 0)
                cc = c * LANES + lax.broadcasted_iota(jnp.int32, (strip, LANES), 1)
                x = jnp.where(rr >= cc, x, -jnp.inf)
            chunks.append(x)
        m_prev = m_sc[h, rows, :]
        m_loc = functools.reduce(jnp.maximum, chunks)
        m_new = jnp.maximum(m_prev, jnp.max(m_loc, axis=-1, keepdims=True))
        alpha = jnp.exp2(m_prev - m_new)
        ps = [jnp.exp2(x - m_new) for x in chunks]
        l_sc[h, rows, :] = alpha * l_sc[h, rows, :] + functools.reduce(jnp.add, ps)
        m_sc[h, rows, :] = m_new
        a_sc[slot, rows, :] = alpha
        dead = [jnp.zeros((strip, LANES), BF16)] * (n_chunk - len(live))
        p_sc[slot, rows, :] = jnp.concatenate([p.astype(BF16) for p in ps] + dead, axis=1)

    def step(masked):
        scores(0)
        for h in range(MLA_HEADS):
            if h + 1 < MLA_HEADS:
                scores(h + 1)
            for r0 in range(0, tq, strip):
                softmax_strip(h, r0, masked)
            pv = jnp.dot(p_sc[h % 2], v_ref[0, h], preferred_element_type=F32)
            acc_sc[h] = a_sc[h % 2] * acc_sc[h] + pv

    @pl.when(ki < qi)
    def _():
        step(False)

    @pl.when(ki == qi)
    def _():
        step(True)
        for h in range(MLA_HEADS):
            o = acc_sc[h] / jnp.sum(l_sc[h], axis=-1, keepdims=True)
            g = g_ref[:, h * V_HEAD:(h + 1) * V_HEAD]
            o_ref[:, h * V_HEAD:(h + 1) * V_HEAD] = (o * _silu(g)).astype(o_ref.dtype)


def _attention(q, k, v, gmla, bsz, seq, tq):
    hd = MLA_HEADS
    nq = seq // tq
    pairs = [(i, j) for i in range(nq) for j in range(i + 1)]
    qi_tab = jnp.asarray([i for i, _ in pairs], jnp.int32)
    ki_tab = jnp.asarray([j for _, j in pairs], jnp.int32)
    q_idx = lambda b, t, qi, ki: (b, 0, qi[t], 0)
    kv_idx = lambda b, t, qi, ki: (b, 0, ki[t], 0)
    row = lambda b, t, qi, ki: (b * nq + qi[t], 0)
    grid_spec = pltpu.PrefetchScalarGridSpec(
        num_scalar_prefetch=2,
        grid=(bsz, len(pairs)),
        in_specs=[pl.BlockSpec((1, hd, tq, QK_PAD), q_idx),
                  pl.BlockSpec((1, hd, tq, QK_PAD), kv_idx),
                  pl.BlockSpec((1, hd, tq, V_HEAD), kv_idx),
                  pl.BlockSpec((tq, MLA_WIDTH), row)],
        out_specs=pl.BlockSpec((tq, MLA_WIDTH), row),
        scratch_shapes=[pltpu.VMEM((hd, tq, LANES), F32), pltpu.VMEM((hd, tq, LANES), F32),
                        pltpu.VMEM((hd, tq, V_HEAD), F32),
                        pltpu.VMEM((2, tq, tq), F32), pltpu.VMEM((2, tq, tq), BF16),
                        pltpu.VMEM((2, tq, LANES), F32)])
    return pl.pallas_call(
        _attn_kernel,
        grid_spec=grid_spec,
        out_shape=jax.ShapeDtypeStruct((bsz * seq, MLA_WIDTH), BF16),
        compiler_params=_params("parallel", "arbitrary"),
        name="attention",
    )(qi_tab, ki_tab, q, k, v, gmla)


def _split3(x):
    hi = x.astype(BF16)
    r1 = x - hi.astype(F32)
    mid = r1.astype(BF16)
    lo = (r1 - mid.astype(F32)).astype(BF16)
    return hi, mid, lo


def _ssd_kernel(z_ref, xbc_ref, kd_ref, cw_ref, cb_ref, dtb_ref, alog_ref, dsk_ref, ng_ref, o_ref,
                xbuf, xact, state, shx, acol_sc, xdtb_sc, xdte_sc, grow_sc, y_sc, *, layer):
    i = pl.program_id(0)
    nb, ts = o_ref.shape[0], o_ref.shape[1]
    L = SSD_CHUNK
    hp = SSD_HEAD_DIM
    gw = SSD_WIDTH // SSD_GROUPS
    pairs_per_group = SSD_HEADS // SSD_GROUPS // 2
    npair = SSD_HEADS // 2

    @pl.when(i == 0)
    def _():
        xbuf[:, 0:SSD_HALO, :] = jnp.zeros((nb, SSD_HALO, SSD_XBC), F32)
        state[...] = jnp.zeros(state.shape, F32)

    @pl.when(i > 0)
    def _():
        xbuf[:, 0:SSD_HALO, :] = xbuf[:, ts:ts + SSD_HALO, :]

    for b in range(nb):
        xbuf[b, SSD_HALO:SSD_HALO + ts, :] = xbc_ref[b]
        for s in range(1, SSD_CONV_K):
            shx[s - 1, :, :] = xbuf[b, SSD_HALO - s:SSD_HALO - s + ts, :]
        acc = cb_ref[layer:layer + 1, :] + cw_ref[SSD_CONV_K - 1:SSD_CONV_K, :] * xbc_ref[b]
        for s in range(1, SSD_CONV_K):
            acc = acc + cw_ref[SSD_CONV_K - 1 - s:SSD_CONV_K - s, :] * shx[s - 1]
        xact[b] = _silu(acc)

    lane = lax.broadcasted_iota(jnp.int32, (1, LANES), 1)
    is_dt = (lane >= DT_LANE0) & (lane < DT_LANE0 + SSD_HEADS)
    a_neg = jnp.where(is_dt, -jnp.exp(alog_ref[...]), 0.0)
    rr = lax.broadcasted_iota(jnp.int32, (L, L), 0)
    cc = lax.broadcasted_iota(jnp.int32, (L, L), 1)
    causal = rr >= cc
    tri = causal.astype(BF16)
    left = lax.broadcasted_iota(jnp.int32, (L, LANES), 1) < hp

    def decay_geometry(b, r0):
        dt = _softplus(kd_ref[b, pl.ds(r0, L), :] + dtb_ref[...])
        hi, mid, lo = _split3(dt * a_neg)
        acs = (jnp.dot(tri, hi, preferred_element_type=F32)
               + jnp.dot(tri, mid, preferred_element_type=F32)
               + jnp.dot(tri, lo, preferred_element_type=F32))
        return dict(dt=dt, acs=acs, acs_t=acs.T)

    def expand(b, r0, v, sl):
        def col(a, h):
            return jnp.broadcast_to(a[:, DT_LANE0 + h:DT_LANE0 + h + 1], (L, LANES))

        lasts = []
        for j in range(npair):
            lanes = slice(j * LANES, (j + 1) * LANES)
            a0, a1 = col(v["acs"], 2 * j), col(v["acs"], 2 * j + 1)
            acol_sc[sl,2 * j] = a0
            acol_sc[sl,2 * j + 1] = a1
            acs_pair = jnp.where(left, a0, a1)
            dt_pair = jnp.where(left, col(v["dt"], 2 * j), col(v["dt"], 2 * j + 1))
            last = acs_pair[L - 1:L, :]
            xdt = xact[b, pl.ds(r0, L), lanes] * dt_pair
            xdtb_sc[sl,:, lanes] = xdt.astype(BF16)
            xdte_sc[sl,:, lanes] = (xdt * jnp.exp(last - acs_pair)).astype(BF16)
            grow_sc[sl,:, lanes] = jnp.exp(acs_pair)
            lasts.append(last)
        v["lasts"] = lasts

    def mix(b, r0, v, sl):
        ssq = jnp.zeros((L, LANES), F32)
        for g in range(SSD_GROUPS):
            c0 = SSD_WIDTH + g * SSD_STATE
            c1 = SSD_WIDTH + (SSD_GROUPS + g) * SSD_STATE
            grp = slice(g * gw, (g + 1) * gw)
            bg = xact[b, pl.ds(r0, L), c0:c0 + SSD_STATE]
            cg_b = xact[b, pl.ds(r0, L), c1:c1 + SSD_STATE].astype(BF16)
            cb = lax.dot_general(cg_b, bg.astype(BF16), (((1,), (1,)), ((), ())), preferred_element_type=F32)
            st_prev = state[b, :, grp]
            st_new = jnp.dot(bg.T.astype(BF16), xdte_sc[sl,:, grp], preferred_element_type=F32)
            y_off = jnp.dot(cg_b, st_prev.astype(BF16), preferred_element_type=F32) * grow_sc[sl,:, grp]
            last_g = jnp.concatenate(v["lasts"][g * pairs_per_group:(g + 1) * pairs_per_group], axis=1)
            state[b, :, grp] = st_prev * jnp.exp(last_g) + st_new
            for jj in range(pairs_per_group):
                j = g * pairs_per_group + jj
                lanes = slice(j * LANES, (j + 1) * LANES)
                pair = xdtb_sc[sl,:, lanes]
                yd = []
                for h in (2 * j, 2 * j + 1):
                    seg = acol_sc[sl,h] - v["acs_t"][DT_LANE0 + h:DT_LANE0 + h + 1, :]
                    decay = jnp.exp(jnp.where(causal, seg, -jnp.inf))
                    yd.append(jnp.dot((cb * decay).astype(BF16), pair, preferred_element_type=F32))
                y = (jnp.where(left, yd[0], yd[1]) + y_off[:, jj * LANES:(jj + 1) * LANES]
                     + dsk_ref[:, lanes] * xact[b, pl.ds(r0, L), lanes])
                yz = y * _silu(z_ref[b, pl.ds(r0, L), lanes])
                y_sc[sl, :, lanes] = yz
                ssq = ssq + yz * yz
        v["inv"] = lax.rsqrt(jnp.sum(ssq, axis=-1, keepdims=True) * (1.0 / SSD_WIDTH) + NORM_EPS)

    def finish(b, r0, v, sl):
        o_ref[b, pl.ds(r0, L), :] = (y_sc[sl] * v["inv"] * ng_ref[layer:layer + 1, :]).astype(o_ref.dtype)

    jobs = [(b, k * L, k * nb + b) for k in range(ts // L) for b in range(nb)]
    vals = [decay_geometry(b, r0) for b, r0, _ in jobs]
    for stage in (expand, mix, finish):
        for (b, r0, sl), v in zip(jobs, vals):
            stage(b, r0, v, sl)


def _ssd(layer, z, xbc, kd, cw, cb, dtb, alog, dsk, ng, bsz, seq, ts):
    blk = lambda w: pl.BlockSpec((bsz, ts, w), lambda i: (0, i, 0))
    slots = bsz * (ts // SSD_CHUNK)
    out = pl.pallas_call(
        functools.partial(_ssd_kernel, layer=layer),
        grid=(seq // ts,),
        in_specs=[blk(SSD_WIDTH), blk(SSD_XBC), blk(KD_WIDTH),
                  _resident(cw, layer), _resident(cb, layer), _resident(dtb, layer), _resident(alog, layer),
                  _resident(dsk, layer), _resident(ng, layer)],
        out_specs=blk(SSD_WIDTH),
        out_shape=jax.ShapeDtypeStruct((bsz, seq, SSD_WIDTH), BF16),
        scratch_shapes=[pltpu.VMEM((bsz, ts + SSD_HALO, SSD_XBC), F32), pltpu.VMEM((bsz, ts, SSD_XBC), F32),
                        pltpu.VMEM((bsz, SSD_STATE, SSD_WIDTH), F32),
                        pltpu.VMEM((SSD_CONV_K - 1, ts, SSD_XBC), F32),
                        pltpu.VMEM((slots, SSD_HEADS, SSD_CHUNK, LANES), F32),
                        pltpu.VMEM((slots, SSD_CHUNK, SSD_WIDTH), BF16), pltpu.VMEM((slots, SSD_CHUNK, SSD_WIDTH), BF16),
                        pltpu.VMEM((slots, SSD_CHUNK, SSD_WIDTH), F32), pltpu.VMEM((slots, SSD_CHUNK, SSD_WIDTH), F32)],
        compiler_params=_params("arbitrary"),
        name="ssd",
    )(z.reshape(bsz, seq, -1), xbc.reshape(bsz, seq, -1), kd.reshape(bsz, seq, -1), cw, cb, dtb, alog, dsk, ng)
    return out.reshape(bsz * seq, SSD_WIDTH)


def _conv_out_kernel(u_ref, gc_ref, x_ref, a_ref, s_ref, wdw_ref, bdw_ref, lng_ref, lnb_ref, wpw_ref, w_ref,
                     o_ref, hbuf, shbuf, ybuf, oc_buf, *, tiles_per_seq, layer):
    i = pl.program_id(0)
    tt = u_ref.shape[0]
    cw = CONV_WIDTH
    n_piece = w_ref.shape[0]
    seq_start = lax.rem(i, tiles_per_seq) == 0

    @pl.when(i == 0)
    def _():
        oc_buf[...] = jnp.zeros(oc_buf.shape, oc_buf.dtype)

    @pl.when(seq_start)
    def _():
        hbuf[0:CONV_HALO, :] = jnp.zeros((CONV_HALO, cw), F32)

    @pl.when(jnp.logical_not(seq_start))
    def _():
        hbuf[0:CONV_HALO, :] = hbuf[tt:tt + CONV_HALO, :]

    n0 = a_ref.shape[1]
    n1 = n0 + cw
    n2 = n1 + s_ref.shape[1]
    base = CONV_HALO - (CONV_K - 1)
    span = tt + CONV_HALO - SUBLANES
    dyn0 = pl.multiple_of(jnp.minimum(i, 0), ANCHOR)

    def fold(v):
        rows = functools.reduce(jnp.add, [v[r:r + ANCHOR, :] for r in range(0, v.shape[0], ANCHOR)])
        return functools.reduce(jnp.add, [rows[:, c:c + LANES] for c in range(0, v.shape[1], LANES)])

    def piece(p):
        cols = slice(p * OUT_PIECE, (p + 1) * OUT_PIECE)
        y = jnp.dot(oc_buf[pl.ds(dyn0, tt), :], w_ref[p, n0:n1, :], preferred_element_type=F32)
        y = y + jnp.dot(a_ref[...], w_ref[p, 0:n0, :], preferred_element_type=F32)
        y = y + jnp.dot(s_ref[...], w_ref[p, n1:n2, :], preferred_element_type=F32)
        o_ref[:, cols] = x_ref[:, cols] + y

    def glu():
        hval = u_ref[:, 0:cw] * _sigmoid(u_ref[:, cw:2 * cw])
        hbuf[CONV_HALO:CONV_HALO + tt, :] = hval
        return fold(hval)

    def shifts(residues):
        for res in residues:
            for r0 in range(0, span, CONV_SUB):
                rows = min(CONV_SUB, span - r0)
                shbuf[res - 1, r0:r0 + rows, :] = hbuf[res + r0:res + r0 + rows, :]

    def strips(first, count):
        mark = None
        for r0 in range(first, first + count * CONV_SUB, CONV_SUB):
            acc = jnp.broadcast_to(bdw_ref[layer:layer + 1, :], (CONV_SUB, cw))
            for kk in range(CONV_K):
                res = (base + kk) % SUBLANES
                start = r0 + (base + kk - res)
                tap = hbuf[start:start + CONV_SUB, :] if res == 0 else shbuf[res - 1, start:start + CONV_SUB, :]
                acc = acc + wdw_ref[kk:kk + 1, :] * tap
            ybuf[r0:r0 + CONV_SUB, :] = acc
            part = fold(acc)
            mark = part if mark is None else mark + part
        return mark

    def anchor(mark):
        oc_buf[pl.ds(tt + dyn0, ANCHOR), 0:LANES] = mark.astype(oc_buf.dtype)

    n_free = 3
    n_groups = n_piece - n_free - 1
    per = tt // CONV_SUB // n_groups
    piece(0)
    anchor(glu())
    piece(1)
    shifts(range(1, SUBLANES))
    for p in range(2, n_free + 1):
        piece(p)
    for k in range(n_groups):
        anchor(strips(k * per * CONV_SUB, per))
        piece(n_free + 1 + k)

    y = ybuf[...]
    mu = jnp.mean(y, axis=-1, keepdims=True)
    yc = y - mu
    var = jnp.mean(yc * yc, axis=-1, keepdims=True)
    hn = _silu(yc * lax.rsqrt(var + LN_EPS) * lng_ref[layer:layer + 1, :] + lnb_ref[layer:layer + 1, :])
    out = jnp.dot(hn.astype(BF16), wpw_ref[...], preferred_element_type=F32)
    oc_buf[0:tt, :] = (out * _silu(gc_ref[...])).astype(oc_buf.dtype)


def _conv_out(layer, uconv, gconv, x2, o_mla, o_ssd, wdw, bdw, lng, lnb, wpw, w_out, seq, tt):
    m, d = x2.shape
    n = m // tt
    n_piece = w_out.shape[1]
    cur_row = lambda i: (jnp.minimum(i, n - 1), 0)
    prev_row = lambda i: (jnp.maximum(i - 1, 0), 0)
    w_spec = pl.BlockSpec((None,) + w_out.shape[1:], lambda i: (layer, 0, 0, 0), pipeline_mode=pl.Buffered(1))
    return pl.pallas_call(
        functools.partial(_conv_out_kernel, tiles_per_seq=seq // tt, layer=layer),
        grid=(n + 1,),
        in_specs=[pl.BlockSpec((tt, 2 * CONV_WIDTH), cur_row),
                  pl.BlockSpec((tt, CONV_WIDTH), cur_row),
                  pl.BlockSpec((tt, d), prev_row),
                  pl.BlockSpec((tt, o_mla.shape[1]), prev_row),
                  pl.BlockSpec((tt, o_ssd.shape[1]), prev_row),
                  _resident(wdw, layer), _resident(bdw, layer), _resident(lng, layer), _resident(lnb, layer),
                  _resident(wpw, layer), w_spec],
        out_specs=pl.BlockSpec((tt, d), prev_row),
        out_shape=jax.ShapeDtypeStruct((m, d), F32),
        scratch_shapes=[pltpu.VMEM((tt + CONV_HALO, CONV_WIDTH), F32),
                        pltpu.VMEM((SUBLANES - 1, tt + CONV_HALO, CONV_WIDTH), F32),
                        pltpu.VMEM((tt, CONV_WIDTH), F32),
                        pltpu.VMEM((tt + ANCHOR, CONV_WIDTH), BF16)],
        compiler_params=_params("arbitrary"),
        name="conv_out_proj",
    )(uconv, gconv, x2, o_mla, o_ssd, wdw, bdw, lng, lnb, wpw, w_out)


def _rows(p):
    return p.astype(F32)


def _pad_rows(p, lane0, width):
    return jnp.pad(p.astype(F32), ((0, 0), (lane0, width - lane0 - p.shape[1])))[:, None, :]


ROW_TILE = 512


def _tile_sizes(m, seq):
    tm_in, t_seq = min(ROW_TILE, m), min(ROW_TILE, seq)
    t_prep = min(2 * ROW_TILE, seq)
    assert m % tm_in == 0 and seq % t_seq == 0 and t_seq % SSD_CHUNK == 0 and t_seq % (4 * CONV_SUB) == 0
    assert seq % t_prep == 0
    return tm_in, t_seq, t_prep


def _rope_tables(seq):
    half = QK_ROPE // 2
    inv_freq = ROPE_THETA ** (-jnp.arange(half, dtype=F32) / half)
    ang = jnp.arange(seq).astype(F32)[:, None] * inv_freq[None, :]
    cos, sin = jnp.cos(ang), jnp.sin(ang)
    z = lambda n: jnp.zeros((seq, n), F32)
    cos_t = jnp.concatenate([cos, cos, z(LANES - QK_ROPE)], axis=1)
    sa_t = jnp.concatenate([-sin, z(LANES - half)], axis=1)
    sb_t = jnp.concatenate([z(half), sin, z(LANES - QK_ROPE)], axis=1)
    return cos_t, sa_t, sb_t


def kernel(x, norm_g, w_in, q_a_norm, w_q_b, kv_a_norm, w_kv_b, q_norm, k_norm, conv_dw_w, conv_dw_b, conv_ln_g,
           conv_ln_b, conv_pw_w, ssd_conv_w, ssd_conv_b, ssd_dt_bias, ssd_A_log, ssd_D, ssd_norm_g, w_out):
    bsz, seq, d = x.shape
    m = bsz * seq
    x2 = x.reshape(m, d)
    tabs = _rope_tables(seq)

    w_in_k = _prep_w_in(w_in)
    wq, wkv, wpw = _prep_small(w_q_b, w_kv_b, conv_pw_w)
    w_out_k = _prep_cast(w_out, min(512, w_out.shape[1]))
    g_in, qan, kvan = _rows(norm_g), _rows(q_a_norm), _rows(kv_a_norm)
    kn = _pad_rows(k_norm, 0, QK_PAD)
    half = QK_ROPE // 2
    q_swapped = jnp.concatenate([q_norm[:, QK_NOPE + half:], q_norm[:, QK_NOPE:QK_NOPE + half]], axis=1)
    qn = jnp.concatenate([_pad_rows(q_norm, 0, QK_PAD), _pad_rows(q_swapped, 0, LANES)], axis=2)
    wdw = jnp.pad(conv_dw_w.astype(F32), ((0, 0), (0, CONV_HALO - CONV_K), (0, 0)))
    bdw, lng, lnb = _rows(conv_dw_b), _rows(conv_ln_g), _rows(conv_ln_b)
    cw = jnp.pad(ssd_conv_w.astype(F32), ((0, 0), (0, SUBLANES - SSD_CONV_K), (0, 0)))
    cb, ng = _rows(ssd_conv_b), _rows(ssd_norm_g)
    dtb, alog = _pad_rows(ssd_dt_bias, DT_LANE0, LANES), _pad_rows(ssd_A_log, DT_LANE0, LANES)
    dsk = jnp.repeat(ssd_D, SSD_HEAD_DIM, axis=1).astype(F32)[:, None, :]

    tm_in, t_seq, t_prep = _tile_sizes(m, seq)
    for layer in range(norm_g.shape[0]):
        cq, ckv, kd, gmla, uconv, gconv, z, xbc = _in_proj(layer, x2, g_in, w_in_k, tm_in)
        q, k, v = _mla_prep(layer, cq, ckv, kd, qan, wq, kvan, wkv, qn, kn, *tabs, bsz, seq, t_prep)
        o_mla = _attention(q, k, v, gmla, bsz, seq, t_seq)
        o_ssd = _ssd(layer, z, xbc, kd, cw, cb, dtb, alog, dsk, ng, bsz, seq, t_seq)
        x2 = _conv_out(layer, uconv, gconv, x2, o_mla, o_ssd, wdw, bdw, lng, lnb, wpw, w_out_k, seq, t_seq)
    return x2.reshape(bsz, seq, d)
```

```python
import functools
import math

import jax
import jax.numpy as jnp
from jax import lax
from jax.experimental import pallas as pl
from jax.experimental.pallas import tpu as pltpu

F32 = jnp.float32
BF16 = jnp.bfloat16

MLA_HEADS = 6
QK_NOPE = 128
QK_ROPE = 64
QK_HEAD = QK_NOPE + QK_ROPE
V_HEAD = 128
Q_LORA = 512
KV_LORA = 256
MLA_WIDTH = MLA_HEADS * V_HEAD
ROPE_THETA = 10000.0
CONV_WIDTH = 512
CONV_K = 31
SSD_HEADS = 12
SSD_HEAD_DIM = 64
SSD_WIDTH = SSD_HEADS * SSD_HEAD_DIM
SSD_GROUPS = 2
SSD_STATE = 128
SSD_CONV_K = 4
SSD_CHUNK = 128
SSD_XBC = SSD_WIDTH + 2 * SSD_GROUPS * SSD_STATE
NORM_EPS = 1e-6
LN_EPS = 1e-5

LANES = 128
SUBLANES = 8
QK_PAD = 2 * LANES
Q_UP = 3 * LANES
KD_WIDTH = LANES
DT_LANE0 = QK_ROPE
CONV_HALO = 32
SSD_HALO = SUBLANES
VMEM_LIMIT = 56 * 1024 * 1024
ATTN_STRIP = 64

SEG_WIDTHS = (Q_LORA, KV_LORA, KD_WIDTH, MLA_WIDTH, 2 * CONV_WIDTH, CONV_WIDTH, SSD_WIDTH, SSD_XBC)


def _sigmoid(x):
    return 1.0 / (1.0 + jnp.exp2(x * (-math.log2(math.e))))


def _silu(x):
    return x * _sigmoid(x)


def _softplus(x):
    return jnp.maximum(x, 0.0) + jnp.log1p(jnp.exp(-jnp.abs(x)))


def _rms(x, g, eps=NORM_EPS):
    ms = jnp.mean(x * x, axis=-1, keepdims=True)
    return x * lax.rsqrt(ms + eps) * g


def _params(*sem):
    return pltpu.CompilerParams(dimension_semantics=sem, vmem_limit_bytes=VMEM_LIMIT)


def _resident(stacked, layer):
    if stacked.ndim == 2:
        return pl.BlockSpec(stacked.shape, lambda *_: (0, 0), pipeline_mode=pl.Buffered(1))
    _, a, b = stacked.shape
    return pl.BlockSpec((None, a, b), lambda *_: (layer, 0, 0), pipeline_mode=pl.Buffered(1))


W_IN_HALF = LANES // 2


def _prep_w_in_kernel(a_ref, b_ref, o_ref, *, depth, n_kt, n_dt):
    j = pl.program_id(0)
    kd_tile = (Q_LORA + KV_LORA) // LANES
    stride = depth * n_kt
    row = lax.broadcasted_iota(jnp.int32, (W_IN_HALF, LANES), 0)
    keep_b = (j != kd_tile) | (row < n_dt)
    for l in range(depth):
        for kt in range(n_kt):
            xa = a_ref[pl.ds(kt * depth + l, W_IN_HALF, stride=stride), :]
            xb = b_ref[pl.ds(kt * depth + l, W_IN_HALF, stride=stride), :]
            x = jnp.concatenate([xa, jnp.where(keep_b, xb, 0.0)], axis=0)
            o_ref[l, kt * LANES:(kt + 1) * LANES, :] = x.T.astype(o_ref.dtype)


def _prep_w_in(w_in):
    depth, d, n = w_in.shape
    n_kt = d // LANES
    n_out = sum(SEG_WIDTHS)
    rows_per_group = W_IN_HALF * n_kt * depth
    view = w_in.transpose(2, 0, 1).reshape(n, depth, n_kt, LANES).transpose(0, 2, 1, 3).reshape(-1, LANES)
    kd_tile = (Q_LORA + KV_LORA) // LANES
    dt_group = (n - SSD_HEADS) // W_IN_HALF
    assert (n - SSD_HEADS) % W_IN_HALF == 0 and (Q_LORA + KV_LORA) % LANES == 0 and QK_ROPE == W_IN_HALF
    a_idx = lambda j: (jnp.where(j <= kd_tile, 2 * j, 2 * j - 1), 0)
    b_idx = lambda j: (jnp.where(j < kd_tile, 2 * j + 1, jnp.where(j == kd_tile, dt_group, 2 * j)), 0)
    return pl.pallas_call(
        functools.partial(_prep_w_in_kernel, depth=depth, n_kt=n_kt, n_dt=SSD_HEADS),
        grid=(n_out // LANES,),
        in_specs=[pl.BlockSpec((rows_per_group, LANES), a_idx),
                  pl.BlockSpec((rows_per_group, LANES), b_idx)],
        out_specs=pl.BlockSpec((depth, d, LANES), lambda j: (0, 0, j)),
        out_shape=jax.ShapeDtypeStruct((depth, d, n_out), BF16),
        compiler_params=_params("parallel"),
        name="prep_w_in",
    )(view, view)


def _prep_small_kernel(wq_ref, wkv_ref, wpw_ref, oq_ref, okv_ref, opw_ref):
    wq = wq_ref[...]
    zq = jnp.zeros((wq.shape[0], QK_PAD - QK_HEAD), F32)
    half = QK_ROPE // 2
    parts = []
    for h in range(MLA_HEADS):
        r0 = h * QK_HEAD + QK_NOPE
        parts += [wq[:, h * QK_HEAD:(h + 1) * QK_HEAD], zq,
                  wq[:, r0 + half:r0 + QK_ROPE], wq[:, r0:r0 + half], zq]
    oq_ref[...] = jnp.concatenate(parts, axis=1).astype(oq_ref.dtype)
    wkv = wkv_ref[...]
    per = QK_NOPE + V_HEAD
    k_cols = [wkv[:, h * per:h * per + QK_NOPE] for h in range(MLA_HEADS)]
    v_cols = [wkv[:, h * per + QK_NOPE:(h + 1) * per] for h in range(MLA_HEADS)]
    okv_ref[...] = jnp.concatenate(k_cols + v_cols, axis=1).astype(okv_ref.dtype)
    opw_ref[...] = wpw_ref[...].astype(opw_ref.dtype)


def _prep_small(w_q_b, w_kv_b, conv_pw_w):
    depth = w_q_b.shape[0]
    whole = lambda a: pl.BlockSpec((None,) + a.shape[1:], lambda l: (l, 0, 0))
    shapes = [(depth, Q_LORA, MLA_HEADS * Q_UP), w_kv_b.shape, conv_pw_w.shape]
    return pl.pallas_call(
        _prep_small_kernel,
        grid=(depth,),
        in_specs=[whole(w_q_b), whole(w_kv_b), whole(conv_pw_w)],
        out_specs=[pl.BlockSpec((None,) + s[1:], lambda l: (l, 0, 0)) for s in shapes],
        out_shape=[jax.ShapeDtypeStruct(s, BF16) for s in shapes],
        compiler_params=_params("parallel"),
        name="prep_small",
    )(w_q_b, w_kv_b, conv_pw_w)


OUT_PIECE = 2 * LANES
CONV_SUB = 32
ANCHOR = 16


def _cast_kernel(w_ref, o_ref):
    for p in range(o_ref.shape[0]):
        o_ref[p] = w_ref[:, p * OUT_PIECE:(p + 1) * OUT_PIECE].astype(o_ref.dtype)


def _prep_cast(w, tk):
    depth, k, n = w.shape
    n_piece = n // OUT_PIECE
    return pl.pallas_call(
        _cast_kernel,
        grid=(depth, k // tk),
        in_specs=[pl.BlockSpec((None, tk, n), lambda l, i: (l, i, 0))],
        out_specs=pl.BlockSpec((None, n_piece, tk, OUT_PIECE), lambda l, i: (l, 0, i, 0)),
        out_shape=jax.ShapeDtypeStruct((depth, n_piece, k, OUT_PIECE), BF16),
        compiler_params=_params("parallel", "parallel"),
        name="prep_cast",
    )(w)


def _in_proj_kernel(x_ref, g_ref, w_ref, *out_refs, layer):
    h = _rms(x_ref[...], g_ref[layer:layer + 1, :]).astype(BF16)
    off = 0
    for o_ref in out_refs:
        n = o_ref.shape[-1]
        o_ref[...] = jnp.dot(h, w_ref[:, off:off + n], preferred_element_type=F32).astype(o_ref.dtype)
        off += n


def _in_proj(layer, x2, g, w, tm):
    m, d = x2.shape
    assert w.shape[2] == sum(SEG_WIDTHS)
    return pl.pallas_call(
        functools.partial(_in_proj_kernel, layer=layer),
        grid=(m // tm,),
        in_specs=[pl.BlockSpec((tm, d), lambda i: (i, 0)),
                  _resident(g, layer),
                  _resident(w, layer)],
        out_specs=[pl.BlockSpec((tm, n), lambda i: (i, 0)) for n in SEG_WIDTHS],
        out_shape=[jax.ShapeDtypeStruct((m, n), F32) for n in SEG_WIDTHS],
        compiler_params=_params("parallel"),
        name="in_proj",
    )(x2, g, w)


def _mla_prep_kernel(cq_ref, ckv_ref, kd_ref, qan_ref, wq_ref, kvan_ref, wkv_ref, qn_ref, kn_ref,
                     cos_ref, sa_ref, sb_ref, q_ref, k_ref, v_ref, *, layer):
    scale = math.log2(math.e) / math.sqrt(QK_HEAD)
    hq = _rms(cq_ref[...], qan_ref[layer:layer + 1, :]).astype(BF16)
    qf = jnp.dot(hq, wq_ref[...], preferred_element_type=F32)
    hkv = _rms(ckv_ref[...], kvan_ref[layer:layer + 1, :]).astype(BF16)
    kvf = jnp.dot(hkv, wkv_ref[...], preferred_element_type=F32)

    cos = cos_ref[...]
    sa = sa_ref[...]
    sb = sb_ref[...]

    def rope(r):
        return r * cos + pltpu.roll(r, LANES - QK_ROPE // 2, 1) * sa + pltpu.roll(r, QK_ROPE // 2, 1) * sb

    kd = kd_ref[...]
    lane = lax.broadcasted_iota(jnp.int32, kd.shape, 1)
    kpe = jnp.where(lane < QK_ROPE, kd, 0.0)
    kpe_ss = jnp.sum(kpe * kpe, axis=-1, keepdims=True)

    qn_a = qn_ref[:, 0:LANES] * scale
    qn_b = qn_ref[:, LANES:QK_PAD] * scale
    qn_s = qn_ref[:, QK_PAD:Q_UP] * scale
    sin_signed = sa + sb
    kn_w = kn_ref[...]
    kpe_rot = rope(kpe * kn_w[:, LANES:QK_PAD])
    for h in range(MLA_HEADS):
        qa = qf[:, h * Q_UP:h * Q_UP + LANES]
        qb = qf[:, h * Q_UP + LANES:h * Q_UP + QK_PAD]
        qs = qf[:, h * Q_UP + QK_PAD:(h + 1) * Q_UP]
        ss = jnp.sum(qa * qa + qb * qb, axis=-1, keepdims=True)
        inv = lax.rsqrt(ss * (1.0 / QK_HEAD) + NORM_EPS)
        q_ref[0, h, :, 0:LANES] = (qa * inv * qn_a).astype(q_ref.dtype)
        q_ref[0, h, :, LANES:QK_PAD] = (inv * (qb * qn_b * cos + qs * qn_s * sin_signed)).astype(q_ref.dtype)

        ka = kvf[:, h * LANES:(h + 1) * LANES]
        ss = jnp.sum(ka * ka, axis=-1, keepdims=True) + kpe_ss
        inv = lax.rsqrt(ss * (1.0 / QK_HEAD) + NORM_EPS)
        k_ref[0, h, :, 0:LANES] = (ka * inv * kn_w[:, 0:LANES]).astype(k_ref.dtype)
        k_ref[0, h, :, LANES:QK_PAD] = (kpe_rot * inv).astype(k_ref.dtype)

        v_ref[0, h, :, :] = kvf[:, MLA_WIDTH + h * V_HEAD:MLA_WIDTH + (h + 1) * V_HEAD].astype(v_ref.dtype)


def _mla_prep(layer, cq, ckv, kd, qan, wq, kvan, wkv, qn, kn, cos_t, sa_t, sb_t, bsz, seq, tm):
    nt = seq // tm
    row = lambda b, i: (b * nt + i, 0)
    pos = lambda b, i: (i, 0)
    hd = MLA_HEADS
    return pl.pallas_call(
        functools.partial(_mla_prep_kernel, layer=layer),
        grid=(bsz, nt),
        in_specs=[pl.BlockSpec((tm, Q_LORA), row),
                  pl.BlockSpec((tm, KV_LORA), row),
                  pl.BlockSpec((tm, KD_WIDTH), row),
                  _resident(qan, layer), _resident(wq, layer), _resident(kvan, layer), _resident(wkv, layer),
                  _resident(qn, layer), _resident(kn, layer),
                  pl.BlockSpec((tm, LANES), pos), pl.BlockSpec((tm, LANES), pos), pl.BlockSpec((tm, LANES), pos)],
        out_specs=[pl.BlockSpec((1, hd, tm, QK_PAD), lambda b, i: (b, 0, i, 0)),
                   pl.BlockSpec((1, hd, tm, QK_PAD), lambda b, i: (b, 0, i, 0)),
                   pl.BlockSpec((1, hd, tm, V_HEAD), lambda b, i: (b, 0, i, 0))],
        out_shape=[jax.ShapeDtypeStruct((bsz, hd, seq, QK_PAD), BF16),
                   jax.ShapeDtypeStruct((bsz, hd, seq, QK_PAD), BF16),
                   jax.ShapeDtypeStruct((bsz, hd, seq, V_HEAD), BF16)],
        compiler_params=_params("parallel", "parallel"),
        name="mla_prep",
    )(cq, ckv, kd, qan, wq, kvan, wkv, qn, kn, cos_t, sa_t, sb_t)


def _attn_kernel(qi_ref, ki_ref, q_ref, k_ref, v_ref, g_ref, o_ref, m_sc, l_sc, acc_sc, s_sc, p_sc, a_sc):
    qi = qi_ref[pl.program_id(1)]
    ki = ki_ref[pl.program_id(1)]
    tq = q_ref.shape[2]
    tk = k_ref.shape[2]

    @pl.when(ki == 0)
    def _():
        m_sc[...] = jnp.full(m_sc.shape, -jnp.inf, F32)
        l_sc[...] = jnp.zeros(l_sc.shape, F32)
        acc_sc[...] = jnp.zeros(acc_sc.shape, F32)

    strip = min(ATTN_STRIP, tq)
    n_chunk = tk // LANES

    def scores(h):
        s_sc[h % 2] = lax.dot_general(q_ref[0, h], k_ref[0, h], (((1,), (1,)), ((), ())),
                                      preferred_element_type=F32)

    def softmax_strip(h, r0, masked):
        slot = h % 2
        rows = slice(r0, r0 + strip)
        live = [c for c in range(n_chunk) if not (masked and c * LANES >= r0 + strip)]
        chunks = []
        for c in live:
            x = s_sc[slot, rows, c * LANES:(c + 1) * LANES]
            if masked and (c + 1) * LANES - 1 > r0:
                rr = r0 + lax.broadcasted_iota(jnp.int32, (strip, LANES), 0)
                cc = c * LANES + lax.broadcasted_iota(jnp.int32, (strip, LANES), 1)
                x = jnp.where(rr >= cc, x, -jnp.inf)
            chunks.append(x)
        m_prev = m_sc[h, rows, :]
        m_loc = functools.reduce(jnp.maximum, chunks)
        m_new = jnp.maximum(m_prev, jnp.max(m_loc, axis=-1, keepdims=True))
        alpha = jnp.exp2(m_prev - m_new)
        ps = [jnp.exp2(x - m_new) for x in chunks]
        l_sc[h, rows, :] = alpha * l_sc[h, rows, :] + functools.reduce(jnp.add, ps)
        m_sc[h, rows, :] = m_new
        a_sc[slot, rows, :] = alpha
        dead = [jnp.zeros((strip, LANES), BF16)] * (n_chunk - len(live))
        p_sc[slot, rows, :] = jnp.concatenate([p.astype(BF16) for p in ps] + dead, axis=1)

    def step(masked):
        scores(0)
        for h in range(MLA_HEADS):
            if h + 1 < MLA_HEADS:
                scores(h + 1)
            for r0 in range(0, tq, strip):
                softmax_strip(h, r0, masked)
            pv = jnp.dot(p_sc[h % 2], v_ref[0, h], preferred_element_type=F32)
            acc_sc[h] = a_sc[h % 2] * acc_sc[h] + pv

    @pl.when(ki < qi)
    def _():
        step(False)

    @pl.when(ki == qi)
    def _():
        step(True)
        for h in range(MLA_HEADS):
            o = acc_sc[h] / jnp.sum(l_sc[h], axis=-1, keepdims=True)
            g = g_ref[:, h * V_HEAD:(h + 1) * V_HEAD]
            o_ref[:, h * V_HEAD:(h + 1) * V_HEAD] = (o * _silu(g)).astype(o_ref.dtype)


def _attention(q, k, v, gmla, bsz, seq, tq):
    hd = MLA_HEADS
    nq = seq // tq
    pairs = [(i, j) for i in range(nq) for j in range(i + 1)]
    qi_tab = jnp.asarray([i for i, _ in pairs], jnp.int32)
    ki_tab = jnp.asarray([j for _, j in pairs], jnp.int32)
    q_idx = lambda b, t, qi, ki: (b, 0, qi[t], 0)
    kv_idx = lambda b, t, qi, ki: (b, 0, ki[t], 0)
    row = lambda b, t, qi, ki: (b * nq + qi[t], 0)
    grid_spec = pltpu.PrefetchScalarGridSpec(
        num_scalar_prefetch=2,
        grid=(bsz, len(pairs)),
        in_specs=[pl.BlockSpec((1, hd, tq, QK_PAD), q_idx),
                  pl.BlockSpec((1, hd, tq, QK_PAD), kv_idx),
                  pl.BlockSpec((1, hd, tq, V_HEAD), kv_idx),
                  pl.BlockSpec((tq, MLA_WIDTH), row)],
        out_specs=pl.BlockSpec((tq, MLA_WIDTH), row),
        scratch_shapes=[pltpu.VMEM((hd, tq, LANES), F32), pltpu.VMEM((hd, tq, LANES), F32),
                        pltpu.VMEM((hd, tq, V_HEAD), F32),
                        pltpu.VMEM((2, tq, tq), F32), pltpu.VMEM((2, tq, tq), BF16),
                        pltpu.VMEM((2, tq, LANES), F32)])
    return pl.pallas_call(
        _attn_kernel,
        grid_spec=grid_spec,
        out_shape=jax.ShapeDtypeStruct((bsz * seq, MLA_WIDTH), BF16),
        compiler_params=_params("parallel", "arbitrary"),
        name="attention",
    )(qi_tab, ki_tab, q, k, v, gmla)


def _split3(x):
    hi = x.astype(BF16)
    r1 = x - hi.astype(F32)
    mid = r1.astype(BF16)
    lo = (r1 - mid.astype(F32)).astype(BF16)
    return hi, mid, lo


def _ssd_kernel(z_ref, xbc_ref, kd_ref, cw_ref, cb_ref, dtb_ref, alog_ref, dsk_ref, ng_ref, o_ref,
                xbuf, xact, state, shx, acol_sc, xdtb_sc, xdte_sc, grow_sc, y_sc, *, layer):
    i = pl.program_id(0)
    nb, ts = o_ref.shape[0], o_ref.shape[1]
    L = SSD_CHUNK
    hp = SSD_HEAD_DIM
    gw = SSD_WIDTH // SSD_GROUPS
    pairs_per_group = SSD_HEADS // SSD_GROUPS // 2
    npair = SSD_HEADS // 2

    @pl.when(i == 0)
    def _():
        xbuf[:, 0:SSD_HALO, :] = jnp.zeros((nb, SSD_HALO, SSD_XBC), F32)
        state[...] = jnp.zeros(state.shape, F32)

    @pl.when(i > 0)
    def _():
        xbuf[:, 0:SSD_HALO, :] = xbuf[:, ts:ts + SSD_HALO, :]

    for b in range(nb):
        xbuf[b, SSD_HALO:SSD_HALO + ts, :] = xbc_ref[b]
        for s in range(1, SSD_CONV_K):
            shx[s - 1, :, :] = xbuf[b, SSD_HALO - s:SSD_HALO - s + ts, :]
        acc = cb_ref[layer:layer + 1, :] + cw_ref[SSD_CONV_K - 1:SSD_CONV_K, :] * xbc_ref[b]
        for s in range(1, SSD_CONV_K):
            acc = acc + cw_ref[SSD_CONV_K - 1 - s:SSD_CONV_K - s, :] * shx[s - 1]
        xact[b] = _silu(acc)

    lane = lax.broadcasted_iota(jnp.int32, (1, LANES), 1)
    is_dt = (lane >= DT_LANE0) & (lane < DT_LANE0 + SSD_HEADS)
    a_neg = jnp.where(is_dt, -jnp.exp(alog_ref[...]) * math.log2(math.e), 0.0)
    rr = lax.broadcasted_iota(jnp.int32, (L, L), 0)
    cc = lax.broadcasted_iota(jnp.int32, (L, L), 1)
    causal = rr >= cc
    tri = causal.astype(BF16)
    left = lax.broadcasted_iota(jnp.int32, (L, LANES), 1) < hp

    def decay_geometry(b, r0):
        dt = _softplus(kd_ref[b, pl.ds(r0, L), :] + dtb_ref[...])
        hi, mid, lo = _split3(dt * a_neg)
        acs = (jnp.dot(tri, hi, preferred_element_type=F32)
               + jnp.dot(tri, mid, preferred_element_type=F32)
               + jnp.dot(tri, lo, preferred_element_type=F32))
        return dict(dt=dt, acs=acs, acs_t=acs.T)

    def expand(b, r0, v, sl):
        def col(a, h):
            return jnp.broadcast_to(a[:, DT_LANE0 + h:DT_LANE0 + h + 1], (L, LANES))

        lasts = []
        for j in range(npair):
            lanes = slice(j * LANES, (j + 1) * LANES)
            a0, a1 = col(v["acs"], 2 * j), col(v["acs"], 2 * j + 1)
            acol_sc[sl,2 * j] = a0
            acol_sc[sl,2 * j + 1] = a1
            acs_pair = jnp.where(left, a0, a1)
            dt_pair = jnp.where(left, col(v["dt"], 2 * j), col(v["dt"], 2 * j + 1))
            last = acs_pair[L - 1:L, :]
            xdt = xact[b, pl.ds(r0, L), lanes] * dt_pair
            xdtb_sc[sl,:, lanes] = xdt.astype(BF16)
            xdte_sc[sl,:, lanes] = (xdt * jnp.exp2(last - acs_pair)).astype(BF16)
            grow_sc[sl,:, lanes] = jnp.exp2(acs_pair)
            lasts.append(last)
        v["lasts"] = lasts

    def mix(b, r0, v, sl):
        ssq = jnp.zeros((L, LANES), F32)
        for g in range(SSD_GROUPS):
            c0 = SSD_WIDTH + g * SSD_STATE
            c1 = SSD_WIDTH + (SSD_GROUPS + g) * SSD_STATE
            grp = slice(g * gw, (g + 1) * gw)
            bg = xact[b, pl.ds(r0, L), c0:c0 + SSD_STATE]
            cg_b = xact[b, pl.ds(r0, L), c1:c1 + SSD_STATE].astype(BF16)
            cb = lax.dot_general(cg_b, bg.astype(BF16), (((1,), (1,)), ((), ())), preferred_element_type=F32)
            st_prev = state[b, :, grp]
            st_new = jnp.dot(bg.T.astype(BF16), xdte_sc[sl,:, grp], preferred_element_type=F32)
            y_off = jnp.dot(cg_b, st_prev.astype(BF16), preferred_element_type=F32) * grow_sc[sl,:, grp]
            last_g = jnp.concatenate(v["lasts"][g * pairs_per_group:(g + 1) * pairs_per_group], axis=1)
            state[b, :, grp] = st_prev * jnp.exp2(last_g) + st_new
            for jj in range(pairs_per_group):
                j = g * pairs_per_group + jj
                lanes = slice(j * LANES, (j + 1) * LANES)
                pair = xdtb_sc[sl,:, lanes]
                yd = []
                for h in (2 * j, 2 * j + 1):
                    seg = acol_sc[sl,h] - v["acs_t"][DT_LANE0 + h:DT_LANE0 + h + 1, :]
                    decay = jnp.exp2(jnp.where(causal, seg, -jnp.inf))
                    yd.append(jnp.dot((cb * decay).astype(BF16), pair, preferred_element_type=F32))
                y = (jnp.where(left, yd[0], yd[1]) + y_off[:, jj * LANES:(jj + 1) * LANES]
                     + dsk_ref[:, lanes] * xact[b, pl.ds(r0, L), lanes])
                yz = y * _silu(z_ref[b, pl.ds(r0, L), lanes])
                y_sc[sl, :, lanes] = yz
                ssq = ssq + yz * yz
        v["inv"] = lax.rsqrt(jnp.sum(ssq, axis=-1, keepdims=True) * (1.0 / SSD_WIDTH) + NORM_EPS)

    def finish(b, r0, v, sl):
        o_ref[b, pl.ds(r0, L), :] = (y_sc[sl] * v["inv"] * ng_ref[layer:layer + 1, :]).astype(o_ref.dtype)

    jobs = [(b, k * L, k * nb + b) for k in range(ts // L) for b in range(nb)]
    vals = [decay_geometry(b, r0) for b, r0, _ in jobs]
    for stage in (expand, mix, finish):
        for (b, r0, sl), v in zip(jobs, vals):
            stage(b, r0, v, sl)


def _ssd(layer, z, xbc, kd, cw, cb, dtb, alog, dsk, ng, bsz, seq, ts):
    blk = lambda w: pl.BlockSpec((bsz, ts, w), lambda i: (0, i, 0))
    slots = bsz * (ts // SSD_CHUNK)
    out = pl.pallas_call(
        functools.partial(_ssd_kernel, layer=layer),
        grid=(seq // ts,),
        in_specs=[blk(SSD_WIDTH), blk(SSD_XBC), blk(KD_WIDTH),
                  _resident(cw, layer), _resident(cb, layer), _resident(dtb, layer), _resident(alog, layer),
                  _resident(dsk, layer), _resident(ng, layer)],
        out_specs=blk(SSD_WIDTH),
        out_shape=jax.ShapeDtypeStruct((bsz, seq, SSD_WIDTH), BF16),
        scratch_shapes=[pltpu.VMEM((bsz, ts + SSD_HALO, SSD_XBC), F32), pltpu.VMEM((bsz, ts, SSD_XBC), F32),
                        pltpu.VMEM((bsz, SSD_STATE, SSD_WIDTH), F32),
                        pltpu.VMEM((SSD_CONV_K - 1, ts, SSD_XBC), F32),
                        pltpu.VMEM((slots, SSD_HEADS, SSD_CHUNK, LANES), F32),
                        pltpu.VMEM((slots, SSD_CHUNK, SSD_WIDTH), BF16), pltpu.VMEM((slots, SSD_CHUNK, SSD_WIDTH), BF16),
                        pltpu.VMEM((slots, SSD_CHUNK, SSD_WIDTH), F32), pltpu.VMEM((slots, SSD_CHUNK, SSD_WIDTH), F32)],
        compiler_params=_params("arbitrary"),
        name="ssd",
    )(z.reshape(bsz, seq, -1), xbc.reshape(bsz, seq, -1), kd.reshape(bsz, seq, -1), cw, cb, dtb, alog, dsk, ng)
    return out.reshape(bsz * seq, SSD_WIDTH)


def _conv_out_kernel(u_ref, gc_ref, x_ref, a_ref, s_ref, wdw_ref, bdw_ref, lng_ref, lnb_ref, wpw_ref, w_ref,
                     o_ref, hbuf, shbuf, ybuf, oc_buf, *, tiles_per_seq, layer):
    i = pl.program_id(0)
    tt = u_ref.shape[0]
    cw = CONV_WIDTH
    n_piece = w_ref.shape[0]
    seq_start = lax.rem(i, tiles_per_seq) == 0

    @pl.when(i == 0)
    def _():
        oc_buf[...] = jnp.zeros(oc_buf.shape, oc_buf.dtype)

    @pl.when(seq_start)
    def _():
        hbuf[0:CONV_HALO, :] = jnp.zeros((CONV_HALO, cw), F32)

    @pl.when(jnp.logical_not(seq_start))
    def _():
        hbuf[0:CONV_HALO, :] = hbuf[tt:tt + CONV_HALO, :]

    n0 = a_ref.shape[1]
    n1 = n0 + cw
    n2 = n1 + s_ref.shape[1]
    base = CONV_HALO - (CONV_K - 1)
    span = tt + CONV_HALO - SUBLANES
    dyn0 = pl.multiple_of(jnp.minimum(i, 0), ANCHOR)

    def fold(v):
        rows = functools.reduce(jnp.add, [v[r:r + ANCHOR, :] for r in range(0, v.shape[0], ANCHOR)])
        return functools.reduce(jnp.add, [rows[:, c:c + LANES] for c in range(0, v.shape[1], LANES)])

    def piece(p):
        cols = slice(p * OUT_PIECE, (p + 1) * OUT_PIECE)
        y = jnp.dot(oc_buf[pl.ds(dyn0, tt), :], w_ref[p, n0:n1, :], preferred_element_type=F32)
        y = y + jnp.dot(a_ref[...], w_ref[p, 0:n0, :], preferred_element_type=F32)
        y = y + jnp.dot(s_ref[...], w_ref[p, n1:n2, :], preferred_element_type=F32)
        o_ref[:, cols] = x_ref[:, cols] + y

    def glu():
        hval = u_ref[:, 0:cw] * _sigmoid(u_ref[:, cw:2 * cw])
        hbuf[CONV_HALO:CONV_HALO + tt, :] = hval
        return fold(hval)

    def shifts(residues):
        for res in residues:
            for r0 in range(0, span, CONV_SUB):
                rows = min(CONV_SUB, span - r0)
                shbuf[res - 1, r0:r0 + rows, :] = hbuf[res + r0:res + r0 + rows, :]

    def strips(first, count):
        mark = None
        for r0 in range(first, first + count * CONV_SUB, CONV_SUB):
            acc = jnp.broadcast_to(bdw_ref[layer:layer + 1, :], (CONV_SUB, cw))
            for kk in range(CONV_K):
                res = (base + kk) % SUBLANES
                start = r0 + (base + kk - res)
                tap = hbuf[start:start + CONV_SUB, :] if res == 0 else shbuf[res - 1, start:start + CONV_SUB, :]
                acc = acc + wdw_ref[kk:kk + 1, :] * tap
            ybuf[r0:r0 + CONV_SUB, :] = acc
            part = fold(acc)
            mark = part if mark is None else mark + part
        return mark

    def anchor(mark):
        oc_buf[pl.ds(tt + dyn0, ANCHOR), 0:LANES] = mark.astype(oc_buf.dtype)

    n_free = 3
    n_groups = n_piece - n_free - 1
    per = tt // CONV_SUB // n_groups
    piece(0)
    anchor(glu())
    piece(1)
    shifts(range(1, SUBLANES))
    for p in range(2, n_free + 1):
        piece(p)
    for k in range(n_groups):
        anchor(strips(k * per * CONV_SUB, per))
        piece(n_free + 1 + k)

    y = ybuf[...]
    mu = jnp.mean(y, axis=-1, keepdims=True)
    yc = y - mu
    var = jnp.mean(yc * yc, axis=-1, keepdims=True)
    hn = _silu(yc * lax.rsqrt(var + LN_EPS) * lng_ref[layer:layer + 1, :] + lnb_ref[layer:layer + 1, :])
    out = jnp.dot(hn.astype(BF16), wpw_ref[...], preferred_element_type=F32)
    oc_buf[0:tt, :] = (out * _silu(gc_ref[...])).astype(oc_buf.dtype)


def _conv_out(layer, uconv, gconv, x2, o_mla, o_ssd, wdw, bdw, lng, lnb, wpw, w_out, seq, tt):
    m, d = x2.shape
    n = m // tt
    n_piece = w_out.shape[1]
    cur_row = lambda i: (jnp.minimum(i, n - 1), 0)
    prev_row = lambda i: (jnp.maximum(i - 1, 0), 0)
    w_spec = pl.BlockSpec((None,) + w_out.shape[1:], lambda i: (layer, 0, 0, 0), pipeline_mode=pl.Buffered(1))
    return pl.pallas_call(
        functools.partial(_conv_out_kernel, tiles_per_seq=seq // tt, layer=layer),
        grid=(n + 1,),
        in_specs=[pl.BlockSpec((tt, 2 * CONV_WIDTH), cur_row),
                  pl.BlockSpec((tt, CONV_WIDTH), cur_row),
                  pl.BlockSpec((tt, d), prev_row),
                  pl.BlockSpec((tt, o_mla.shape[1]), prev_row),
                  pl.BlockSpec((tt, o_ssd.shape[1]), prev_row),
                  _resident(wdw, layer), _resident(bdw, layer), _resident(lng, layer), _resident(lnb, layer),
                  _resident(wpw, layer), w_spec],
        out_specs=pl.BlockSpec((tt, d), prev_row),
        out_shape=jax.ShapeDtypeStruct((m, d), F32),
        scratch_shapes=[pltpu.VMEM((tt + CONV_HALO, CONV_WIDTH), F32),
                        pltpu.VMEM((SUBLANES - 1, tt + CONV_HALO, CONV_WIDTH), F32),
                        pltpu.VMEM((tt, CONV_WIDTH), F32),
                        pltpu.VMEM((tt + ANCHOR, CONV_WIDTH), BF16)],
        compiler_params=_params("arbitrary"),
        name="conv_out_proj",
    )(uconv, gconv, x2, o_mla, o_ssd, wdw, bdw, lng, lnb, wpw, w_out)


def _rows(p):
    return p.astype(F32)


def _pad_rows(p, lane0, width):
    return jnp.pad(p.astype(F32), ((0, 0), (lane0, width - lane0 - p.shape[1])))[:, None, :]


ROW_TILE = 512


def _tile_sizes(m, seq):
    tm_in, t_seq = min(ROW_TILE, m), min(ROW_TILE, seq)
    t_prep = min(2 * ROW_TILE, seq)
    assert m % tm_in == 0 and seq % t_seq == 0 and t_seq % SSD_CHUNK == 0 and t_seq % (4 * CONV_SUB) == 0
    assert seq % t_prep == 0
    return tm_in, t_seq, t_prep


def _rope_tables(seq):
    half = QK_ROPE // 2
    inv_freq = ROPE_THETA ** (-jnp.arange(half, dtype=F32) / half)
    ang = jnp.arange(seq).astype(F32)[:, None] * inv_freq[None, :]
    cos, sin = jnp.cos(ang), jnp.sin(ang)
    z = lambda n: jnp.zeros((seq, n), F32)
    cos_t = jnp.concatenate([cos, cos, z(LANES - QK_ROPE)], axis=1)
    sa_t = jnp.concatenate([-sin, z(LANES - half)], axis=1)
    sb_t = jnp.concatenate([z(half), sin, z(LANES - QK_ROPE)], axis=1)
    return cos_t, sa_t, sb_t


def kernel(x, norm_g, w_in, q_a_norm, w_q_b, kv_a_norm, w_kv_b, q_norm, k_norm, conv_dw_w, conv_dw_b, conv_ln_g,
           conv_ln_b, conv_pw_w, ssd_conv_w, ssd_conv_b, ssd_dt_bias, ssd_A_log, ssd_D, ssd_norm_g, w_out):
    bsz, seq, d = x.shape
    m = bsz * seq
    x2 = x.reshape(m, d)
    tabs = _rope_tables(seq)

    w_in_k = _prep_w_in(w_in)
    wq, wkv, wpw = _prep_small(w_q_b, w_kv_b, conv_pw_w)
    w_out_k = _prep_cast(w_out, min(512, w_out.shape[1]))
    g_in, qan, kvan = _rows(norm_g), _rows(q_a_norm), _rows(kv_a_norm)
    kn = _pad_rows(k_norm, 0, QK_PAD)
    half = QK_ROPE // 2
    q_swapped = jnp.concatenate([q_norm[:, QK_NOPE + half:], q_norm[:, QK_NOPE:QK_NOPE + half]], axis=1)
    qn = jnp.concatenate([_pad_rows(q_norm, 0, QK_PAD), _pad_rows(q_swapped, 0, LANES)], axis=2)
    wdw = jnp.pad(conv_dw_w.astype(F32), ((0, 0), (0, CONV_HALO - CONV_K), (0, 0)))
    bdw, lng, lnb = _rows(conv_dw_b), _rows(conv_ln_g), _rows(conv_ln_b)
    cw = ssd_conv_w.astype(F32)
    cb, ng = _rows(ssd_conv_b), _rows(ssd_norm_g)
    dtb, alog = _pad_rows(ssd_dt_bias, DT_LANE0, LANES), _pad_rows(ssd_A_log, DT_LANE0, LANES)
    dsk = jnp.repeat(ssd_D, SSD_HEAD_DIM, axis=1).astype(F32)[:, None, :]

    tm_in, t_seq, t_prep = _tile_sizes(m, seq)
    for layer in range(norm_g.shape[0]):
        cq, ckv, kd, gmla, uconv, gconv, z, xbc = _in_proj(layer, x2, g_in, w_in_k, tm_in)
        q, k, v = _mla_prep(layer, cq, ckv, kd, qan, wq, kvan, wkv, qn, kn, *tabs, bsz, seq, t_prep)
        o_mla = _attention(q, k, v, gmla, bsz, seq, t_seq)
        o_ssd = _ssd(layer, z, xbc, kd, cw, cb, dtb, alog, dsk, ng, bsz, seq, t_seq)
        x2 = _conv_out(layer, uconv, gconv, x2, o_mla, o_ssd, wdw, bdw, lng, lnb, wpw, w_out_k, seq, t_seq)
    return x2.reshape(bsz, seq, d)
```

```python
import functools
import math

import jax
import jax.numpy as jnp
import numpy as np
from jax import lax
from jax.experimental import pallas as pl
from jax.experimental.pallas import tpu as pltpu

F32 = jnp.float32
BF16 = jnp.bfloat16

MLA_HEADS = 6
QK_NOPE = 128
QK_ROPE = 64
QK_HEAD = QK_NOPE + QK_ROPE
V_HEAD = 128
Q_LORA = 512
KV_LORA = 256
MLA_WIDTH = MLA_HEADS * V_HEAD
ROPE_THETA = 10000.0
CONV_WIDTH = 512
CONV_K = 31
SSD_HEADS = 12
SSD_HEAD_DIM = 64
SSD_WIDTH = SSD_HEADS * SSD_HEAD_DIM
SSD_GROUPS = 2
SSD_STATE = 128
SSD_CONV_K = 4
SSD_CHUNK = 128
SSD_XBC = SSD_WIDTH + 2 * SSD_GROUPS * SSD_STATE
NORM_EPS = 1e-6
LN_EPS = 1e-5

LANES = 128
SUBLANES = 8
QK_PAD = 2 * LANES
Q_UP = 3 * LANES
KD_WIDTH = LANES
DT_LANE0 = QK_ROPE
CONV_HALO = 32
SSD_HALO = SUBLANES
VMEM_LIMIT = 56 * 1024 * 1024
ATTN_STRIP = 64

SEG_WIDTHS = (Q_LORA, KV_LORA, KD_WIDTH, MLA_WIDTH, 2 * CONV_WIDTH, CONV_WIDTH, SSD_WIDTH, SSD_XBC)


def _sigmoid(x):
    return 1.0 / (1.0 + jnp.exp2(x * (-math.log2(math.e))))


def _silu(x):
    return x * _sigmoid(x)


def _softplus(x):
    return jnp.maximum(x, 0.0) + jnp.log1p(jnp.exp(-jnp.abs(x)))


def _rms(x, g, eps=NORM_EPS):
    ms = jnp.mean(x * x, axis=-1, keepdims=True)
    return x * lax.rsqrt(ms + eps) * g


def _params(*sem):
    return pltpu.CompilerParams(dimension_semantics=sem, vmem_limit_bytes=VMEM_LIMIT)


def _resident(stacked, layer):
    if stacked.ndim == 2:
        return pl.BlockSpec(stacked.shape, lambda *_: (0, 0), pipeline_mode=pl.Buffered(1))
    _, a, b = stacked.shape
    return pl.BlockSpec((None, a, b), lambda *_: (layer, 0, 0), pipeline_mode=pl.Buffered(1))


W_IN_HALF = LANES // 2


def _prep_w_in_kernel(a_ref, b_ref, o_ref, *, depth, n_kt, n_dt):
    j = pl.program_id(0)
    kd_tile = (Q_LORA + KV_LORA) // LANES
    stride = depth * n_kt
    row = lax.broadcasted_iota(jnp.int32, (W_IN_HALF, LANES), 0)
    keep_b = (j != kd_tile) | (row < n_dt)
    for l in range(depth):
        for kt in range(n_kt):
            xa = a_ref[pl.ds(kt * depth + l, W_IN_HALF, stride=stride), :]
            xb = b_ref[pl.ds(kt * depth + l, W_IN_HALF, stride=stride), :]
            x = jnp.concatenate([xa, jnp.where(keep_b, xb, 0.0)], axis=0)
            o_ref[l, kt * LANES:(kt + 1) * LANES, :] = x.T.astype(o_ref.dtype)


def _prep_w_in(w_in):
    depth, d, n = w_in.shape
    n_kt = d // LANES
    n_out = sum(SEG_WIDTHS)
    rows_per_group = W_IN_HALF * n_kt * depth
    view = w_in.transpose(2, 0, 1).reshape(n, depth, n_kt, LANES).transpose(0, 2, 1, 3).reshape(-1, LANES)
    kd_tile = (Q_LORA + KV_LORA) // LANES
    dt_group = (n - SSD_HEADS) // W_IN_HALF
    assert (n - SSD_HEADS) % W_IN_HALF == 0 and (Q_LORA + KV_LORA) % LANES == 0 and QK_ROPE == W_IN_HALF
    a_idx = lambda j: (jnp.where(j <= kd_tile, 2 * j, 2 * j - 1), 0)
    b_idx = lambda j: (jnp.where(j < kd_tile, 2 * j + 1, jnp.where(j == kd_tile, dt_group, 2 * j)), 0)
    return pl.pallas_call(
        functools.partial(_prep_w_in_kernel, depth=depth, n_kt=n_kt, n_dt=SSD_HEADS),
        grid=(n_out // LANES,),
        in_specs=[pl.BlockSpec((rows_per_group, LANES), a_idx),
                  pl.BlockSpec((rows_per_group, LANES), b_idx)],
        out_specs=pl.BlockSpec((depth, d, LANES), lambda j: (0, 0, j)),
        out_shape=jax.ShapeDtypeStruct((depth, d, n_out), BF16),
        compiler_params=_params("parallel"),
        name="prep_w_in",
    )(view, view)


def _prep_small_kernel(wq_ref, wkv_ref, wpw_ref, oq_ref, okv_ref, opw_ref):
    wq = wq_ref[...]
    zq = jnp.zeros((wq.shape[0], QK_PAD - QK_HEAD), F32)
    half = QK_ROPE // 2
    parts = []
    for h in range(MLA_HEADS):
        r0 = h * QK_HEAD + QK_NOPE
        parts += [wq[:, h * QK_HEAD:(h + 1) * QK_HEAD], zq,
                  wq[:, r0 + half:r0 + QK_ROPE], wq[:, r0:r0 + half], zq]
    oq_ref[...] = jnp.concatenate(parts, axis=1).astype(oq_ref.dtype)
    wkv = wkv_ref[...]
    per = QK_NOPE + V_HEAD
    k_cols = [wkv[:, h * per:h * per + QK_NOPE] for h in range(MLA_HEADS)]
    v_cols = [wkv[:, h * per + QK_NOPE:(h + 1) * per] for h in range(MLA_HEADS)]
    okv_ref[...] = jnp.concatenate(k_cols + v_cols, axis=1).astype(okv_ref.dtype)
    opw_ref[...] = wpw_ref[...].astype(opw_ref.dtype)


def _prep_small(w_q_b, w_kv_b, conv_pw_w):
    depth = w_q_b.shape[0]
    whole = lambda a: pl.BlockSpec((None,) + a.shape[1:], lambda l: (l, 0, 0))
    shapes = [(depth, Q_LORA, MLA_HEADS * Q_UP), w_kv_b.shape, conv_pw_w.shape]
    return pl.pallas_call(
        _prep_small_kernel,
        grid=(depth,),
        in_specs=[whole(w_q_b), whole(w_kv_b), whole(conv_pw_w)],
        out_specs=[pl.BlockSpec((None,) + s[1:], lambda l: (l, 0, 0)) for s in shapes],
        out_shape=[jax.ShapeDtypeStruct(s, BF16) for s in shapes],
        compiler_params=_params("parallel"),
        name="prep_small",
    )(w_q_b, w_kv_b, conv_pw_w)


OUT_PIECE = 2 * LANES
CONV_SUB = 32
ANCHOR = 16


def _cast_kernel(w_ref, o_ref):
    for p in range(o_ref.shape[0]):
        o_ref[p] = w_ref[:, p * OUT_PIECE:(p + 1) * OUT_PIECE].astype(o_ref.dtype)


def _prep_cast(w, tk):
    depth, k, n = w.shape
    n_piece = n // OUT_PIECE
    return pl.pallas_call(
        _cast_kernel,
        grid=(depth, k // tk),
        in_specs=[pl.BlockSpec((None, tk, n), lambda l, i: (l, i, 0))],
        out_specs=pl.BlockSpec((None, n_piece, tk, OUT_PIECE), lambda l, i: (l, 0, i, 0)),
        out_shape=jax.ShapeDtypeStruct((depth, n_piece, k, OUT_PIECE), BF16),
        compiler_params=_params("parallel", "parallel"),
        name="prep_cast",
    )(w)


def _in_proj_kernel(x_ref, g_ref, w_ref, *out_refs, layer):
    h = _rms(x_ref[...], g_ref[layer:layer + 1, :]).astype(BF16)
    off = 0
    for o_ref in out_refs:
        n = o_ref.shape[-1]
        o_ref[...] = jnp.dot(h, w_ref[:, off:off + n], preferred_element_type=F32).astype(o_ref.dtype)
        off += n


def _in_proj(layer, x2, g, w, tm):
    m, d = x2.shape
    assert w.shape[2] == sum(SEG_WIDTHS)
    return pl.pallas_call(
        functools.partial(_in_proj_kernel, layer=layer),
        grid=(m // tm,),
        in_specs=[pl.BlockSpec((tm, d), lambda i: (i, 0)),
                  _resident(g, layer),
                  _resident(w, layer)],
        out_specs=[pl.BlockSpec((tm, n), lambda i: (i, 0)) for n in SEG_WIDTHS],
        out_shape=[jax.ShapeDtypeStruct((m, n), F32) for n in SEG_WIDTHS],
        compiler_params=_params("parallel"),
        name="in_proj",
    )(x2, g, w)


def _mla_prep_kernel(cq_ref, ckv_ref, kd_ref, qan_ref, wq_ref, kvan_ref, wkv_ref, qn_ref, kn_ref,
                     cos_ref, sa_ref, sb_ref, q_ref, k_ref, v_ref, *, layer):
    scale = math.log2(math.e) / math.sqrt(QK_HEAD)
    hq = _rms(cq_ref[...], qan_ref[layer:layer + 1, :]).astype(BF16)
    qf = jnp.dot(hq, wq_ref[...], preferred_element_type=F32)
    hkv = _rms(ckv_ref[...], kvan_ref[layer:layer + 1, :]).astype(BF16)
    kvf = jnp.dot(hkv, wkv_ref[...], preferred_element_type=F32)

    cos = cos_ref[...]
    sa = sa_ref[...]
    sb = sb_ref[...]

    def rope(r):
        return r * cos + pltpu.roll(r, LANES - QK_ROPE // 2, 1) * sa + pltpu.roll(r, QK_ROPE // 2, 1) * sb

    kd = kd_ref[...]
    lane = lax.broadcasted_iota(jnp.int32, kd.shape, 1)
    kpe = jnp.where(lane < QK_ROPE, kd, 0.0)
    kpe_ss = jnp.sum(kpe * kpe, axis=-1, keepdims=True)

    qn_a = qn_ref[:, 0:LANES] * scale
    qn_b = qn_ref[:, LANES:QK_PAD] * scale
    qn_s = qn_ref[:, QK_PAD:Q_UP] * scale
    sin_signed = sa + sb
    kn_w = kn_ref[...]
    kpe_rot = rope(kpe * kn_w[:, LANES:QK_PAD])
    for h in range(MLA_HEADS):
        qa = qf[:, h * Q_UP:h * Q_UP + LANES]
        qb = qf[:, h * Q_UP + LANES:h * Q_UP + QK_PAD]
        qs = qf[:, h * Q_UP + QK_PAD:(h + 1) * Q_UP]
        ss = jnp.sum(qa * qa + qb * qb, axis=-1, keepdims=True)
        inv = lax.rsqrt(ss * (1.0 / QK_HEAD) + NORM_EPS)
        q_ref[0, h, :, 0:LANES] = (qa * inv * qn_a).astype(q_ref.dtype)
        q_ref[0, h, :, LANES:QK_PAD] = (inv * (qb * qn_b * cos + qs * qn_s * sin_signed)).astype(q_ref.dtype)

        ka = kvf[:, h * LANES:(h + 1) * LANES]
        ss = jnp.sum(ka * ka, axis=-1, keepdims=True) + kpe_ss
        inv = lax.rsqrt(ss * (1.0 / QK_HEAD) + NORM_EPS)
        k_ref[0, h, :, 0:LANES] = (ka * inv * kn_w[:, 0:LANES]).astype(k_ref.dtype)
        k_ref[0, h, :, LANES:QK_PAD] = (kpe_rot * inv).astype(k_ref.dtype)

        v_ref[0, h, :, :] = kvf[:, MLA_WIDTH + h * V_HEAD:MLA_WIDTH + (h + 1) * V_HEAD].astype(v_ref.dtype)


def _mla_prep(layer, cq, ckv, kd, qan, wq, kvan, wkv, qn, kn, cos_t, sa_t, sb_t, bsz, seq, tm):
    nt = seq // tm
    row = lambda b, i: (b * nt + i, 0)
    pos = lambda b, i: (i, 0)
    hd = MLA_HEADS
    return pl.pallas_call(
        functools.partial(_mla_prep_kernel, layer=layer),
        grid=(bsz, nt),
        in_specs=[pl.BlockSpec((tm, Q_LORA), row),
                  pl.BlockSpec((tm, KV_LORA), row),
                  pl.BlockSpec((tm, KD_WIDTH), row),
                  _resident(qan, layer), _resident(wq, layer), _resident(kvan, layer), _resident(wkv, layer),
                  _resident(qn, layer), _resident(kn, layer),
                  pl.BlockSpec((tm, LANES), pos), pl.BlockSpec((tm, LANES), pos), pl.BlockSpec((tm, LANES), pos)],
        out_specs=[pl.BlockSpec((1, hd, tm, QK_PAD), lambda b, i: (b, 0, i, 0)),
                   pl.BlockSpec((1, hd, tm, QK_PAD), lambda b, i: (b, 0, i, 0)),
                   pl.BlockSpec((1, hd, tm, V_HEAD), lambda b, i: (b, 0, i, 0))],
        out_shape=[jax.ShapeDtypeStruct((bsz, hd, seq, QK_PAD), BF16),
                   jax.ShapeDtypeStruct((bsz, hd, seq, QK_PAD), BF16),
                   jax.ShapeDtypeStruct((bsz, hd, seq, V_HEAD), BF16)],
        compiler_params=_params("parallel", "parallel"),
        name="mla_prep",
    )(cq, ckv, kd, qan, wq, kvan, wkv, qn, kn, cos_t, sa_t, sb_t)


def _attn_kernel(qi_ref, ki_ref, q_ref, k_ref, v_ref, g_ref, o_ref, m_sc, l_sc, acc_sc, s_sc, p_sc, a_sc):
    qi = qi_ref[pl.program_id(1)]
    ki = ki_ref[pl.program_id(1)]
    tq = q_ref.shape[2]
    tk = k_ref.shape[2]

    @pl.when(ki == 0)
    def _():
        m_sc[...] = jnp.full(m_sc.shape, -jnp.inf, F32)
        l_sc[...] = jnp.zeros(l_sc.shape, F32)
        acc_sc[...] = jnp.zeros(acc_sc.shape, F32)

    strip = min(ATTN_STRIP, tq)
    n_chunk = tk // LANES

    def scores(h):
        s_sc[h % 2] = lax.dot_general(q_ref[0, h], k_ref[0, h], (((1,), (1,)), ((), ())),
                                      preferred_element_type=F32)

    def softmax_strip(h, r0, masked):
        slot = h % 2
        rows = slice(r0, r0 + strip)
        live = [c for c in range(n_chunk) if not (masked and c * LANES >= r0 + strip)]
        chunks = []
        for c in live:
            x = s_sc[slot, rows, c * LANES:(c + 1) * LANES]
            if masked and (c + 1) * LANES - 1 > r0:
                rr = r0 + lax.broadcasted_iota(jnp.int32, (strip, LANES), 0)
                cc = c * LANES + lax.broadcasted_iota(jnp.int32, (strip, LANES), 1)
                x = jnp.where(rr >= cc, x, -jnp.inf)
            chunks.append(x)
        m_prev = m_sc[h, rows, :]
        m_loc = functools.reduce(jnp.maximum, chunks)
        m_new = jnp.maximum(m_prev, jnp.max(m_loc, axis=-1, keepdims=True))
        alpha = jnp.exp2(m_prev - m_new)
        ps = [jnp.exp2(x - m_new) for x in chunks]
        l_sc[h, rows, :] = alpha * l_sc[h, rows, :] + functools.reduce(jnp.add, ps)
        m_sc[h, rows, :] = m_new
        a_sc[slot, rows, :] = alpha
        dead = [jnp.zeros((strip, LANES), BF16)] * (n_chunk - len(live))
        p_sc[slot, rows, :] = jnp.concatenate([p.astype(BF16) for p in ps] + dead, axis=1)

    def step(masked):
        scores(0)
        for h in range(MLA_HEADS):
            if h + 1 < MLA_HEADS:
                scores(h + 1)
            for r0 in range(0, tq, strip):
                softmax_strip(h, r0, masked)
            pv = jnp.dot(p_sc[h % 2], v_ref[0, h], preferred_element_type=F32)
            acc_sc[h] = a_sc[h % 2] * acc_sc[h] + pv

    @pl.when(ki < qi)
    def _():
        step(False)

    @pl.when(ki == qi)
    def _():
        step(True)
        for h in range(MLA_HEADS):
            o = acc_sc[h] / jnp.sum(l_sc[h], axis=-1, keepdims=True)
            g = g_ref[:, h * V_HEAD:(h + 1) * V_HEAD]
            o_ref[:, h * V_HEAD:(h + 1) * V_HEAD] = (o * _silu(g)).astype(o_ref.dtype)


def _attention(q, k, v, gmla, bsz, seq, tq):
    hd = MLA_HEADS
    nq = seq // tq
    pairs = [(i, j) for i in range(nq) for j in range(i + 1)]
    qi_tab = jnp.asarray([i for i, _ in pairs], jnp.int32)
    ki_tab = jnp.asarray([j for _, j in pairs], jnp.int32)
    q_idx = lambda b, t, qi, ki: (b, 0, qi[t], 0)
    kv_idx = lambda b, t, qi, ki: (b, 0, ki[t], 0)
    row = lambda b, t, qi, ki: (b * nq + qi[t], 0)
    grid_spec = pltpu.PrefetchScalarGridSpec(
        num_scalar_prefetch=2,
        grid=(bsz, len(pairs)),
        in_specs=[pl.BlockSpec((1, hd, tq, QK_PAD), q_idx),
                  pl.BlockSpec((1, hd, tq, QK_PAD), kv_idx),
                  pl.BlockSpec((1, hd, tq, V_HEAD), kv_idx),
                  pl.BlockSpec((tq, MLA_WIDTH), row)],
        out_specs=pl.BlockSpec((tq, MLA_WIDTH), row),
        scratch_shapes=[pltpu.VMEM((hd, tq, LANES), F32), pltpu.VMEM((hd, tq, LANES), F32),
                        pltpu.VMEM((hd, tq, V_HEAD), F32),
                        pltpu.VMEM((2, tq, tq), F32), pltpu.VMEM((2, tq, tq), BF16),
                        pltpu.VMEM((2, tq, LANES), F32)])
    return pl.pallas_call(
        _attn_kernel,
        grid_spec=grid_spec,
        out_shape=jax.ShapeDtypeStruct((bsz * seq, MLA_WIDTH), BF16),
        compiler_params=_params("parallel", "arbitrary"),
        name="attention",
    )(qi_tab, ki_tab, q, k, v, gmla)


def _split3(x):
    hi = x.astype(BF16)
    r1 = x - hi.astype(F32)
    mid = r1.astype(BF16)
    lo = (r1 - mid.astype(F32)).astype(BF16)
    return hi, mid, lo


def _ssd_kernel(z_ref, xbc_ref, kd_ref, cw_ref, cb_ref, dtb_ref, alog_ref, dsk_ref, ng_ref, o_ref,
                xbuf, xact, state, shx, acol_sc, xdtb_sc, xdte_sc, grow_sc, y_sc, *, layer):
    i = pl.program_id(0)
    nb, ts = o_ref.shape[0], o_ref.shape[1]
    L = SSD_CHUNK
    hp = SSD_HEAD_DIM
    gw = SSD_WIDTH // SSD_GROUPS
    pairs_per_group = SSD_HEADS // SSD_GROUPS // 2
    npair = SSD_HEADS // 2

    @pl.when(i == 0)
    def _():
        xbuf[:, 0:SSD_HALO, :] = jnp.zeros((nb, SSD_HALO, SSD_XBC), F32)
        state[...] = jnp.zeros(state.shape, F32)

    @pl.when(i > 0)
    def _():
        xbuf[:, 0:SSD_HALO, :] = xbuf[:, ts:ts + SSD_HALO, :]

    for b in range(nb):
        xbuf[b, SSD_HALO:SSD_HALO + ts, :] = xbc_ref[b]
        for s in range(1, SSD_CONV_K):
            shx[s - 1, :, :] = xbuf[b, SSD_HALO - s:SSD_HALO - s + ts, :]
        acc = cb_ref[layer:layer + 1, :] + cw_ref[SSD_CONV_K - 1:SSD_CONV_K, :] * xbc_ref[b]
        for s in range(1, SSD_CONV_K):
            acc = acc + cw_ref[SSD_CONV_K - 1 - s:SSD_CONV_K - s, :] * shx[s - 1]
        xact[b] = _silu(acc)

    lane = lax.broadcasted_iota(jnp.int32, (1, LANES), 1)
    is_dt = (lane >= DT_LANE0) & (lane < DT_LANE0 + SSD_HEADS)
    a_neg = jnp.where(is_dt, -jnp.exp(alog_ref[...]) * math.log2(math.e), 0.0)
    rr = lax.broadcasted_iota(jnp.int32, (L, L), 0)
    cc = lax.broadcasted_iota(jnp.int32, (L, L), 1)
    causal = rr >= cc
    tri = causal.astype(BF16)
    left = lax.broadcasted_iota(jnp.int32, (L, LANES), 1) < hp

    def decay_geometry(b, r0):
        dt = _softplus(kd_ref[b, pl.ds(r0, L), :] + dtb_ref[...])
        hi, mid, lo = _split3(dt * a_neg)
        acs = (jnp.dot(tri, hi, preferred_element_type=F32)
               + jnp.dot(tri, mid, preferred_element_type=F32)
               + jnp.dot(tri, lo, preferred_element_type=F32))
        return dict(dt=dt, acs=acs, acs_t=acs.T)

    def expand(b, r0, v, sl):
        def col(a, h):
            return jnp.broadcast_to(a[:, DT_LANE0 + h:DT_LANE0 + h + 1], (L, LANES))

        lasts = []
        for j in range(npair):
            lanes = slice(j * LANES, (j + 1) * LANES)
            a0, a1 = col(v["acs"], 2 * j), col(v["acs"], 2 * j + 1)
            acol_sc[sl,2 * j] = a0
            acol_sc[sl,2 * j + 1] = a1
            acs_pair = jnp.where(left, a0, a1)
            dt_pair = jnp.where(left, col(v["dt"], 2 * j), col(v["dt"], 2 * j + 1))
            last = acs_pair[L - 1:L, :]
            xdt = xact[b, pl.ds(r0, L), lanes] * dt_pair
            xdtb_sc[sl,:, lanes] = xdt.astype(BF16)
            xdte_sc[sl,:, lanes] = (xdt * jnp.exp2(last - acs_pair)).astype(BF16)
            grow_sc[sl,:, lanes] = jnp.exp2(acs_pair)
            lasts.append(last)
        v["lasts"] = lasts

    def mix(b, r0, v, sl):
        ssq = jnp.zeros((L, LANES), F32)
        for g in range(SSD_GROUPS):
            c0 = SSD_WIDTH + g * SSD_STATE
            c1 = SSD_WIDTH + (SSD_GROUPS + g) * SSD_STATE
            grp = slice(g * gw, (g + 1) * gw)
            bg = xact[b, pl.ds(r0, L), c0:c0 + SSD_STATE]
            cg_b = xact[b, pl.ds(r0, L), c1:c1 + SSD_STATE].astype(BF16)
            cb = lax.dot_general(cg_b, bg.astype(BF16), (((1,), (1,)), ((), ())), preferred_element_type=F32)
            st_prev = state[b, :, grp]
            st_new = jnp.dot(bg.T.astype(BF16), xdte_sc[sl,:, grp], preferred_element_type=F32)
            y_off = jnp.dot(cg_b, st_prev.astype(BF16), preferred_element_type=F32) * grow_sc[sl,:, grp]
            last_g = jnp.concatenate(v["lasts"][g * pairs_per_group:(g + 1) * pairs_per_group], axis=1)
            state[b, :, grp] = st_prev * jnp.exp2(last_g) + st_new
            for jj in range(pairs_per_group):
                j = g * pairs_per_group + jj
                lanes = slice(j * LANES, (j + 1) * LANES)
                pair = xdtb_sc[sl,:, lanes]
                yd = []
                for h in (2 * j, 2 * j + 1):
                    seg = acol_sc[sl,h] - v["acs_t"][DT_LANE0 + h:DT_LANE0 + h + 1, :]
                    decay = jnp.exp2(jnp.where(causal, seg, -jnp.inf))
                    yd.append(jnp.dot((cb * decay).astype(BF16), pair, preferred_element_type=F32))
                y = (jnp.where(left, yd[0], yd[1]) + y_off[:, jj * LANES:(jj + 1) * LANES]
                     + dsk_ref[:, lanes] * xact[b, pl.ds(r0, L), lanes])
                yz = y * _silu(z_ref[b, pl.ds(r0, L), lanes])
                y_sc[sl, :, lanes] = yz
                ssq = ssq + yz * yz
        v["inv"] = lax.rsqrt(jnp.sum(ssq, axis=-1, keepdims=True) * (1.0 / SSD_WIDTH) + NORM_EPS)

    def finish(b, r0, v, sl):
        o_ref[b, pl.ds(r0, L), :] = (y_sc[sl] * v["inv"] * ng_ref[layer:layer + 1, :]).astype(o_ref.dtype)

    jobs = [(b, k * L, k * nb + b) for k in range(ts // L) for b in range(nb)]
    vals = [decay_geometry(b, r0) for b, r0, _ in jobs]
    for stage in (expand, mix, finish):
        for (b, r0, sl), v in zip(jobs, vals):
            stage(b, r0, v, sl)


def _ssd(layer, z, xbc, kd, cw, cb, dtb, alog, dsk, ng, bsz, seq, ts):
    blk = lambda w: pl.BlockSpec((bsz, ts, w), lambda i: (0, i, 0))
    slots = bsz * (ts // SSD_CHUNK)
    out = pl.pallas_call(
        functools.partial(_ssd_kernel, layer=layer),
        grid=(seq // ts,),
        in_specs=[blk(SSD_WIDTH), blk(SSD_XBC), blk(KD_WIDTH),
                  _resident(cw, layer), _resident(cb, layer), _resident(dtb, layer), _resident(alog, layer),
                  _resident(dsk, layer), _resident(ng, layer)],
        out_specs=blk(SSD_WIDTH),
        out_shape=jax.ShapeDtypeStruct((bsz, seq, SSD_WIDTH), BF16),
        scratch_shapes=[pltpu.VMEM((bsz, ts + SSD_HALO, SSD_XBC), F32), pltpu.VMEM((bsz, ts, SSD_XBC), F32),
                        pltpu.VMEM((bsz, SSD_STATE, SSD_WIDTH), F32),
                        pltpu.VMEM((SSD_CONV_K - 1, ts, SSD_XBC), F32),
                        pltpu.VMEM((slots, SSD_HEADS, SSD_CHUNK, LANES), F32),
                        pltpu.VMEM((slots, SSD_CHUNK, SSD_WIDTH), BF16), pltpu.VMEM((slots, SSD_CHUNK, SSD_WIDTH), BF16),
                        pltpu.VMEM((slots, SSD_CHUNK, SSD_WIDTH), F32), pltpu.VMEM((slots, SSD_CHUNK, SSD_WIDTH), F32)],
        compiler_params=_params("arbitrary"),
        name="ssd",
    )(z.reshape(bsz, seq, -1), xbc.reshape(bsz, seq, -1), kd.reshape(bsz, seq, -1), cw, cb, dtb, alog, dsk, ng)
    return out.reshape(bsz * seq, SSD_WIDTH)


def _conv_out_kernel(u_ref, gc_ref, x_ref, a_ref, s_ref, wdw_ref, bdw_ref, lng_ref, lnb_ref, wpw_ref, w_ref,
                     o_ref, hbuf, shbuf, ybuf, oc_buf, *, tiles_per_seq, layer):
    i = pl.program_id(0)
    tt = u_ref.shape[0]
    cw = CONV_WIDTH
    n_piece = w_ref.shape[0]
    seq_start = lax.rem(i, tiles_per_seq) == 0

    @pl.when(i == 0)
    def _():
        oc_buf[...] = jnp.zeros(oc_buf.shape, oc_buf.dtype)

    @pl.when(seq_start)
    def _():
        hbuf[0:CONV_HALO, :] = jnp.zeros((CONV_HALO, cw), F32)

    @pl.when(jnp.logical_not(seq_start))
    def _():
        hbuf[0:CONV_HALO, :] = hbuf[tt:tt + CONV_HALO, :]

    n0 = a_ref.shape[1]
    n1 = n0 + cw
    n2 = n1 + s_ref.shape[1]
    base = CONV_HALO - (CONV_K - 1)
    span = tt + CONV_HALO - SUBLANES
    dyn0 = pl.multiple_of(jnp.minimum(i, 0), ANCHOR)

    def fold(v):
        rows = functools.reduce(jnp.add, [v[r:r + ANCHOR, :] for r in range(0, v.shape[0], ANCHOR)])
        return functools.reduce(jnp.add, [rows[:, c:c + LANES] for c in range(0, v.shape[1], LANES)])

    def piece(p):
        cols = slice(p * OUT_PIECE, (p + 1) * OUT_PIECE)
        y = jnp.dot(oc_buf[pl.ds(dyn0, tt), :], w_ref[p, n0:n1, :], preferred_element_type=F32)
        y = y + jnp.dot(a_ref[...], w_ref[p, 0:n0, :], preferred_element_type=F32)
        y = y + jnp.dot(s_ref[...], w_ref[p, n1:n2, :], preferred_element_type=F32)
        o_ref[:, cols] = x_ref[:, cols] + y

    def glu():
        hval = u_ref[:, 0:cw] * _sigmoid(u_ref[:, cw:2 * cw])
        hbuf[CONV_HALO:CONV_HALO + tt, :] = hval
        return fold(hval)

    def shifts(residues):
        for res in residues:
            for r0 in range(0, span, CONV_SUB):
                rows = min(CONV_SUB, span - r0)
                shbuf[res - 1, r0:r0 + rows, :] = hbuf[res + r0:res + r0 + rows, :]

    def strips(first, count):
        mark = None
        for r0 in range(first, first + count * CONV_SUB, CONV_SUB):
            acc = jnp.broadcast_to(bdw_ref[layer:layer + 1, :], (CONV_SUB, cw))
            for kk in range(CONV_K):
                res = (base + kk) % SUBLANES
                start = r0 + (base + kk - res)
                tap = hbuf[start:start + CONV_SUB, :] if res == 0 else shbuf[res - 1, start:start + CONV_SUB, :]
                acc = acc + wdw_ref[kk:kk + 1, :] * tap
            ybuf[r0:r0 + CONV_SUB, :] = acc
            part = fold(acc)
            mark = part if mark is None else mark + part
        return mark

    def anchor(mark):
        oc_buf[pl.ds(tt + dyn0, ANCHOR), 0:LANES] = mark.astype(oc_buf.dtype)

    n_free = 3
    n_groups = n_piece - n_free - 1
    per = tt // CONV_SUB // n_groups
    piece(0)
    anchor(glu())
    piece(1)
    shifts(range(1, SUBLANES))
    for p in range(2, n_free + 1):
        piece(p)
    for k in range(n_groups):
        anchor(strips(k * per * CONV_SUB, per))
        piece(n_free + 1 + k)

    y = ybuf[...]
    mu = jnp.mean(y, axis=-1, keepdims=True)
    yc = y - mu
    var = jnp.mean(yc * yc, axis=-1, keepdims=True)
    hn = _silu(yc * lax.rsqrt(var + LN_EPS) * lng_ref[layer:layer + 1, :] + lnb_ref[layer:layer + 1, :])
    out = jnp.dot(hn.astype(BF16), wpw_ref[...], preferred_element_type=F32)
    oc_buf[0:tt, :] = (out * _silu(gc_ref[...])).astype(oc_buf.dtype)


def _conv_out(layer, uconv, gconv, x2, o_mla, o_ssd, wdw, bdw, lng, lnb, wpw, w_out, seq, tt):
    m, d = x2.shape
    n = m // tt
    n_piece = w_out.shape[1]
    cur_row = lambda i: (jnp.minimum(i, n - 1), 0)
    prev_row = lambda i: (jnp.maximum(i - 1, 0), 0)
    w_spec = pl.BlockSpec((None,) + w_out.shape[1:], lambda i: (layer, 0, 0, 0), pipeline_mode=pl.Buffered(1))
    return pl.pallas_call(
        functools.partial(_conv_out_kernel, tiles_per_seq=seq // tt, layer=layer),
        grid=(n + 1,),
        in_specs=[pl.BlockSpec((tt, 2 * CONV_WIDTH), cur_row),
                  pl.BlockSpec((tt, CONV_WIDTH), cur_row),
                  pl.BlockSpec((tt, d), prev_row),
                  pl.BlockSpec((tt, o_mla.shape[1]), prev_row),
                  pl.BlockSpec((tt, o_ssd.shape[1]), prev_row),
                  _resident(wdw, layer), _resident(bdw, layer), _resident(lng, layer), _resident(lnb, layer),
                  _resident(wpw, layer), w_spec],
        out_specs=pl.BlockSpec((tt, d), prev_row),
        out_shape=jax.ShapeDtypeStruct((m, d), F32),
        scratch_shapes=[pltpu.VMEM((tt + CONV_HALO, CONV_WIDTH), F32),
                        pltpu.VMEM((SUBLANES - 1, tt + CONV_HALO, CONV_WIDTH), F32),
                        pltpu.VMEM((tt, CONV_WIDTH), F32),
                        pltpu.VMEM((tt + ANCHOR, CONV_WIDTH), BF16)],
        compiler_params=_params("arbitrary"),
        name="conv_out_proj",
    )(uconv, gconv, x2, o_mla, o_ssd, wdw, bdw, lng, lnb, wpw, w_out)


def _rows(p):
    return p.astype(F32)


def _pad_rows(p, lane0, width):
    return jnp.pad(p.astype(F32), ((0, 0), (lane0, width - lane0 - p.shape[1])))[:, None, :]


ROW_TILE = 512


def _tile_sizes(m, seq):
    tm_in, t_seq = min(ROW_TILE, m), min(ROW_TILE, seq)
    t_prep = min(2 * ROW_TILE, seq)
    assert m % tm_in == 0 and seq % t_seq == 0 and t_seq % SSD_CHUNK == 0 and t_seq % (4 * CONV_SUB) == 0
    assert seq % t_prep == 0
    return tm_in, t_seq, t_prep


def _rope_tables(seq):
    half = QK_ROPE // 2
    f32 = np.float32
    inv_freq = np.power(f32(ROPE_THETA), -np.arange(half, dtype=f32) / f32(half)).astype(f32)
    ang = (np.arange(seq).astype(f32)[:, None] * inv_freq[None, :]).astype(f32)
    cos = np.cos(ang.astype(np.float64)).astype(f32)
    sin = np.sin(ang.astype(np.float64)).astype(f32)
    z = lambda n: np.zeros((seq, n), f32)
    cos_t = np.concatenate([cos, cos, z(LANES - QK_ROPE)], axis=1)
    sa_t = np.concatenate([-sin, z(LANES - half)], axis=1)
    sb_t = np.concatenate([z(half), sin, z(LANES - QK_ROPE)], axis=1)
    return jnp.asarray(cos_t), jnp.asarray(sa_t), jnp.asarray(sb_t)


def kernel(x, norm_g, w_in, q_a_norm, w_q_b, kv_a_norm, w_kv_b, q_norm, k_norm, conv_dw_w, conv_dw_b, conv_ln_g,
           conv_ln_b, conv_pw_w, ssd_conv_w, ssd_conv_b, ssd_dt_bias, ssd_A_log, ssd_D, ssd_norm_g, w_out):
    bsz, seq, d = x.shape
    m = bsz * seq
    x2 = x.reshape(m, d)
    tabs = _rope_tables(seq)

    w_in_k = _prep_w_in(w_in)
    wq, wkv, wpw = _prep_small(w_q_b, w_kv_b, conv_pw_w)
    w_out_k = _prep_cast(w_out, min(512, w_out.shape[1]))
    g_in, qan, kvan = _rows(norm_g), _rows(q_a_norm), _rows(kv_a_norm)
    kn = _pad_rows(k_norm, 0, QK_PAD)
    half = QK_ROPE // 2
    q_swapped = jnp.concatenate([q_norm[:, QK_NOPE + half:], q_norm[:, QK_NOPE:QK_NOPE + half]], axis=1)
    qn = jnp.concatenate([_pad_rows(q_norm, 0, QK_PAD), _pad_rows(q_swapped, 0, LANES)], axis=2)
    wdw = jnp.pad(conv_dw_w.astype(F32), ((0, 0), (0, CONV_HALO - CONV_K), (0, 0)))
    bdw, lng, lnb = _rows(conv_dw_b), _rows(conv_ln_g), _rows(conv_ln_b)
    cw = ssd_conv_w.astype(F32)
    cb, ng = _rows(ssd_conv_b), _rows(ssd_norm_g)
    dtb, alog = _pad_rows(ssd_dt_bias, DT_LANE0, LANES), _pad_rows(ssd_A_log, DT_LANE0, LANES)
    dsk = jnp.repeat(ssd_D, SSD_HEAD_DIM, axis=1).astype(F32)[:, None, :]

    tm_in, t_seq, t_prep = _tile_sizes(m, seq)
    for layer in range(norm_g.shape[0]):
        cq, ckv, kd, gmla, uconv, gconv, z, xbc = _in_proj(layer, x2, g_in, w_in_k, tm_in)
        q, k, v = _mla_prep(layer, cq, ckv, kd, qan, wq, kvan, wkv, qn, kn, *tabs, bsz, seq, t_prep)
        o_mla = _attention(q, k, v, gmla, bsz, seq, t_seq)
        o_ssd = _ssd(layer, z, xbc, kd, cw, cb, dtb, alog, dsk, ng, bsz, seq, t_seq)
        x2 = _conv_out(layer, uconv, gconv, x2, o_mla, o_ssd, wdw, bdw, lng, lnb, wpw, w_out_k, seq, t_seq)
    return x2.reshape(bsz, seq, d)
```
